```python
import math
import jax, jax.numpy as jnp
from jax import lax
import numpy as np

D_MODEL = 2048
BATCH = 8
SEQ = 4096
DEPTH = 4

N_A_LAYERS = DEPTH // 2
N_B_LAYERS = DEPTH - N_A_LAYERS
RMS_EPS = 1e-6

E_A = D_MODEL
POOL_WINDOWS = (2, 4, 8, 16)
N_POOL_GROUPS = len(POOL_WINDOWS)
POOL_GROUP_DIM = E_A // N_POOL_GROUPS

HEAD_DIM = 128
HEADS_PER_GROUP = D_MODEL // HEAD_DIM
DILATED_PAIRS = ((128, 1), (512, 4), (2048, 16))
N_DIL_GROUPS = len(DILATED_PAIRS)
E_B = HEADS_PER_GROUP * HEAD_DIM
ROPE_THETA = 10000.0
NEG_INF = -1e30

kernel_name = "yoco_pool_dilated_hybrid"


def rmsnorm(x, g):
    xf = x.astype(jnp.float32)
    inv = lax.rsqrt(jnp.mean(xf * xf, axis=-1, keepdims=True) + RMS_EPS)
    return (xf * inv).astype(x.dtype) * g


def rope_tables(seq):
    inv_freq = 1.0 / (ROPE_THETA ** (jnp.arange(0, HEAD_DIM, 2, dtype=jnp.float32) / HEAD_DIM))
    ang = jnp.arange(seq, dtype=jnp.float32)[:, None] * inv_freq[None, :]
    return jnp.cos(ang), jnp.sin(ang)


def apply_rope(t, cos, sin):
    tf = t.astype(jnp.float32)
    t1, t2 = tf[..., : HEAD_DIM // 2], tf[..., HEAD_DIM // 2:]
    c, s = cos[None, :, None, :], sin[None, :, None, :]
    return jnp.concatenate([t1 * c - t2 * s, t2 * c + t1 * s], axis=-1).astype(t.dtype)


def multiscale_causal_pool(u):
    b, s, _ = u.shape
    u4 = u.reshape(b, s, N_POOL_GROUPS, POOL_GROUP_DIM)
    csum = jnp.cumsum(u4.astype(jnp.float32), axis=1)
    csum = jnp.concatenate([jnp.zeros_like(csum[:, :1]), csum], axis=1)
    win = jnp.asarray(POOL_WINDOWS, dtype=jnp.int32)
    t1 = jnp.arange(1, s + 1, dtype=jnp.int32)[:, None]
    lower = jnp.maximum(t1 - win[None, :], 0)
    c_lo = csum[:, lower, jnp.arange(N_POOL_GROUPS)[None, :], :]
    count = jnp.minimum(t1, win[None, :]).astype(jnp.float32)
    mean = (csum[:, 1:] - c_lo) / count[None, :, :, None]
    return (mean - u4.astype(jnp.float32)).astype(u.dtype)


def dilated_window_attention(q, k, v, window, dilation):
    b, s, h, hd = q.shape
    d = dilation
    nb = window // dilation
    m = s // d
    nblk = -(-m // nb)
    m_pad = nblk * nb

    def residues(t):
        return t.reshape(b, m, d, h, hd).transpose(0, 2, 3, 1, 4)

    qr, kr, vr = residues(q), residues(k), residues(v)
    qb = jnp.pad(qr, ((0, 0), (0, 0), (0, 0), (0, m_pad - m), (0, 0))).reshape(b, d, h, nblk, nb, hd)
    kv_pad = ((0, 0), (0, 0), (0, 0), (nb, m_pad - m), (0, 0))
    kp = jnp.pad(kr, kv_pad).reshape(b, d, h, nblk + 1, nb, hd)
    vp = jnp.pad(vr, kv_pad).reshape(b, d, h, nblk + 1, nb, hd)
    kb = jnp.concatenate([kp[:, :, :, :-1], kp[:, :, :, 1:]], axis=4)
    vb = jnp.concatenate([vp[:, :, :, :-1], vp[:, :, :, 1:]], axis=4)

    scores = jnp.einsum('bdhnqc,bdhnkc->bdhnqk', qb, kb).astype(jnp.float32) * (1.0 / math.sqrt(hd))
    r_idx = jnp.arange(nb)[:, None]
    c_idx = jnp.arange(2 * nb)[None, :]
    band = (c_idx >= r_idx) & (c_idx <= r_idx + nb)
    blk = jnp.arange(nblk)[:, None, None]
    mask = band[None] & (blk * nb + c_idx[None] >= nb)
    scores = jnp.where(mask[None, None, None], scores, NEG_INF)
    lse = jax.nn.logsumexp(scores, axis=-1)
    p = jnp.exp(scores - lse[..., None]).astype(v.dtype)
    out = jnp.einsum('bdhnqk,bdhnkc->bdhnqc', p, vb)

    out = out.reshape(b, d, h, m_pad, hd)[:, :, :, :m]
    lse = lse.reshape(b, d, h, m_pad)[:, :, :, :m]
    out = out.transpose(0, 3, 1, 2, 4).reshape(b, s, h, hd)
    lse = lse.transpose(0, 3, 1, 2).reshape(b, s, h)
    return out, lse


def _fwd_setup_inputs(seed: int = 0) -> dict:
    key = jax.random.key(seed)
    ks = jax.random.split(key, 13)
    f32 = jnp.float32
    x = jax.random.normal(ks[0], (BATCH, SEQ, D_MODEL), f32)
    norm_a = 1.0 + 0.1 * jax.random.normal(ks[1], (N_A_LAYERS, D_MODEL), f32)
    w_in_a = jax.random.normal(ks[2], (N_A_LAYERS, D_MODEL, 2 * E_A), f32) * D_MODEL ** -0.5
    w_grp_a = jax.random.normal(ks[3], (N_A_LAYERS, N_POOL_GROUPS, POOL_GROUP_DIM, POOL_GROUP_DIM), f32) * POOL_GROUP_DIM ** -0.5
    scale_a = 1.0 + 0.1 * jax.random.normal(ks[4], (N_A_LAYERS, E_A), f32)
    w_out_a = jax.random.normal(ks[5], (N_A_LAYERS, E_A, D_MODEL), f32) * E_A ** -0.5
    norm_kv = 1.0 + 0.1 * jax.random.normal(ks[6], (D_MODEL,), f32)
    w_k = jax.random.normal(ks[7], (D_MODEL, E_B), f32) * D_MODEL ** -0.5
    w_v = jax.random.normal(ks[8], (D_MODEL, E_B), f32) * D_MODEL ** -0.5
    norm_b = 1.0 + 0.1 * jax.random.normal(ks[9], (N_B_LAYERS, D_MODEL), f32)
    w_in_b = jax.random.normal(ks[10], (N_B_LAYERS, D_MODEL, N_DIL_GROUPS * E_B + E_B), f32) * D_MODEL ** -0.5
    w_out_b = jax.random.normal(ks[11], (N_B_LAYERS, E_B, D_MODEL), f32) * E_B ** -0.5
    norm_f = 1.0 + 0.1 * jax.random.normal(ks[12], (D_MODEL,), f32)
    return {"x": x, "norm_a": norm_a, "w_in_a": w_in_a, "w_grp_a": w_grp_a, "scale_a": scale_a,
            "w_out_a": w_out_a, "norm_kv": norm_kv, "w_k": w_k, "w_v": w_v, "norm_b": norm_b,
            "w_in_b": w_in_b, "w_out_b": w_out_b, "norm_f": norm_f}


def _fwd_reference(x, norm_a, w_in_a, w_grp_a, scale_a, w_out_a, norm_kv, w_k, w_v, norm_b, w_in_b, w_out_b, norm_f):
    b, s, _ = x.shape
    cos, sin = rope_tables(s)
    k_shared = None
    v_shared = None
    for layer in range(DEPTH):
        if layer < N_A_LAYERS:
            i = layer
            hdn = rmsnorm(x, norm_a[i])
            proj = hdn @ w_in_a[i]
            u, gate = proj[..., :E_A], proj[..., E_A:]
            pooled = multiscale_causal_pool(u)
            y = jnp.einsum('bsgc,gcd->bsgd', pooled, w_grp_a[i]).reshape(b, s, E_A) * scale_a[i]
            x = x + (y * jax.nn.silu(gate)) @ w_out_a[i]
            if layer == N_A_LAYERS - 1:
                kv_in = rmsnorm(x, norm_kv)
                k_shared = apply_rope((kv_in @ w_k).reshape(b, s, HEADS_PER_GROUP, HEAD_DIM), cos, sin)
                v_shared = (kv_in @ w_v).reshape(b, s, HEADS_PER_GROUP, HEAD_DIM)
        else:
            i = layer - N_A_LAYERS
            hdn = rmsnorm(x, norm_b[i])
            proj = hdn @ w_in_b[i]
            q_all = proj[..., : N_DIL_GROUPS * E_B].reshape(b, s, N_DIL_GROUPS, HEADS_PER_GROUP, HEAD_DIM)
            gate = proj[..., N_DIL_GROUPS * E_B:]
            outs = []
            lses = []
            for g, (window, dilation) in enumerate(DILATED_PAIRS):
                q = apply_rope(q_all[:, :, g], cos, sin)
                o_g, lse_g = dilated_window_attention(q, k_shared, v_shared, window, dilation)
                outs.append(o_g)
                lses.append(lse_g)
            alpha = jax.nn.softmax(jnp.stack(lses, axis=0), axis=0)
            merged = jnp.sum(alpha[..., None].astype(x.dtype) * jnp.stack(outs, axis=0), axis=0)
            merged = merged.reshape(b, s, E_B)
            x = x + (merged * jax.nn.silu(gate)) @ w_out_b[i]
    return rmsnorm(x, norm_f)


import jax as _jax
import jax.numpy as _jnp

TWIN_FORMAT = 'train_step'
FWD_PARAMS = ['x', 'norm_a', 'w_in_a', 'w_grp_a', 'scale_a', 'w_out_a', 'norm_kv', 'w_k', 'w_v', 'norm_b', 'w_in_b', 'w_out_b', 'norm_f']
TWIN_WEIGHTS = ['norm_a', 'w_in_a', 'w_grp_a', 'scale_a', 'w_out_a', 'norm_kv', 'w_k', 'w_v', 'norm_b', 'w_in_b', 'w_out_b', 'norm_f']
TWIN_DIFF_INPUT = 'x'
TWIN_INPUTS = ['x', 'norm_a', 'w_in_a', 'w_grp_a', 'scale_a', 'w_out_a', 'norm_kv', 'w_k', 'w_v', 'norm_b', 'w_in_b', 'w_out_b', 'norm_f', 'loss_target', 'm_norm_a', 'm_w_in_a', 'm_w_grp_a', 'm_scale_a', 'm_w_out_a', 'm_norm_kv', 'm_w_k', 'm_w_v', 'm_norm_b', 'm_w_in_b', 'm_w_out_b', 'm_norm_f', 'v_norm_a', 'v_w_in_a', 'v_w_grp_a', 'v_scale_a', 'v_w_out_a', 'v_norm_kv', 'v_w_k', 'v_w_v', 'v_norm_b', 'v_w_in_b', 'v_w_out_b', 'v_norm_f']
TWIN_OUTPUTS = ['loss', 'grad_x', 'grad_norm_a', 'grad_w_in_a', 'grad_w_grp_a', 'grad_scale_a', 'grad_w_out_a', 'grad_norm_kv', 'grad_w_k', 'grad_w_v', 'grad_norm_b', 'grad_w_in_b', 'grad_w_out_b', 'grad_norm_f', 'delta_norm_a', 'delta_w_in_a', 'delta_w_grp_a', 'delta_scale_a', 'delta_w_out_a', 'delta_norm_kv', 'delta_w_k', 'delta_w_v', 'delta_norm_b', 'delta_w_in_b', 'delta_w_out_b', 'delta_norm_f', 'new_m_norm_a', 'new_m_w_in_a', 'new_m_w_grp_a', 'new_m_scale_a', 'new_m_w_out_a', 'new_m_norm_kv', 'new_m_w_k', 'new_m_w_v', 'new_m_norm_b', 'new_m_w_in_b', 'new_m_w_out_b', 'new_m_norm_f', 'new_v_norm_a', 'new_v_w_in_a', 'new_v_w_grp_a', 'new_v_scale_a', 'new_v_w_out_a', 'new_v_norm_kv', 'new_v_w_k', 'new_v_w_v', 'new_v_norm_b', 'new_v_w_in_b', 'new_v_w_out_b', 'new_v_norm_f']
TWIN_LEAF_KINDS = {'loss': 'loss', 'grad_x': 'grad_x', 'grad_norm_a': 'grad_w', 'grad_w_in_a': 'grad_w', 'grad_w_grp_a': 'grad_w', 'grad_scale_a': 'grad_w', 'grad_w_out_a': 'grad_w', 'grad_norm_kv': 'grad_w', 'grad_w_k': 'grad_w', 'grad_w_v': 'grad_w', 'grad_norm_b': 'grad_w', 'grad_w_in_b': 'grad_w', 'grad_w_out_b': 'grad_w', 'grad_norm_f': 'grad_w', 'delta_norm_a': 'delta_w', 'delta_w_in_a': 'delta_w', 'delta_w_grp_a': 'delta_w', 'delta_scale_a': 'delta_w', 'delta_w_out_a': 'delta_w', 'delta_norm_kv': 'delta_w', 'delta_w_k': 'delta_w', 'delta_w_v': 'delta_w', 'delta_norm_b': 'delta_w', 'delta_w_in_b': 'delta_w', 'delta_w_out_b': 'delta_w', 'delta_norm_f': 'delta_w', 'new_m_norm_a': 'new_m', 'new_m_w_in_a': 'new_m', 'new_m_w_grp_a': 'new_m', 'new_m_scale_a': 'new_m', 'new_m_w_out_a': 'new_m', 'new_m_norm_kv': 'new_m', 'new_m_w_k': 'new_m', 'new_m_w_v': 'new_m', 'new_m_norm_b': 'new_m', 'new_m_w_in_b': 'new_m', 'new_m_w_out_b': 'new_m', 'new_m_norm_f': 'new_m', 'new_v_norm_a': 'new_v', 'new_v_w_in_a': 'new_v', 'new_v_w_grp_a': 'new_v', 'new_v_scale_a': 'new_v', 'new_v_w_out_a': 'new_v', 'new_v_norm_kv': 'new_v', 'new_v_w_k': 'new_v', 'new_v_w_v': 'new_v', 'new_v_norm_b': 'new_v', 'new_v_w_in_b': 'new_v', 'new_v_w_out_b': 'new_v', 'new_v_norm_f': 'new_v'}


def _forward(args):
    return _fwd_reference(*[args[k] for k in FWD_PARAMS])


def _output_shape():
    def fwd():
        inp = _fwd_setup_inputs(0)
        return _fwd_reference(*[inp[k] for k in FWD_PARAMS])
    out = _jax.eval_shape(fwd)
    return out.shape, out.dtype

N_MICROBATCH = 1
ADAM_LR = 0.001
ADAM_B1 = 0.9
ADAM_B2 = 0.999
ADAM_EPS = 1e-08
ADAM_WD = 0.01
ADAM_STEP = 10
PER_EXAMPLE_BATCH_AXIS = {'x': 0, 'loss_target': 0}
SHARED_INPUTS = []
_WEIGHT_DTYPES = {'norm_a': _jnp.float32, 'w_in_a': _jnp.float32, 'w_grp_a': _jnp.float32, 'scale_a': _jnp.float32, 'w_out_a': _jnp.float32, 'norm_kv': _jnp.float32, 'w_k': _jnp.float32, 'w_v': _jnp.float32, 'norm_b': _jnp.float32, 'w_in_b': _jnp.float32, 'w_out_b': _jnp.float32, 'norm_f': _jnp.float32}
MOMENT_SCALE = {'norm_a': 6.140849e-02, 'w_in_a': 4.423979e-02, 'w_grp_a': 4.353625e-02, 'scale_a': 4.447569e-02, 'w_out_a': 4.411847e-02, 'norm_kv': 1.676147e-02, 'w_k': 9.970979e-03, 'w_v': 1.324481e-02, 'norm_b': 1.186804e-02, 'w_in_b': 5.994357e-03, 'w_out_b': 9.416375e-03, 'norm_f': 1.612288e+01}


def _to_microbatches(a, axis):
    t = _jnp.moveaxis(a, axis, 0)
    t = t.reshape((N_MICROBATCH, t.shape[0] // N_MICROBATCH) + t.shape[1:])
    return _jnp.moveaxis(t, 1, axis + 1)


def setup_inputs(seed: int = 0) -> dict:
    inp = _fwd_setup_inputs(seed)
    key = _jax.random.fold_in(_jax.random.key(seed), 7919)
    shape, _ = _output_shape()
    out = dict(inp)
    out["loss_target"] = _jax.random.normal(_jax.random.fold_in(key, 0), shape, _jnp.float32)
    for i, name in enumerate(TWIN_WEIGHTS):
        w = inp[name].astype(_jnp.float32)
        if MOMENT_SCALE is None:
            s = _jnp.sqrt(_jnp.mean(_jnp.square(w)) + 1e-30)
        else:
            s = MOMENT_SCALE[name]
        km, kv = _jax.random.split(_jax.random.fold_in(key, i + 1))
        out[name] = w
        out["m_" + name] = s * _jax.random.normal(km, w.shape, _jnp.float32)
        out["v_" + name] = (s * s) * _jax.random.uniform(kv, w.shape, _jnp.float32, 0.5, 1.5)
    if N_MICROBATCH > 1:
        for name, axis in PER_EXAMPLE_BATCH_AXIS.items():
            out[name] = _to_microbatches(out[name], axis)
    return {'x': out['x'], 'norm_a': out['norm_a'], 'w_in_a': out['w_in_a'], 'w_grp_a': out['w_grp_a'], 'scale_a': out['scale_a'], 'w_out_a': out['w_out_a'], 'norm_kv': out['norm_kv'], 'w_k': out['w_k'], 'w_v': out['w_v'], 'norm_b': out['norm_b'], 'w_in_b': out['w_in_b'], 'w_out_b': out['w_out_b'], 'norm_f': out['norm_f'], 'loss_target': out['loss_target'], 'm_norm_a': out['m_norm_a'], 'm_w_in_a': out['m_w_in_a'], 'm_w_grp_a': out['m_w_grp_a'], 'm_scale_a': out['m_scale_a'], 'm_w_out_a': out['m_w_out_a'], 'm_norm_kv': out['m_norm_kv'], 'm_w_k': out['m_w_k'], 'm_w_v': out['m_w_v'], 'm_norm_b': out['m_norm_b'], 'm_w_in_b': out['m_w_in_b'], 'm_w_out_b': out['m_w_out_b'], 'm_norm_f': out['m_norm_f'], 'v_norm_a': out['v_norm_a'], 'v_w_in_a': out['v_w_in_a'], 'v_w_grp_a': out['v_w_grp_a'], 'v_scale_a': out['v_scale_a'], 'v_w_out_a': out['v_w_out_a'], 'v_norm_kv': out['v_norm_kv'], 'v_w_k': out['v_w_k'], 'v_w_v': out['v_w_v'], 'v_norm_b': out['v_norm_b'], 'v_w_in_b': out['v_w_in_b'], 'v_w_out_b': out['v_w_out_b'], 'v_norm_f': out['v_norm_f']}


def _loss(weights, diff, rest, loss_target):
    with _jax.named_scope("forward"):
        args = {**rest, TWIN_DIFF_INPUT: diff, **{k: w.astype(_WEIGHT_DTYPES[k]) for k, w in weights.items()}}
        y = _forward(args)
    with _jax.named_scope("loss_head"):
        err = _jnp.square(y.astype(_jnp.float32) - loss_target)
        return 0.5 * _jnp.sum(_jnp.mean(err, axis=-1)) if err.ndim else 0.5 * err


def _adamw(w, g, m, v):
    m = ADAM_B1 * m + (1.0 - ADAM_B1) * g
    v = ADAM_B2 * v + (1.0 - ADAM_B2) * _jnp.square(g)
    m_hat = m / (1.0 - ADAM_B1 ** ADAM_STEP)
    v_hat = v / (1.0 - ADAM_B2 ** ADAM_STEP)
    delta = -ADAM_LR * (m_hat / (_jnp.sqrt(v_hat) + ADAM_EPS) + ADAM_WD * w)
    return delta, m, v


def reference(x, norm_a, w_in_a, w_grp_a, scale_a, w_out_a, norm_kv, w_k, w_v, norm_b, w_in_b, w_out_b, norm_f, loss_target, m_norm_a, m_w_in_a, m_w_grp_a, m_scale_a, m_w_out_a, m_norm_kv, m_w_k, m_w_v, m_norm_b, m_w_in_b, m_w_out_b, m_norm_f, v_norm_a, v_w_in_a, v_w_grp_a, v_scale_a, v_w_out_a, v_norm_kv, v_w_k, v_w_v, v_norm_b, v_w_in_b, v_w_out_b, v_norm_f):
    given = dict(x=x, norm_a=norm_a, w_in_a=w_in_a, w_grp_a=w_grp_a, scale_a=scale_a, w_out_a=w_out_a, norm_kv=norm_kv, w_k=w_k, w_v=w_v, norm_b=norm_b, w_in_b=w_in_b, w_out_b=w_out_b, norm_f=norm_f, loss_target=loss_target, m_norm_a=m_norm_a, m_w_in_a=m_w_in_a, m_w_grp_a=m_w_grp_a, m_scale_a=m_scale_a, m_w_out_a=m_w_out_a, m_norm_kv=m_norm_kv, m_w_k=m_w_k, m_w_v=m_w_v, m_norm_b=m_norm_b, m_w_in_b=m_w_in_b, m_w_out_b=m_w_out_b, m_norm_f=m_norm_f, v_norm_a=v_norm_a, v_w_in_a=v_w_in_a, v_w_grp_a=v_w_grp_a, v_scale_a=v_scale_a, v_w_out_a=v_w_out_a, v_norm_kv=v_norm_kv, v_w_k=v_w_k, v_w_v=v_w_v, v_norm_b=v_norm_b, v_w_in_b=v_w_in_b, v_w_out_b=v_w_out_b, v_norm_f=v_norm_f)
    weights = {n: given[n] for n in TWIN_WEIGHTS}
    shared = {n: given[n] for n in SHARED_INPUTS}
    per_example = {n: given[n] for n in ['x']}
    grad_fn = _jax.value_and_grad(_loss, argnums=(0, 1))

    def one_microbatch(ex, loss_target):
        ex = dict(ex)
        diff = ex.pop(TWIN_DIFF_INPUT)
        return grad_fn(weights, diff, {**shared, **ex}, loss_target)

    if N_MICROBATCH == 1:
        loss, (grad_w, grad_x) = one_microbatch(per_example, given["loss_target"])
    else:
        def body(carry, xs):
            loss_sum, grad_sum = carry
            l_k, (gw_k, gx_k) = one_microbatch(xs[0], xs[1])
            with _jax.named_scope("update"):
                return (loss_sum + l_k, _jax.tree.map(_jnp.add, grad_sum, gw_k)), gx_k

        init = (_jnp.zeros((), _jnp.float32), _jax.tree.map(_jnp.zeros_like, weights))
        (loss, grad_w), grad_x = _jax.lax.scan(body, init, (per_example, given["loss_target"]))
    with _jax.named_scope("update"):
        delta_w, new_m, new_v = {}, {}, {}
        for n in TWIN_WEIGHTS:
            delta_w[n], new_m[n], new_v[n] = _adamw(weights[n], grad_w[n], given["m_" + n], given["v_" + n])
    return (loss, grad_x, *[grad_w[n] for n in TWIN_WEIGHTS], *[delta_w[n] for n in TWIN_WEIGHTS],
            *[new_m[n] for n in TWIN_WEIGHTS], *[new_v[n] for n in TWIN_WEIGHTS])
```

```python
import math

import jax
import jax.numpy as jnp
from jax import lax
from jax.experimental import pallas as pl
from jax.experimental.pallas import tpu as pltpu

F32 = jnp.float32
BF16 = jnp.bfloat16

N_DEV = 8
MESH = pl.DeviceIdType.MESH
RMS_EPS = 1e-6
HEAD_DIM = 128
HALF_HEAD = HEAD_DIM // 2
BAND = 128
DILATIONS = (1, 4, 16)
POOL_WINDOWS = (2, 4, 8, 16)
POOL_HALO = 16
ROPE_THETA = 10000.0
NEG_INF = -1e30
ATTN_SCALE = 1.0 / math.sqrt(HEAD_DIM)
ADAM_LR, ADAM_B1, ADAM_B2, ADAM_EPS, ADAM_WD, ADAM_STEP = 0.001, 0.9, 0.999, 1e-08, 0.01, 10
VMEM_LIMIT_BYTES = 56 * 1024 * 1024
ANY = pl.BlockSpec(memory_space=pl.ANY)
NT = (((1,), (1,)), ((), ()))
TN = (((0,), (0,)), ((), ()))


def _params(*semantics):
    return pltpu.CompilerParams(dimension_semantics=semantics, vmem_limit_bytes=VMEM_LIMIT_BYTES)


def _sigmoid(t):
    return 1.0 / (1.0 + jnp.exp(-t))


def _rope(t, cos2, sin2):
    return t * cos2 + pltpu.roll(t, HALF_HEAD, 1) * sin2


def _rope_bwd(dt, cos2, sin2):
    return dt * cos2 + pltpu.roll(dt * sin2, HALF_HEAD, 1)


def _norm_matmul(x, gain, w, w_block, w_index, n_col_blocks, tn, name):
    S, D = x.shape
    tm = min(512, S)

    def body(x_ref, g_ref, w_ref, o_ref, hn_ref, hs_ref):
        @pl.when(pl.program_id(1) == 0)
        def _():
            xf = x_ref[...]
            inv = lax.rsqrt(jnp.mean(xf * xf, axis=-1, keepdims=True) + RMS_EPS)
            hb = ((xf * inv) * g_ref[...]).astype(BF16)
            hs_ref[...] = hb
            hn_ref[...] = hb
        o_ref[...] = jnp.dot(hs_ref[...], w_ref[...], preferred_element_type=F32)

    return pl.pallas_call(
        body, name=name, grid=(S // tm, n_col_blocks),
        in_specs=[pl.BlockSpec((tm, D), lambda i, j: (i, 0)),
                  pl.BlockSpec((1, D), lambda i, j: (0, 0)),
                  pl.BlockSpec(w_block, w_index)],
        out_specs=[pl.BlockSpec((tm, tn), lambda i, j: (i, j)),
                   pl.BlockSpec((tm, D), lambda i, j: (i, 0))],
        out_shape=[jax.ShapeDtypeStruct((S, n_col_blocks * tn), F32), jax.ShapeDtypeStruct((S, D), BF16)],
        scratch_shapes=[pltpu.VMEM((tm, D), BF16)],
        compiler_params=_params("parallel", "arbitrary"))(x, gain, w)


def _matmul_res(a, w, layer, res, name):
    S, K = a.shape
    nd, _, rb, N = w.shape
    tm = min(512, S)

    def body(a_ref, w_ref, r_ref, o_ref):
        acc = jnp.dot(a_ref[:, 0:rb], w_ref[0], preferred_element_type=F32)
        for k in range(1, nd):
            acc = acc + jnp.dot(a_ref[:, k * rb:(k + 1) * rb], w_ref[k], preferred_element_type=F32)
        o_ref[...] = r_ref[...] + acc

    return pl.pallas_call(
        body, name=name, grid=(S // tm,),
        in_specs=[pl.BlockSpec((tm, K), lambda i: (i, 0)),
                  pl.BlockSpec((nd, None, rb, N), lambda i: (0, layer, 0, 0)),
                  pl.BlockSpec((tm, N), lambda i: (i, 0))],
        out_specs=pl.BlockSpec((tm, N), lambda i: (i, 0)),
        out_shape=jax.ShapeDtypeStruct((S, N), F32),
        compiler_params=_params("parallel"))(a, w, res)


def _matmul_nt_rows(dy, w, layer, name):
    S, N = dy.shape
    nd, _, rb, _ = w.shape
    tm = min(512, S)

    def body(d_ref, w_ref, o_ref):
        db = d_ref[...].astype(BF16)
        for k in range(nd):
            o_ref[:, k * rb:(k + 1) * rb] = lax.dot_general(db, w_ref[k], NT, preferred_element_type=F32)

    return pl.pallas_call(
        body, name=name, grid=(S // tm,),
        in_specs=[pl.BlockSpec((tm, N), lambda i: (i, 0)),
                  pl.BlockSpec((nd, None, rb, N), lambda i: (0, layer, 0, 0))],
        out_specs=pl.BlockSpec((tm, nd * rb), lambda i: (i, 0)),
        out_shape=jax.ShapeDtypeStruct((S, nd * rb), F32),
        compiler_params=_params("parallel"))(dy, w)


def _matmul_nt_cols(dp, w, w_block, w_index, n_red, tc, n_out, name):
    S = dp.shape[0]
    tm = min(512, S)

    def body(d_ref, w_ref, o_ref):
        @pl.when(pl.program_id(1) == 0)
        def _():
            o_ref[...] = jnp.zeros_like(o_ref)
        o_ref[...] += lax.dot_general(d_ref[...], w_ref[...], NT, preferred_element_type=F32)

    return pl.pallas_call(
        body, name=name, grid=(S // tm, n_red),
        in_specs=[pl.BlockSpec((tm, tc), lambda i, j: (i, j)),
                  pl.BlockSpec(w_block, w_index)],
        out_specs=pl.BlockSpec((tm, n_out), lambda i, j: (i, 0)),
        out_shape=jax.ShapeDtypeStruct((S, n_out), F32),
        compiler_params=_params("parallel", "arbitrary"))(dp, w)


def _matmul_tn(a, b, a_block, a_index, b_block, b_index, out_shape, out_block, out_index, acc_shape, n_outer, name):
    S = a.shape[0]
    ts = a_block[0]
    n_tok = S // ts

    def body(a_ref, b_ref, o_ref, acc_ref):
        s = pl.program_id(1)

        @pl.when(s == 0)
        def _():
            acc_ref[...] = jnp.zeros_like(acc_ref)
        acc_ref[...] += lax.dot_general(a_ref[...].astype(BF16), b_ref[...].astype(BF16), TN,
                                        preferred_element_type=F32)

        @pl.when(s == n_tok - 1)
        def _():
            o_ref[...] = acc_ref[...].astype(o_ref.dtype)

    return pl.pallas_call(
        body, name=name, grid=(n_outer, n_tok),
        in_specs=[pl.BlockSpec(a_block, a_index), pl.BlockSpec(b_block, b_index)],
        out_specs=pl.BlockSpec(out_block, out_index),
        out_shape=jax.ShapeDtypeStruct(out_shape, BF16),
        scratch_shapes=[pltpu.VMEM(acc_shape, F32)],
        compiler_params=_params("parallel", "arbitrary"))(a, b)


def _pool(scr_ref, u, row0, tm, E):
    gc = E // len(POOL_WINDOWS)
    t1 = row0 + lax.broadcasted_iota(jnp.int32, (tm, 1), 0) + 1
    out = []
    for g, win in enumerate(POOL_WINDOWS):
        cs = slice(g * gc, (g + 1) * gc)
        acc = u[:, cs]
        for k in range(1, win):
            acc = acc + scr_ref[pl.ds(POOL_HALO - k, tm), cs]
        count = jnp.minimum(t1, win).astype(F32)
        out.append(acc / count - u[:, cs])
    return out


def _a_mid_fwd(proj, wg, scale, name):
    S, E2 = proj.shape
    E = E2 // 2
    gc = E // len(POOL_WINDOWS)
    tm = min(256, S)
    hb = tm // POOL_HALO

    def body(u_ref, uh_ref, gt_ref, wg_ref, sc_ref, z_ref, scr_ref):
        i = pl.program_id(0)
        scr_ref[0:POOL_HALO, :] = jnp.where(i > 0, uh_ref[...], 0.0)
        u = u_ref[...]
        scr_ref[POOL_HALO:POOL_HALO + tm, :] = u
        pooled = _pool(scr_ref, u, i * tm, tm, E)
        for g in range(len(POOL_WINDOWS)):
            cs = slice(g * gc, (g + 1) * gc)
            y = jnp.dot(pooled[g].astype(BF16), wg_ref[g], preferred_element_type=F32) * sc_ref[:, cs]
            gate = gt_ref[:, cs]
            z_ref[:, cs] = (y * (gate * _sigmoid(gate))).astype(BF16)

    return pl.pallas_call(
        body, name=name, grid=(S // tm,),
        in_specs=[pl.BlockSpec((tm, E), lambda i: (i, 0)),
                  pl.BlockSpec((POOL_HALO, E), lambda i: (jnp.maximum(i * hb - 1, 0), 0)),
                  pl.BlockSpec((tm, E), lambda i: (i, 1)),
                  pl.BlockSpec((len(POOL_WINDOWS), gc, gc), lambda i: (0, 0, 0)),
                  pl.BlockSpec((1, E), lambda i: (0, 0))],
        out_specs=pl.BlockSpec((tm, E), lambda i: (i, 0)),
        out_shape=jax.ShapeDtypeStruct((S, E), BF16),
        scratch_shapes=[pltpu.VMEM((POOL_HALO + tm, E), F32)],
        compiler_params=_params("parallel"))(proj, proj, proj, wg, scale)


def _a_mid_bwd(dz, proj, wg, scale, name):
    S, E2 = proj.shape
    E = E2 // 2
    n_grp = len(POOL_WINDOWS)
    gc = E // n_grp
    tm = min(256, S)
    hb = tm // POOL_HALO
    n_tiles = S // tm
    last_halo = S // POOL_HALO - 1

    def body(dz_ref, dzh_ref, u_ref, uh_ref, gt_ref, gth_ref, wg_ref, sc_ref, dp_ref, dwg_ref, dsc_ref, scr_ref, q_ref):
        i = pl.program_id(0)

        @pl.when(i == 0)
        def _():
            dwg_ref[...] = jnp.zeros_like(dwg_ref)
            dsc_ref[...] = jnp.zeros_like(dsc_ref)

        scr_ref[0:POOL_HALO, :] = jnp.where(i > 0, uh_ref[...], 0.0)
        u = u_ref[...]
        scr_ref[POOL_HALO:POOL_HALO + tm, :] = u
        pooled = _pool(scr_ref, u, i * tm, tm, E)
        t1 = i * tm + lax.broadcasted_iota(jnp.int32, (tm, 1), 0) + 1
        t1h = (i + 1) * tm + lax.broadcasted_iota(jnp.int32, (POOL_HALO, 1), 0) + 1
        not_last = i < n_tiles - 1
        for g, win in enumerate(POOL_WINDOWS):
            cs = slice(g * gc, (g + 1) * gc)
            w = wg_ref[g]
            sc = sc_ref[:, cs]
            pb = pooled[g].astype(BF16)
            ypre = jnp.dot(pb, w, preferred_element_type=F32)
            gate = gt_ref[:, cs]
            sg = _sigmoid(gate)
            silu = gate * sg
            dzg = dz_ref[:, cs]
            dy = dzg * silu
            dp_ref[:, E + g * gc:E + (g + 1) * gc] = (dzg * (ypre * sc) * (sg * (1.0 + gate * (1.0 - sg)))).astype(BF16)
            dsc_ref[:, cs] += jnp.sum(dy * ypre, axis=0, keepdims=True)
            dyp = (dy * sc).astype(BF16)
            dwg_ref[g] += lax.dot_general(pb, dyp, TN, preferred_element_type=F32)
            dpool = lax.dot_general(dyp, w, NT, preferred_element_type=F32)
            gate_h = gth_ref[:, cs]
            dyp_h = (dzh_ref[:, cs] * (gate_h * _sigmoid(gate_h)) * sc).astype(BF16)
            dpool_h = lax.dot_general(dyp_h, w, NT, preferred_element_type=F32)
            q_ref[0:tm, cs] = dpool / jnp.minimum(t1, win).astype(F32)
            q_ref[tm:tm + POOL_HALO, cs] = jnp.where(not_last, dpool_h / jnp.minimum(t1h, win).astype(F32), 0.0)
            acc = q_ref[0:tm, cs] - dpool
            for k in range(1, win):
                acc = acc + q_ref[pl.ds(k, tm), cs]
            dp_ref[:, cs] = acc.astype(BF16)

    return pl.pallas_call(
        body, name=name, grid=(n_tiles,),
        in_specs=[pl.BlockSpec((tm, E), lambda i: (i, 0)),
                  pl.BlockSpec((POOL_HALO, E), lambda i: (jnp.minimum((i + 1) * hb, last_halo), 0)),
                  pl.BlockSpec((tm, E), lambda i: (i, 0)),
                  pl.BlockSpec((POOL_HALO, E), lambda i: (jnp.maximum(i * hb - 1, 0), 0)),
                  pl.BlockSpec((tm, E), lambda i: (i, 1)),
                  pl.BlockSpec((POOL_HALO, E), lambda i: (jnp.minimum((i + 1) * hb, last_halo), 1)),
                  pl.BlockSpec((n_grp, gc, gc), lambda i: (0, 0, 0)),
                  pl.BlockSpec((1, E), lambda i: (0, 0))],
        out_specs=[pl.BlockSpec((tm, E2), lambda i: (i, 0)),
                   pl.BlockSpec((n_grp, gc, gc), lambda i: (0, 0, 0)),
                   pl.BlockSpec((1, E), lambda i: (0, 0))],
        out_shape=[jax.ShapeDtypeStruct((S, E2), BF16),
                   jax.ShapeDtypeStruct((n_grp, gc, gc), F32),
                   jax.ShapeDtypeStruct((1, E), F32)],
        scratch_shapes=[pltpu.VMEM((POOL_HALO + tm, E), F32), pltpu.VMEM((tm + POOL_HALO, E), F32)],
        compiler_params=_params("arbitrary"))(dz, dz, proj, proj, proj, proj, wg, scale)


def _rope_kv(kvp, cos2, sin2, name):
    S, E2 = kvp.shape
    E = E2 // 2
    tm = min(256, S)

    def body(k_ref, v_ref, c_ref, s_ref, ko_ref, vo_ref):
        cosv, sinv = c_ref[...], s_ref[...]
        for h in range(E // HEAD_DIM):
            hs = slice(h * HEAD_DIM, (h + 1) * HEAD_DIM)
            ko_ref[:, hs] = _rope(k_ref[:, hs], cosv, sinv).astype(BF16)
        vo_ref[...] = v_ref[...].astype(BF16)

    return pl.pallas_call(
        body, name=name, grid=(S // tm,),
        in_specs=[pl.BlockSpec((tm, E), lambda i: (i, 0)), pl.BlockSpec((tm, E), lambda i: (i, 1)),
                  pl.BlockSpec((tm, HEAD_DIM), lambda i: (i, 0)), pl.BlockSpec((tm, HEAD_DIM), lambda i: (i, 0))],
        out_specs=[pl.BlockSpec((tm, E), lambda i: (i, 0)), pl.BlockSpec((tm, E), lambda i: (i, 0))],
        out_shape=[jax.ShapeDtypeStruct((S, E), BF16), jax.ShapeDtypeStruct((S, E), BF16)],
        compiler_params=_params("parallel"))(kvp, kvp, cos2, sin2)


def _band_mask(n):
    row = lax.broadcasted_iota(jnp.int32, (BAND, 2 * BAND), 0)
    col = lax.broadcasted_iota(jnp.int32, (BAND, 2 * BAND), 1)
    return (col >= row) & (col <= row + BAND) & ((n > 0) | (col >= BAND))


def _attn_fwd(proj, kr, vb, cos2, sin2, group, dil, name):
    S, PW = proj.shape
    E = kr.shape[1]
    H = E // HEAD_DIM
    M = S // dil
    nblk = M // BAND
    qcols = PW // E

    def body(q_ref, kc_ref, kp_ref, vc_ref, vp_ref, c_ref, s_ref, o_ref, l_ref):
        n = pl.program_id(1)
        band = _band_mask(n)
        cosv, sinv = c_ref[...], s_ref[...]
        lane = lax.broadcasted_iota(jnp.int32, (BAND, HEAD_DIM), 1)
        lse_tile = jnp.zeros((BAND, HEAD_DIM), F32)
        for h in range(H):
            hs = slice(h * HEAD_DIM, (h + 1) * HEAD_DIM)
            qr = _rope(q_ref[:, hs], cosv, sinv).astype(BF16)
            kcat = jnp.concatenate([kp_ref[:, hs], kc_ref[:, hs]], axis=0)
            vcat = jnp.concatenate([vp_ref[:, hs], vc_ref[:, hs]], axis=0)
            s = lax.dot_general(qr, kcat, NT, preferred_element_type=F32) * ATTN_SCALE
            s = jnp.where(band, s, NEG_INF)
            m = jnp.max(s, axis=-1, keepdims=True)
            p = jnp.exp(s - m)
            l = jnp.sum(p, axis=-1, keepdims=True)
            o_ref[:, hs] = jnp.dot(p.astype(BF16), vcat, preferred_element_type=F32) / l
            lse_tile = jnp.where(lane == h, m + jnp.log(l), lse_tile)
        l_ref[...] = lse_tile

    cur = lambda r, n: (n, r)
    prev = lambda r, n: (jnp.maximum(n - 1, 0), r)
    out, lse = pl.pallas_call(
        body, name=name, grid=(dil, nblk),
        in_specs=[pl.BlockSpec((BAND, E), lambda r, n: (n, r * qcols + group)),
                  pl.BlockSpec((BAND, E), cur), pl.BlockSpec((BAND, E), prev),
                  pl.BlockSpec((BAND, E), cur), pl.BlockSpec((BAND, E), prev),
                  pl.BlockSpec((BAND, HEAD_DIM), cur), pl.BlockSpec((BAND, HEAD_DIM), cur)],
        out_specs=[pl.BlockSpec((BAND, E), cur), pl.BlockSpec((BAND, HEAD_DIM), cur)],
        out_shape=[jax.ShapeDtypeStruct((M, dil * E), F32), jax.ShapeDtypeStruct((M, dil * HEAD_DIM), F32)],
        compiler_params=_params("parallel", "arbitrary"))(
            proj.reshape(M, dil * PW), kr.reshape(M, dil * E), kr.reshape(M, dil * E),
            vb.reshape(M, dil * E), vb.reshape(M, dil * E),
            cos2.reshape(M, dil * HEAD_DIM), sin2.reshape(M, dil * HEAD_DIM))
    return out.reshape(S, E), lse.reshape(S, HEAD_DIM)


def _attn_bwd(proj, kr, vb, cos2, sin2, do, lse, dlt, group, dil, name):
    S, PW = proj.shape
    E = kr.shape[1]
    H = E // HEAD_DIM
    M = S // dil
    nblk = M // BAND
    qcols = PW // E

    def body(q_ref, kc_ref, kp_ref, vc_ref, vp_ref, c_ref, s_ref, do_ref, l_ref, dl_ref,
             dq_ref, dk_ref, dv_ref, ck_ref, cv_ref):
        n = pl.program_id(1)

        @pl.when(n == 0)
        def _():
            ck_ref[...] = jnp.zeros_like(ck_ref)
            cv_ref[...] = jnp.zeros_like(cv_ref)

        @pl.when(n < nblk)
        def _():
            band = _band_mask(n)
            cosv, sinv = c_ref[...], s_ref[...]
            for h in range(H):
                hs = slice(h * HEAD_DIM, (h + 1) * HEAD_DIM)
                qr = _rope(q_ref[:, hs], cosv, sinv).astype(BF16)
                kcat = jnp.concatenate([kp_ref[:, hs], kc_ref[:, hs]], axis=0)
                vcat = jnp.concatenate([vp_ref[:, hs], vc_ref[:, hs]], axis=0)
                s = lax.dot_general(qr, kcat, NT, preferred_element_type=F32) * ATTN_SCALE
                s = jnp.where(band, s, NEG_INF)
                p = jnp.exp(s - l_ref[:, h:h + 1])
                dob = do_ref[:, hs]
                dpr = lax.dot_general(dob, vcat, NT, preferred_element_type=F32)
                ds = (p * (dpr - dl_ref[:, h:h + 1]) * ATTN_SCALE).astype(BF16)
                dq = jnp.dot(ds, kcat, preferred_element_type=F32)
                dq_ref[:, hs] = _rope_bwd(dq, cosv, sinv).astype(BF16)
                dkc = lax.dot_general(ds, qr, TN, preferred_element_type=F32)
                dvc = lax.dot_general(p.astype(BF16), dob, TN, preferred_element_type=F32)
                dk_ref[:, hs] = ck_ref[:, hs] + dkc[0:BAND]
                ck_ref[:, hs] = dkc[BAND:2 * BAND]
                dv_ref[:, hs] = cv_ref[:, hs] + dvc[0:BAND]
                cv_ref[:, hs] = dvc[BAND:2 * BAND]

        @pl.when(n == nblk)
        def _():
            dk_ref[...] = ck_ref[...]
            dv_ref[...] = cv_ref[...]

    last = nblk - 1
    cur = lambda r, n: (jnp.minimum(n, last), r)
    prev = lambda r, n: (jnp.maximum(jnp.minimum(n, last) - 1, 0), r)
    late = lambda r, n: (jnp.maximum(n - 1, 0), r)
    dq, dk, dv = pl.pallas_call(
        body, name=name, grid=(dil, nblk + 1),
        in_specs=[pl.BlockSpec((BAND, E), lambda r, n: (jnp.minimum(n, last), r * qcols + group)),
                  pl.BlockSpec((BAND, E), cur), pl.BlockSpec((BAND, E), prev),
                  pl.BlockSpec((BAND, E), cur), pl.BlockSpec((BAND, E), prev),
                  pl.BlockSpec((BAND, HEAD_DIM), cur), pl.BlockSpec((BAND, HEAD_DIM), cur),
                  pl.BlockSpec((BAND, E), cur),
                  pl.BlockSpec((BAND, HEAD_DIM), cur), pl.BlockSpec((BAND, HEAD_DIM), cur)],
        out_specs=[pl.BlockSpec((BAND, E), cur), pl.BlockSpec((BAND, E), late), pl.BlockSpec((BAND, E), late)],
        out_shape=[jax.ShapeDtypeStruct((M, dil * E), BF16), jax.ShapeDtypeStruct((M, dil * E), F32),
                   jax.ShapeDtypeStruct((M, dil * E), F32)],
        scratch_shapes=[pltpu.VMEM((BAND, E), F32), pltpu.VMEM((BAND, E), F32)],
        compiler_params=_params("parallel", "arbitrary"))(
            proj.reshape(M, dil * PW), kr.reshape(M, dil * E), kr.reshape(M, dil * E),
            vb.reshape(M, dil * E), vb.reshape(M, dil * E),
            cos2.reshape(M, dil * HEAD_DIM), sin2.reshape(M, dil * HEAD_DIM),
            do.reshape(M, dil * E), lse.reshape(M, dil * HEAD_DIM), dlt.reshape(M, dil * HEAD_DIM))
    return dq.reshape(S, E), dk.reshape(S, E), dv.reshape(S, E)


def _group_weights(l_refs, h):
    ls = [r[:, h:h + 1] for r in l_refs]
    mx = jnp.maximum(jnp.maximum(ls[0], ls[1]), ls[2])
    es = [jnp.exp(l - mx) for l in ls]
    inv = 1.0 / (es[0] + es[1] + es[2])
    return [e * inv for e in es]


def _merge_fwd(outs, lses, proj, name):
    S, E = outs[0].shape
    tm = min(256, S)
    gate_col = proj.shape[1] // E - 1

    def body(o0, o1, o2, l0, l1, l2, gt_ref, z_ref):
        for h in range(E // HEAD_DIM):
            hs = slice(h * HEAD_DIM, (h + 1) * HEAD_DIM)
            a = _group_weights((l0, l1, l2), h)
            merged = a[0] * o0[:, hs] + a[1] * o1[:, hs] + a[2] * o2[:, hs]
            gate = gt_ref[:, hs]
            z_ref[:, hs] = (merged * (gate * _sigmoid(gate))).astype(BF16)

    wide = pl.BlockSpec((tm, E), lambda i: (i, 0))
    thin = pl.BlockSpec((tm, HEAD_DIM), lambda i: (i, 0))
    return pl.pallas_call(
        body, name=name, grid=(S // tm,),
        in_specs=[wide, wide, wide, thin, thin, thin, pl.BlockSpec((tm, E), lambda i: (i, gate_col))],
        out_specs=wide,
        out_shape=jax.ShapeDtypeStruct((S, E), BF16),
        compiler_params=_params("parallel"))(*outs, *lses, proj)


def _merge_bwd(dz, outs, lses, proj, name):
    S, E = outs[0].shape
    tm = min(256, S)
    gate_col = proj.shape[1] // E - 1

    def body(dz_ref, o0, o1, o2, l0, l1, l2, gt_ref, d0, d1, d2, t0, t1, t2, dg_ref):
        o_refs, d_refs, t_refs = (o0, o1, o2), (d0, d1, d2), (t0, t1, t2)
        lane = lax.broadcasted_iota(jnp.int32, (tm, HEAD_DIM), 1)
        tiles = [jnp.zeros((tm, HEAD_DIM), F32) for _ in range(3)]
        for h in range(E // HEAD_DIM):
            hs = slice(h * HEAD_DIM, (h + 1) * HEAD_DIM)
            a = _group_weights((l0, l1, l2), h)
            merged = a[0] * o0[:, hs] + a[1] * o1[:, hs] + a[2] * o2[:, hs]
            gate = gt_ref[:, hs]
            sg = _sigmoid(gate)
            dzh = dz_ref[:, hs]
            dmerged = dzh * (gate * sg)
            dg_ref[:, hs] = (dzh * merged * (sg * (1.0 + gate * (1.0 - sg)))).astype(BF16)
            tot = jnp.sum(dmerged * merged, axis=-1, keepdims=True)
            for g in range(3):
                d_refs[g][:, hs] = (a[g] * dmerged).astype(BF16)
                tiles[g] = jnp.where(lane == h, a[g] * tot, tiles[g])
        for g in range(3):
            t_refs[g][...] = tiles[g]

    wide = pl.BlockSpec((tm, E), lambda i: (i, 0))
    thin = pl.BlockSpec((tm, HEAD_DIM), lambda i: (i, 0))
    res = pl.pallas_call(
        body, name=name, grid=(S // tm,),
        in_specs=[wide, wide, wide, wide, thin, thin, thin, pl.BlockSpec((tm, E), lambda i: (i, gate_col))],
        out_specs=[wide, wide, wide, thin, thin, thin, wide],
        out_shape=[jax.ShapeDtypeStruct((S, E), BF16)] * 3 + [jax.ShapeDtypeStruct((S, HEAD_DIM), F32)] * 3
        + [jax.ShapeDtypeStruct((S, E), BF16)],
        compiler_params=_params("parallel"))(dz, *outs, *lses, proj)
    return res[0:3], res[3:6], res[6]


def _kv_bwd(dks, dvs, cos2, sin2, name):
    S, E = dks[0].shape
    n = len(dks)
    tm = min(128, S)

    def body(*refs):
        dk_refs, dv_refs = refs[0:n], refs[n:2 * n]
        c_ref, s_ref, o_ref = refs[2 * n:]
        cosv, sinv = c_ref[...], s_ref[...]
        for h in range(E // HEAD_DIM):
            hs = slice(h * HEAD_DIM, (h + 1) * HEAD_DIM)
            dk = dk_refs[0][:, hs]
            dv = dv_refs[0][:, hs]
            for j in range(1, n):
                dk = dk + dk_refs[j][:, hs]
                dv = dv + dv_refs[j][:, hs]
            o_ref[:, hs] = _rope_bwd(dk, cosv, sinv).astype(BF16)
            o_ref[:, E + h * HEAD_DIM:E + (h + 1) * HEAD_DIM] = dv.astype(BF16)

    wide = pl.BlockSpec((tm, E), lambda i: (i, 0))
    thin = pl.BlockSpec((tm, HEAD_DIM), lambda i: (i, 0))
    return pl.pallas_call(
        body, name=name, grid=(S // tm,),
        in_specs=[wide] * (2 * n) + [thin, thin],
        out_specs=pl.BlockSpec((tm, 2 * E), lambda i: (i, 0)),
        out_shape=jax.ShapeDtypeStruct((S, 2 * E), BF16),
        compiler_params=_params("parallel"))(*dks, *dvs, cos2, sin2)


def _norm_bwd(dhn, x, gain, dres, name):
    S, D = x.shape
    tm = min(256, S)

    def body(dh_ref, x_ref, g_ref, r_ref, dx_ref, dg_ref):
        @pl.when(pl.program_id(0) == 0)
        def _():
            dg_ref[...] = jnp.zeros_like(dg_ref)
        xf = x_ref[...]
        inv = lax.rsqrt(jnp.mean(xf * xf, axis=-1, keepdims=True) + RMS_EPS)
        xhat = xf * inv
        dh = dh_ref[...]
        dg_ref[...] += jnp.sum(dh * xhat, axis=0, keepdims=True)
        dxh = dh * g_ref[...]
        dx_ref[...] = r_ref[...] + inv * (dxh - xhat * jnp.mean(dxh * xhat, axis=-1, keepdims=True))

    tile = pl.BlockSpec((tm, D), lambda i: (i, 0))
    vec = pl.BlockSpec((1, D), lambda i: (0, 0))
    return pl.pallas_call(
        body, name=name, grid=(S // tm,),
        in_specs=[tile, tile, vec, tile],
        out_specs=[tile, vec],
        out_shape=[jax.ShapeDtypeStruct((S, D), F32), jax.ShapeDtypeStruct((1, D), F32)],
        compiler_params=_params("arbitrary"))(dhn, x, gain, dres)


def _final_norm_loss(x, target, gain, name):
    S, D = x.shape
    tm = min(256, S)

    def body(x_ref, t_ref, g_ref, loss_ref, dx_ref, dg_ref):
        @pl.when(pl.program_id(0) == 0)
        def _():
            loss_ref[...] = jnp.zeros_like(loss_ref)
            dg_ref[...] = jnp.zeros_like(dg_ref)
        xf = x_ref[...]
        inv = lax.rsqrt(jnp.mean(xf * xf, axis=-1, keepdims=True) + RMS_EPS)
        xhat = xf * inv
        g = g_ref[...]
        err = xhat * g - t_ref[...]
        loss_ref[...] += 0.5 * jnp.sum(jnp.mean(err * err, axis=-1, keepdims=True), axis=0, keepdims=True)
        dy = err / D
        dg_ref[...] += jnp.sum(dy * xhat, axis=0, keepdims=True)
        dxh = dy * g
        dx_ref[...] = inv * (dxh - xhat * jnp.mean(dxh * xhat, axis=-1, keepdims=True))

    tile = pl.BlockSpec((tm, D), lambda i: (i, 0))
    vec = pl.BlockSpec((1, D), lambda i: (0, 0))
    return pl.pallas_call(
        body, name=name, grid=(S // tm,),
        in_specs=[tile, tile, vec],
        out_specs=[pl.BlockSpec((1, 1), lambda i: (0, 0)), tile, vec],
        out_shape=[jax.ShapeDtypeStruct((1, 1), F32), jax.ShapeDtypeStruct((S, D), F32),
                   jax.ShapeDtypeStruct((1, D), F32)],
        compiler_params=_params("arbitrary"))(x, target, gain)


def _adamw_math(g, w, m, v):
    m = ADAM_B1 * m + (1.0 - ADAM_B1) * g
    v = ADAM_B2 * v + (1.0 - ADAM_B2) * (g * g)
    m_hat = m / (1.0 - ADAM_B1 ** ADAM_STEP)
    v_hat = v / (1.0 - ADAM_B2 ** ADAM_STEP)
    delta = -ADAM_LR * (m_hat / (jnp.sqrt(v_hat) + ADAM_EPS) + ADAM_WD * w)
    return delta, m, v


def _adamw(parts, w, m, v, name):
    n, R, C = parts.shape
    tr = R
    while tr * C > 128 * 1024 and tr % 16 == 0:
        tr //= 2

    def body(p_ref, w_ref, m_ref, v_ref, g_ref, d_ref, mo_ref, vo_ref):
        g = p_ref[0].astype(F32)
        for j in range(1, n):
            g = g + p_ref[j].astype(F32)
        delta, mn, vn = _adamw_math(g, w_ref[...], m_ref[...], v_ref[...])
        g_ref[...] = g
        d_ref[...] = delta
        mo_ref[...] = mn
        vo_ref[...] = vn

    tile = pl.BlockSpec((tr, C), lambda i: (i, 0))
    return pl.pallas_call(
        body, name=name, grid=(R // tr,),
        in_specs=[pl.BlockSpec((n, tr, C), lambda i: (0, i, 0)), tile, tile, tile],
        out_specs=[tile] * 4,
        out_shape=[jax.ShapeDtypeStruct((R, C), F32)] * 4,
        compiler_params=_params("parallel"))(parts, w, m, v)


def _position():
    return lax.axis_index("x"), lax.axis_index("y"), lax.axis_index("c")


def _block_index(px, py, pc):
    return 4 * px + 2 * py + pc


def _all_gather(shards, name):
    n = len(shards)

    def body(*refs):
        ins, outs = refs[0:n], refs[n:2 * n]
        send_sems, recv_sems, local_sems = refs[2 * n:]
        x, y, c = _position()
        me, sibling = (x, y, c), (x, y, 1 - c)
        chips = [(1 - x, y), (x, 1 - y), (1 - x, 1 - y)]

        def copy(a, k, block, to, src=None):
            rows = outs[a].at[_block_index(*block)]
            return pltpu.make_async_remote_copy(
                src_ref=rows if src is None else src, dst_ref=rows,
                send_sem=send_sems.at[a, k], recv_sem=recv_sems.at[a, k], device_id=to, device_id_type=MESH)

        mine, first, passed = [], [], []
        for a in range(n):
            cp = pltpu.make_async_copy(ins[a], outs[a].at[_block_index(*me)], local_sems.at[a])
            cp.start()
            mine.append(cp)
            first.append(copy(a, 0, me, sibling, src=ins[a]))
            first += [copy(a, 1 + j, me, (*chip, c), src=ins[a]) for j, chip in enumerate(chips)]
        for cp in first:
            cp.start()
        for j, chip in enumerate(chips):
            for a in range(n):
                copy(a, 1 + j, (*chip, c), me).wait_recv()
                fwd = copy(a, 4 + j, (*chip, c), sibling)
                fwd.start()
                passed.append(fwd)
        for a in range(n):
            copy(a, 0, sibling, me).wait_recv()
            for j, chip in enumerate(chips):
                copy(a, 4 + j, (*chip, 1 - c), me).wait_recv()
        for cp in first + passed:
            cp.wait_send()
        for cp in mine:
            cp.wait()

    return pl.pallas_call(
        body, name=name,
        in_specs=[ANY] * n, out_specs=[ANY] * n,
        out_shape=[jax.ShapeDtypeStruct((N_DEV,) + s.shape, s.dtype) for s in shards],
        scratch_shapes=[pltpu.SemaphoreType.DMA((n, 7)), pltpu.SemaphoreType.DMA((n, 7)),
                        pltpu.SemaphoreType.DMA((n,))],
    )(*shards)


def _all_to_all(blocks, slots, out_shapes, name):
    n = len(blocks)
    n_out = len(out_shapes)

    def body(*refs):
        ins, outs = refs[0:n], refs[n:n + n_out]
        send_sems, recv_sems, local_sems = refs[n + n_out:]
        x, y, c = _position()
        me_idx = _block_index(x, y, c)

        def landing(a, sender_idx):
            o, layer = slots[a]
            return outs[o].at[sender_idx] if layer is None else outs[o].at[sender_idx, layer]

        copies, mine = [], []
        for a in range(n):
            cp = pltpu.make_async_copy(ins[a].at[me_idx], landing(a, me_idx), local_sems.at[a])
            cp.start()
            mine.append(cp)
            for k in range(1, N_DEV):
                px = (1 - x) if k & 4 else x
                py = (1 - y) if k & 2 else y
                pc = (1 - c) if k & 1 else c
                p_idx = _block_index(px, py, pc)
                rc = pltpu.make_async_remote_copy(
                    src_ref=ins[a].at[p_idx], dst_ref=landing(a, me_idx),
                    send_sem=send_sems.at[a, k - 1], recv_sem=recv_sems.at[a, k - 1],
                    device_id=(px, py, pc), device_id_type=MESH)
                rc.start()
                copies.append((rc, pltpu.make_async_remote_copy(
                    src_ref=ins[a].at[p_idx], dst_ref=landing(a, p_idx),
                    send_sem=send_sems.at[a, k - 1], recv_sem=recv_sems.at[a, k - 1],
                    device_id=(px, py, pc), device_id_type=MESH)))
        for rc, arrival in copies:
            rc.wait_send()
            arrival.wait_recv()
        for cp in mine:
            cp.wait()

    return pl.pallas_call(
        body, name=name,
        in_specs=[ANY] * n, out_specs=[ANY] * n_out,
        out_shape=[jax.ShapeDtypeStruct(s, BF16) for s in out_shapes],
        scratch_shapes=[pltpu.SemaphoreType.DMA((n, 7)), pltpu.SemaphoreType.DMA((n, 7)),
                        pltpu.SemaphoreType.DMA((n,))],
    )(*blocks)


def _all_reduce_rows(v, name):
    R, D = v.shape

    def body(v_ref, o_ref, buf_ref, send_sems, recv_sems):
        x, y, c = _position()
        me_idx = _block_index(x, y, c)
        buf_ref[me_idx] = v_ref[...]
        copies = []
        for k in range(1, N_DEV):
            px = (1 - x) if k & 4 else x
            py = (1 - y) if k & 2 else y
            pc = (1 - c) if k & 1 else c
            rc = pltpu.make_async_remote_copy(
                src_ref=v_ref, dst_ref=buf_ref.at[me_idx],
                send_sem=send_sems.at[k - 1], recv_sem=recv_sems.at[k - 1],
                device_id=(px, py, pc), device_id_type=MESH)
            rc.start()
            copies.append((rc, pltpu.make_async_remote_copy(
                src_ref=v_ref, dst_ref=buf_ref.at[_block_index(px, py, pc)],
                send_sem=send_sems.at[k - 1], recv_sem=recv_sems.at[k - 1],
                device_id=(px, py, pc), device_id_type=MESH)))
        for rc, arrival in copies:
            rc.wait_send()
            arrival.wait_recv()
        acc = buf_ref[0]
        for j in range(1, N_DEV):
            acc = acc + buf_ref[j]
        o_ref[...] = acc

    return pl.pallas_call(
        body, name=name,
        in_specs=[pl.BlockSpec(memory_space=pltpu.VMEM)],
        out_specs=pl.BlockSpec(memory_space=pltpu.VMEM),
        out_shape=jax.ShapeDtypeStruct((R, D), F32),
        scratch_shapes=[pltpu.VMEM((N_DEV, R, D), F32),
                        pltpu.SemaphoreType.DMA((7,)), pltpu.SemaphoreType.DMA((7,))],
    )(v)


def _rope_tables(S):
    inv_freq = 1.0 / (ROPE_THETA ** (jnp.arange(0, HEAD_DIM, 2, dtype=F32) / HEAD_DIM))
    ang = jnp.arange(S, dtype=F32)[:, None] * inv_freq[None, :]
    cos, sin = jnp.cos(ang), jnp.sin(ang)
    return jnp.concatenate([cos, cos], axis=1), jnp.concatenate([-sin, sin], axis=1)


def _local_step(xs, target, wts):
    S, D = xs.shape
    E = D
    n_a, n_b = wts["w_in_a"].shape[1], wts["w_in_b"].shape[1]
    cos2, sin2 = _rope_tables(S)
    ts = min(512, S)

    def col_blocks(w, layer):
        cb = w.shape[3]
        tn = min(cb, 1024)
        per = cb // tn
        return (None, None, D, tn), (lambda i, j: (j // per, layer, 0, j % per)), N_DEV * per, tn

    def grad_in(hn, dproj, cb, name):
        return _matmul_tn(hn, dproj, (ts, D), lambda j, s: (s, 0), (ts, cb), lambda j, s: (s, j),
                          (N_DEV, D, cb), (None, D, cb), lambda j, s: (j, 0, 0), (D, cb), N_DEV, name)

    def grad_out(z, dx, name, col=0):
        ta = min(1024, E)
        per = E // ta
        out = _matmul_tn(z, dx, (ts, ta), lambda a, s: (s, col * per + a), (ts, dx.shape[1]), lambda a, s: (s, 0),
                         (E, dx.shape[1]), (ta, dx.shape[1]), lambda a, s: (a, 0), (ta, dx.shape[1]), per, name)
        return out.reshape(N_DEV, E // N_DEV, dx.shape[1])

    x = xs
    a_saved, b_saved = [], []
    for i in range(n_a):
        blk, idx, nblocks, tn = col_blocks(wts["w_in_a"], i)
        proj, hn = _norm_matmul(x, wts["norm_a"][i:i + 1], wts["w_in_a"], blk, idx, nblocks, tn, f"a{i}_in")
        z = _a_mid_fwd(proj, wts["w_grp_a"][i], wts["scale_a"][i:i + 1], f"a{i}_mid")
        x_next = _matmul_res(z, wts["w_out_a"], i, x, f"a{i}_out")
        a_saved.append((x, hn, proj, z))
        x = x_next
    x_kv = x
    kvp, hn_kv = _norm_matmul(x, wts["norm_kv"], wts["w_kv"], (D, min(E, 1024)), lambda i, j: (0, j),
                              2 * E // min(E, 1024), min(E, 1024), "kv_in")
    kr, vb = _rope_kv(kvp, cos2, sin2, "kv_rope")
    for i in range(n_b):
        blk, idx, nblocks, tn = col_blocks(wts["w_in_b"], i)
        proj, hn = _norm_matmul(x, wts["norm_b"][i:i + 1], wts["w_in_b"], blk, idx, nblocks, tn, f"b{i}_in")
        outs, lses = [], []
        for g, dil in enumerate(DILATIONS):
            o, l = _attn_fwd(proj, kr, vb, cos2, sin2, g, dil, f"b{i}_attn{g}")
            outs.append(o)
            lses.append(l)
        z = _merge_fwd(outs, lses, proj, f"b{i}_merge")
        x_next = _matmul_res(z, wts["w_out_b"], i, x, f"b{i}_out")
        b_saved.append((x, hn, proj, z, outs, lses))
        x = x_next
    loss, dx, dg_f = _final_norm_loss(x, target, wts["norm_f"], "final")

    grads = {k: [None] * n_a for k in ("w_in_a", "w_grp_a", "w_out_a")}
    grads.update({k: [None] * n_b for k in ("w_in_b", "w_out_b")})
    vec = {"norm_a": [None] * n_a, "scale_a": [None] * n_a, "norm_b": [None] * n_b, "norm_f": [dg_f]}
    dks, dvs = [], []
    for i in reversed(range(n_b)):
        x_in, hn, proj, z, outs, lses = b_saved[i]
        grads["w_out_b"][i] = grad_out(z, dx, f"b{i}_dwout")
        dz = _matmul_nt_rows(dx, wts["w_out_b"], i, f"b{i}_dz")
        dos, dlts, dgate = _merge_bwd(dz, outs, lses, proj, f"b{i}_dmerge")
        dqs = []
        for g, dil in enumerate(DILATIONS):
            dq, dk, dv = _attn_bwd(proj, kr, vb, cos2, sin2, dos[g], lses[g], dlts[g], g, dil, f"b{i}_dattn{g}")
            dqs.append(dq)
            dks.append(dk)
            dvs.append(dv)
        dproj = jnp.concatenate(dqs + [dgate], axis=1)
        cb = wts["w_in_b"].shape[3]
        grads["w_in_b"][i] = grad_in(hn, dproj, cb, f"b{i}_dwin")
        dhn = _matmul_nt_cols(dproj, wts["w_in_b"], (None, None, D, cb), lambda t, j: (j, i, 0, 0), N_DEV, cb, D,
                              f"b{i}_dhn")
        dx, vec["norm_b"][i] = _norm_bwd(dhn, x_in, wts["norm_b"][i:i + 1], dx, f"b{i}_dnorm")

    dkv = _kv_bwd(dks, dvs, cos2, sin2, "kv_dsum")
    ta = min(1024, D)
    per = D // ta
    dwk, dwv = [
        _matmul_tn(hn_kv, dkv, (ts, ta), lambda a, s: (s, a), (ts, E), (lambda a, s, c=c: (s, c)),
                   (D, E), (ta, E), lambda a, s: (a, 0), (ta, E), per, nm).reshape(N_DEV, D // N_DEV, E)
        for c, nm in ((0, "kv_dwk"), (1, "kv_dwv"))]
    grads["w_k"], grads["w_v"] = [dwk], [dwv]
    dhn = _matmul_nt_cols(dkv, wts["w_kv"], (D, E), lambda t, j: (0, j), 2, E, D, "kv_dhn")
    dx, dg_kv = _norm_bwd(dhn, x_kv, wts["norm_kv"], dx, "kv_dnorm")
    vec["norm_kv"] = [dg_kv]

    for i in reversed(range(n_a)):
        x_in, hn, proj, z = a_saved[i]
        grads["w_out_a"][i] = grad_out(z, dx, f"a{i}_dwout")
        dz = _matmul_nt_rows(dx, wts["w_out_a"], i, f"a{i}_dz")
        dproj, dwg, dsc = _a_mid_bwd(dz, proj, wts["w_grp_a"][i], wts["scale_a"][i:i + 1], f"a{i}_dmid")
        n_grp, gc, _ = dwg.shape
        grads["w_grp_a"][i] = dwg.reshape(n_grp, N_DEV, gc // N_DEV, gc).transpose(1, 0, 2, 3).astype(BF16)
        vec["scale_a"][i] = dsc
        cb = wts["w_in_a"].shape[3]
        grads["w_in_a"][i] = grad_in(hn, dproj, cb, f"a{i}_dwin")
        dhn = _matmul_nt_cols(dproj, wts["w_in_a"], (None, None, D, cb), lambda t, j: (j, i, 0, 0), N_DEV, cb, D,
                              f"a{i}_dhn")
        dx, vec["norm_a"][i] = _norm_bwd(dhn, x_in, wts["norm_a"][i:i + 1], dx, f"a{i}_dnorm")

    return loss, dx, grads, {k: jnp.concatenate(v, axis=0) for k, v in vec.items()}


MATRICES = ("w_in_a", "w_grp_a", "w_out_a", "w_k", "w_v", "w_in_b", "w_out_b")
VECTORS = ("norm_a", "scale_a", "norm_kv", "norm_b", "norm_f")
SHARDED_VECTORS = ("norm_a", "scale_a")


def kernel(x, norm_a, w_in_a, w_grp_a, scale_a, w_out_a, norm_kv, w_k, w_v, norm_b, w_in_b, w_out_b, norm_f, loss_target, m_norm_a, m_w_in_a, m_w_grp_a, m_scale_a, m_w_out_a, m_norm_kv, m_w_k, m_w_v, m_norm_b, m_w_in_b, m_w_out_b, m_norm_f, v_norm_a, v_w_in_a, v_w_grp_a, v_scale_a, v_w_out_a, v_norm_kv, v_w_k, v_w_v, v_norm_b, v_w_in_b, v_w_out_b, v_norm_f):
    w = dict(norm_a=norm_a, w_in_a=w_in_a, w_grp_a=w_grp_a, scale_a=scale_a, w_out_a=w_out_a, norm_kv=norm_kv,
             w_k=w_k, w_v=w_v, norm_b=norm_b, w_in_b=w_in_b, w_out_b=w_out_b, norm_f=norm_f)
    m = dict(norm_a=m_norm_a, w_in_a=m_w_in_a, w_grp_a=m_w_grp_a, scale_a=m_scale_a, w_out_a=m_w_out_a,
             norm_kv=m_norm_kv, w_k=m_w_k, w_v=m_w_v, norm_b=m_norm_b, w_in_b=m_w_in_b, w_out_b=m_w_out_b,
             norm_f=m_norm_f)
    v = dict(norm_a=v_norm_a, w_in_a=v_w_in_a, w_grp_a=v_w_grp_a, scale_a=v_scale_a, w_out_a=v_w_out_a,
             norm_kv=v_norm_kv, w_k=v_w_k, w_v=v_w_v, norm_b=v_norm_b, w_in_b=v_w_in_b, w_out_b=v_w_out_b,
             norm_f=v_norm_f)
    S, D = x.shape[1], x.shape[2]
    me = _block_index(*_position())

    gathered = _all_gather([w[k].astype(BF16) for k in MATRICES] + [w[k] for k in SHARDED_VECTORS], "gather_weights")
    wts = dict(zip(MATRICES + SHARDED_VECTORS, gathered))
    wg = wts["w_grp_a"]
    wts["w_grp_a"] = wg.transpose(1, 2, 0, 3, 4).reshape(wg.shape[1], wg.shape[2], wg.shape[4], wg.shape[4])
    wts["w_kv"] = jnp.concatenate([wts["w_k"].reshape(D, D), wts["w_v"].reshape(D, D)], axis=1)
    for k in SHARDED_VECTORS:
        wts[k] = wts[k].transpose(1, 0, 2).reshape(w[k].shape[0], D)
    wts["norm_kv"], wts["norm_b"], wts["norm_f"] = norm_kv[None, :], norm_b, norm_f[None, :]

    loss, dx, grads, vec = _local_step(x[0], loss_target[0], wts)

    blocks, slots, out_shapes = [], [], []
    for o, k in enumerate(MATRICES):
        layered = w[k].ndim > 2 and k not in ("w_k", "w_v")
        for layer, blk in enumerate(grads[k]):
            blocks.append(blk)
            slots.append((o, layer if layered else None))
        out_shapes.append((N_DEV,) + w[k].shape)
    received = dict(zip(MATRICES, _all_to_all(blocks, slots, out_shapes, "exchange_grads")))
    rows = jnp.concatenate([vec[k] for k in VECTORS], axis=0)
    rows = _all_reduce_rows(rows, "reduce_vectors")

    out = {}
    for k in MATRICES:
        cols = w[k].shape[-1]
        flat = lambda t: t.reshape(-1, cols)
        res = _adamw(received[k].reshape(N_DEV, -1, cols), flat(w[k]), flat(m[k]), flat(v[k]), f"adamw_{k}")
        out[k] = [r.reshape(w[k].shape) for r in res]
    start = 0
    for k in VECTORS:
        n_rows = vec[k].shape[0]
        g = rows[start:start + n_rows]
        start += n_rows
        if k in SHARDED_VECTORS:
            g = lax.dynamic_slice_in_dim(g, me * (D // N_DEV), D // N_DEV, axis=1)
        shape2 = g.shape
        res = _adamw(g[None], w[k].reshape(shape2), m[k].reshape(shape2), v[k].reshape(shape2), f"adamw_{k}")
        out[k] = [r.reshape(w[k].shape) for r in res]

    names = ("norm_a", "w_in_a", "w_grp_a", "scale_a", "w_out_a", "norm_kv", "w_k", "w_v", "norm_b", "w_in_b",
             "w_out_b", "norm_f")
    total = lax.psum(loss[0, 0], ("x", "y", "c"))
    return (total, dx[None], *[out[k][0] for k in names], *[out[k][1] for k in names],
            *[out[k][2] for k in names], *[out[k][3] for k in names])
```

```python
import math

import jax
import jax.numpy as jnp
from jax import lax
from jax.experimental import pallas as pl
from jax.experimental.pallas import tpu as pltpu

F32 = jnp.float32
BF16 = jnp.bfloat16

N_DEV = 8
MESH = pl.DeviceIdType.MESH
RMS_EPS = 1e-6
HEAD_DIM = 128
HALF_HEAD = HEAD_DIM // 2
BAND = 128
DILATIONS = (1, 4, 16)
POOL_WINDOWS = (2, 4, 8, 16)
POOL_HALO = 16
ROPE_THETA = 10000.0
NEG_INF = -1e30
ATTN_SCALE = 1.0 / math.sqrt(HEAD_DIM)
ADAM_LR, ADAM_B1, ADAM_B2, ADAM_EPS, ADAM_WD, ADAM_STEP = 0.001, 0.9, 0.999, 1e-08, 0.01, 10
VMEM_LIMIT_BYTES = 56 * 1024 * 1024
ANY = pl.BlockSpec(memory_space=pl.ANY)
NT = (((1,), (1,)), ((), ()))
TN = (((0,), (0,)), ((), ()))


def _params(*semantics):
    return pltpu.CompilerParams(dimension_semantics=semantics, vmem_limit_bytes=VMEM_LIMIT_BYTES)


def _sigmoid(t):
    return 1.0 / (1.0 + jnp.exp(-t))


def _rope(t, cos2, sin2):
    return t * cos2 + pltpu.roll(t, HALF_HEAD, 1) * sin2


def _rope_bwd(dt, cos2, sin2):
    return dt * cos2 + pltpu.roll(dt * sin2, HALF_HEAD, 1)


def _norm_matmul(x, gain, w, w_block, w_index, n_col_blocks, tn, name):
    S, D = x.shape
    tm = min(512, S)

    def body(x_ref, g_ref, w_ref, o_ref, hn_ref, hs_ref):
        @pl.when(pl.program_id(1) == 0)
        def _():
            xf = x_ref[...]
            inv = lax.rsqrt(jnp.mean(xf * xf, axis=-1, keepdims=True) + RMS_EPS)
            hb = ((xf * inv) * g_ref[...]).astype(BF16)
            hs_ref[...] = hb
            hn_ref[...] = hb
        o_ref[...] = jnp.dot(hs_ref[...], w_ref[...], preferred_element_type=F32)

    return pl.pallas_call(
        body, name=name, grid=(S // tm, n_col_blocks),
        in_specs=[pl.BlockSpec((tm, D), lambda i, j: (i, 0)),
                  pl.BlockSpec((1, D), lambda i, j: (0, 0)),
                  pl.BlockSpec(w_block, w_index)],
        out_specs=[pl.BlockSpec((tm, tn), lambda i, j: (i, j)),
                   pl.BlockSpec((tm, D), lambda i, j: (i, 0))],
        out_shape=[jax.ShapeDtypeStruct((S, n_col_blocks * tn), F32), jax.ShapeDtypeStruct((S, D), BF16)],
        scratch_shapes=[pltpu.VMEM((tm, D), BF16)],
        compiler_params=_params("parallel", "arbitrary"))(x, gain, w)


def _matmul_res(a, w, res, name):
    S, K = a.shape
    nd, rb, N = w.shape
    tm = min(512, S)

    def body(a_ref, w_ref, r_ref, o_ref):
        acc = jnp.dot(a_ref[:, 0:rb], w_ref[0], preferred_element_type=F32)
        for k in range(1, nd):
            acc = acc + jnp.dot(a_ref[:, k * rb:(k + 1) * rb], w_ref[k], preferred_element_type=F32)
        o_ref[...] = r_ref[...] + acc

    return pl.pallas_call(
        body, name=name, grid=(S // tm,),
        in_specs=[pl.BlockSpec((tm, K), lambda i: (i, 0)),
                  pl.BlockSpec((nd, rb, N), lambda i: (0, 0, 0)),
                  pl.BlockSpec((tm, N), lambda i: (i, 0))],
        out_specs=pl.BlockSpec((tm, N), lambda i: (i, 0)),
        out_shape=jax.ShapeDtypeStruct((S, N), F32),
        compiler_params=_params("parallel"))(a, w, res)


def _matmul_nt_rows(dy, w, name):
    S, N = dy.shape
    nd, rb, _ = w.shape
    tm = min(512, S)

    def body(d_ref, w_ref, o_ref):
        db = d_ref[...].astype(BF16)
        for k in range(nd):
            o_ref[:, k * rb:(k + 1) * rb] = lax.dot_general(db, w_ref[k], NT, preferred_element_type=F32)

    return pl.pallas_call(
        body, name=name, grid=(S // tm,),
        in_specs=[pl.BlockSpec((tm, N), lambda i: (i, 0)),
                  pl.BlockSpec((nd, rb, N), lambda i: (0, 0, 0))],
        out_specs=pl.BlockSpec((tm, nd * rb), lambda i: (i, 0)),
        out_shape=jax.ShapeDtypeStruct((S, nd * rb), F32),
        compiler_params=_params("parallel"))(dy, w)


def _matmul_nt_cols(dp, w, w_block, w_index, n_red, tc, n_out, name):
    S = dp.shape[0]
    tm = min(512, S)

    def body(d_ref, w_ref, o_ref):
        @pl.when(pl.program_id(1) == 0)
        def _():
            o_ref[...] = jnp.zeros_like(o_ref)
        o_ref[...] += lax.dot_general(d_ref[...], w_ref[...], NT, preferred_element_type=F32)

    return pl.pallas_call(
        body, name=name, grid=(S // tm, n_red),
        in_specs=[pl.BlockSpec((tm, tc), lambda i, j: (i, j)),
                  pl.BlockSpec(w_block, w_index)],
        out_specs=pl.BlockSpec((tm, n_out), lambda i, j: (i, 0)),
        out_shape=jax.ShapeDtypeStruct((S, n_out), F32),
        compiler_params=_params("parallel", "arbitrary"))(dp, w)


def _matmul_tn(a, b, a_block, a_index, b_block, b_index, out_shape, out_block, out_index, acc_shape, n_outer, name):
    S = a.shape[0]
    ts = a_block[0]
    n_tok = S // ts

    def body(a_ref, b_ref, o_ref, acc_ref):
        s = pl.program_id(1)

        @pl.when(s == 0)
        def _():
            acc_ref[...] = jnp.zeros_like(acc_ref)
        acc_ref[...] += lax.dot_general(a_ref[...].astype(BF16), b_ref[...].astype(BF16), TN,
                                        preferred_element_type=F32)

        @pl.when(s == n_tok - 1)
        def _():
            o_ref[...] = acc_ref[...].astype(o_ref.dtype)

    return pl.pallas_call(
        body, name=name, grid=(n_outer, n_tok),
        in_specs=[pl.BlockSpec(a_block, a_index), pl.BlockSpec(b_block, b_index)],
        out_specs=pl.BlockSpec(out_block, out_index),
        out_shape=jax.ShapeDtypeStruct(out_shape, BF16),
        scratch_shapes=[pltpu.VMEM(acc_shape, F32)],
        compiler_params=_params("parallel", "arbitrary"))(a, b)


def _pool(scr_ref, u, row0, tm, E):
    gc = E // len(POOL_WINDOWS)
    t1 = row0 + lax.broadcasted_iota(jnp.int32, (tm, 1), 0) + 1
    out = []
    for g, win in enumerate(POOL_WINDOWS):
        cs = slice(g * gc, (g + 1) * gc)
        acc = u[:, cs]
        for k in range(1, win):
            acc = acc + scr_ref[pl.ds(POOL_HALO - k, tm), cs]
        count = jnp.minimum(t1, win).astype(F32)
        out.append(acc / count - u[:, cs])
    return out


def _a_mid_fwd(proj, wg, scale, name):
    S, E2 = proj.shape
    E = E2 // 2
    gc = E // len(POOL_WINDOWS)
    tm = min(256, S)
    hb = tm // POOL_HALO

    def body(u_ref, uh_ref, gt_ref, wg_ref, sc_ref, z_ref, scr_ref):
        i = pl.program_id(0)
        scr_ref[0:POOL_HALO, :] = jnp.where(i > 0, uh_ref[...], 0.0)
        u = u_ref[...]
        scr_ref[POOL_HALO:POOL_HALO + tm, :] = u
        pooled = _pool(scr_ref, u, i * tm, tm, E)
        for g in range(len(POOL_WINDOWS)):
            cs = slice(g * gc, (g + 1) * gc)
            y = jnp.dot(pooled[g].astype(BF16), wg_ref[g], preferred_element_type=F32) * sc_ref[:, cs]
            gate = gt_ref[:, cs]
            z_ref[:, cs] = (y * (gate * _sigmoid(gate))).astype(BF16)

    return pl.pallas_call(
        body, name=name, grid=(S // tm,),
        in_specs=[pl.BlockSpec((tm, E), lambda i: (i, 0)),
                  pl.BlockSpec((POOL_HALO, E), lambda i: (jnp.maximum(i * hb - 1, 0), 0)),
                  pl.BlockSpec((tm, E), lambda i: (i, 1)),
                  pl.BlockSpec((len(POOL_WINDOWS), gc, gc), lambda i: (0, 0, 0)),
                  pl.BlockSpec((1, E), lambda i: (0, 0))],
        out_specs=pl.BlockSpec((tm, E), lambda i: (i, 0)),
        out_shape=jax.ShapeDtypeStruct((S, E), BF16),
        scratch_shapes=[pltpu.VMEM((POOL_HALO + tm, E), F32)],
        compiler_params=_params("parallel"))(proj, proj, proj, wg, scale)


def _a_mid_bwd(dz, proj, wg, scale, name):
    S, E2 = proj.shape
    E = E2 // 2
    n_grp = len(POOL_WINDOWS)
    gc = E // n_grp
    tm = min(256, S)
    hb = tm // POOL_HALO
    n_tiles = S // tm
    last_halo = S // POOL_HALO - 1

    def body(dz_ref, dzh_ref, u_ref, uh_ref, gt_ref, gth_ref, wg_ref, sc_ref, dp_ref, dwg_ref, dsc_ref, scr_ref, q_ref):
        i = pl.program_id(0)

        @pl.when(i == 0)
        def _():
            dwg_ref[...] = jnp.zeros_like(dwg_ref)
            dsc_ref[...] = jnp.zeros_like(dsc_ref)

        scr_ref[0:POOL_HALO, :] = jnp.where(i > 0, uh_ref[...], 0.0)
        u = u_ref[...]
        scr_ref[POOL_HALO:POOL_HALO + tm, :] = u
        pooled = _pool(scr_ref, u, i * tm, tm, E)
        t1 = i * tm + lax.broadcasted_iota(jnp.int32, (tm, 1), 0) + 1
        t1h = (i + 1) * tm + lax.broadcasted_iota(jnp.int32, (POOL_HALO, 1), 0) + 1
        not_last = i < n_tiles - 1
        for g, win in enumerate(POOL_WINDOWS):
            cs = slice(g * gc, (g + 1) * gc)
            w = wg_ref[g]
            sc = sc_ref[:, cs]
            pb = pooled[g].astype(BF16)
            ypre = jnp.dot(pb, w, preferred_element_type=F32)
            gate = gt_ref[:, cs]
            sg = _sigmoid(gate)
            silu = gate * sg
            dzg = dz_ref[:, cs]
            dy = dzg * silu
            dp_ref[:, E + g * gc:E + (g + 1) * gc] = (dzg * (ypre * sc) * (sg * (1.0 + gate * (1.0 - sg)))).astype(BF16)
            dsc_ref[:, cs] += jnp.sum(dy * ypre, axis=0, keepdims=True)
            dyp = (dy * sc).astype(BF16)
            dwg_ref[g] += lax.dot_general(pb, dyp, TN, preferred_element_type=F32)
            dpool = lax.dot_general(dyp, w, NT, preferred_element_type=F32)
            gate_h = gth_ref[:, cs]
            dyp_h = (dzh_ref[:, cs] * (gate_h * _sigmoid(gate_h)) * sc).astype(BF16)
            dpool_h = lax.dot_general(dyp_h, w, NT, preferred_element_type=F32)
            q_ref[0:tm, cs] = dpool / jnp.minimum(t1, win).astype(F32)
            q_ref[tm:tm + POOL_HALO, cs] = jnp.where(not_last, dpool_h / jnp.minimum(t1h, win).astype(F32), 0.0)
            acc = q_ref[0:tm, cs] - dpool
            for k in range(1, win):
                acc = acc + q_ref[pl.ds(k, tm), cs]
            dp_ref[:, cs] = acc.astype(BF16)

    return pl.pallas_call(
        body, name=name, grid=(n_tiles,),
        in_specs=[pl.BlockSpec((tm, E), lambda i: (i, 0)),
                  pl.BlockSpec((POOL_HALO, E), lambda i: (jnp.minimum((i + 1) * hb, last_halo), 0)),
                  pl.BlockSpec((tm, E), lambda i: (i, 0)),
                  pl.BlockSpec((POOL_HALO, E), lambda i: (jnp.maximum(i * hb - 1, 0), 0)),
                  pl.BlockSpec((tm, E), lambda i: (i, 1)),
                  pl.BlockSpec((POOL_HALO, E), lambda i: (jnp.minimum((i + 1) * hb, last_halo), 1)),
                  pl.BlockSpec((n_grp, gc, gc), lambda i: (0, 0, 0)),
                  pl.BlockSpec((1, E), lambda i: (0, 0))],
        out_specs=[pl.BlockSpec((tm, E2), lambda i: (i, 0)),
                   pl.BlockSpec((n_grp, gc, gc), lambda i: (0, 0, 0)),
                   pl.BlockSpec((1, E), lambda i: (0, 0))],
        out_shape=[jax.ShapeDtypeStruct((S, E2), BF16),
                   jax.ShapeDtypeStruct((n_grp, gc, gc), F32),
                   jax.ShapeDtypeStruct((1, E), F32)],
        scratch_shapes=[pltpu.VMEM((POOL_HALO + tm, E), F32), pltpu.VMEM((tm + POOL_HALO, E), F32)],
        compiler_params=_params("arbitrary"))(dz, dz, proj, proj, proj, proj, wg, scale)


def _rope_kv(kvp, cos2, sin2, name):
    S, E2 = kvp.shape
    E = E2 // 2
    tm = min(256, S)

    def body(k_ref, v_ref, c_ref, s_ref, ko_ref, vo_ref):
        cosv, sinv = c_ref[...], s_ref[...]
        for h in range(E // HEAD_DIM):
            hs = slice(h * HEAD_DIM, (h + 1) * HEAD_DIM)
            ko_ref[:, hs] = _rope(k_ref[:, hs], cosv, sinv).astype(BF16)
        vo_ref[...] = v_ref[...].astype(BF16)

    return pl.pallas_call(
        body, name=name, grid=(S // tm,),
        in_specs=[pl.BlockSpec((tm, E), lambda i: (i, 0)), pl.BlockSpec((tm, E), lambda i: (i, 1)),
                  pl.BlockSpec((tm, HEAD_DIM), lambda i: (i, 0)), pl.BlockSpec((tm, HEAD_DIM), lambda i: (i, 0))],
        out_specs=[pl.BlockSpec((tm, E), lambda i: (i, 0)), pl.BlockSpec((tm, E), lambda i: (i, 0))],
        out_shape=[jax.ShapeDtypeStruct((S, E), BF16), jax.ShapeDtypeStruct((S, E), BF16)],
        compiler_params=_params("parallel"))(kvp, kvp, cos2, sin2)


def _band_mask(n):
    row = lax.broadcasted_iota(jnp.int32, (BAND, 2 * BAND), 0)
    col = lax.broadcasted_iota(jnp.int32, (BAND, 2 * BAND), 1)
    return (col >= row) & (col <= row + BAND) & ((n > 0) | (col >= BAND))


def _attn_fwd(proj, kr, vb, cos2, sin2, group, dil, name):
    S, PW = proj.shape
    E = kr.shape[1]
    H = E // HEAD_DIM
    M = S // dil
    nblk = M // BAND
    qcols = PW // E

    def body(q_ref, kc_ref, kp_ref, vc_ref, vp_ref, c_ref, s_ref, o_ref, l_ref):
        n = pl.program_id(1)
        band = _band_mask(n)
        cosv, sinv = c_ref[...], s_ref[...]
        lane = lax.broadcasted_iota(jnp.int32, (BAND, HEAD_DIM), 1)
        lse_tile = jnp.zeros((BAND, HEAD_DIM), F32)
        for h in range(H):
            hs = slice(h * HEAD_DIM, (h + 1) * HEAD_DIM)
            qr = _rope(q_ref[:, hs], cosv, sinv).astype(BF16)
            kcat = jnp.concatenate([kp_ref[:, hs], kc_ref[:, hs]], axis=0)
            vcat = jnp.concatenate([vp_ref[:, hs], vc_ref[:, hs]], axis=0)
            s = lax.dot_general(qr, kcat, NT, preferred_element_type=F32) * ATTN_SCALE
            s = jnp.where(band, s, NEG_INF)
            m = jnp.max(s, axis=-1, keepdims=True)
            p = jnp.exp(s - m)
            l = jnp.sum(p, axis=-1, keepdims=True)
            o_ref[:, hs] = jnp.dot(p.astype(BF16), vcat, preferred_element_type=F32) / l
            lse_tile = jnp.where(lane == h, m + jnp.log(l), lse_tile)
        l_ref[...] = lse_tile

    cur = lambda r, n: (n, r)
    prev = lambda r, n: (jnp.maximum(n - 1, 0), r)
    out, lse = pl.pallas_call(
        body, name=name, grid=(dil, nblk),
        in_specs=[pl.BlockSpec((BAND, E), lambda r, n: (n, r * qcols + group)),
                  pl.BlockSpec((BAND, E), cur), pl.BlockSpec((BAND, E), prev),
                  pl.BlockSpec((BAND, E), cur), pl.BlockSpec((BAND, E), prev),
                  pl.BlockSpec((BAND, HEAD_DIM), cur), pl.BlockSpec((BAND, HEAD_DIM), cur)],
        out_specs=[pl.BlockSpec((BAND, E), cur), pl.BlockSpec((BAND, HEAD_DIM), cur)],
        out_shape=[jax.ShapeDtypeStruct((M, dil * E), F32), jax.ShapeDtypeStruct((M, dil * HEAD_DIM), F32)],
        compiler_params=_params("parallel", "arbitrary"))(
            proj.reshape(M, dil * PW), kr.reshape(M, dil * E), kr.reshape(M, dil * E),
            vb.reshape(M, dil * E), vb.reshape(M, dil * E),
            cos2.reshape(M, dil * HEAD_DIM), sin2.reshape(M, dil * HEAD_DIM))
    return out.reshape(S, E), lse.reshape(S, HEAD_DIM)


def _attn_bwd(proj, kr, vb, cos2, sin2, do, lse, dlt, group, dil, name):
    S, PW = proj.shape
    E = kr.shape[1]
    H = E // HEAD_DIM
    M = S // dil
    nblk = M // BAND
    qcols = PW // E

    def body(q_ref, kc_ref, kp_ref, vc_ref, vp_ref, c_ref, s_ref, do_ref, l_ref, dl_ref,
             dq_ref, dk_ref, dv_ref, ck_ref, cv_ref):
        n = pl.program_id(1)

        @pl.when(n == 0)
        def _():
            ck_ref[...] = jnp.zeros_like(ck_ref)
            cv_ref[...] = jnp.zeros_like(cv_ref)

        @pl.when(n < nblk)
        def _():
            band = _band_mask(n)
            cosv, sinv = c_ref[...], s_ref[...]
            for h in range(H):
                hs = slice(h * HEAD_DIM, (h + 1) * HEAD_DIM)
                qr = _rope(q_ref[:, hs], cosv, sinv).astype(BF16)
                kcat = jnp.concatenate([kp_ref[:, hs], kc_ref[:, hs]], axis=0)
                vcat = jnp.concatenate([vp_ref[:, hs], vc_ref[:, hs]], axis=0)
                s = lax.dot_general(qr, kcat, NT, preferred_element_type=F32) * ATTN_SCALE
                s = jnp.where(band, s, NEG_INF)
                p = jnp.exp(s - l_ref[:, h:h + 1])
                dob = do_ref[:, hs]
                dpr = lax.dot_general(dob, vcat, NT, preferred_element_type=F32)
                ds = (p * (dpr - dl_ref[:, h:h + 1]) * ATTN_SCALE).astype(BF16)
                dq = jnp.dot(ds, kcat, preferred_element_type=F32)
                dq_ref[:, hs] = _rope_bwd(dq, cosv, sinv).astype(BF16)
                dkc = lax.dot_general(ds, qr, TN, preferred_element_type=F32)
                dvc = lax.dot_general(p.astype(BF16), dob, TN, preferred_element_type=F32)
                dk_ref[:, hs] = ck_ref[:, hs] + dkc[0:BAND]
                ck_ref[:, hs] = dkc[BAND:2 * BAND]
                dv_ref[:, hs] = cv_ref[:, hs] + dvc[0:BAND]
                cv_ref[:, hs] = dvc[BAND:2 * BAND]

        @pl.when(n == nblk)
        def _():
            dk_ref[...] = ck_ref[...]
            dv_ref[...] = cv_ref[...]

    last = nblk - 1
    cur = lambda r, n: (jnp.minimum(n, last), r)
    prev = lambda r, n: (jnp.maximum(jnp.minimum(n, last) - 1, 0), r)
    late = lambda r, n: (jnp.maximum(n - 1, 0), r)
    dq, dk, dv = pl.pallas_call(
        body, name=name, grid=(dil, nblk + 1),
        in_specs=[pl.BlockSpec((BAND, E), lambda r, n: (jnp.minimum(n, last), r * qcols + group)),
                  pl.BlockSpec((BAND, E), cur), pl.BlockSpec((BAND, E), prev),
                  pl.BlockSpec((BAND, E), cur), pl.BlockSpec((BAND, E), prev),
                  pl.BlockSpec((BAND, HEAD_DIM), cur), pl.BlockSpec((BAND, HEAD_DIM), cur),
                  pl.BlockSpec((BAND, E), cur),
                  pl.BlockSpec((BAND, HEAD_DIM), cur), pl.BlockSpec((BAND, HEAD_DIM), cur)],
        out_specs=[pl.BlockSpec((BAND, E), cur), pl.BlockSpec((BAND, E), late), pl.BlockSpec((BAND, E), late)],
        out_shape=[jax.ShapeDtypeStruct((M, dil * E), BF16), jax.ShapeDtypeStruct((M, dil * E), F32),
                   jax.ShapeDtypeStruct((M, dil * E), F32)],
        scratch_shapes=[pltpu.VMEM((BAND, E), F32), pltpu.VMEM((BAND, E), F32)],
        compiler_params=_params("parallel", "arbitrary"))(
            proj.reshape(M, dil * PW), kr.reshape(M, dil * E), kr.reshape(M, dil * E),
            vb.reshape(M, dil * E), vb.reshape(M, dil * E),
            cos2.reshape(M, dil * HEAD_DIM), sin2.reshape(M, dil * HEAD_DIM),
            do.reshape(M, dil * E), lse.reshape(M, dil * HEAD_DIM), dlt.reshape(M, dil * HEAD_DIM))
    return dq.reshape(S, E), dk.reshape(S, E), dv.reshape(S, E)


def _group_weights(l_refs, h):
    ls = [r[:, h:h + 1] for r in l_refs]
    mx = jnp.maximum(jnp.maximum(ls[0], ls[1]), ls[2])
    es = [jnp.exp(l - mx) for l in ls]
    inv = 1.0 / (es[0] + es[1] + es[2])
    return [e * inv for e in es]


def _merge_fwd(outs, lses, proj, name):
    S, E = outs[0].shape
    tm = min(256, S)
    gate_col = proj.shape[1] // E - 1

    def body(o0, o1, o2, l0, l1, l2, gt_ref, z_ref):
        for h in range(E // HEAD_DIM):
            hs = slice(h * HEAD_DIM, (h + 1) * HEAD_DIM)
            a = _group_weights((l0, l1, l2), h)
            merged = a[0] * o0[:, hs] + a[1] * o1[:, hs] + a[2] * o2[:, hs]
            gate = gt_ref[:, hs]
            z_ref[:, hs] = (merged * (gate * _sigmoid(gate))).astype(BF16)

    wide = pl.BlockSpec((tm, E), lambda i: (i, 0))
    thin = pl.BlockSpec((tm, HEAD_DIM), lambda i: (i, 0))
    return pl.pallas_call(
        body, name=name, grid=(S // tm,),
        in_specs=[wide, wide, wide, thin, thin, thin, pl.BlockSpec((tm, E), lambda i: (i, gate_col))],
        out_specs=wide,
        out_shape=jax.ShapeDtypeStruct((S, E), BF16),
        compiler_params=_params("parallel"))(*outs, *lses, proj)


def _merge_bwd(dz, outs, lses, proj, name):
    S, E = outs[0].shape
    tm = min(256, S)
    gate_col = proj.shape[1] // E - 1

    def body(dz_ref, o0, o1, o2, l0, l1, l2, gt_ref, d0, d1, d2, t0, t1, t2, dg_ref):
        o_refs, d_refs, t_refs = (o0, o1, o2), (d0, d1, d2), (t0, t1, t2)
        lane = lax.broadcasted_iota(jnp.int32, (tm, HEAD_DIM), 1)
        tiles = [jnp.zeros((tm, HEAD_DIM), F32) for _ in range(3)]
        for h in range(E // HEAD_DIM):
            hs = slice(h * HEAD_DIM, (h + 1) * HEAD_DIM)
            a = _group_weights((l0, l1, l2), h)
            merged = a[0] * o0[:, hs] + a[1] * o1[:, hs] + a[2] * o2[:, hs]
            gate = gt_ref[:, hs]
            sg = _sigmoid(gate)
            dzh = dz_ref[:, hs]
            dmerged = dzh * (gate * sg)
            dg_ref[:, hs] = (dzh * merged * (sg * (1.0 + gate * (1.0 - sg)))).astype(BF16)
            tot = jnp.sum(dmerged * merged, axis=-1, keepdims=True)
            for g in range(3):
                d_refs[g][:, hs] = (a[g] * dmerged).astype(BF16)
                tiles[g] = jnp.where(lane == h, a[g] * tot, tiles[g])
        for g in range(3):
            t_refs[g][...] = tiles[g]

    wide = pl.BlockSpec((tm, E), lambda i: (i, 0))
    thin = pl.BlockSpec((tm, HEAD_DIM), lambda i: (i, 0))
    res = pl.pallas_call(
        body, name=name, grid=(S // tm,),
        in_specs=[wide, wide, wide, wide, thin, thin, thin, pl.BlockSpec((tm, E), lambda i: (i, gate_col))],
        out_specs=[wide, wide, wide, thin, thin, thin, wide],
        out_shape=[jax.ShapeDtypeStruct((S, E), BF16)] * 3 + [jax.ShapeDtypeStruct((S, HEAD_DIM), F32)] * 3
        + [jax.ShapeDtypeStruct((S, E), BF16)],
        compiler_params=_params("parallel"))(dz, *outs, *lses, proj)
    return res[0:3], res[3:6], res[6]


def _kv_bwd(dks, dvs, cos2, sin2, name):
    S, E = dks[0].shape
    n = len(dks)
    tm = min(128, S)

    def body(*refs):
        dk_refs, dv_refs = refs[0:n], refs[n:2 * n]
        c_ref, s_ref, o_ref = refs[2 * n:]
        cosv, sinv = c_ref[...], s_ref[...]
        for h in range(E // HEAD_DIM):
            hs = slice(h * HEAD_DIM, (h + 1) * HEAD_DIM)
            dk = dk_refs[0][:, hs]
            dv = dv_refs[0][:, hs]
            for j in range(1, n):
                dk = dk + dk_refs[j][:, hs]
                dv = dv + dv_refs[j][:, hs]
            o_ref[:, hs] = _rope_bwd(dk, cosv, sinv).astype(BF16)
            o_ref[:, E + h * HEAD_DIM:E + (h + 1) * HEAD_DIM] = dv.astype(BF16)

    wide = pl.BlockSpec((tm, E), lambda i: (i, 0))
    thin = pl.BlockSpec((tm, HEAD_DIM), lambda i: (i, 0))
    return pl.pallas_call(
        body, name=name, grid=(S // tm,),
        in_specs=[wide] * (2 * n) + [thin, thin],
        out_specs=pl.BlockSpec((tm, 2 * E), lambda i: (i, 0)),
        out_shape=jax.ShapeDtypeStruct((S, 2 * E), BF16),
        compiler_params=_params("parallel"))(*dks, *dvs, cos2, sin2)


def _norm_bwd(dhn, x, gain, dres, name):
    S, D = x.shape
    tm = min(256, S)

    def body(dh_ref, x_ref, g_ref, r_ref, dx_ref, dg_ref):
        @pl.when(pl.program_id(0) == 0)
        def _():
            dg_ref[...] = jnp.zeros_like(dg_ref)
        xf = x_ref[...]
        inv = lax.rsqrt(jnp.mean(xf * xf, axis=-1, keepdims=True) + RMS_EPS)
        xhat = xf * inv
        dh = dh_ref[...]
        dg_ref[...] += jnp.sum(dh * xhat, axis=0, keepdims=True)
        dxh = dh * g_ref[...]
        dx_ref[...] = r_ref[...] + inv * (dxh - xhat * jnp.mean(dxh * xhat, axis=-1, keepdims=True))

    tile = pl.BlockSpec((tm, D), lambda i: (i, 0))
    vec = pl.BlockSpec((1, D), lambda i: (0, 0))
    return pl.pallas_call(
        body, name=name, grid=(S // tm,),
        in_specs=[tile, tile, vec, tile],
        out_specs=[tile, vec],
        out_shape=[jax.ShapeDtypeStruct((S, D), F32), jax.ShapeDtypeStruct((1, D), F32)],
        compiler_params=_params("arbitrary"))(dhn, x, gain, dres)


def _final_norm_loss(x, target, gain, name):
    S, D = x.shape
    tm = min(256, S)

    def body(x_ref, t_ref, g_ref, loss_ref, dx_ref, dg_ref):
        @pl.when(pl.program_id(0) == 0)
        def _():
            loss_ref[...] = jnp.zeros_like(loss_ref)
            dg_ref[...] = jnp.zeros_like(dg_ref)
        xf = x_ref[...]
        inv = lax.rsqrt(jnp.mean(xf * xf, axis=-1, keepdims=True) + RMS_EPS)
        xhat = xf * inv
        g = g_ref[...]
        err = xhat * g - t_ref[...]
        loss_ref[...] += 0.5 * jnp.sum(jnp.mean(err * err, axis=-1, keepdims=True), axis=0, keepdims=True)
        dy = err / D
        dg_ref[...] += jnp.sum(dy * xhat, axis=0, keepdims=True)
        dxh = dy * g
        dx_ref[...] = inv * (dxh - xhat * jnp.mean(dxh * xhat, axis=-1, keepdims=True))

    tile = pl.BlockSpec((tm, D), lambda i: (i, 0))
    vec = pl.BlockSpec((1, D), lambda i: (0, 0))
    return pl.pallas_call(
        body, name=name, grid=(S // tm,),
        in_specs=[tile, tile, vec],
        out_specs=[pl.BlockSpec((1, 1), lambda i: (0, 0)), tile, vec],
        out_shape=[jax.ShapeDtypeStruct((1, 1), F32), jax.ShapeDtypeStruct((S, D), F32),
                   jax.ShapeDtypeStruct((1, D), F32)],
        compiler_params=_params("arbitrary"))(x, target, gain)


def _adamw_math(g, w, m, v):
    m = ADAM_B1 * m + (1.0 - ADAM_B1) * g
    v = ADAM_B2 * v + (1.0 - ADAM_B2) * (g * g)
    m_hat = m / (1.0 - ADAM_B1 ** ADAM_STEP)
    v_hat = v / (1.0 - ADAM_B2 ** ADAM_STEP)
    delta = -ADAM_LR * (m_hat / (jnp.sqrt(v_hat) + ADAM_EPS) + ADAM_WD * w)
    return delta, m, v


def _adamw_rows(g, w, m, v, name):
    def body(g_ref, w_ref, m_ref, v_ref, d_ref, mo_ref, vo_ref):
        d_ref[...], mo_ref[...], vo_ref[...] = _adamw_math(g_ref[...], w_ref[...], m_ref[...], v_ref[...])

    whole = pl.BlockSpec(memory_space=pltpu.VMEM)
    return pl.pallas_call(
        body, name=name, in_specs=[whole] * 4, out_specs=[whole] * 3,
        out_shape=[jax.ShapeDtypeStruct(g.shape, F32)] * 3)(g, w, m, v)


def _adamw_blocks(own, others, w, m, v, layer, earlier, name):
    L, R, C = w.shape
    n = others.shape[0]
    tr = R
    while tr * C > 128 * 1024 and tr % 16 == 0:
        tr //= 2

    def body(o_ref, p_ref, w_ref, m_ref, v_ref, *rest):
        g_ref, d_ref, mo_ref, vo_ref = rest[-4:]
        g = o_ref[...].astype(F32)
        for j in range(n):
            g = g + p_ref[j].astype(F32)
        g_ref[...] = g
        d_ref[...], mo_ref[...], vo_ref[...] = _adamw_math(g, w_ref[...], m_ref[...], v_ref[...])

    tile = pl.BlockSpec((None, tr, C), lambda i: (layer, i, 0))
    kept = [] if earlier is None else list(earlier)
    return pl.pallas_call(
        body, name=name, grid=(R // tr,),
        in_specs=[pl.BlockSpec((tr, C), lambda i: (i, 0)), pl.BlockSpec((n, tr, C), lambda i: (0, i, 0)),
                  tile, tile, tile] + [ANY] * len(kept),
        out_specs=[tile] * 4,
        out_shape=[jax.ShapeDtypeStruct((L, R, C), F32)] * 4,
        input_output_aliases={5 + j: j for j in range(len(kept))},
        compiler_params=_params("parallel"))(own, others, w, m, v, *kept)


def _position():
    return lax.axis_index("x"), lax.axis_index("y"), lax.axis_index("c")


def _block_index(px, py, pc):
    return 4 * px + 2 * py + pc


def _all_gather(shards, name):
    n = len(shards)

    def body(*refs):
        ins, outs = refs[0:n], refs[n:2 * n]
        send_sems, recv_sems, local_sems = refs[2 * n:]
        x, y, c = _position()
        me, sibling = (x, y, c), (x, y, 1 - c)
        chips = [(1 - x, y), (x, 1 - y), (1 - x, 1 - y)]

        def copy(a, k, block, to, src=None):
            rows = outs[a].at[_block_index(*block)]
            return pltpu.make_async_remote_copy(
                src_ref=rows if src is None else src, dst_ref=rows,
                send_sem=send_sems.at[a, k], recv_sem=recv_sems.at[a, k], device_id=to, device_id_type=MESH)

        mine, first, passed = [], [], []
        for a in range(n):
            cp = pltpu.make_async_copy(ins[a], outs[a].at[_block_index(*me)], local_sems.at[a])
            cp.start()
            mine.append(cp)
            first.append(copy(a, 0, me, sibling, src=ins[a]))
            first += [copy(a, 1 + j, me, (*chip, c), src=ins[a]) for j, chip in enumerate(chips)]
        for cp in first:
            cp.start()
        for j, chip in enumerate(chips):
            for a in range(n):
                copy(a, 1 + j, (*chip, c), me).wait_recv()
                fwd = copy(a, 4 + j, (*chip, c), sibling)
                fwd.start()
                passed.append(fwd)
        for a in range(n):
            copy(a, 0, sibling, me).wait_recv()
            for j, chip in enumerate(chips):
                copy(a, 4 + j, (*chip, 1 - c), me).wait_recv()
        for cp in first + passed:
            cp.wait_send()
        for cp in mine:
            cp.wait()

    return pl.pallas_call(
        body, name=name,
        in_specs=[ANY] * n, out_specs=[ANY] * n,
        out_shape=[jax.ShapeDtypeStruct((N_DEV,) + s.shape, s.dtype) for s in shards],
        scratch_shapes=[pltpu.SemaphoreType.DMA((n, 7)), pltpu.SemaphoreType.DMA((n, 7)),
                        pltpu.SemaphoreType.DMA((n,))],
    )(*shards)


def _peers(x, y, c):
    return [((1 - x) if k & 4 else x, (1 - y) if k & 2 else y, (1 - c) if k & 1 else c) for k in range(1, N_DEV)]


HBM = pl.BlockSpec(memory_space=pltpu.HBM)
SEM = pl.BlockSpec(memory_space=pltpu.SEMAPHORE)
EFFECT = pltpu.SideEffectType.DATAFLOW_SIDE_EFFECTING


def _push_copy(src_refs, land_refs, send_sems, recv_sems, a, k, peer, per_peer, by_sender, arriving):
    me_idx, p_idx = _block_index(*_position()), _block_index(*peer)
    src = src_refs[a].at[p_idx] if per_peer else src_refs[a]
    if by_sender:
        slot = p_idx if arriving else me_idx
    else:
        slot = k
    return pltpu.make_async_remote_copy(
        src_ref=src, dst_ref=land_refs[a].at[slot], send_sem=send_sems.at[a * (N_DEV - 1) + k],
        recv_sem=recv_sems.at[a * (N_DEV - 1) + k], device_id=peer, device_id_type=MESH)


def _push_start(srcs, lands, per_peer, by_sender, after, name):
    n = len(srcs)

    def body(*refs):
        src_refs, land_refs = refs[0:n], refs[n:2 * n]
        send_sems, recv_sems = refs[2 * n + 1], refs[2 * n + 2]
        token = refs[-1]
        for a in range(n):
            for k, peer in enumerate(_peers(*_position())):
                _push_copy(src_refs, land_refs, send_sems, recv_sems, a, k, peer, per_peer, by_sender, False).start()
        token[...] = jnp.zeros_like(token)

    args = [pltpu.with_memory_space_constraint(t, pltpu.HBM) for t in list(srcs) + list(lands)]
    res = pl.pallas_call(
        body, name=name,
        in_specs=[HBM] * (2 * n) + [ANY],
        out_specs=[SEM, SEM] + [HBM] * (2 * n) + [pl.BlockSpec(memory_space=pltpu.VMEM)],
        out_shape=[pltpu.SemaphoreType.DMA((n * (N_DEV - 1),)), pltpu.SemaphoreType.DMA((n * (N_DEV - 1),))]
        + [pltpu.HBM(t.shape, t.dtype) for t in args] + [jax.ShapeDtypeStruct((8, 128), F32)],
        input_output_aliases={i: 2 + i for i in range(2 * n)},
        compiler_params=pltpu.CompilerParams(has_side_effects=EFFECT))(*args, after)
    return res[0], res[1], res[2:2 + n], res[2 + n:2 + 2 * n], res[-1]


def _push_wait(started, per_peer, by_sender, after, name):
    send_sems, recv_sems, srcs, lands, _ = started
    n = len(srcs)

    def body(*refs):
        src_refs, land_refs = refs[0:n], refs[n:2 * n]
        send_s, recv_s = refs[2 * n], refs[2 * n + 1]
        for a in range(n):
            for k, peer in enumerate(_peers(*_position())):
                _push_copy(src_refs, land_refs, send_s, recv_s, a, k, peer, per_peer, by_sender, False).wait_send()
                _push_copy(src_refs, land_refs, send_s, recv_s, a, k, peer, per_peer, by_sender, True).wait_recv()

    res = pl.pallas_call(
        body, name=name,
        in_specs=[HBM] * (2 * n) + [SEM, SEM, ANY],
        out_specs=[HBM] * (2 * n),
        out_shape=[pltpu.HBM(t.shape, t.dtype) for t in list(srcs) + list(lands)],
        input_output_aliases={i: i for i in range(2 * n)},
        compiler_params=pltpu.CompilerParams(has_side_effects=EFFECT))(*srcs, *lands, send_sems, recv_sems, after)
    return res[0:n], res[n:2 * n]


def _all_reduce_rows(v, name):
    R, D = v.shape

    def body(v_ref, o_ref, buf_ref, send_sems, recv_sems):
        x, y, c = _position()
        me_idx = _block_index(x, y, c)
        buf_ref[me_idx] = v_ref[...]
        copies = []
        for k in range(1, N_DEV):
            px = (1 - x) if k & 4 else x
            py = (1 - y) if k & 2 else y
            pc = (1 - c) if k & 1 else c
            rc = pltpu.make_async_remote_copy(
                src_ref=v_ref, dst_ref=buf_ref.at[me_idx],
                send_sem=send_sems.at[k - 1], recv_sem=recv_sems.at[k - 1],
                device_id=(px, py, pc), device_id_type=MESH)
            rc.start()
            copies.append((rc, pltpu.make_async_remote_copy(
                src_ref=v_ref, dst_ref=buf_ref.at[_block_index(px, py, pc)],
                send_sem=send_sems.at[k - 1], recv_sem=recv_sems.at[k - 1],
                device_id=(px, py, pc), device_id_type=MESH)))
        for rc, arrival in copies:
            rc.wait_send()
            arrival.wait_recv()
        acc = buf_ref[0]
        for j in range(1, N_DEV):
            acc = acc + buf_ref[j]
        o_ref[...] = acc

    return pl.pallas_call(
        body, name=name,
        in_specs=[pl.BlockSpec(memory_space=pltpu.VMEM)],
        out_specs=pl.BlockSpec(memory_space=pltpu.VMEM),
        out_shape=jax.ShapeDtypeStruct((R, D), F32),
        scratch_shapes=[pltpu.VMEM((N_DEV, R, D), F32),
                        pltpu.SemaphoreType.DMA((7,)), pltpu.SemaphoreType.DMA((7,))],
    )(v)


def _rope_tables(S):
    inv_freq = 1.0 / (ROPE_THETA ** (jnp.arange(0, HEAD_DIM, 2, dtype=F32) / HEAD_DIM))
    ang = jnp.arange(S, dtype=F32)[:, None] * inv_freq[None, :]
    cos, sin = jnp.cos(ang), jnp.sin(ang)
    return jnp.concatenate([cos, cos], axis=1), jnp.concatenate([-sin, sin], axis=1)


def _local_step(xs, target, vecs, n_a, n_b, get_weights, put_grads):
    S, D = xs.shape
    E = D
    cos2, sin2 = _rope_tables(S)
    ts = min(512, S)

    def col_blocks(w):
        cb = w.shape[2]
        tn = min(cb, 1024)
        per = cb // tn
        return (None, D, tn), (lambda i, j: (j // per, 0, j % per)), N_DEV * per, tn

    def grad_in(hn, dproj, cb, name):
        return _matmul_tn(hn, dproj, (ts, D), lambda j, s: (s, 0), (ts, cb), lambda j, s: (s, j),
                          (N_DEV, D, cb), (None, D, cb), lambda j, s: (j, 0, 0), (D, cb), N_DEV, name)

    def grad_out(z, dx, name, col=0):
        rows = z.shape[1]
        ta = min(1024, rows)
        out = _matmul_tn(z, dx, (ts, ta), lambda a, s: (s, a), (ts, E), lambda a, s: (s, col),
                         (rows, E), (ta, E), lambda a, s: (a, 0), (ta, E), rows // ta, name)
        return out.reshape(N_DEV, rows // N_DEV, E)

    x = xs
    a_saved, b_saved = [], []
    for i in range(n_a):
        w = get_weights(f"a{i}", x)
        blk, idx, nblocks, tn = col_blocks(w["w_in"])
        proj, hn = _norm_matmul(x, vecs["norm_a"][i:i + 1], w["w_in"], blk, idx, nblocks, tn, f"a{i}_in")
        z = _a_mid_fwd(proj, w["w_grp"], vecs["scale_a"][i:i + 1], f"a{i}_mid")
        x_next = _matmul_res(z, w["w_out"], x, f"a{i}_out")
        a_saved.append((x, hn, proj, z, w))
        x = x_next
    x_kv = x
    w_kv = get_weights("kv", x)["w_kv"]
    tn = min(E, 1024)
    kvp, hn_kv = _norm_matmul(x, vecs["norm_kv"], w_kv, (D, tn), lambda i, j: (0, j), 2 * E // tn, tn, "kv_in")
    kr, vb = _rope_kv(kvp, cos2, sin2, "kv_rope")
    after = kr
    for i in range(n_b):
        w = get_weights(f"b{i}", after)
        blk, idx, nblocks, tn = col_blocks(w["w_in"])
        proj, hn = _norm_matmul(x, vecs["norm_b"][i:i + 1], w["w_in"], blk, idx, nblocks, tn, f"b{i}_in")
        outs, lses = [], []
        for g, dil in enumerate(DILATIONS):
            o, l = _attn_fwd(proj, kr, vb, cos2, sin2, g, dil, f"b{i}_attn{g}")
            outs.append(o)
            lses.append(l)
        z = _merge_fwd(outs, lses, proj, f"b{i}_merge")
        x_next = _matmul_res(z, w["w_out"], x, f"b{i}_out")
        b_saved.append((x, hn, proj, z, outs, lses, w))
        x = x_next
        after = x
    loss, dx, dg_f = _final_norm_loss(x, target, vecs["norm_f"], "final")

    vec = {"norm_a": [None] * n_a, "scale_a": [None] * n_a, "norm_b": [None] * n_b, "norm_f": [dg_f]}
    dks, dvs = [], []
    for i in reversed(range(n_b)):
        x_in, hn, proj, z, outs, lses, w = b_saved[i]
        dw_out = grad_out(z, dx, f"b{i}_dwout")
        dz = _matmul_nt_rows(dx, w["w_out"], f"b{i}_dz")
        dos, dlts, dgate = _merge_bwd(dz, outs, lses, proj, f"b{i}_dmerge")
        dqs = []
        for g, dil in enumerate(DILATIONS):
            dq, dk, dv = _attn_bwd(proj, kr, vb, cos2, sin2, dos[g], lses[g], dlts[g], g, dil, f"b{i}_dattn{g}")
            dqs.append(dq)
            dks.append(dk)
            dvs.append(dv)
        dproj = jnp.concatenate(dqs + [dgate], axis=1)
        cb = w["w_in"].shape[2]
        put_grads(f"b{i}", {"w_out": dw_out, "w_in": grad_in(hn, dproj, cb, f"b{i}_dwin")})
        dhn = _matmul_nt_cols(dproj, w["w_in"], (None, D, cb), lambda t, j: (j, 0, 0), N_DEV, cb, D, f"b{i}_dhn")
        dx, vec["norm_b"][i] = _norm_bwd(dhn, x_in, vecs["norm_b"][i:i + 1], dx, f"b{i}_dnorm")

    dkv = _kv_bwd(dks, dvs, cos2, sin2, "kv_dsum")
    put_grads("kv", {"w_k": grad_out(hn_kv, dkv, "kv_dwk", 0), "w_v": grad_out(hn_kv, dkv, "kv_dwv", 1)})
    dhn = _matmul_nt_cols(dkv, w_kv, (D, E), lambda t, j: (0, j), 2, E, D, "kv_dhn")
    dx, dg_kv = _norm_bwd(dhn, x_kv, vecs["norm_kv"], dx, "kv_dnorm")
    vec["norm_kv"] = [dg_kv]

    for i in reversed(range(n_a)):
        x_in, hn, proj, z, w = a_saved[i]
        dw_out = grad_out(z, dx, f"a{i}_dwout")
        dz = _matmul_nt_rows(dx, w["w_out"], f"a{i}_dz")
        dproj, dwg, dsc = _a_mid_bwd(dz, proj, w["w_grp"], vecs["scale_a"][i:i + 1], f"a{i}_dmid")
        n_grp, gc, _ = dwg.shape
        dwg = dwg.reshape(n_grp, N_DEV, gc // N_DEV, gc).transpose(1, 0, 2, 3).astype(BF16)
        vec["scale_a"][i] = dsc
        cb = w["w_in"].shape[2]
        put_grads(f"a{i}", {"w_out": dw_out, "w_grp": dwg, "w_in": grad_in(hn, dproj, cb, f"a{i}_dwin")})
        dhn = _matmul_nt_cols(dproj, w["w_in"], (None, D, cb), lambda t, j: (j, 0, 0), N_DEV, cb, D, f"a{i}_dhn")
        dx, vec["norm_a"][i] = _norm_bwd(dhn, x_in, vecs["norm_a"][i:i + 1], dx, f"a{i}_dnorm")

    return loss, dx, {k: jnp.concatenate(v, axis=0) for k, v in vec.items()}


VECTORS = ("norm_a", "scale_a", "norm_kv", "norm_b", "norm_f")
SHARDED_VECTORS = ("norm_a", "scale_a")
GROUPS = {
    "a0": (("w_in", "w_in_a", 0), ("w_grp", "w_grp_a", 0), ("w_out", "w_out_a", 0)),
    "a1": (("w_in", "w_in_a", 1), ("w_grp", "w_grp_a", 1), ("w_out", "w_out_a", 1)),
    "kv": (("w_k", "w_k", None), ("w_v", "w_v", None)),
    "b0": (("w_in", "w_in_b", 0), ("w_out", "w_out_b", 0)),
    "b1": (("w_in", "w_in_b", 1), ("w_out", "w_out_b", 1)),
}
PREFETCHED = ("a1", "kv", "b0", "b1")


def kernel(x, norm_a, w_in_a, w_grp_a, scale_a, w_out_a, norm_kv, w_k, w_v, norm_b, w_in_b, w_out_b, norm_f, loss_target, m_norm_a, m_w_in_a, m_w_grp_a, m_scale_a, m_w_out_a, m_norm_kv, m_w_k, m_w_v, m_norm_b, m_w_in_b, m_w_out_b, m_norm_f, v_norm_a, v_w_in_a, v_w_grp_a, v_scale_a, v_w_out_a, v_norm_kv, v_w_k, v_w_v, v_norm_b, v_w_in_b, v_w_out_b, v_norm_f):
    w = dict(norm_a=norm_a, w_in_a=w_in_a, w_grp_a=w_grp_a, scale_a=scale_a, w_out_a=w_out_a, norm_kv=norm_kv,
             w_k=w_k, w_v=w_v, norm_b=norm_b, w_in_b=w_in_b, w_out_b=w_out_b, norm_f=norm_f)
    m = dict(norm_a=m_norm_a, w_in_a=m_w_in_a, w_grp_a=m_w_grp_a, scale_a=m_scale_a, w_out_a=m_w_out_a,
             norm_kv=m_norm_kv, w_k=m_w_k, w_v=m_w_v, norm_b=m_norm_b, w_in_b=m_w_in_b, w_out_b=m_w_out_b,
             norm_f=m_norm_f)
    v = dict(norm_a=v_norm_a, w_in_a=v_w_in_a, w_grp_a=v_w_grp_a, scale_a=v_scale_a, w_out_a=v_w_out_a,
             norm_kv=v_norm_kv, w_k=v_w_k, w_v=v_w_v, norm_b=v_norm_b, w_in_b=v_w_in_b, w_out_b=v_w_out_b,
             norm_f=v_norm_f)
    D = x.shape[2]
    me = _block_index(*_position())

    def shard(group):
        return [w[p].astype(BF16) if layer is None else w[p][layer].astype(BF16) for _, p, layer in GROUPS[group]]

    def as_weights(group, gathered):
        out = dict(zip([n for n, _, _ in GROUPS[group]], gathered))
        if "w_grp" in out:
            g = out["w_grp"]
            out["w_grp"] = g.transpose(1, 0, 2, 3).reshape(g.shape[1], g.shape[3], g.shape[3])
        if "w_k" in out:
            out = {"w_kv": jnp.concatenate([out["w_k"].reshape(D, D), out["w_v"].reshape(D, D)], axis=1)}
        return out

    first = _all_gather(shard("a0") + [w[k] for k in SHARDED_VECTORS], "gather_first")
    n_first = len(GROUPS["a0"])
    vecs = {k: g.transpose(1, 0, 2).reshape(w[k].shape[0], D) for k, g in zip(SHARDED_VECTORS, first[n_first:])}
    vecs.update(norm_kv=norm_kv[None, :], norm_b=norm_b, norm_f=norm_f[None, :])
    srcs, lands = [], []
    for group in PREFETCHED:
        for s in shard(group):
            srcs.append(s)
            lands.append(lax.dynamic_update_index_in_dim(lax.empty((N_DEV,) + s.shape, s.dtype), s[None], me, 0))
    inflight, at = {}, 0
    token = None
    for group in PREFETCHED:
        n = len(GROUPS[group])
        inflight[group] = _push_start(srcs[at:at + n], lands[at:at + n], False, True,
                                      first[0] if token is None else token, f"gather_{group}_start")
        token = inflight[group][4]
        at += n
    vecs["norm_a"] = vecs["norm_a"] + token[0:1, 0:1]

    def get_weights(group, after):
        if group == "a0":
            return as_weights(group, first[0:n_first])
        return as_weights(group, _push_wait(inflight[group], False, True, after, f"gather_{group}_wait")[1])

    sent = {}

    def put_grads(group, grads):
        blocks = [grads[n] for n, _, _ in GROUPS[group]]
        lands = [lax.empty((N_DEV - 1,) + b.shape[1:], b.dtype) for b in blocks]
        sent[group] = _push_start(blocks, lands, True, False, jnp.zeros((8, 128), F32), f"exchange_{group}_start")

    loss, dx, vec = _local_step(x[0], loss_target[0], vecs, w_in_a.shape[0], w_in_b.shape[0], get_weights, put_grads)
    rows = _all_reduce_rows(jnp.concatenate([vec[k] for k in VECTORS], axis=0), "reduce_vectors")

    out = {}
    after = dx
    for group in sent:
        blocks, arrived = _push_wait(sent[group], True, False, after, f"exchange_{group}_wait")
        for (_, p, layer), blk, got in zip(GROUPS[group], blocks, arrived):
            cols = w[p].shape[-1]
            own = lax.dynamic_index_in_dim(blk, me, 0, keepdims=False).reshape(-1, cols)
            n_layers = 1 if layer is None else w[p].shape[0]
            stacked = lambda t: t.reshape(n_layers, -1, cols)
            res = _adamw_blocks(own, got.reshape(N_DEV - 1, -1, cols), stacked(w[p]), stacked(m[p]), stacked(v[p]),
                                0 if layer is None else layer, out.get(p), f"adamw_{group}_{p}")
            out[p] = res
            after = res[1]
    out = {p: [r.reshape(w[p].shape) for r in res] for p, res in out.items()}
    start = 0
    for k in VECTORS:
        n_rows = vec[k].shape[0]
        g = rows[start:start + n_rows]
        start += n_rows
        if k in SHARDED_VECTORS:
            g = lax.dynamic_slice_in_dim(g, me * (D // N_DEV), D // N_DEV, axis=1)
        res = _adamw_rows(g, w[k].reshape(g.shape), m[k].reshape(g.shape), v[k].reshape(g.shape), f"adamw_{k}")
        out[k] = [r.reshape(w[k].shape) for r in [g] + list(res)]

    names = ("norm_a", "w_in_a", "w_grp_a", "scale_a", "w_out_a", "norm_kv", "w_k", "w_v", "norm_b", "w_in_b",
             "w_out_b", "norm_f")
    total = lax.psum(loss[0, 0], ("x", "y", "c"))
    return (total, dx[None], *[out[k][0] for k in names], *[out[k][1] for k in names],
            *[out[k][2] for k in names], *[out[k][3] for k in names])
```

```python
import math

import jax
import jax.numpy as jnp
from jax import lax
from jax.experimental import pallas as pl
from jax.experimental.pallas import tpu as pltpu

F32 = jnp.float32
BF16 = jnp.bfloat16

N_DEV = 8
MESH = pl.DeviceIdType.MESH
RMS_EPS = 1e-6
HEAD_DIM = 128
HALF_HEAD = HEAD_DIM // 2
BAND = 128
DILATIONS = (1, 4, 16)
POOL_WINDOWS = (2, 4, 8, 16)
POOL_HALO = 16
ROPE_THETA = 10000.0
NEG_INF = -1e30
ATTN_SCALE = 1.0 / math.sqrt(HEAD_DIM)
ADAM_LR, ADAM_B1, ADAM_B2, ADAM_EPS, ADAM_WD, ADAM_STEP = 0.001, 0.9, 0.999, 1e-08, 0.01, 10
VMEM_LIMIT_BYTES = 56 * 1024 * 1024
ANY = pl.BlockSpec(memory_space=pl.ANY)
NT = (((1,), (1,)), ((), ()))
TN = (((0,), (0,)), ((), ()))


def _params(*semantics):
    return pltpu.CompilerParams(dimension_semantics=semantics, vmem_limit_bytes=VMEM_LIMIT_BYTES)


def _sigmoid(t):
    return 1.0 / (1.0 + jnp.exp(-t))


def _rope(t, cos2, sin2):
    return t * cos2 + pltpu.roll(t, HALF_HEAD, 1) * sin2


def _rope_bwd(dt, cos2, sin2):
    return dt * cos2 + pltpu.roll(dt * sin2, HALF_HEAD, 1)


def _norm_matmul(x, gain, w, w_block, w_index, n_col_blocks, tn, name):
    S, D = x.shape
    tm = min(512, S)

    def body(x_ref, g_ref, w_ref, o_ref, hn_ref, hs_ref):
        @pl.when(pl.program_id(1) == 0)
        def _():
            xf = x_ref[...]
            inv = lax.rsqrt(jnp.mean(xf * xf, axis=-1, keepdims=True) + RMS_EPS)
            hb = ((xf * inv) * g_ref[...]).astype(BF16)
            hs_ref[...] = hb
            hn_ref[...] = hb
        o_ref[...] = jnp.dot(hs_ref[...], w_ref[...], preferred_element_type=F32)

    return pl.pallas_call(
        body, name=name, grid=(S // tm, n_col_blocks),
        in_specs=[pl.BlockSpec((tm, D), lambda i, j: (i, 0)),
                  pl.BlockSpec((1, D), lambda i, j: (0, 0)),
                  pl.BlockSpec(w_block, w_index)],
        out_specs=[pl.BlockSpec((tm, tn), lambda i, j: (i, j)),
                   pl.BlockSpec((tm, D), lambda i, j: (i, 0))],
        out_shape=[jax.ShapeDtypeStruct((S, n_col_blocks * tn), F32), jax.ShapeDtypeStruct((S, D), BF16)],
        scratch_shapes=[pltpu.VMEM((tm, D), BF16)],
        compiler_params=_params("parallel", "arbitrary"))(x, gain, w)


def _matmul_res(a, w, res, name):
    S, K = a.shape
    nd, rb, N = w.shape
    tm = min(512, S)

    def body(a_ref, w_ref, r_ref, o_ref):
        acc = jnp.dot(a_ref[:, 0:rb], w_ref[0], preferred_element_type=F32)
        for k in range(1, nd):
            acc = acc + jnp.dot(a_ref[:, k * rb:(k + 1) * rb], w_ref[k], preferred_element_type=F32)
        o_ref[...] = r_ref[...] + acc

    return pl.pallas_call(
        body, name=name, grid=(S // tm,),
        in_specs=[pl.BlockSpec((tm, K), lambda i: (i, 0)),
                  pl.BlockSpec((nd, rb, N), lambda i: (0, 0, 0)),
                  pl.BlockSpec((tm, N), lambda i: (i, 0))],
        out_specs=pl.BlockSpec((tm, N), lambda i: (i, 0)),
        out_shape=jax.ShapeDtypeStruct((S, N), F32),
        compiler_params=_params("parallel"))(a, w, res)


def _matmul_nt_rows(dy, w, name):
    S, N = dy.shape
    nd, rb, _ = w.shape
    tm = min(512, S)

    def body(d_ref, w_ref, o_ref):
        db = d_ref[...].astype(BF16)
        for k in range(nd):
            o_ref[:, k * rb:(k + 1) * rb] = lax.dot_general(db, w_ref[k], NT, preferred_element_type=F32)

    return pl.pallas_call(
        body, name=name, grid=(S // tm,),
        in_specs=[pl.BlockSpec((tm, N), lambda i: (i, 0)),
                  pl.BlockSpec((nd, rb, N), lambda i: (0, 0, 0))],
        out_specs=pl.BlockSpec((tm, nd * rb), lambda i: (i, 0)),
        out_shape=jax.ShapeDtypeStruct((S, nd * rb), F32),
        compiler_params=_params("parallel"))(dy, w)


def _matmul_nt_cols(dp, w, w_block, w_index, n_red, tc, n_out, name):
    S = dp.shape[0]
    tm = min(512, S)

    def body(d_ref, w_ref, o_ref):
        @pl.when(pl.program_id(1) == 0)
        def _():
            o_ref[...] = jnp.zeros_like(o_ref)
        o_ref[...] += lax.dot_general(d_ref[...], w_ref[...], NT, preferred_element_type=F32)

    return pl.pallas_call(
        body, name=name, grid=(S // tm, n_red),
        in_specs=[pl.BlockSpec((tm, tc), lambda i, j: (i, j)),
                  pl.BlockSpec(w_block, w_index)],
        out_specs=pl.BlockSpec((tm, n_out), lambda i, j: (i, 0)),
        out_shape=jax.ShapeDtypeStruct((S, n_out), F32),
        compiler_params=_params("parallel", "arbitrary"))(dp, w)


def _matmul_tn(a, b, a_block, a_index, b_block, b_index, out_shape, out_block, out_index, acc_shape, n_outer, name):
    S = a.shape[0]
    ts = a_block[0]
    n_tok = S // ts

    def body(a_ref, b_ref, o_ref, acc_ref):
        s = pl.program_id(1)

        @pl.when(s == 0)
        def _():
            acc_ref[...] = jnp.zeros_like(acc_ref)
        acc_ref[...] += lax.dot_general(a_ref[...].astype(BF16), b_ref[...].astype(BF16), TN,
                                        preferred_element_type=F32)

        @pl.when(s == n_tok - 1)
        def _():
            o_ref[...] = acc_ref[...].astype(o_ref.dtype)

    return pl.pallas_call(
        body, name=name, grid=(n_outer, n_tok),
        in_specs=[pl.BlockSpec(a_block, a_index), pl.BlockSpec(b_block, b_index)],
        out_specs=pl.BlockSpec(out_block, out_index),
        out_shape=jax.ShapeDtypeStruct(out_shape, BF16),
        scratch_shapes=[pltpu.VMEM(acc_shape, F32)],
        compiler_params=_params("parallel", "arbitrary"))(a, b)


def _pool(scr_ref, u, row0, tm, E):
    gc = E // len(POOL_WINDOWS)
    t1 = row0 + lax.broadcasted_iota(jnp.int32, (tm, 1), 0) + 1
    out = []
    for g, win in enumerate(POOL_WINDOWS):
        cs = slice(g * gc, (g + 1) * gc)
        acc = u[:, cs]
        for k in range(1, win):
            acc = acc + scr_ref[pl.ds(POOL_HALO - k, tm), cs]
        count = jnp.minimum(t1, win).astype(F32)
        out.append(acc / count - u[:, cs])
    return out


def _a_mid_fwd(proj, wg, scale, name):
    S, E2 = proj.shape
    E = E2 // 2
    gc = E // len(POOL_WINDOWS)
    tm = min(256, S)
    hb = tm // POOL_HALO

    def body(u_ref, uh_ref, gt_ref, wg_ref, sc_ref, z_ref, scr_ref):
        i = pl.program_id(0)
        scr_ref[0:POOL_HALO, :] = jnp.where(i > 0, uh_ref[...], 0.0)
        u = u_ref[...]
        scr_ref[POOL_HALO:POOL_HALO + tm, :] = u
        pooled = _pool(scr_ref, u, i * tm, tm, E)
        for g in range(len(POOL_WINDOWS)):
            cs = slice(g * gc, (g + 1) * gc)
            y = jnp.dot(pooled[g].astype(BF16), wg_ref[g], preferred_element_type=F32) * sc_ref[:, cs]
            gate = gt_ref[:, cs]
            z_ref[:, cs] = (y * (gate * _sigmoid(gate))).astype(BF16)

    return pl.pallas_call(
        body, name=name, grid=(S // tm,),
        in_specs=[pl.BlockSpec((tm, E), lambda i: (i, 0)),
                  pl.BlockSpec((POOL_HALO, E), lambda i: (jnp.maximum(i * hb - 1, 0), 0)),
                  pl.BlockSpec((tm, E), lambda i: (i, 1)),
                  pl.BlockSpec((len(POOL_WINDOWS), gc, gc), lambda i: (0, 0, 0)),
                  pl.BlockSpec((1, E), lambda i: (0, 0))],
        out_specs=pl.BlockSpec((tm, E), lambda i: (i, 0)),
        out_shape=jax.ShapeDtypeStruct((S, E), BF16),
        scratch_shapes=[pltpu.VMEM((POOL_HALO + tm, E), F32)],
        compiler_params=_params("parallel"))(proj, proj, proj, wg, scale)


def _a_mid_bwd(dz, proj, wg, scale, name):
    S, E2 = proj.shape
    E = E2 // 2
    n_grp = len(POOL_WINDOWS)
    gc = E // n_grp
    tm = min(256, S)
    hb = tm // POOL_HALO
    n_tiles = S // tm
    last_halo = S // POOL_HALO - 1

    def body(dz_ref, dzh_ref, u_ref, uh_ref, gt_ref, gth_ref, wg_ref, sc_ref, dp_ref, dwg_ref, dsc_ref, scr_ref, q_ref):
        i = pl.program_id(0)

        @pl.when(i == 0)
        def _():
            dwg_ref[...] = jnp.zeros_like(dwg_ref)
            dsc_ref[...] = jnp.zeros_like(dsc_ref)

        scr_ref[0:POOL_HALO, :] = jnp.where(i > 0, uh_ref[...], 0.0)
        u = u_ref[...]
        scr_ref[POOL_HALO:POOL_HALO + tm, :] = u
        pooled = _pool(scr_ref, u, i * tm, tm, E)
        t1 = i * tm + lax.broadcasted_iota(jnp.int32, (tm, 1), 0) + 1
        t1h = (i + 1) * tm + lax.broadcasted_iota(jnp.int32, (POOL_HALO, 1), 0) + 1
        not_last = i < n_tiles - 1
        for g, win in enumerate(POOL_WINDOWS):
            cs = slice(g * gc, (g + 1) * gc)
            w = wg_ref[g]
            sc = sc_ref[:, cs]
            pb = pooled[g].astype(BF16)
            ypre = jnp.dot(pb, w, preferred_element_type=F32)
            gate = gt_ref[:, cs]
            sg = _sigmoid(gate)
            silu = gate * sg
            dzg = dz_ref[:, cs]
            dy = dzg * silu
            dp_ref[:, E + g * gc:E + (g + 1) * gc] = (dzg * (ypre * sc) * (sg * (1.0 + gate * (1.0 - sg)))).astype(BF16)
            dsc_ref[:, cs] += jnp.sum(dy * ypre, axis=0, keepdims=True)
            dyp = (dy * sc).astype(BF16)
            dwg_ref[g] += lax.dot_general(pb, dyp, TN, preferred_element_type=F32)
            dpool = lax.dot_general(dyp, w, NT, preferred_element_type=F32)
            gate_h = gth_ref[:, cs]
            dyp_h = (dzh_ref[:, cs] * (gate_h * _sigmoid(gate_h)) * sc).astype(BF16)
            dpool_h = lax.dot_general(dyp_h, w, NT, preferred_element_type=F32)
            q_ref[0:tm, cs] = dpool / jnp.minimum(t1, win).astype(F32)
            q_ref[tm:tm + POOL_HALO, cs] = jnp.where(not_last, dpool_h / jnp.minimum(t1h, win).astype(F32), 0.0)
            acc = q_ref[0:tm, cs] - dpool
            for k in range(1, win):
                acc = acc + q_ref[pl.ds(k, tm), cs]
            dp_ref[:, cs] = acc.astype(BF16)

    return pl.pallas_call(
        body, name=name, grid=(n_tiles,),
        in_specs=[pl.BlockSpec((tm, E), lambda i: (i, 0)),
                  pl.BlockSpec((POOL_HALO, E), lambda i: (jnp.minimum((i + 1) * hb, last_halo), 0)),
                  pl.BlockSpec((tm, E), lambda i: (i, 0)),
                  pl.BlockSpec((POOL_HALO, E), lambda i: (jnp.maximum(i * hb - 1, 0), 0)),
                  pl.BlockSpec((tm, E), lambda i: (i, 1)),
                  pl.BlockSpec((POOL_HALO, E), lambda i: (jnp.minimum((i + 1) * hb, last_halo), 1)),
                  pl.BlockSpec((n_grp, gc, gc), lambda i: (0, 0, 0)),
                  pl.BlockSpec((1, E), lambda i: (0, 0))],
        out_specs=[pl.BlockSpec((tm, E2), lambda i: (i, 0)),
                   pl.BlockSpec((n_grp, gc, gc), lambda i: (0, 0, 0)),
                   pl.BlockSpec((1, E), lambda i: (0, 0))],
        out_shape=[jax.ShapeDtypeStruct((S, E2), BF16),
                   jax.ShapeDtypeStruct((n_grp, gc, gc), F32),
                   jax.ShapeDtypeStruct((1, E), F32)],
        scratch_shapes=[pltpu.VMEM((POOL_HALO + tm, E), F32), pltpu.VMEM((tm + POOL_HALO, E), F32)],
        compiler_params=_params("arbitrary"))(dz, dz, proj, proj, proj, proj, wg, scale)


def _rope_kv(kvp, cos2, sin2, name):
    S, E2 = kvp.shape
    E = E2 // 2
    tm = min(256, S)

    def body(k_ref, v_ref, c_ref, s_ref, ko_ref, vo_ref):
        cosv, sinv = c_ref[...], s_ref[...]
        for h in range(E // HEAD_DIM):
            hs = slice(h * HEAD_DIM, (h + 1) * HEAD_DIM)
            ko_ref[:, hs] = _rope(k_ref[:, hs], cosv, sinv).astype(BF16)
        vo_ref[...] = v_ref[...].astype(BF16)

    return pl.pallas_call(
        body, name=name, grid=(S // tm,),
        in_specs=[pl.BlockSpec((tm, E), lambda i: (i, 0)), pl.BlockSpec((tm, E), lambda i: (i, 1)),
                  pl.BlockSpec((tm, HEAD_DIM), lambda i: (i, 0)), pl.BlockSpec((tm, HEAD_DIM), lambda i: (i, 0))],
        out_specs=[pl.BlockSpec((tm, E), lambda i: (i, 0)), pl.BlockSpec((tm, E), lambda i: (i, 0))],
        out_shape=[jax.ShapeDtypeStruct((S, E), BF16), jax.ShapeDtypeStruct((S, E), BF16)],
        compiler_params=_params("parallel"))(kvp, kvp, cos2, sin2)


def _band_mask(n):
    row = lax.broadcasted_iota(jnp.int32, (BAND, 2 * BAND), 0)
    col = lax.broadcasted_iota(jnp.int32, (BAND, 2 * BAND), 1)
    return (col >= row) & (col <= row + BAND) & ((n > 0) | (col >= BAND))


def _attn_fwd(proj, kr, vb, cos2, sin2, group, dil, name):
    S, PW = proj.shape
    E = kr.shape[1]
    H = E // HEAD_DIM
    M = S // dil
    nblk = M // BAND
    qcols = PW // E

    def body(q_ref, kc_ref, kp_ref, vc_ref, vp_ref, c_ref, s_ref, o_ref, l_ref):
        n = pl.program_id(1)
        band = _band_mask(n)
        cosv, sinv = c_ref[...], s_ref[...]
        lane = lax.broadcasted_iota(jnp.int32, (BAND, HEAD_DIM), 1)
        lse_tile = jnp.zeros((BAND, HEAD_DIM), F32)
        for h in range(H):
            hs = slice(h * HEAD_DIM, (h + 1) * HEAD_DIM)
            qr = _rope(q_ref[:, hs], cosv, sinv).astype(BF16)
            kcat = jnp.concatenate([kp_ref[:, hs], kc_ref[:, hs]], axis=0)
            vcat = jnp.concatenate([vp_ref[:, hs], vc_ref[:, hs]], axis=0)
            s = lax.dot_general(qr, kcat, NT, preferred_element_type=F32) * ATTN_SCALE
            s = jnp.where(band, s, NEG_INF)
            m = jnp.max(s, axis=-1, keepdims=True)
            p = jnp.exp(s - m)
            l = jnp.sum(p, axis=-1, keepdims=True)
            o_ref[:, hs] = jnp.dot(p.astype(BF16), vcat, preferred_element_type=F32) / l
            lse_tile = jnp.where(lane == h, m + jnp.log(l), lse_tile)
        l_ref[...] = lse_tile

    cur = lambda r, n: (n, r)
    prev = lambda r, n: (jnp.maximum(n - 1, 0), r)
    out, lse = pl.pallas_call(
        body, name=name, grid=(dil, nblk),
        in_specs=[pl.BlockSpec((BAND, E), lambda r, n: (n, r * qcols + group)),
                  pl.BlockSpec((BAND, E), cur), pl.BlockSpec((BAND, E), prev),
                  pl.BlockSpec((BAND, E), cur), pl.BlockSpec((BAND, E), prev),
                  pl.BlockSpec((BAND, HEAD_DIM), cur), pl.BlockSpec((BAND, HEAD_DIM), cur)],
        out_specs=[pl.BlockSpec((BAND, E), cur), pl.BlockSpec((BAND, HEAD_DIM), cur)],
        out_shape=[jax.ShapeDtypeStruct((M, dil * E), F32), jax.ShapeDtypeStruct((M, dil * HEAD_DIM), F32)],
        compiler_params=_params("parallel", "arbitrary"))(
            proj.reshape(M, dil * PW), kr.reshape(M, dil * E), kr.reshape(M, dil * E),
            vb.reshape(M, dil * E), vb.reshape(M, dil * E),
            cos2.reshape(M, dil * HEAD_DIM), sin2.reshape(M, dil * HEAD_DIM))
    return out.reshape(S, E), lse.reshape(S, HEAD_DIM)


def _attn_bwd(proj, kr, vb, cos2, sin2, do, lse, dlt, group, dil, name):
    S, PW = proj.shape
    E = kr.shape[1]
    H = E // HEAD_DIM
    M = S // dil
    nblk = M // BAND
    qcols = PW // E

    def body(q_ref, kc_ref, kp_ref, vc_ref, vp_ref, c_ref, s_ref, do_ref, l_ref, dl_ref,
             dq_ref, dk_ref, dv_ref, ck_ref, cv_ref):
        n = pl.program_id(1)

        @pl.when(n == 0)
        def _():
            ck_ref[...] = jnp.zeros_like(ck_ref)
            cv_ref[...] = jnp.zeros_like(cv_ref)

        @pl.when(n < nblk)
        def _():
            band = _band_mask(n)
            cosv, sinv = c_ref[...], s_ref[...]
            for h in range(H):
                hs = slice(h * HEAD_DIM, (h + 1) * HEAD_DIM)
                qr = _rope(q_ref[:, hs], cosv, sinv).astype(BF16)
                kcat = jnp.concatenate([kp_ref[:, hs], kc_ref[:, hs]], axis=0)
                vcat = jnp.concatenate([vp_ref[:, hs], vc_ref[:, hs]], axis=0)
                s = lax.dot_general(qr, kcat, NT, preferred_element_type=F32) * ATTN_SCALE
                s = jnp.where(band, s, NEG_INF)
                p = jnp.exp(s - l_ref[:, h:h + 1])
                dob = do_ref[:, hs]
                dpr = lax.dot_general(dob, vcat, NT, preferred_element_type=F32)
                ds = (p * (dpr - dl_ref[:, h:h + 1]) * ATTN_SCALE).astype(BF16)
                dq = jnp.dot(ds, kcat, preferred_element_type=F32)
                dq_ref[:, hs] = _rope_bwd(dq, cosv, sinv).astype(BF16)
                dkc = lax.dot_general(ds, qr, TN, preferred_element_type=F32)
                dvc = lax.dot_general(p.astype(BF16), dob, TN, preferred_element_type=F32)
                dk_ref[:, hs] = ck_ref[:, hs] + dkc[0:BAND]
                ck_ref[:, hs] = dkc[BAND:2 * BAND]
                dv_ref[:, hs] = cv_ref[:, hs] + dvc[0:BAND]
                cv_ref[:, hs] = dvc[BAND:2 * BAND]

        @pl.when(n == nblk)
        def _():
            dk_ref[...] = ck_ref[...]
            dv_ref[...] = cv_ref[...]

    last = nblk - 1
    cur = lambda r, n: (jnp.minimum(n, last), r)
    prev = lambda r, n: (jnp.maximum(jnp.minimum(n, last) - 1, 0), r)
    late = lambda r, n: (jnp.maximum(n - 1, 0), r)
    dq, dk, dv = pl.pallas_call(
        body, name=name, grid=(dil, nblk + 1),
        in_specs=[pl.BlockSpec((BAND, E), lambda r, n: (jnp.minimum(n, last), r * qcols + group)),
                  pl.BlockSpec((BAND, E), cur), pl.BlockSpec((BAND, E), prev),
                  pl.BlockSpec((BAND, E), cur), pl.BlockSpec((BAND, E), prev),
                  pl.BlockSpec((BAND, HEAD_DIM), cur), pl.BlockSpec((BAND, HEAD_DIM), cur),
                  pl.BlockSpec((BAND, E), cur),
                  pl.BlockSpec((BAND, HEAD_DIM), cur), pl.BlockSpec((BAND, HEAD_DIM), cur)],
        out_specs=[pl.BlockSpec((BAND, E), cur), pl.BlockSpec((BAND, E), late), pl.BlockSpec((BAND, E), late)],
        out_shape=[jax.ShapeDtypeStruct((M, dil * E), BF16), jax.ShapeDtypeStruct((M, dil * E), F32),
                   jax.ShapeDtypeStruct((M, dil * E), F32)],
        scratch_shapes=[pltpu.VMEM((BAND, E), F32), pltpu.VMEM((BAND, E), F32)],
        compiler_params=_params("parallel", "arbitrary"))(
            proj.reshape(M, dil * PW), kr.reshape(M, dil * E), kr.reshape(M, dil * E),
            vb.reshape(M, dil * E), vb.reshape(M, dil * E),
            cos2.reshape(M, dil * HEAD_DIM), sin2.reshape(M, dil * HEAD_DIM),
            do.reshape(M, dil * E), lse.reshape(M, dil * HEAD_DIM), dlt.reshape(M, dil * HEAD_DIM))
    return dq.reshape(S, E), dk.reshape(S, E), dv.reshape(S, E)


def _group_weights(l_refs, h):
    ls = [r[:, h:h + 1] for r in l_refs]
    mx = jnp.maximum(jnp.maximum(ls[0], ls[1]), ls[2])
    es = [jnp.exp(l - mx) for l in ls]
    inv = 1.0 / (es[0] + es[1] + es[2])
    return [e * inv for e in es]


def _merge_fwd(outs, lses, proj, name):
    S, E = outs[0].shape
    tm = min(256, S)
    gate_col = proj.shape[1] // E - 1

    def body(o0, o1, o2, l0, l1, l2, gt_ref, z_ref):
        for h in range(E // HEAD_DIM):
            hs = slice(h * HEAD_DIM, (h + 1) * HEAD_DIM)
            a = _group_weights((l0, l1, l2), h)
            merged = a[0] * o0[:, hs] + a[1] * o1[:, hs] + a[2] * o2[:, hs]
            gate = gt_ref[:, hs]
            z_ref[:, hs] = (merged * (gate * _sigmoid(gate))).astype(BF16)

    wide = pl.BlockSpec((tm, E), lambda i: (i, 0))
    thin = pl.BlockSpec((tm, HEAD_DIM), lambda i: (i, 0))
    return pl.pallas_call(
        body, name=name, grid=(S // tm,),
        in_specs=[wide, wide, wide, thin, thin, thin, pl.BlockSpec((tm, E), lambda i: (i, gate_col))],
        out_specs=wide,
        out_shape=jax.ShapeDtypeStruct((S, E), BF16),
        compiler_params=_params("parallel"))(*outs, *lses, proj)


def _merge_bwd(dz, outs, lses, proj, name):
    S, E = outs[0].shape
    tm = min(256, S)
    gate_col = proj.shape[1] // E - 1

    def body(dz_ref, o0, o1, o2, l0, l1, l2, gt_ref, d0, d1, d2, t0, t1, t2, dg_ref):
        o_refs, d_refs, t_refs = (o0, o1, o2), (d0, d1, d2), (t0, t1, t2)
        lane = lax.broadcasted_iota(jnp.int32, (tm, HEAD_DIM), 1)
        tiles = [jnp.zeros((tm, HEAD_DIM), F32) for _ in range(3)]
        for h in range(E // HEAD_DIM):
            hs = slice(h * HEAD_DIM, (h + 1) * HEAD_DIM)
            a = _group_weights((l0, l1, l2), h)
            merged = a[0] * o0[:, hs] + a[1] * o1[:, hs] + a[2] * o2[:, hs]
            gate = gt_ref[:, hs]
            sg = _sigmoid(gate)
            dzh = dz_ref[:, hs]
            dmerged = dzh * (gate * sg)
            dg_ref[:, hs] = (dzh * merged * (sg * (1.0 + gate * (1.0 - sg)))).astype(BF16)
            tot = jnp.sum(dmerged * merged, axis=-1, keepdims=True)
            for g in range(3):
                d_refs[g][:, hs] = (a[g] * dmerged).astype(BF16)
                tiles[g] = jnp.where(lane == h, a[g] * tot, tiles[g])
        for g in range(3):
            t_refs[g][...] = tiles[g]

    wide = pl.BlockSpec((tm, E), lambda i: (i, 0))
    thin = pl.BlockSpec((tm, HEAD_DIM), lambda i: (i, 0))
    res = pl.pallas_call(
        body, name=name, grid=(S // tm,),
        in_specs=[wide, wide, wide, wide, thin, thin, thin, pl.BlockSpec((tm, E), lambda i: (i, gate_col))],
        out_specs=[wide, wide, wide, thin, thin, thin, wide],
        out_shape=[jax.ShapeDtypeStruct((S, E), BF16)] * 3 + [jax.ShapeDtypeStruct((S, HEAD_DIM), F32)] * 3
        + [jax.ShapeDtypeStruct((S, E), BF16)],
        compiler_params=_params("parallel"))(dz, *outs, *lses, proj)
    return res[0:3], res[3:6], res[6]


def _kv_bwd(dks, dvs, cos2, sin2, name):
    S, E = dks[0].shape
    n = len(dks)
    tm = min(128, S)

    def body(*refs):
        dk_refs, dv_refs = refs[0:n], refs[n:2 * n]
        c_ref, s_ref, o_ref = refs[2 * n:]
        cosv, sinv = c_ref[...], s_ref[...]
        for h in range(E // HEAD_DIM):
            hs = slice(h * HEAD_DIM, (h + 1) * HEAD_DIM)
            dk = dk_refs[0][:, hs]
            dv = dv_refs[0][:, hs]
            for j in range(1, n):
                dk = dk + dk_refs[j][:, hs]
                dv = dv + dv_refs[j][:, hs]
            o_ref[:, hs] = _rope_bwd(dk, cosv, sinv).astype(BF16)
            o_ref[:, E + h * HEAD_DIM:E + (h + 1) * HEAD_DIM] = dv.astype(BF16)

    wide = pl.BlockSpec((tm, E), lambda i: (i, 0))
    thin = pl.BlockSpec((tm, HEAD_DIM), lambda i: (i, 0))
    return pl.pallas_call(
        body, name=name, grid=(S // tm,),
        in_specs=[wide] * (2 * n) + [thin, thin],
        out_specs=pl.BlockSpec((tm, 2 * E), lambda i: (i, 0)),
        out_shape=jax.ShapeDtypeStruct((S, 2 * E), BF16),
        compiler_params=_params("parallel"))(*dks, *dvs, cos2, sin2)


def _norm_bwd(dhn, x, gain, dres, name):
    S, D = x.shape
    tm = min(256, S)

    def body(dh_ref, x_ref, g_ref, r_ref, dx_ref, dg_ref):
        @pl.when(pl.program_id(0) == 0)
        def _():
            dg_ref[...] = jnp.zeros_like(dg_ref)
        xf = x_ref[...]
        inv = lax.rsqrt(jnp.mean(xf * xf, axis=-1, keepdims=True) + RMS_EPS)
        xhat = xf * inv
        dh = dh_ref[...]
        dg_ref[...] += jnp.sum(dh * xhat, axis=0, keepdims=True)
        dxh = dh * g_ref[...]
        dx_ref[...] = r_ref[...] + inv * (dxh - xhat * jnp.mean(dxh * xhat, axis=-1, keepdims=True))

    tile = pl.BlockSpec((tm, D), lambda i: (i, 0))
    vec = pl.BlockSpec((1, D), lambda i: (0, 0))
    return pl.pallas_call(
        body, name=name, grid=(S // tm,),
        in_specs=[tile, tile, vec, tile],
        out_specs=[tile, vec],
        out_shape=[jax.ShapeDtypeStruct((S, D), F32), jax.ShapeDtypeStruct((1, D), F32)],
        compiler_params=_params("arbitrary"))(dhn, x, gain, dres)


def _final_norm_loss(x, target, gain, name):
    S, D = x.shape
    tm = min(256, S)

    def body(x_ref, t_ref, g_ref, loss_ref, dx_ref, dg_ref):
        @pl.when(pl.program_id(0) == 0)
        def _():
            loss_ref[...] = jnp.zeros_like(loss_ref)
            dg_ref[...] = jnp.zeros_like(dg_ref)
        xf = x_ref[...]
        inv = lax.rsqrt(jnp.mean(xf * xf, axis=-1, keepdims=True) + RMS_EPS)
        xhat = xf * inv
        g = g_ref[...]
        err = xhat * g - t_ref[...]
        loss_ref[...] += 0.5 * jnp.sum(jnp.mean(err * err, axis=-1, keepdims=True), axis=0, keepdims=True)
        dy = err / D
        dg_ref[...] += jnp.sum(dy * xhat, axis=0, keepdims=True)
        dxh = dy * g
        dx_ref[...] = inv * (dxh - xhat * jnp.mean(dxh * xhat, axis=-1, keepdims=True))

    tile = pl.BlockSpec((tm, D), lambda i: (i, 0))
    vec = pl.BlockSpec((1, D), lambda i: (0, 0))
    return pl.pallas_call(
        body, name=name, grid=(S // tm,),
        in_specs=[tile, tile, vec],
        out_specs=[pl.BlockSpec((1, 1), lambda i: (0, 0)), tile, vec],
        out_shape=[jax.ShapeDtypeStruct((1, 1), F32), jax.ShapeDtypeStruct((S, D), F32),
                   jax.ShapeDtypeStruct((1, D), F32)],
        compiler_params=_params("arbitrary"))(x, target, gain)


def _adamw_math(g, w, m, v):
    m = ADAM_B1 * m + (1.0 - ADAM_B1) * g
    v = ADAM_B2 * v + (1.0 - ADAM_B2) * (g * g)
    m_hat = m / (1.0 - ADAM_B1 ** ADAM_STEP)
    v_hat = v / (1.0 - ADAM_B2 ** ADAM_STEP)
    delta = -ADAM_LR * (m_hat / (jnp.sqrt(v_hat) + ADAM_EPS) + ADAM_WD * w)
    return delta, m, v


def _adamw_rows(g, w, m, v, name):
    def body(g_ref, w_ref, m_ref, v_ref, d_ref, mo_ref, vo_ref):
        d_ref[...], mo_ref[...], vo_ref[...] = _adamw_math(g_ref[...], w_ref[...], m_ref[...], v_ref[...])

    whole = pl.BlockSpec(memory_space=pltpu.VMEM)
    return pl.pallas_call(
        body, name=name, in_specs=[whole] * 4, out_specs=[whole] * 3,
        out_shape=[jax.ShapeDtypeStruct(g.shape, F32)] * 3)(g, w, m, v)


def _adamw_blocks(own, others, w, m, v, layer, earlier, name):
    L, R, C = w.shape
    n = others.shape[0]
    tr = R
    while tr * C > 128 * 1024 and tr % 16 == 0:
        tr //= 2

    def body(o_ref, p_ref, w_ref, m_ref, v_ref, *rest):
        g_ref, d_ref, mo_ref, vo_ref = rest[-4:]
        g = o_ref[...].astype(F32)
        for j in range(n):
            g = g + p_ref[j].astype(F32)
        g_ref[...] = g
        d_ref[...], mo_ref[...], vo_ref[...] = _adamw_math(g, w_ref[...], m_ref[...], v_ref[...])

    tile = pl.BlockSpec((None, tr, C), lambda i: (layer, i, 0))
    kept = [] if earlier is None else list(earlier)
    return pl.pallas_call(
        body, name=name, grid=(R // tr,),
        in_specs=[pl.BlockSpec((tr, C), lambda i: (i, 0)), pl.BlockSpec((n, tr, C), lambda i: (0, i, 0)),
                  tile, tile, tile] + [ANY] * len(kept),
        out_specs=[tile] * 4,
        out_shape=[jax.ShapeDtypeStruct((L, R, C), F32)] * 4,
        input_output_aliases={5 + j: j for j in range(len(kept))},
        compiler_params=_params("parallel"))(own, others, w, m, v, *kept)


def _position():
    return lax.axis_index("x"), lax.axis_index("y"), lax.axis_index("c")


def _block_index(px, py, pc):
    return 4 * px + 2 * py + pc


def _all_gather(shards, name):
    n = len(shards)

    def body(*refs):
        ins, outs = refs[0:n], refs[n:2 * n]
        send_sems, recv_sems, local_sems = refs[2 * n:]
        x, y, c = _position()
        me, sibling = (x, y, c), (x, y, 1 - c)
        chips = [(1 - x, y), (x, 1 - y), (1 - x, 1 - y)]

        def copy(a, k, block, to, src=None):
            rows = outs[a].at[_block_index(*block)]
            return pltpu.make_async_remote_copy(
                src_ref=rows if src is None else src, dst_ref=rows,
                send_sem=send_sems.at[a, k], recv_sem=recv_sems.at[a, k], device_id=to, device_id_type=MESH)

        mine, first, passed = [], [], []
        for a in range(n):
            cp = pltpu.make_async_copy(ins[a], outs[a].at[_block_index(*me)], local_sems.at[a])
            cp.start()
            mine.append(cp)
            first.append(copy(a, 0, me, sibling, src=ins[a]))
            first += [copy(a, 1 + j, me, (*chip, c), src=ins[a]) for j, chip in enumerate(chips)]
        for cp in first:
            cp.start()
        for j, chip in enumerate(chips):
            for a in range(n):
                copy(a, 1 + j, (*chip, c), me).wait_recv()
                fwd = copy(a, 4 + j, (*chip, c), sibling)
                fwd.start()
                passed.append(fwd)
        for a in range(n):
            copy(a, 0, sibling, me).wait_recv()
            for j, chip in enumerate(chips):
                copy(a, 4 + j, (*chip, 1 - c), me).wait_recv()
        for cp in first + passed:
            cp.wait_send()
        for cp in mine:
            cp.wait()

    return pl.pallas_call(
        body, name=name,
        in_specs=[ANY] * n, out_specs=[ANY] * n,
        out_shape=[jax.ShapeDtypeStruct((N_DEV,) + s.shape, s.dtype) for s in shards],
        scratch_shapes=[pltpu.SemaphoreType.DMA((n, 7)), pltpu.SemaphoreType.DMA((n, 7)),
                        pltpu.SemaphoreType.DMA((n,))],
    )(*shards)


def _peers(x, y, c):
    return [((1 - x) if k & 4 else x, (1 - y) if k & 2 else y, (1 - c) if k & 1 else c) for k in range(1, N_DEV)]


HBM = pl.BlockSpec(memory_space=pltpu.HBM)
SEM = pl.BlockSpec(memory_space=pltpu.SEMAPHORE)
EFFECT = pltpu.SideEffectType.DATAFLOW_SIDE_EFFECTING


def _push_copy(src_refs, land_refs, send_sems, recv_sems, a, k, peer, per_peer, by_sender, arriving):
    me_idx, p_idx = _block_index(*_position()), _block_index(*peer)
    src = src_refs[a].at[p_idx] if per_peer else src_refs[a]
    if by_sender:
        slot = p_idx if arriving else me_idx
    else:
        slot = k
    return pltpu.make_async_remote_copy(
        src_ref=src, dst_ref=land_refs[a].at[slot], send_sem=send_sems.at[a * (N_DEV - 1) + k],
        recv_sem=recv_sems.at[a * (N_DEV - 1) + k], device_id=peer, device_id_type=MESH)


def _push_start(srcs, lands, per_peer, by_sender, after, name):
    n = len(srcs)

    def body(*refs):
        src_refs, land_refs = refs[0:n], refs[n:2 * n]
        send_sems, recv_sems = refs[2 * n + 1], refs[2 * n + 2]
        token = refs[-1]
        for a in range(n):
            for k, peer in enumerate(_peers(*_position())):
                _push_copy(src_refs, land_refs, send_sems, recv_sems, a, k, peer, per_peer, by_sender, False).start()
        token[...] = jnp.zeros_like(token)

    args = [pltpu.with_memory_space_constraint(t, pltpu.HBM) for t in list(srcs) + list(lands)]
    res = pl.pallas_call(
        body, name=name,
        in_specs=[HBM] * (2 * n) + [ANY],
        out_specs=[SEM, SEM] + [HBM] * (2 * n) + [pl.BlockSpec(memory_space=pltpu.VMEM)],
        out_shape=[pltpu.SemaphoreType.DMA((n * (N_DEV - 1),)), pltpu.SemaphoreType.DMA((n * (N_DEV - 1),))]
        + [pltpu.HBM(t.shape, t.dtype) for t in args] + [jax.ShapeDtypeStruct((8, 128), F32)],
        input_output_aliases={i: 2 + i for i in range(2 * n)},
        compiler_params=pltpu.CompilerParams(has_side_effects=EFFECT))(*args, after)
    return res[0], res[1], res[2:2 + n], res[2 + n:2 + 2 * n], res[-1]


def _push_wait(started, per_peer, by_sender, after, name):
    send_sems, recv_sems, srcs, lands, _ = started
    n = len(srcs)

    def body(*refs):
        src_refs, land_refs = refs[0:n], refs[n:2 * n]
        send_s, recv_s = refs[2 * n], refs[2 * n + 1]
        for a in range(n):
            for k, peer in enumerate(_peers(*_position())):
                _push_copy(src_refs, land_refs, send_s, recv_s, a, k, peer, per_peer, by_sender, False).wait_send()
                _push_copy(src_refs, land_refs, send_s, recv_s, a, k, peer, per_peer, by_sender, True).wait_recv()

    res = pl.pallas_call(
        body, name=name,
        in_specs=[HBM] * (2 * n) + [SEM, SEM, ANY],
        out_specs=[HBM] * (2 * n),
        out_shape=[pltpu.HBM(t.shape, t.dtype) for t in list(srcs) + list(lands)],
        input_output_aliases={i: i for i in range(2 * n)},
        compiler_params=pltpu.CompilerParams(has_side_effects=EFFECT))(*srcs, *lands, send_sems, recv_sems, after)
    return res[0:n], res[n:2 * n]


def _all_reduce_rows(v, name):
    R, D = v.shape

    def body(v_ref, o_ref, buf_ref, send_sems, recv_sems):
        x, y, c = _position()
        me_idx = _block_index(x, y, c)
        buf_ref[me_idx] = v_ref[...]
        copies = []
        for k in range(1, N_DEV):
            px = (1 - x) if k & 4 else x
            py = (1 - y) if k & 2 else y
            pc = (1 - c) if k & 1 else c
            rc = pltpu.make_async_remote_copy(
                src_ref=v_ref, dst_ref=buf_ref.at[me_idx],
                send_sem=send_sems.at[k - 1], recv_sem=recv_sems.at[k - 1],
                device_id=(px, py, pc), device_id_type=MESH)
            rc.start()
            copies.append((rc, pltpu.make_async_remote_copy(
                src_ref=v_ref, dst_ref=buf_ref.at[_block_index(px, py, pc)],
                send_sem=send_sems.at[k - 1], recv_sem=recv_sems.at[k - 1],
                device_id=(px, py, pc), device_id_type=MESH)))
        for rc, arrival in copies:
            rc.wait_send()
            arrival.wait_recv()
        acc = buf_ref[0]
        for j in range(1, N_DEV):
            acc = acc + buf_ref[j]
        o_ref[...] = acc

    return pl.pallas_call(
        body, name=name,
        in_specs=[pl.BlockSpec(memory_space=pltpu.VMEM)],
        out_specs=pl.BlockSpec(memory_space=pltpu.VMEM),
        out_shape=jax.ShapeDtypeStruct((R, D), F32),
        scratch_shapes=[pltpu.VMEM((N_DEV, R, D), F32),
                        pltpu.SemaphoreType.DMA((7,)), pltpu.SemaphoreType.DMA((7,))],
    )(v)


def _rope_tables(S):
    inv_freq = 1.0 / (ROPE_THETA ** (jnp.arange(0, HEAD_DIM, 2, dtype=F32) / HEAD_DIM))
    ang = jnp.arange(S, dtype=F32)[:, None] * inv_freq[None, :]
    cos, sin = jnp.cos(ang), jnp.sin(ang)
    return jnp.concatenate([cos, cos], axis=1), jnp.concatenate([-sin, sin], axis=1)


def _local_step(xs, target, vecs, n_a, n_b, get_weights, put_grads):
    S, D = xs.shape
    E = D
    cos2, sin2 = _rope_tables(S)
    ts = min(512, S)

    def col_blocks(w):
        cb = w.shape[2]
        tn = min(cb, 1024)
        per = cb // tn
        return (None, D, tn), (lambda i, j: (j // per, 0, j % per)), N_DEV * per, tn

    def grad_in(hn, dproj, cb, name):
        return _matmul_tn(hn, dproj, (ts, D), lambda j, s: (s, 0), (ts, cb), lambda j, s: (s, j),
                          (N_DEV, D, cb), (None, D, cb), lambda j, s: (j, 0, 0), (D, cb), N_DEV, name)

    def grad_out(z, dx, name, col=0):
        rows = z.shape[1]
        ta = min(1024, rows)
        out = _matmul_tn(z, dx, (ts, ta), lambda a, s: (s, a), (ts, E), lambda a, s: (s, col),
                         (rows, E), (ta, E), lambda a, s: (a, 0), (ta, E), rows // ta, name)
        return out.reshape(N_DEV, rows // N_DEV, E)

    x = xs
    a_saved, b_saved = [], []
    for i in range(n_a):
        w = get_weights(f"a{i}", x)
        blk, idx, nblocks, tn = col_blocks(w["w_in"])
        proj, hn = _norm_matmul(x, vecs["norm_a"][i:i + 1], w["w_in"], blk, idx, nblocks, tn, f"a{i}_in")
        z = _a_mid_fwd(proj, w["w_grp"], vecs["scale_a"][i:i + 1], f"a{i}_mid")
        x_next = _matmul_res(z, w["w_out"], x, f"a{i}_out")
        a_saved.append((x, hn, proj, z, w))
        x = x_next
    x_kv = x
    w_kv = get_weights("kv", x)["w_kv"]
    tn = min(E, 1024)
    kvp, hn_kv = _norm_matmul(x, vecs["norm_kv"], w_kv, (D, tn), lambda i, j: (0, j), 2 * E // tn, tn, "kv_in")
    kr, vb = _rope_kv(kvp, cos2, sin2, "kv_rope")
    after = kr
    for i in range(n_b):
        w = get_weights(f"b{i}", after)
        blk, idx, nblocks, tn = col_blocks(w["w_in"])
        proj, hn = _norm_matmul(x, vecs["norm_b"][i:i + 1], w["w_in"], blk, idx, nblocks, tn, f"b{i}_in")
        outs, lses = [], []
        for g, dil in enumerate(DILATIONS):
            o, l = _attn_fwd(proj, kr, vb, cos2, sin2, g, dil, f"b{i}_attn{g}")
            outs.append(o)
            lses.append(l)
        z = _merge_fwd(outs, lses, proj, f"b{i}_merge")
        x_next = _matmul_res(z, w["w_out"], x, f"b{i}_out")
        b_saved.append((x, hn, proj, z, outs, lses, w))
        x = x_next
        after = x
    loss, dx, dg_f = _final_norm_loss(x, target, vecs["norm_f"], "final")

    vec = {"norm_a": [None] * n_a, "scale_a": [None] * n_a, "norm_b": [None] * n_b, "norm_f": [dg_f]}
    dks, dvs = [], []
    for i in reversed(range(n_b)):
        x_in, hn, proj, z, outs, lses, w = b_saved[i]
        dw_out = grad_out(z, dx, f"b{i}_dwout")
        dz = _matmul_nt_rows(dx, w["w_out"], f"b{i}_dz")
        dos, dlts, dgate = _merge_bwd(dz, outs, lses, proj, f"b{i}_dmerge")
        dqs = []
        for g, dil in enumerate(DILATIONS):
            dq, dk, dv = _attn_bwd(proj, kr, vb, cos2, sin2, dos[g], lses[g], dlts[g], g, dil, f"b{i}_dattn{g}")
            dqs.append(dq)
            dks.append(dk)
            dvs.append(dv)
        dproj = jnp.concatenate(dqs + [dgate], axis=1)
        cb = w["w_in"].shape[2]
        tok = put_grads(f"b{i}", {"w_out": dw_out, "w_in": grad_in(hn, dproj, cb, f"b{i}_dwin")})
        dhn = _matmul_nt_cols(dproj, w["w_in"], (None, D, cb), lambda t, j: (j, 0, 0), N_DEV, cb, D, f"b{i}_dhn")
        dx, vec["norm_b"][i] = _norm_bwd(dhn, x_in, vecs["norm_b"][i:i + 1] + tok[0:1, 0:1], dx, f"b{i}_dnorm")

    dkv = _kv_bwd(dks, dvs, cos2, sin2, "kv_dsum")
    tok = put_grads("kv", {"w_k": grad_out(hn_kv, dkv, "kv_dwk", 0), "w_v": grad_out(hn_kv, dkv, "kv_dwv", 1)})
    dhn = _matmul_nt_cols(dkv, w_kv, (D, E), lambda t, j: (0, j), 2, E, D, "kv_dhn")
    dx, dg_kv = _norm_bwd(dhn, x_kv, vecs["norm_kv"] + tok[0:1, 0:1], dx, "kv_dnorm")
    vec["norm_kv"] = [dg_kv]

    for i in reversed(range(n_a)):
        x_in, hn, proj, z, w = a_saved[i]
        dw_out = grad_out(z, dx, f"a{i}_dwout")
        dz = _matmul_nt_rows(dx, w["w_out"], f"a{i}_dz")
        dproj, dwg, dsc = _a_mid_bwd(dz, proj, w["w_grp"], vecs["scale_a"][i:i + 1], f"a{i}_dmid")
        n_grp, gc, _ = dwg.shape
        dwg = dwg.reshape(n_grp, N_DEV, gc // N_DEV, gc).transpose(1, 0, 2, 3).astype(BF16)
        vec["scale_a"][i] = dsc
        cb = w["w_in"].shape[2]
        tok = put_grads(f"a{i}", {"w_out": dw_out, "w_grp": dwg, "w_in": grad_in(hn, dproj, cb, f"a{i}_dwin")})
        dhn = _matmul_nt_cols(dproj, w["w_in"], (None, D, cb), lambda t, j: (j, 0, 0), N_DEV, cb, D, f"a{i}_dhn")
        dx, vec["norm_a"][i] = _norm_bwd(dhn, x_in, vecs["norm_a"][i:i + 1] + tok[0:1, 0:1], dx, f"a{i}_dnorm")

    return loss, dx, {k: jnp.concatenate(v, axis=0) for k, v in vec.items()}


VECTORS = ("norm_a", "scale_a", "norm_kv", "norm_b", "norm_f")
SHARDED_VECTORS = ("norm_a", "scale_a")
GROUPS = {
    "a0": (("w_in", "w_in_a", 0), ("w_grp", "w_grp_a", 0), ("w_out", "w_out_a", 0)),
    "a1": (("w_in", "w_in_a", 1), ("w_grp", "w_grp_a", 1), ("w_out", "w_out_a", 1)),
    "kv": (("w_k", "w_k", None), ("w_v", "w_v", None)),
    "b0": (("w_in", "w_in_b", 0), ("w_out", "w_out_b", 0)),
    "b1": (("w_in", "w_in_b", 1), ("w_out", "w_out_b", 1)),
}
PREFETCHED = ("a1", "kv", "b0", "b1")


def kernel(x, norm_a, w_in_a, w_grp_a, scale_a, w_out_a, norm_kv, w_k, w_v, norm_b, w_in_b, w_out_b, norm_f, loss_target, m_norm_a, m_w_in_a, m_w_grp_a, m_scale_a, m_w_out_a, m_norm_kv, m_w_k, m_w_v, m_norm_b, m_w_in_b, m_w_out_b, m_norm_f, v_norm_a, v_w_in_a, v_w_grp_a, v_scale_a, v_w_out_a, v_norm_kv, v_w_k, v_w_v, v_norm_b, v_w_in_b, v_w_out_b, v_norm_f):
    w = dict(norm_a=norm_a, w_in_a=w_in_a, w_grp_a=w_grp_a, scale_a=scale_a, w_out_a=w_out_a, norm_kv=norm_kv,
             w_k=w_k, w_v=w_v, norm_b=norm_b, w_in_b=w_in_b, w_out_b=w_out_b, norm_f=norm_f)
    m = dict(norm_a=m_norm_a, w_in_a=m_w_in_a, w_grp_a=m_w_grp_a, scale_a=m_scale_a, w_out_a=m_w_out_a,
             norm_kv=m_norm_kv, w_k=m_w_k, w_v=m_w_v, norm_b=m_norm_b, w_in_b=m_w_in_b, w_out_b=m_w_out_b,
             norm_f=m_norm_f)
    v = dict(norm_a=v_norm_a, w_in_a=v_w_in_a, w_grp_a=v_w_grp_a, scale_a=v_scale_a, w_out_a=v_w_out_a,
             norm_kv=v_norm_kv, w_k=v_w_k, w_v=v_w_v, norm_b=v_norm_b, w_in_b=v_w_in_b, w_out_b=v_w_out_b,
             norm_f=v_norm_f)
    D = x.shape[2]
    me = _block_index(*_position())

    def shard(group):
        return [w[p].astype(BF16) if layer is None else w[p][layer].astype(BF16) for _, p, layer in GROUPS[group]]

    def as_weights(group, gathered):
        out = dict(zip([n for n, _, _ in GROUPS[group]], gathered))
        if "w_grp" in out:
            g = out["w_grp"]
            out["w_grp"] = g.transpose(1, 0, 2, 3).reshape(g.shape[1], g.shape[3], g.shape[3])
        if "w_k" in out:
            out = {"w_kv": jnp.concatenate([out["w_k"].reshape(D, D), out["w_v"].reshape(D, D)], axis=1)}
        return out

    first = _all_gather(shard("a0") + [w[k] for k in SHARDED_VECTORS], "gather_first")
    n_first = len(GROUPS["a0"])
    vecs = {k: g.transpose(1, 0, 2).reshape(w[k].shape[0], D) for k, g in zip(SHARDED_VECTORS, first[n_first:])}
    vecs.update(norm_kv=norm_kv[None, :], norm_b=norm_b, norm_f=norm_f[None, :])
    srcs, lands = [], []
    for group in PREFETCHED:
        for s in shard(group):
            srcs.append(s)
            lands.append(lax.dynamic_update_index_in_dim(lax.empty((N_DEV,) + s.shape, s.dtype), s[None], me, 0))
    inflight, at = {}, 0
    token = None
    for group in PREFETCHED:
        n = len(GROUPS[group])
        inflight[group] = _push_start(srcs[at:at + n], lands[at:at + n], False, True,
                                      first[0] if token is None else token, f"gather_{group}_start")
        token = inflight[group][4]
        at += n
    vecs["norm_a"] = vecs["norm_a"] + token[0:1, 0:1]

    def get_weights(group, after):
        if group == "a0":
            return as_weights(group, first[0:n_first])
        return as_weights(group, _push_wait(inflight[group], False, True, after, f"gather_{group}_wait")[1])

    sent = {}

    def put_grads(group, grads):
        blocks = [grads[n] for n, _, _ in GROUPS[group]]
        lands = [lax.empty((N_DEV - 1,) + b.shape[1:], b.dtype) for b in blocks]
        sent[group] = _push_start(blocks, lands, True, False, jnp.zeros((8, 128), F32), f"exchange_{group}_start")
        return sent[group][4]

    loss, dx, vec = _local_step(x[0], loss_target[0], vecs, w_in_a.shape[0], w_in_b.shape[0], get_weights, put_grads)
    rows = _all_reduce_rows(jnp.concatenate([vec[k] for k in VECTORS], axis=0), "reduce_vectors")

    out = {}
    after = dx
    for group in sent:
        blocks, arrived = _push_wait(sent[group], True, False, after, f"exchange_{group}_wait")
        for (_, p, layer), blk, got in zip(GROUPS[group], blocks, arrived):
            cols = w[p].shape[-1]
            own = lax.dynamic_index_in_dim(blk, me, 0, keepdims=False).reshape(-1, cols)
            n_layers = 1 if layer is None else w[p].shape[0]
            stacked = lambda t: t.reshape(n_layers, -1, cols)
            res = _adamw_blocks(own, got.reshape(N_DEV - 1, -1, cols), stacked(w[p]), stacked(m[p]), stacked(v[p]),
                                0 if layer is None else layer, out.get(p), f"adamw_{group}_{p}")
            out[p] = res
            after = res[1]
    out = {p: [r.reshape(w[p].shape) for r in res] for p, res in out.items()}
    start = 0
    for k in VECTORS:
        n_rows = vec[k].shape[0]
        g = rows[start:start + n_rows]
        start += n_rows
        if k in SHARDED_VECTORS:
            g = lax.dynamic_slice_in_dim(g, me * (D // N_DEV), D // N_DEV, axis=1)
        res = _adamw_rows(g, w[k].reshape(g.shape), m[k].reshape(g.shape), v[k].reshape(g.shape), f"adamw_{k}")
        out[k] = [r.reshape(w[k].shape) for r in [g] + list(res)]

    names = ("norm_a", "w_in_a", "w_grp_a", "scale_a", "w_out_a", "norm_kv", "w_k", "w_v", "norm_b", "w_in_b",
             "w_out_b", "norm_f")
    total = lax.psum(loss[0, 0], ("x", "y", "c"))
    return (total, dx[None], *[out[k][0] for k in names], *[out[k][1] for k in names],
            *[out[k][2] for k in names], *[out[k][3] for k in names])
```

```python
import math

import jax
import jax.numpy as jnp
from jax import lax
from jax.experimental import pallas as pl
from jax.experimental.pallas import tpu as pltpu

F32 = jnp.float32
BF16 = jnp.bfloat16

N_DEV = 8
MESH = pl.DeviceIdType.MESH
RMS_EPS = 1e-6
HEAD_DIM = 128
HALF_HEAD = HEAD_DIM // 2
BAND = 128
DILATIONS = (1, 4, 16)
POOL_WINDOWS = (2, 4, 8, 16)
POOL_HALO = 16
ROPE_THETA = 10000.0
NEG_INF = -1e30
ATTN_SCALE = 1.0 / math.sqrt(HEAD_DIM)
ADAM_LR, ADAM_B1, ADAM_B2, ADAM_EPS, ADAM_WD, ADAM_STEP = 0.001, 0.9, 0.999, 1e-08, 0.01, 10
VMEM_LIMIT_BYTES = 56 * 1024 * 1024
ANY = pl.BlockSpec(memory_space=pl.ANY)
NT = (((1,), (1,)), ((), ()))
TN = (((0,), (0,)), ((), ()))


def _params(*semantics):
    return pltpu.CompilerParams(dimension_semantics=semantics, vmem_limit_bytes=VMEM_LIMIT_BYTES)


def _sigmoid(t):
    return 1.0 / (1.0 + jnp.exp(-t))


def _rope(t, cos2, sin2):
    return t * cos2 + pltpu.roll(t, HALF_HEAD, 1) * sin2


def _rope_bwd(dt, cos2, sin2):
    return dt * cos2 + pltpu.roll(dt * sin2, HALF_HEAD, 1)


def _norm_matmul(x, gain, w, w_block, w_index, n_col_blocks, tn, name):
    S, D = x.shape
    tm = min(512, S)

    def body(x_ref, g_ref, w_ref, o_ref, hn_ref, hs_ref):
        @pl.when(pl.program_id(1) == 0)
        def _():
            xf = x_ref[...]
            inv = lax.rsqrt(jnp.mean(xf * xf, axis=-1, keepdims=True) + RMS_EPS)
            hb = ((xf * inv) * g_ref[...]).astype(BF16)
            hs_ref[...] = hb
            hn_ref[...] = hb
        o_ref[...] = jnp.dot(hs_ref[...], w_ref[...], preferred_element_type=F32)

    return pl.pallas_call(
        body, name=name, grid=(S // tm, n_col_blocks),
        in_specs=[pl.BlockSpec((tm, D), lambda i, j: (i, 0)),
                  pl.BlockSpec((1, D), lambda i, j: (0, 0)),
                  pl.BlockSpec(w_block, w_index)],
        out_specs=[pl.BlockSpec((tm, tn), lambda i, j: (i, j)),
                   pl.BlockSpec((tm, D), lambda i, j: (i, 0))],
        out_shape=[jax.ShapeDtypeStruct((S, n_col_blocks * tn), F32), jax.ShapeDtypeStruct((S, D), BF16)],
        scratch_shapes=[pltpu.VMEM((tm, D), BF16)],
        compiler_params=_params("parallel", "arbitrary"))(x, gain, w)


def _matmul_res(a, w, res, name):
    S, K = a.shape
    nd, rb, N = w.shape
    tm = min(512, S)

    def body(a_ref, w_ref, r_ref, o_ref):
        acc = jnp.dot(a_ref[:, 0:rb], w_ref[0], preferred_element_type=F32)
        for k in range(1, nd):
            acc = acc + jnp.dot(a_ref[:, k * rb:(k + 1) * rb], w_ref[k], preferred_element_type=F32)
        o_ref[...] = r_ref[...] + acc

    return pl.pallas_call(
        body, name=name, grid=(S // tm,),
        in_specs=[pl.BlockSpec((tm, K), lambda i: (i, 0)),
                  pl.BlockSpec((nd, rb, N), lambda i: (0, 0, 0)),
                  pl.BlockSpec((tm, N), lambda i: (i, 0))],
        out_specs=pl.BlockSpec((tm, N), lambda i: (i, 0)),
        out_shape=jax.ShapeDtypeStruct((S, N), F32),
        compiler_params=_params("parallel"))(a, w, res)


def _matmul_nt_rows(dy, w, name):
    S, N = dy.shape
    nd, rb, _ = w.shape
    tm = min(512, S)

    def body(d_ref, w_ref, o_ref):
        db = d_ref[...].astype(BF16)
        for k in range(nd):
            o_ref[:, k * rb:(k + 1) * rb] = lax.dot_general(db, w_ref[k], NT, preferred_element_type=F32)

    return pl.pallas_call(
        body, name=name, grid=(S // tm,),
        in_specs=[pl.BlockSpec((tm, N), lambda i: (i, 0)),
                  pl.BlockSpec((nd, rb, N), lambda i: (0, 0, 0))],
        out_specs=pl.BlockSpec((tm, nd * rb), lambda i: (i, 0)),
        out_shape=jax.ShapeDtypeStruct((S, nd * rb), F32),
        compiler_params=_params("parallel"))(dy, w)


def _matmul_nt_cols(dp, w, w_block, w_index, n_red, tc, n_out, name):
    S = dp.shape[0]
    tm = min(512, S)

    def body(d_ref, w_ref, o_ref):
        @pl.when(pl.program_id(1) == 0)
        def _():
            o_ref[...] = jnp.zeros_like(o_ref)
        o_ref[...] += lax.dot_general(d_ref[...], w_ref[...], NT, preferred_element_type=F32)

    return pl.pallas_call(
        body, name=name, grid=(S // tm, n_red),
        in_specs=[pl.BlockSpec((tm, tc), lambda i, j: (i, j)),
                  pl.BlockSpec(w_block, w_index)],
        out_specs=pl.BlockSpec((tm, n_out), lambda i, j: (i, 0)),
        out_shape=jax.ShapeDtypeStruct((S, n_out), F32),
        compiler_params=_params("parallel", "arbitrary"))(dp, w)


def _matmul_tn(a, b, a_block, a_index, b_block, b_index, out_shape, out_block, out_index, acc_shape, n_outer, name):
    S = a.shape[0]
    ts = a_block[0]
    n_tok = S // ts

    def body(a_ref, b_ref, o_ref, acc_ref):
        s = pl.program_id(1)

        @pl.when(s == 0)
        def _():
            acc_ref[...] = jnp.zeros_like(acc_ref)
        acc_ref[...] += lax.dot_general(a_ref[...].astype(BF16), b_ref[...].astype(BF16), TN,
                                        preferred_element_type=F32)

        @pl.when(s == n_tok - 1)
        def _():
            o_ref[...] = acc_ref[...].astype(o_ref.dtype)

    return pl.pallas_call(
        body, name=name, grid=(n_outer, n_tok),
        in_specs=[pl.BlockSpec(a_block, a_index), pl.BlockSpec(b_block, b_index)],
        out_specs=pl.BlockSpec(out_block, out_index),
        out_shape=jax.ShapeDtypeStruct(out_shape, BF16),
        scratch_shapes=[pltpu.VMEM(acc_shape, F32)],
        compiler_params=_params("parallel", "arbitrary"))(a, b)


def _pool(scr_ref, u, row0, tm, E):
    gc = E // len(POOL_WINDOWS)
    t1 = row0 + lax.broadcasted_iota(jnp.int32, (tm, 1), 0) + 1
    out = []
    for g, win in enumerate(POOL_WINDOWS):
        cs = slice(g * gc, (g + 1) * gc)
        acc = u[:, cs]
        for k in range(1, win):
            acc = acc + scr_ref[pl.ds(POOL_HALO - k, tm), cs]
        count = jnp.minimum(t1, win).astype(F32)
        out.append(acc / count - u[:, cs])
    return out


def _a_mid_fwd(proj, wg, scale, name):
    S, E2 = proj.shape
    E = E2 // 2
    gc = E // len(POOL_WINDOWS)
    tm = min(256, S)
    hb = tm // POOL_HALO

    def body(u_ref, uh_ref, gt_ref, wg_ref, sc_ref, z_ref, scr_ref):
        i = pl.program_id(0)
        scr_ref[0:POOL_HALO, :] = jnp.where(i > 0, uh_ref[...], 0.0)
        u = u_ref[...]
        scr_ref[POOL_HALO:POOL_HALO + tm, :] = u
        pooled = _pool(scr_ref, u, i * tm, tm, E)
        for g in range(len(POOL_WINDOWS)):
            cs = slice(g * gc, (g + 1) * gc)
            y = jnp.dot(pooled[g].astype(BF16), wg_ref[g], preferred_element_type=F32) * sc_ref[:, cs]
            gate = gt_ref[:, cs]
            z_ref[:, cs] = (y * (gate * _sigmoid(gate))).astype(BF16)

    return pl.pallas_call(
        body, name=name, grid=(S // tm,),
        in_specs=[pl.BlockSpec((tm, E), lambda i: (i, 0)),
                  pl.BlockSpec((POOL_HALO, E), lambda i: (jnp.maximum(i * hb - 1, 0), 0)),
                  pl.BlockSpec((tm, E), lambda i: (i, 1)),
                  pl.BlockSpec((len(POOL_WINDOWS), gc, gc), lambda i: (0, 0, 0)),
                  pl.BlockSpec((1, E), lambda i: (0, 0))],
        out_specs=pl.BlockSpec((tm, E), lambda i: (i, 0)),
        out_shape=jax.ShapeDtypeStruct((S, E), BF16),
        scratch_shapes=[pltpu.VMEM((POOL_HALO + tm, E), F32)],
        compiler_params=_params("parallel"))(proj, proj, proj, wg, scale)


def _a_mid_bwd(dz, proj, wg, scale, name):
    S, E2 = proj.shape
    E = E2 // 2
    n_grp = len(POOL_WINDOWS)
    gc = E // n_grp
    tm = min(256, S)
    hb = tm // POOL_HALO
    n_tiles = S // tm
    last_halo = S // POOL_HALO - 1

    def body(dz_ref, dzh_ref, u_ref, uh_ref, gt_ref, gth_ref, wg_ref, sc_ref, dp_ref, dwg_ref, dsc_ref, scr_ref, q_ref):
        i = pl.program_id(0)

        @pl.when(i == 0)
        def _():
            dwg_ref[...] = jnp.zeros_like(dwg_ref)
            dsc_ref[...] = jnp.zeros_like(dsc_ref)

        scr_ref[0:POOL_HALO, :] = jnp.where(i > 0, uh_ref[...], 0.0)
        u = u_ref[...]
        scr_ref[POOL_HALO:POOL_HALO + tm, :] = u
        pooled = _pool(scr_ref, u, i * tm, tm, E)
        t1 = i * tm + lax.broadcasted_iota(jnp.int32, (tm, 1), 0) + 1
        t1h = (i + 1) * tm + lax.broadcasted_iota(jnp.int32, (POOL_HALO, 1), 0) + 1
        not_last = i < n_tiles - 1
        for g, win in enumerate(POOL_WINDOWS):
            cs = slice(g * gc, (g + 1) * gc)
            w = wg_ref[g]
            sc = sc_ref[:, cs]
            pb = pooled[g].astype(BF16)
            ypre = jnp.dot(pb, w, preferred_element_type=F32)
            gate = gt_ref[:, cs]
            sg = _sigmoid(gate)
            silu = gate * sg
            dzg = dz_ref[:, cs]
            dy = dzg * silu
            dp_ref[:, E + g * gc:E + (g + 1) * gc] = (dzg * (ypre * sc) * (sg * (1.0 + gate * (1.0 - sg)))).astype(BF16)
            dsc_ref[:, cs] += jnp.sum(dy * ypre, axis=0, keepdims=True)
            dyp = (dy * sc).astype(BF16)
            dwg_ref[g] += lax.dot_general(pb, dyp, TN, preferred_element_type=F32)
            dpool = lax.dot_general(dyp, w, NT, preferred_element_type=F32)
            gate_h = gth_ref[:, cs]
            dyp_h = (dzh_ref[:, cs] * (gate_h * _sigmoid(gate_h)) * sc).astype(BF16)
            dpool_h = lax.dot_general(dyp_h, w, NT, preferred_element_type=F32)
            q_ref[0:tm, cs] = dpool / jnp.minimum(t1, win).astype(F32)
            q_ref[tm:tm + POOL_HALO, cs] = jnp.where(not_last, dpool_h / jnp.minimum(t1h, win).astype(F32), 0.0)
            acc = q_ref[0:tm, cs] - dpool
            for k in range(1, win):
                acc = acc + q_ref[pl.ds(k, tm), cs]
            dp_ref[:, cs] = acc.astype(BF16)

    return pl.pallas_call(
        body, name=name, grid=(n_tiles,),
        in_specs=[pl.BlockSpec((tm, E), lambda i: (i, 0)),
                  pl.BlockSpec((POOL_HALO, E), lambda i: (jnp.minimum((i + 1) * hb, last_halo), 0)),
                  pl.BlockSpec((tm, E), lambda i: (i, 0)),
                  pl.BlockSpec((POOL_HALO, E), lambda i: (jnp.maximum(i * hb - 1, 0), 0)),
                  pl.BlockSpec((tm, E), lambda i: (i, 1)),
                  pl.BlockSpec((POOL_HALO, E), lambda i: (jnp.minimum((i + 1) * hb, last_halo), 1)),
                  pl.BlockSpec((n_grp, gc, gc), lambda i: (0, 0, 0)),
                  pl.BlockSpec((1, E), lambda i: (0, 0))],
        out_specs=[pl.BlockSpec((tm, E2), lambda i: (i, 0)),
                   pl.BlockSpec((n_grp, gc, gc), lambda i: (0, 0, 0)),
                   pl.BlockSpec((1, E), lambda i: (0, 0))],
        out_shape=[jax.ShapeDtypeStruct((S, E2), BF16),
                   jax.ShapeDtypeStruct((n_grp, gc, gc), F32),
                   jax.ShapeDtypeStruct((1, E), F32)],
        scratch_shapes=[pltpu.VMEM((POOL_HALO + tm, E), F32), pltpu.VMEM((tm + POOL_HALO, E), F32)],
        compiler_params=_params("arbitrary"))(dz, dz, proj, proj, proj, proj, wg, scale)


def _rope_k(kvp, cos2, sin2, name):
    S, E2 = kvp.shape
    E = E2 // 2
    tm = min(256, S)

    def body(k_ref, c_ref, s_ref, ko_ref):
        cosv, sinv = c_ref[...], s_ref[...]
        for h in range(E // HEAD_DIM):
            hs = slice(h * HEAD_DIM, (h + 1) * HEAD_DIM)
            ko_ref[:, hs] = _rope(k_ref[:, hs], cosv, sinv)

    return pl.pallas_call(
        body, name=name, grid=(S // tm,),
        in_specs=[pl.BlockSpec((tm, E), lambda i: (i, 0)),
                  pl.BlockSpec((tm, HEAD_DIM), lambda i: (i, 0)), pl.BlockSpec((tm, HEAD_DIM), lambda i: (i, 0))],
        out_specs=pl.BlockSpec((tm, E), lambda i: (i, 0)),
        out_shape=jax.ShapeDtypeStruct((S, E), F32),
        compiler_params=_params("parallel"))(kvp, cos2, sin2)


ATTN_ROWS = 2048


def _rows(r, b, dil, n=BAND):
    start = r + b * BAND * dil
    return pl.ds(start, n) if dil == 1 else pl.ds(start, n, stride=dil)


def _band_mask(first):
    row = lax.broadcasted_iota(jnp.int32, (BAND, 2 * BAND), 0)
    col = lax.broadcasted_iota(jnp.int32, (BAND, 2 * BAND), 1)
    mask = (col >= row) & (col <= row + BAND)
    return mask if first is None else mask & (jnp.logical_not(first) | (col >= BAND))


def _lane_column(tile, lane, h):
    return jnp.sum(jnp.where(lane == h, tile, 0.0), axis=-1, keepdims=True)


def _keys(cur_ref, prev_ref, r, b, dil, nq):
    if b > 0:
        return cur_ref[_rows(r, b - 1, dil, 2 * BAND), :].astype(BF16)
    return jnp.concatenate([prev_ref[_rows(r, nq - 1, dil), :], cur_ref[_rows(r, 0, dil), :]], axis=0).astype(BF16)


def _attn_fwd(proj, kr, kvp, cos2, sin2, group, dil, name):
    S, PW = proj.shape
    E = kr.shape[1]
    H = E // HEAD_DIM
    R = min(ATTN_ROWS, S)
    nq = R // (BAND * dil)
    q_col0, v_col0 = group * H, H

    def body(q_ref, kc_ref, kp_ref, vc_ref, vp_ref, c_ref, s_ref, o_ref, l_ref):
        n, h = pl.program_id(0), pl.program_id(1)
        edge, inner = _band_mask(n == 0), _band_mask(None)
        lane = lax.broadcasted_iota(jnp.int32, (BAND, HEAD_DIM), 1)

        @pl.when(h == 0)
        def _():
            l_ref[...] = jnp.zeros_like(l_ref)

        for r in range(dil):
            for b in range(nq):
                rows = _rows(r, b, dil)
                qr = _rope(q_ref[rows, :], c_ref[rows, :], s_ref[rows, :]).astype(BF16)
                kcat = _keys(kc_ref, kp_ref, r, b, dil, nq)
                vcat = _keys(vc_ref, vp_ref, r, b, dil, nq)
                s = lax.dot_general(qr, kcat, NT, preferred_element_type=F32) * ATTN_SCALE
                s = jnp.where(edge if b == 0 else inner, s, NEG_INF)
                m = jnp.max(s, axis=-1, keepdims=True)
                p = jnp.exp(s - m)
                l = jnp.sum(p, axis=-1, keepdims=True)
                o_ref[rows, :] = jnp.dot(p.astype(BF16), vcat, preferred_element_type=F32) / l
                l_ref[rows, :] = jnp.where(lane == h, m + jnp.log(l), l_ref[rows, :])

    blk = (R, HEAD_DIM)
    cur = lambda n, h: (n, h)
    prev = lambda n, h: (jnp.maximum(n - 1, 0), h)
    thin = pl.BlockSpec(blk, lambda n, h: (n, 0))
    return pl.pallas_call(
        body, name=name, grid=(S // R, H),
        in_specs=[pl.BlockSpec(blk, lambda n, h: (n, q_col0 + h)),
                  pl.BlockSpec(blk, cur), pl.BlockSpec(blk, prev),
                  pl.BlockSpec(blk, lambda n, h: (n, v_col0 + h)),
                  pl.BlockSpec(blk, lambda n, h: (jnp.maximum(n - 1, 0), v_col0 + h)),
                  thin, thin],
        out_specs=[pl.BlockSpec(blk, cur), thin],
        out_shape=[jax.ShapeDtypeStruct((S, E), F32), jax.ShapeDtypeStruct((S, HEAD_DIM), F32)],
        compiler_params=_params("parallel", "arbitrary"))(proj, kr, kr, kvp, kvp, cos2, sin2)


def _attn_bwd(proj, kr, kvp, cos2, sin2, do, lse, dlt, dproj, group, dil, name):
    S, PW = proj.shape
    E = kr.shape[1]
    H = E // HEAD_DIM
    R = min(ATTN_ROWS, S)
    nq = R // (BAND * dil)
    nsb = S // R
    q_col0, v_col0 = group * H, H

    def body(q_ref, kc_ref, kp_ref, vc_ref, vp_ref, c_ref, s_ref, do_ref, l_ref, dl_ref, dproj_ref,
             dq_ref, dk_ref, dv_ref, dq_scr, ck_ref, cv_ref):
        h, n = pl.program_id(0), pl.program_id(1)

        @pl.when(n == 0)
        def _():
            ck_ref[...] = jnp.zeros_like(ck_ref)
            cv_ref[...] = jnp.zeros_like(cv_ref)

        dk_ref[...] = ck_ref[...]
        dv_ref[...] = cv_ref[...]

        @pl.when(n < nsb)
        def _():
            ck_ref[...] = jnp.zeros_like(ck_ref)
            cv_ref[...] = jnp.zeros_like(cv_ref)
            edge, inner = _band_mask(n == 0), _band_mask(None)
            lane = lax.broadcasted_iota(jnp.int32, (BAND, HEAD_DIM), 1)
            for r in range(dil):
                for b in range(nq):
                    rows = _rows(r, b, dil)
                    cosv, sinv = c_ref[rows, :], s_ref[rows, :]
                    qr = _rope(q_ref[rows, :], cosv, sinv).astype(BF16)
                    kcat = _keys(kc_ref, kp_ref, r, b, dil, nq)
                    vcat = _keys(vc_ref, vp_ref, r, b, dil, nq)
                    s = lax.dot_general(qr, kcat, NT, preferred_element_type=F32) * ATTN_SCALE
                    s = jnp.where(edge if b == 0 else inner, s, NEG_INF)
                    p = jnp.exp(s - _lane_column(l_ref[rows, :], lane, h))
                    dob = do_ref[rows, :].astype(BF16)
                    dpr = lax.dot_general(dob, vcat, NT, preferred_element_type=F32)
                    ds = (p * (dpr - _lane_column(dl_ref[rows, :], lane, h)) * ATTN_SCALE).astype(BF16)
                    dq = jnp.dot(ds, kcat, preferred_element_type=F32)
                    dq_scr[rows, :] = _rope_bwd(dq, cosv, sinv)
                    dkc = lax.dot_general(ds, qr, TN, preferred_element_type=F32)
                    dvc = lax.dot_general(p.astype(BF16), dob, TN, preferred_element_type=F32)
                    if b > 0:
                        both = _rows(r, b - 1, dil, 2 * BAND)
                        ck_ref[both, :] += dkc
                        cv_ref[both, :] += dvc
                    else:
                        last = _rows(r, nq - 1, dil)
                        dk_ref[last, :] += dkc[0:BAND]
                        dv_ref[last, :] += dvc[0:BAND]
                        ck_ref[rows, :] += dkc[BAND:2 * BAND]
                        cv_ref[rows, :] += dvc[BAND:2 * BAND]
            dq_ref[...] = dq_scr[...].astype(BF16)

    top = nsb - 1
    blk = (R, HEAD_DIM)
    cur = lambda h, n: (jnp.minimum(n, top), h)
    prev = lambda h, n: (jnp.maximum(jnp.minimum(n, top) - 1, 0), h)
    late = lambda h, n: (jnp.maximum(n - 1, 0), h)
    thin = pl.BlockSpec(blk, lambda h, n: (jnp.minimum(n, top), 0))
    return pl.pallas_call(
        body, name=name, grid=(H, nsb + 1),
        in_specs=[pl.BlockSpec(blk, lambda h, n: (jnp.minimum(n, top), q_col0 + h)),
                  pl.BlockSpec(blk, cur), pl.BlockSpec(blk, prev),
                  pl.BlockSpec(blk, lambda h, n: (jnp.minimum(n, top), v_col0 + h)),
                  pl.BlockSpec(blk, lambda h, n: (jnp.maximum(jnp.minimum(n, top) - 1, 0), v_col0 + h)),
                  thin, thin, pl.BlockSpec(blk, cur), thin, thin, ANY],
        out_specs=[pl.BlockSpec(blk, lambda h, n: (jnp.minimum(n, top), q_col0 + h)),
                   pl.BlockSpec(blk, late), pl.BlockSpec(blk, late)],
        out_shape=[jax.ShapeDtypeStruct(dproj.shape, BF16), jax.ShapeDtypeStruct((S, E), F32),
                   jax.ShapeDtypeStruct((S, E), F32)],
        scratch_shapes=[pltpu.VMEM(blk, F32), pltpu.VMEM(blk, F32), pltpu.VMEM(blk, F32)],
        input_output_aliases={10: 0},
        compiler_params=_params("parallel", "arbitrary"))(proj, kr, kr, kvp, kvp, cos2, sin2, do, lse, dlt, dproj)


def _group_weights(l_refs, h):
    ls = [r[:, h:h + 1] for r in l_refs]
    mx = jnp.maximum(jnp.maximum(ls[0], ls[1]), ls[2])
    es = [jnp.exp(l - mx) for l in ls]
    inv = 1.0 / (es[0] + es[1] + es[2])
    return [e * inv for e in es]


def _merge_fwd(outs, lses, proj, name):
    S, E = outs[0].shape
    tm = min(256, S)
    gate_col = proj.shape[1] // E - 1

    def body(o0, o1, o2, l0, l1, l2, gt_ref, z_ref):
        for h in range(E // HEAD_DIM):
            hs = slice(h * HEAD_DIM, (h + 1) * HEAD_DIM)
            a = _group_weights((l0, l1, l2), h)
            merged = a[0] * o0[:, hs] + a[1] * o1[:, hs] + a[2] * o2[:, hs]
            gate = gt_ref[:, hs]
            z_ref[:, hs] = (merged * (gate * _sigmoid(gate))).astype(BF16)

    wide = pl.BlockSpec((tm, E), lambda i: (i, 0))
    thin = pl.BlockSpec((tm, HEAD_DIM), lambda i: (i, 0))
    return pl.pallas_call(
        body, name=name, grid=(S // tm,),
        in_specs=[wide, wide, wide, thin, thin, thin, pl.BlockSpec((tm, E), lambda i: (i, gate_col))],
        out_specs=wide,
        out_shape=jax.ShapeDtypeStruct((S, E), BF16),
        compiler_params=_params("parallel"))(*outs, *lses, proj)


def _merge_bwd(dz, outs, lses, proj, name):
    S, E = outs[0].shape
    tm = min(256, S)
    gate_col = proj.shape[1] // E - 1

    def body(dz_ref, o0, o1, o2, l0, l1, l2, gt_ref, d0, d1, d2, t0, t1, t2, dg_ref):
        o_refs, d_refs, t_refs = (o0, o1, o2), (d0, d1, d2), (t0, t1, t2)
        lane = lax.broadcasted_iota(jnp.int32, (tm, HEAD_DIM), 1)
        tiles = [jnp.zeros((tm, HEAD_DIM), F32) for _ in range(3)]
        for h in range(E // HEAD_DIM):
            hs = slice(h * HEAD_DIM, (h + 1) * HEAD_DIM)
            a = _group_weights((l0, l1, l2), h)
            merged = a[0] * o0[:, hs] + a[1] * o1[:, hs] + a[2] * o2[:, hs]
            gate = gt_ref[:, hs]
            sg = _sigmoid(gate)
            dzh = dz_ref[:, hs]
            dmerged = dzh * (gate * sg)
            dg_ref[:, hs] = (dzh * merged * (sg * (1.0 + gate * (1.0 - sg)))).astype(BF16)
            tot = jnp.sum(dmerged * merged, axis=-1, keepdims=True)
            for g in range(3):
                d_refs[g][:, hs] = a[g] * dmerged
                tiles[g] = jnp.where(lane == h, a[g] * tot, tiles[g])
        for g in range(3):
            t_refs[g][...] = tiles[g]

    wide = pl.BlockSpec((tm, E), lambda i: (i, 0))
    thin = pl.BlockSpec((tm, HEAD_DIM), lambda i: (i, 0))
    res = pl.pallas_call(
        body, name=name, grid=(S // tm,),
        in_specs=[wide, wide, wide, wide, thin, thin, thin, pl.BlockSpec((tm, E), lambda i: (i, gate_col))],
        out_specs=[wide, wide, wide, thin, thin, thin, pl.BlockSpec((tm, E), lambda i: (i, gate_col))],
        out_shape=[jax.ShapeDtypeStruct((S, E), F32)] * 3 + [jax.ShapeDtypeStruct((S, HEAD_DIM), F32)] * 3
        + [jax.ShapeDtypeStruct(proj.shape, BF16)],
        compiler_params=_params("parallel"))(dz, *outs, *lses, proj)
    return res[0:3], res[3:6], res[6]


def _kv_bwd(dks, dvs, cos2, sin2, name):
    S, E = dks[0].shape
    n = len(dks)
    tm = min(128, S)

    def body(*refs):
        dk_refs, dv_refs = refs[0:n], refs[n:2 * n]
        c_ref, s_ref, o_ref = refs[2 * n:]
        cosv, sinv = c_ref[...], s_ref[...]
        for h in range(E // HEAD_DIM):
            hs = slice(h * HEAD_DIM, (h + 1) * HEAD_DIM)
            dk = dk_refs[0][:, hs]
            dv = dv_refs[0][:, hs]
            for j in range(1, n):
                dk = dk + dk_refs[j][:, hs]
                dv = dv + dv_refs[j][:, hs]
            o_ref[:, hs] = _rope_bwd(dk, cosv, sinv).astype(BF16)
            o_ref[:, E + h * HEAD_DIM:E + (h + 1) * HEAD_DIM] = dv.astype(BF16)

    wide = pl.BlockSpec((tm, E), lambda i: (i, 0))
    thin = pl.BlockSpec((tm, HEAD_DIM), lambda i: (i, 0))
    return pl.pallas_call(
        body, name=name, grid=(S // tm,),
        in_specs=[wide] * (2 * n) + [thin, thin],
        out_specs=pl.BlockSpec((tm, 2 * E), lambda i: (i, 0)),
        out_shape=jax.ShapeDtypeStruct((S, 2 * E), BF16),
        compiler_params=_params("parallel"))(*dks, *dvs, cos2, sin2)


def _norm_bwd(dhn, x, gain, dres, name):
    S, D = x.shape
    tm = min(256, S)

    def body(dh_ref, x_ref, g_ref, r_ref, dx_ref, dg_ref):
        @pl.when(pl.program_id(0) == 0)
        def _():
            dg_ref[...] = jnp.zeros_like(dg_ref)
        xf = x_ref[...]
        inv = lax.rsqrt(jnp.mean(xf * xf, axis=-1, keepdims=True) + RMS_EPS)
        xhat = xf * inv
        dh = dh_ref[...]
        dg_ref[...] += jnp.sum(dh * xhat, axis=0, keepdims=True)
        dxh = dh * g_ref[...]
        dx_ref[...] = r_ref[...] + inv * (dxh - xhat * jnp.mean(dxh * xhat, axis=-1, keepdims=True))

    tile = pl.BlockSpec((tm, D), lambda i: (i, 0))
    vec = pl.BlockSpec((1, D), lambda i: (0, 0))
    return pl.pallas_call(
        body, name=name, grid=(S // tm,),
        in_specs=[tile, tile, vec, tile],
        out_specs=[tile, vec],
        out_shape=[jax.ShapeDtypeStruct((S, D), F32), jax.ShapeDtypeStruct((1, D), F32)],
        compiler_params=_params("arbitrary"))(dhn, x, gain, dres)


def _final_norm_loss(x, target, gain, name):
    S, D = x.shape
    tm = min(256, S)

    def body(x_ref, t_ref, g_ref, loss_ref, dx_ref, dg_ref):
        @pl.when(pl.program_id(0) == 0)
        def _():
            loss_ref[...] = jnp.zeros_like(loss_ref)
            dg_ref[...] = jnp.zeros_like(dg_ref)
        xf = x_ref[...]
        inv = lax.rsqrt(jnp.mean(xf * xf, axis=-1, keepdims=True) + RMS_EPS)
        xhat = xf * inv
        g = g_ref[...]
        err = xhat * g - t_ref[...]
        loss_ref[...] += 0.5 * jnp.sum(jnp.mean(err * err, axis=-1, keepdims=True), axis=0, keepdims=True)
        dy = err / D
        dg_ref[...] += jnp.sum(dy * xhat, axis=0, keepdims=True)
        dxh = dy * g
        dx_ref[...] = inv * (dxh - xhat * jnp.mean(dxh * xhat, axis=-1, keepdims=True))

    tile = pl.BlockSpec((tm, D), lambda i: (i, 0))
    vec = pl.BlockSpec((1, D), lambda i: (0, 0))
    return pl.pallas_call(
        body, name=name, grid=(S // tm,),
        in_specs=[tile, tile, vec],
        out_specs=[pl.BlockSpec((1, 1), lambda i: (0, 0)), tile, vec],
        out_shape=[jax.ShapeDtypeStruct((1, 1), F32), jax.ShapeDtypeStruct((S, D), F32),
                   jax.ShapeDtypeStruct((1, D), F32)],
        compiler_params=_params("arbitrary"))(x, target, gain)


def _adamw_math(g, w, m, v):
    m = ADAM_B1 * m + (1.0 - ADAM_B1) * g
    v = ADAM_B2 * v + (1.0 - ADAM_B2) * (g * g)
    m_hat = m / (1.0 - ADAM_B1 ** ADAM_STEP)
    v_hat = v / (1.0 - ADAM_B2 ** ADAM_STEP)
    delta = -ADAM_LR * (m_hat / (jnp.sqrt(v_hat) + ADAM_EPS) + ADAM_WD * w)
    return delta, m, v


def _adamw_rows(g, w, m, v, name):
    def body(g_ref, w_ref, m_ref, v_ref, d_ref, mo_ref, vo_ref):
        d_ref[...], mo_ref[...], vo_ref[...] = _adamw_math(g_ref[...], w_ref[...], m_ref[...], v_ref[...])

    whole = pl.BlockSpec(memory_space=pltpu.VMEM)
    return pl.pallas_call(
        body, name=name, in_specs=[whole] * 4, out_specs=[whole] * 3,
        out_shape=[jax.ShapeDtypeStruct(g.shape, F32)] * 3)(g, w, m, v)


def _adamw_blocks(own, others, w, m, v, layer, earlier, name):
    L, R, C = w.shape
    n = others.shape[0]
    tr = R
    while tr * C > 128 * 1024 and tr % 16 == 0:
        tr //= 2

    def body(o_ref, p_ref, w_ref, m_ref, v_ref, *rest):
        g_ref, d_ref, mo_ref, vo_ref = rest[-4:]
        g = o_ref[...].astype(F32)
        for j in range(n):
            g = g + p_ref[j].astype(F32)
        g_ref[...] = g
        d_ref[...], mo_ref[...], vo_ref[...] = _adamw_math(g, w_ref[...], m_ref[...], v_ref[...])

    tile = pl.BlockSpec((None, tr, C), lambda i: (layer, i, 0))
    kept = [] if earlier is None else list(earlier)
    return pl.pallas_call(
        body, name=name, grid=(R // tr,),
        in_specs=[pl.BlockSpec((tr, C), lambda i: (i, 0)), pl.BlockSpec((n, tr, C), lambda i: (0, i, 0)),
                  tile, tile, tile] + [ANY] * len(kept),
        out_specs=[tile] * 4,
        out_shape=[jax.ShapeDtypeStruct((L, R, C), F32)] * 4,
        input_output_aliases={5 + j: j for j in range(len(kept))},
        compiler_params=_params("parallel"))(own, others, w, m, v, *kept)


def _position():
    return lax.axis_index("x"), lax.axis_index("y"), lax.axis_index("c")


def _block_index(px, py, pc):
    return 4 * px + 2 * py + pc


def _all_gather(shards, name):
    n = len(shards)

    def body(*refs):
        ins, outs = refs[0:n], refs[n:2 * n]
        send_sems, recv_sems, local_sems = refs[2 * n:]
        x, y, c = _position()
        me, sibling = (x, y, c), (x, y, 1 - c)
        chips = [(1 - x, y), (x, 1 - y), (1 - x, 1 - y)]

        def copy(a, k, block, to, src=None):
            rows = outs[a].at[_block_index(*block)]
            return pltpu.make_async_remote_copy(
                src_ref=rows if src is None else src, dst_ref=rows,
                send_sem=send_sems.at[a, k], recv_sem=recv_sems.at[a, k], device_id=to, device_id_type=MESH)

        mine, first, passed = [], [], []
        for a in range(n):
            cp = pltpu.make_async_copy(ins[a], outs[a].at[_block_index(*me)], local_sems.at[a])
            cp.start()
            mine.append(cp)
            first.append(copy(a, 0, me, sibling, src=ins[a]))
            first += [copy(a, 1 + j, me, (*chip, c), src=ins[a]) for j, chip in enumerate(chips)]
        for cp in first:
            cp.start()
        for j, chip in enumerate(chips):
            for a in range(n):
                copy(a, 1 + j, (*chip, c), me).wait_recv()
                fwd = copy(a, 4 + j, (*chip, c), sibling)
                fwd.start()
                passed.append(fwd)
        for a in range(n):
            copy(a, 0, sibling, me).wait_recv()
            for j, chip in enumerate(chips):
                copy(a, 4 + j, (*chip, 1 - c), me).wait_recv()
        for cp in first + passed:
            cp.wait_send()
        for cp in mine:
            cp.wait()

    return pl.pallas_call(
        body, name=name,
        in_specs=[ANY] * n, out_specs=[ANY] * n,
        out_shape=[jax.ShapeDtypeStruct((N_DEV,) + s.shape, s.dtype) for s in shards],
        scratch_shapes=[pltpu.SemaphoreType.DMA((n, 7)), pltpu.SemaphoreType.DMA((n, 7)),
                        pltpu.SemaphoreType.DMA((n,))],
    )(*shards)


def _peers(x, y, c):
    return [((1 - x) if k & 4 else x, (1 - y) if k & 2 else y, (1 - c) if k & 1 else c) for k in range(1, N_DEV)]


HBM = pl.BlockSpec(memory_space=pltpu.HBM)
SEM = pl.BlockSpec(memory_space=pltpu.SEMAPHORE)
EFFECT = pltpu.SideEffectType.DATAFLOW_SIDE_EFFECTING


def _push_copy(src_refs, land_refs, send_sems, recv_sems, a, k, peer, per_peer, by_sender, arriving):
    me_idx, p_idx = _block_index(*_position()), _block_index(*peer)
    src = src_refs[a].at[p_idx] if per_peer else src_refs[a]
    if by_sender:
        slot = p_idx if arriving else me_idx
    else:
        slot = k
    return pltpu.make_async_remote_copy(
        src_ref=src, dst_ref=land_refs[a].at[slot], send_sem=send_sems.at[a * (N_DEV - 1) + k],
        recv_sem=recv_sems.at[a * (N_DEV - 1) + k], device_id=peer, device_id_type=MESH)


def _push_start(srcs, lands, per_peer, by_sender, after, name):
    n = len(srcs)

    def body(*refs):
        src_refs, land_refs = refs[0:n], refs[n:2 * n]
        send_sems, recv_sems = refs[2 * n + 1], refs[2 * n + 2]
        token = refs[-1]
        for a in range(n):
            for k, peer in enumerate(_peers(*_position())):
                _push_copy(src_refs, land_refs, send_sems, recv_sems, a, k, peer, per_peer, by_sender, False).start()
        token[...] = jnp.zeros_like(token)

    args = [pltpu.with_memory_space_constraint(t, pltpu.HBM) for t in list(srcs) + list(lands)]
    res = pl.pallas_call(
        body, name=name,
        in_specs=[HBM] * (2 * n) + [ANY],
        out_specs=[SEM, SEM] + [HBM] * (2 * n) + [pl.BlockSpec(memory_space=pltpu.VMEM)],
        out_shape=[pltpu.SemaphoreType.DMA((n * (N_DEV - 1),)), pltpu.SemaphoreType.DMA((n * (N_DEV - 1),))]
        + [pltpu.HBM(t.shape, t.dtype) for t in args] + [jax.ShapeDtypeStruct((8, 128), F32)],
        input_output_aliases={i: 2 + i for i in range(2 * n)},
        compiler_params=pltpu.CompilerParams(has_side_effects=EFFECT))(*args, after)
    return res[0], res[1], res[2:2 + n], res[2 + n:2 + 2 * n], res[-1]


def _push_wait(started, per_peer, by_sender, after, name):
    send_sems, recv_sems, srcs, lands, _ = started
    n = len(srcs)

    def body(*refs):
        src_refs, land_refs = refs[0:n], refs[n:2 * n]
        send_s, recv_s = refs[2 * n], refs[2 * n + 1]
        for a in range(n):
            for k, peer in enumerate(_peers(*_position())):
                _push_copy(src_refs, land_refs, send_s, recv_s, a, k, peer, per_peer, by_sender, False).wait_send()
                _push_copy(src_refs, land_refs, send_s, recv_s, a, k, peer, per_peer, by_sender, True).wait_recv()

    res = pl.pallas_call(
        body, name=name,
        in_specs=[HBM] * (2 * n) + [SEM, SEM, ANY],
        out_specs=[HBM] * (2 * n),
        out_shape=[pltpu.HBM(t.shape, t.dtype) for t in list(srcs) + list(lands)],
        input_output_aliases={i: i for i in range(2 * n)},
        compiler_params=pltpu.CompilerParams(has_side_effects=EFFECT))(*srcs, *lands, send_sems, recv_sems, after)
    return res[0:n], res[n:2 * n]


def _all_reduce_rows(v, name):
    R, D = v.shape

    def body(v_ref, o_ref, buf_ref, send_sems, recv_sems):
        x, y, c = _position()
        me_idx = _block_index(x, y, c)
        buf_ref[me_idx] = v_ref[...]
        copies = []
        for k in range(1, N_DEV):
            px = (1 - x) if k & 4 else x
            py = (1 - y) if k & 2 else y
            pc = (1 - c) if k & 1 else c
            rc = pltpu.make_async_remote_copy(
                src_ref=v_ref, dst_ref=buf_ref.at[me_idx],
                send_sem=send_sems.at[k - 1], recv_sem=recv_sems.at[k - 1],
                device_id=(px, py, pc), device_id_type=MESH)
            rc.start()
            copies.append((rc, pltpu.make_async_remote_copy(
                src_ref=v_ref, dst_ref=buf_ref.at[_block_index(px, py, pc)],
                send_sem=send_sems.at[k - 1], recv_sem=recv_sems.at[k - 1],
                device_id=(px, py, pc), device_id_type=MESH)))
        for rc, arrival in copies:
            rc.wait_send()
            arrival.wait_recv()
        acc = buf_ref[0]
        for j in range(1, N_DEV):
            acc = acc + buf_ref[j]
        o_ref[...] = acc

    return pl.pallas_call(
        body, name=name,
        in_specs=[pl.BlockSpec(memory_space=pltpu.VMEM)],
        out_specs=pl.BlockSpec(memory_space=pltpu.VMEM),
        out_shape=jax.ShapeDtypeStruct((R, D), F32),
        scratch_shapes=[pltpu.VMEM((N_DEV, R, D), F32),
                        pltpu.SemaphoreType.DMA((7,)), pltpu.SemaphoreType.DMA((7,))],
    )(v)


def _rope_tables(S):
    inv_freq = 1.0 / (ROPE_THETA ** (jnp.arange(0, HEAD_DIM, 2, dtype=F32) / HEAD_DIM))
    ang = jnp.arange(S, dtype=F32)[:, None] * inv_freq[None, :]
    cos, sin = jnp.cos(ang), jnp.sin(ang)
    return jnp.concatenate([cos, cos], axis=1), jnp.concatenate([-sin, sin], axis=1)


def _local_step(xs, target, vecs, n_a, n_b, get_weights, put_grads):
    S, D = xs.shape
    E = D
    cos2, sin2 = _rope_tables(S)
    ts = min(512, S)

    def col_blocks(w):
        cb = w.shape[2]
        tn = min(cb, 1024)
        per = cb // tn
        return (None, D, tn), (lambda i, j: (j // per, 0, j % per)), N_DEV * per, tn

    def grad_in(hn, dproj, cb, name):
        return _matmul_tn(hn, dproj, (ts, D), lambda j, s: (s, 0), (ts, cb), lambda j, s: (s, j),
                          (N_DEV, D, cb), (None, D, cb), lambda j, s: (j, 0, 0), (D, cb), N_DEV, name)

    def grad_out(z, dx, name, col=0):
        rows = z.shape[1]
        ta = min(1024, rows)
        out = _matmul_tn(z, dx, (ts, ta), lambda a, s: (s, a), (ts, E), lambda a, s: (s, col),
                         (rows, E), (ta, E), lambda a, s: (a, 0), (ta, E), rows // ta, name)
        return out.reshape(N_DEV, rows // N_DEV, E)

    x = xs
    a_saved, b_saved = [], []
    for i in range(n_a):
        w = get_weights(f"a{i}", x)
        blk, idx, nblocks, tn = col_blocks(w["w_in"])
        proj, hn = _norm_matmul(x, vecs["norm_a"][i:i + 1], w["w_in"], blk, idx, nblocks, tn, f"a{i}_in")
        z = _a_mid_fwd(proj, w["w_grp"], vecs["scale_a"][i:i + 1], f"a{i}_mid")
        x_next = _matmul_res(z, w["w_out"], x, f"a{i}_out")
        a_saved.append((x, hn, proj, z, w))
        x = x_next
    x_kv = x
    w_kv = get_weights("kv", x)["w_kv"]
    tn = min(E, 1024)
    kvp, hn_kv = _norm_matmul(x, vecs["norm_kv"], w_kv, (D, tn), lambda i, j: (0, j), 2 * E // tn, tn, "kv_in")
    kr = _rope_k(kvp, cos2, sin2, "kv_rope")
    after = kr
    for i in range(n_b):
        w = get_weights(f"b{i}", after)
        blk, idx, nblocks, tn = col_blocks(w["w_in"])
        proj, hn = _norm_matmul(x, vecs["norm_b"][i:i + 1], w["w_in"], blk, idx, nblocks, tn, f"b{i}_in")
        outs, lses = [], []
        for g, dil in enumerate(DILATIONS):
            o, l = _attn_fwd(proj, kr, kvp, cos2, sin2, g, dil, f"b{i}_attn{g}")
            outs.append(o)
            lses.append(l)
        z = _merge_fwd(outs, lses, proj, f"b{i}_merge")
        x_next = _matmul_res(z, w["w_out"], x, f"b{i}_out")
        b_saved.append((x, hn, proj, z, outs, lses, w))
        x = x_next
        after = x
    loss, dx, dg_f = _final_norm_loss(x, target, vecs["norm_f"], "final")

    vec = {"norm_a": [None] * n_a, "scale_a": [None] * n_a, "norm_b": [None] * n_b, "norm_f": [dg_f]}
    dks, dvs = [], []
    for i in reversed(range(n_b)):
        x_in, hn, proj, z, outs, lses, w = b_saved[i]
        dw_out = grad_out(z, dx, f"b{i}_dwout")
        dz = _matmul_nt_rows(dx, w["w_out"], f"b{i}_dz")
        dos, dlts, dproj = _merge_bwd(dz, outs, lses, proj, f"b{i}_dmerge")
        for g, dil in enumerate(DILATIONS):
            dproj, dk, dv = _attn_bwd(proj, kr, kvp, cos2, sin2, dos[g], lses[g], dlts[g], dproj, g, dil,
                                      f"b{i}_dattn{g}")
            dks.append(dk)
            dvs.append(dv)
        cb = w["w_in"].shape[2]
        tok = put_grads(f"b{i}", {"w_out": dw_out, "w_in": grad_in(hn, dproj, cb, f"b{i}_dwin")})
        dhn = _matmul_nt_cols(dproj, w["w_in"], (None, D, cb), lambda t, j: (j, 0, 0), N_DEV, cb, D, f"b{i}_dhn")
        dx, vec["norm_b"][i] = _norm_bwd(dhn, x_in, vecs["norm_b"][i:i + 1] + tok[0:1, 0:1], dx, f"b{i}_dnorm")

    dkv = _kv_bwd(dks, dvs, cos2, sin2, "kv_dsum")
    tok = put_grads("kv", {"w_k": grad_out(hn_kv, dkv, "kv_dwk", 0), "w_v": grad_out(hn_kv, dkv, "kv_dwv", 1)})
    dhn = _matmul_nt_cols(dkv, w_kv, (D, E), lambda t, j: (0, j), 2, E, D, "kv_dhn")
    dx, dg_kv = _norm_bwd(dhn, x_kv, vecs["norm_kv"] + tok[0:1, 0:1], dx, "kv_dnorm")
    vec["norm_kv"] = [dg_kv]

    for i in reversed(range(n_a)):
        x_in, hn, proj, z, w = a_saved[i]
        dw_out = grad_out(z, dx, f"a{i}_dwout")
        dz = _matmul_nt_rows(dx, w["w_out"], f"a{i}_dz")
        dproj, dwg, dsc = _a_mid_bwd(dz, proj, w["w_grp"], vecs["scale_a"][i:i + 1], f"a{i}_dmid")
        n_grp, gc, _ = dwg.shape
        dwg = dwg.reshape(n_grp, N_DEV, gc // N_DEV, gc).transpose(1, 0, 2, 3).astype(BF16)
        vec["scale_a"][i] = dsc
        cb = w["w_in"].shape[2]
        tok = put_grads(f"a{i}", {"w_out": dw_out, "w_grp": dwg, "w_in": grad_in(hn, dproj, cb, f"a{i}_dwin")})
        dhn = _matmul_nt_cols(dproj, w["w_in"], (None, D, cb), lambda t, j: (j, 0, 0), N_DEV, cb, D, f"a{i}_dhn")
        dx, vec["norm_a"][i] = _norm_bwd(dhn, x_in, vecs["norm_a"][i:i + 1] + tok[0:1, 0:1], dx, f"a{i}_dnorm")

    return loss, dx, {k: jnp.concatenate(v, axis=0) for k, v in vec.items()}


VECTORS = ("norm_a", "scale_a", "norm_kv", "norm_b", "norm_f")
SHARDED_VECTORS = ("norm_a", "scale_a")
GROUPS = {
    "a0": (("w_in", "w_in_a", 0), ("w_grp", "w_grp_a", 0), ("w_out", "w_out_a", 0)),
    "a1": (("w_in", "w_in_a", 1), ("w_grp", "w_grp_a", 1), ("w_out", "w_out_a", 1)),
    "kv": (("w_k", "w_k", None), ("w_v", "w_v", None)),
    "b0": (("w_in", "w_in_b", 0), ("w_out", "w_out_b", 0)),
    "b1": (("w_in", "w_in_b", 1), ("w_out", "w_out_b", 1)),
}
PREFETCHED = ("a1", "kv", "b0", "b1")


def kernel(x, norm_a, w_in_a, w_grp_a, scale_a, w_out_a, norm_kv, w_k, w_v, norm_b, w_in_b, w_out_b, norm_f, loss_target, m_norm_a, m_w_in_a, m_w_grp_a, m_scale_a, m_w_out_a, m_norm_kv, m_w_k, m_w_v, m_norm_b, m_w_in_b, m_w_out_b, m_norm_f, v_norm_a, v_w_in_a, v_w_grp_a, v_scale_a, v_w_out_a, v_norm_kv, v_w_k, v_w_v, v_norm_b, v_w_in_b, v_w_out_b, v_norm_f):
    w = dict(norm_a=norm_a, w_in_a=w_in_a, w_grp_a=w_grp_a, scale_a=scale_a, w_out_a=w_out_a, norm_kv=norm_kv,
             w_k=w_k, w_v=w_v, norm_b=norm_b, w_in_b=w_in_b, w_out_b=w_out_b, norm_f=norm_f)
    m = dict(norm_a=m_norm_a, w_in_a=m_w_in_a, w_grp_a=m_w_grp_a, scale_a=m_scale_a, w_out_a=m_w_out_a,
             norm_kv=m_norm_kv, w_k=m_w_k, w_v=m_w_v, norm_b=m_norm_b, w_in_b=m_w_in_b, w_out_b=m_w_out_b,
             norm_f=m_norm_f)
    v = dict(norm_a=v_norm_a, w_in_a=v_w_in_a, w_grp_a=v_w_grp_a, scale_a=v_scale_a, w_out_a=v_w_out_a,
             norm_kv=v_norm_kv, w_k=v_w_k, w_v=v_w_v, norm_b=v_norm_b, w_in_b=v_w_in_b, w_out_b=v_w_out_b,
             norm_f=v_norm_f)
    D = x.shape[2]
    me = _block_index(*_position())

    def shard(group):
        return [w[p].astype(BF16) if layer is None else w[p][layer].astype(BF16) for _, p, layer in GROUPS[group]]

    def as_weights(group, gathered):
        out = dict(zip([n for n, _, _ in GROUPS[group]], gathered))
        if "w_grp" in out:
            g = out["w_grp"]
            out["w_grp"] = g.transpose(1, 0, 2, 3).reshape(g.shape[1], g.shape[3], g.shape[3])
        if "w_k" in out:
            out = {"w_kv": jnp.concatenate([out["w_k"].reshape(D, D), out["w_v"].reshape(D, D)], axis=1)}
        return out

    first = _all_gather(shard("a0") + [w[k] for k in SHARDED_VECTORS], "gather_first")
    n_first = len(GROUPS["a0"])
    vecs = {k: g.transpose(1, 0, 2).reshape(w[k].shape[0], D) for k, g in zip(SHARDED_VECTORS, first[n_first:])}
    vecs.update(norm_kv=norm_kv[None, :], norm_b=norm_b, norm_f=norm_f[None, :])
    srcs, lands = [], []
    for group in PREFETCHED:
        for s in shard(group):
            srcs.append(s)
            lands.append(lax.dynamic_update_index_in_dim(lax.empty((N_DEV,) + s.shape, s.dtype), s[None], me, 0))
    inflight, at = {}, 0
    token = None
    for group in PREFETCHED:
        n = len(GROUPS[group])
        inflight[group] = _push_start(srcs[at:at + n], lands[at:at + n], False, True,
                                      first[0] if token is None else token, f"gather_{group}_start")
        token = inflight[group][4]
        at += n
    vecs["norm_a"] = vecs["norm_a"] + token[0:1, 0:1]

    def get_weights(group, after):
        if group == "a0":
            return as_weights(group, first[0:n_first])
        return as_weights(group, _push_wait(inflight[group], False, True, after, f"gather_{group}_wait")[1])

    sent = {}

    def put_grads(group, grads):
        blocks = [grads[n] for n, _, _ in GROUPS[group]]
        lands = [lax.empty((N_DEV - 1,) + b.shape[1:], b.dtype) for b in blocks]
        sent[group] = _push_start(blocks, lands, True, False, jnp.zeros((8, 128), F32), f"exchange_{group}_start")
        return sent[group][4]

    loss, dx, vec = _local_step(x[0], loss_target[0], vecs, w_in_a.shape[0], w_in_b.shape[0], get_weights, put_grads)
    rows = _all_reduce_rows(jnp.concatenate([vec[k] for k in VECTORS], axis=0), "reduce_vectors")

    out = {}
    after = dx
    for group in sent:
        blocks, arrived = _push_wait(sent[group], True, False, after, f"exchange_{group}_wait")
        for (_, p, layer), blk, got in zip(GROUPS[group], blocks, arrived):
            cols = w[p].shape[-1]
            own = lax.dynamic_index_in_dim(blk, me, 0, keepdims=False).reshape(-1, cols)
            n_layers = 1 if layer is None else w[p].shape[0]
            stacked = lambda t: t.reshape(n_layers, -1, cols)
            res = _adamw_blocks(own, got.reshape(N_DEV - 1, -1, cols), stacked(w[p]), stacked(m[p]), stacked(v[p]),
                                0 if layer is None else layer, out.get(p), f"adamw_{group}_{p}")
            out[p] = res
            after = res[1]
    out = {p: [r.reshape(w[p].shape) for r in res] for p, res in out.items()}
    start = 0
    for k in VECTORS:
        n_rows = vec[k].shape[0]
        g = rows[start:start + n_rows]
        start += n_rows
        if k in SHARDED_VECTORS:
            g = lax.dynamic_slice_in_dim(g, me * (D // N_DEV), D // N_DEV, axis=1)
        res = _adamw_rows(g, w[k].reshape(g.shape), m[k].reshape(g.shape), v[k].reshape(g.shape), f"adamw_{k}")
        out[k] = [r.reshape(w[k].shape) for r in [g] + list(res)]

    names = ("norm_a", "w_in_a", "w_grp_a", "scale_a", "w_out_a", "norm_kv", "w_k", "w_v", "norm_b", "w_in_b",
             "w_out_b", "norm_f")
    total = lax.psum(loss[0, 0], ("x", "y", "c"))
    return (total, dx[None], *[out[k][0] for k in names], *[out[k][1] for k in names],
            *[out[k][2] for k in names], *[out[k][3] for k in names])
```

```python
import math

import jax
import jax.numpy as jnp
from jax import lax
from jax.experimental import pallas as pl
from jax.experimental.pallas import tpu as pltpu

F32 = jnp.float32
BF16 = jnp.bfloat16

N_DEV = 8
MESH = pl.DeviceIdType.MESH
RMS_EPS = 1e-6
HEAD_DIM = 128
HALF_HEAD = HEAD_DIM // 2
BAND = 128
DILATIONS = (1, 4, 16)
POOL_WINDOWS = (2, 4, 8, 16)
POOL_HALO = 16
ROPE_THETA = 10000.0
NEG_INF = -1e30
ATTN_SCALE = 1.0 / math.sqrt(HEAD_DIM)
ADAM_LR, ADAM_B1, ADAM_B2, ADAM_EPS, ADAM_WD, ADAM_STEP = 0.001, 0.9, 0.999, 1e-08, 0.01, 10
VMEM_LIMIT_BYTES = 56 * 1024 * 1024
ANY = pl.BlockSpec(memory_space=pl.ANY)
NT = (((1,), (1,)), ((), ()))
TN = (((0,), (0,)), ((), ()))


def _params(*semantics):
    return pltpu.CompilerParams(dimension_semantics=semantics, vmem_limit_bytes=VMEM_LIMIT_BYTES)


def _sigmoid(t):
    return 1.0 / (1.0 + jnp.exp(-t))


def _rope(t, cos2, sin2):
    return t * cos2 + pltpu.roll(t, HALF_HEAD, 1) * sin2


def _rope_bwd(dt, cos2, sin2):
    return dt * cos2 + pltpu.roll(dt * sin2, HALF_HEAD, 1)


def _norm_matmul(x, gain, w, w_block, w_index, n_col_blocks, tn, name):
    S, D = x.shape
    tm = min(512, S)

    def body(x_ref, g_ref, w_ref, o_ref, hn_ref, hs_ref):
        @pl.when(pl.program_id(1) == 0)
        def _():
            xf = x_ref[...]
            inv = lax.rsqrt(jnp.mean(xf * xf, axis=-1, keepdims=True) + RMS_EPS)
            hb = ((xf * inv) * g_ref[...]).astype(BF16)
            hs_ref[...] = hb
            hn_ref[...] = hb
        o_ref[...] = jnp.dot(hs_ref[...], w_ref[...], preferred_element_type=F32)

    return pl.pallas_call(
        body, name=name, grid=(S // tm, n_col_blocks),
        in_specs=[pl.BlockSpec((tm, D), lambda i, j: (i, 0)),
                  pl.BlockSpec((1, D), lambda i, j: (0, 0)),
                  pl.BlockSpec(w_block, w_index)],
        out_specs=[pl.BlockSpec((tm, tn), lambda i, j: (i, j)),
                   pl.BlockSpec((tm, D), lambda i, j: (i, 0))],
        out_shape=[jax.ShapeDtypeStruct((S, n_col_blocks * tn), F32), jax.ShapeDtypeStruct((S, D), BF16)],
        scratch_shapes=[pltpu.VMEM((tm, D), BF16)],
        compiler_params=_params("parallel", "arbitrary"))(x, gain, w)


def _matmul_res(a, w, res, name):
    S, K = a.shape
    nd, rb, N = w.shape
    tm = min(512, S)

    def body(a_ref, w_ref, r_ref, o_ref):
        acc = jnp.dot(a_ref[:, 0:rb], w_ref[0], preferred_element_type=F32)
        for k in range(1, nd):
            acc = acc + jnp.dot(a_ref[:, k * rb:(k + 1) * rb], w_ref[k], preferred_element_type=F32)
        o_ref[...] = r_ref[...] + acc

    return pl.pallas_call(
        body, name=name, grid=(S // tm,),
        in_specs=[pl.BlockSpec((tm, K), lambda i: (i, 0)),
                  pl.BlockSpec((nd, rb, N), lambda i: (0, 0, 0)),
                  pl.BlockSpec((tm, N), lambda i: (i, 0))],
        out_specs=pl.BlockSpec((tm, N), lambda i: (i, 0)),
        out_shape=jax.ShapeDtypeStruct((S, N), F32),
        compiler_params=_params("parallel"))(a, w, res)


def _matmul_nt_rows(dy, w, name):
    S, N = dy.shape
    nd, rb, _ = w.shape
    tm = min(512, S)

    def body(d_ref, w_ref, o_ref):
        db = d_ref[...].astype(BF16)
        for k in range(nd):
            o_ref[:, k * rb:(k + 1) * rb] = lax.dot_general(db, w_ref[k], NT, preferred_element_type=F32)

    return pl.pallas_call(
        body, name=name, grid=(S // tm,),
        in_specs=[pl.BlockSpec((tm, N), lambda i: (i, 0)),
                  pl.BlockSpec((nd, rb, N), lambda i: (0, 0, 0))],
        out_specs=pl.BlockSpec((tm, nd * rb), lambda i: (i, 0)),
        out_shape=jax.ShapeDtypeStruct((S, nd * rb), F32),
        compiler_params=_params("parallel"))(dy, w)


def _matmul_nt_cols(dp, w, w_block, w_index, n_red, tc, n_out, name):
    S = dp.shape[0]
    tm = min(512, S)

    def body(d_ref, w_ref, o_ref):
        @pl.when(pl.program_id(1) == 0)
        def _():
            o_ref[...] = jnp.zeros_like(o_ref)
        o_ref[...] += lax.dot_general(d_ref[...], w_ref[...], NT, preferred_element_type=F32)

    return pl.pallas_call(
        body, name=name, grid=(S // tm, n_red),
        in_specs=[pl.BlockSpec((tm, tc), lambda i, j: (i, j)),
                  pl.BlockSpec(w_block, w_index)],
        out_specs=pl.BlockSpec((tm, n_out), lambda i, j: (i, 0)),
        out_shape=jax.ShapeDtypeStruct((S, n_out), F32),
        compiler_params=_params("parallel", "arbitrary"))(dp, w)


def _matmul_tn(a, b, a_block, a_index, b_block, b_index, out_shape, out_block, out_index, acc_shape, n_outer, name):
    S = a.shape[0]
    ts = a_block[0]
    n_tok = S // ts

    def body(a_ref, b_ref, o_ref, acc_ref):
        s = pl.program_id(1)

        @pl.when(s == 0)
        def _():
            acc_ref[...] = jnp.zeros_like(acc_ref)
        acc_ref[...] += lax.dot_general(a_ref[...].astype(BF16), b_ref[...].astype(BF16), TN,
                                        preferred_element_type=F32)

        @pl.when(s == n_tok - 1)
        def _():
            o_ref[...] = acc_ref[...].astype(o_ref.dtype)

    return pl.pallas_call(
        body, name=name, grid=(n_outer, n_tok),
        in_specs=[pl.BlockSpec(a_block, a_index), pl.BlockSpec(b_block, b_index)],
        out_specs=pl.BlockSpec(out_block, out_index),
        out_shape=jax.ShapeDtypeStruct(out_shape, BF16),
        scratch_shapes=[pltpu.VMEM(acc_shape, F32)],
        compiler_params=_params("parallel", "arbitrary"))(a, b)


def _pool(scr_ref, u, row0, tm, E):
    gc = E // len(POOL_WINDOWS)
    t1 = row0 + lax.broadcasted_iota(jnp.int32, (tm, 1), 0) + 1
    out = []
    for g, win in enumerate(POOL_WINDOWS):
        cs = slice(g * gc, (g + 1) * gc)
        acc = u[:, cs]
        for k in range(1, win):
            acc = acc + scr_ref[pl.ds(POOL_HALO - k, tm), cs]
        count = jnp.minimum(t1, win).astype(F32)
        out.append(acc / count - u[:, cs])
    return out


def _a_mid_fwd(proj, wg, scale, name):
    S, E2 = proj.shape
    E = E2 // 2
    gc = E // len(POOL_WINDOWS)
    tm = min(256, S)
    hb = tm // POOL_HALO

    def body(u_ref, uh_ref, gt_ref, wg_ref, sc_ref, z_ref, scr_ref):
        i = pl.program_id(0)
        scr_ref[0:POOL_HALO, :] = jnp.where(i > 0, uh_ref[...], 0.0)
        u = u_ref[...]
        scr_ref[POOL_HALO:POOL_HALO + tm, :] = u
        pooled = _pool(scr_ref, u, i * tm, tm, E)
        for g in range(len(POOL_WINDOWS)):
            cs = slice(g * gc, (g + 1) * gc)
            y = jnp.dot(pooled[g].astype(BF16), wg_ref[g], preferred_element_type=F32) * sc_ref[:, cs]
            gate = gt_ref[:, cs]
            z_ref[:, cs] = (y * (gate * _sigmoid(gate))).astype(BF16)

    return pl.pallas_call(
        body, name=name, grid=(S // tm,),
        in_specs=[pl.BlockSpec((tm, E), lambda i: (i, 0)),
                  pl.BlockSpec((POOL_HALO, E), lambda i: (jnp.maximum(i * hb - 1, 0), 0)),
                  pl.BlockSpec((tm, E), lambda i: (i, 1)),
                  pl.BlockSpec((len(POOL_WINDOWS), gc, gc), lambda i: (0, 0, 0)),
                  pl.BlockSpec((1, E), lambda i: (0, 0))],
        out_specs=pl.BlockSpec((tm, E), lambda i: (i, 0)),
        out_shape=jax.ShapeDtypeStruct((S, E), BF16),
        scratch_shapes=[pltpu.VMEM((POOL_HALO + tm, E), F32)],
        compiler_params=_params("parallel"))(proj, proj, proj, wg, scale)


def _a_mid_bwd(dz, proj, wg, scale, name):
    S, E2 = proj.shape
    E = E2 // 2
    n_grp = len(POOL_WINDOWS)
    gc = E // n_grp
    tm = min(256, S)
    hb = tm // POOL_HALO
    n_tiles = S // tm
    last_halo = S // POOL_HALO - 1

    def body(dz_ref, dzh_ref, u_ref, uh_ref, gt_ref, gth_ref, wg_ref, sc_ref, dp_ref, dwg_ref, dsc_ref, scr_ref, q_ref):
        i = pl.program_id(0)

        @pl.when(i == 0)
        def _():
            dwg_ref[...] = jnp.zeros_like(dwg_ref)
            dsc_ref[...] = jnp.zeros_like(dsc_ref)

        scr_ref[0:POOL_HALO, :] = jnp.where(i > 0, uh_ref[...], 0.0)
        u = u_ref[...]
        scr_ref[POOL_HALO:POOL_HALO + tm, :] = u
        pooled = _pool(scr_ref, u, i * tm, tm, E)
        t1 = i * tm + lax.broadcasted_iota(jnp.int32, (tm, 1), 0) + 1
        t1h = (i + 1) * tm + lax.broadcasted_iota(jnp.int32, (POOL_HALO, 1), 0) + 1
        not_last = i < n_tiles - 1
        for g, win in enumerate(POOL_WINDOWS):
            cs = slice(g * gc, (g + 1) * gc)
            w = wg_ref[g]
            sc = sc_ref[:, cs]
            pb = pooled[g].astype(BF16)
            ypre = jnp.dot(pb, w, preferred_element_type=F32)
            gate = gt_ref[:, cs]
            sg = _sigmoid(gate)
            silu = gate * sg
            dzg = dz_ref[:, cs]
            dy = dzg * silu
            dp_ref[:, E + g * gc:E + (g + 1) * gc] = (dzg * (ypre * sc) * (sg * (1.0 + gate * (1.0 - sg)))).astype(BF16)
            dsc_ref[:, cs] += jnp.sum(dy * ypre, axis=0, keepdims=True)
            dyp = (dy * sc).astype(BF16)
            dwg_ref[g] += lax.dot_general(pb, dyp, TN, preferred_element_type=F32)
            dpool = lax.dot_general(dyp, w, NT, preferred_element_type=F32)
            gate_h = gth_ref[:, cs]
            dyp_h = (dzh_ref[:, cs] * (gate_h * _sigmoid(gate_h)) * sc).astype(BF16)
            dpool_h = lax.dot_general(dyp_h, w, NT, preferred_element_type=F32)
            q_ref[0:tm, cs] = dpool / jnp.minimum(t1, win).astype(F32)
            q_ref[tm:tm + POOL_HALO, cs] = jnp.where(not_last, dpool_h / jnp.minimum(t1h, win).astype(F32), 0.0)
            acc = q_ref[0:tm, cs] - dpool
            for k in range(1, win):
                acc = acc + q_ref[pl.ds(k, tm), cs]
            dp_ref[:, cs] = acc.astype(BF16)

    return pl.pallas_call(
        body, name=name, grid=(n_tiles,),
        in_specs=[pl.BlockSpec((tm, E), lambda i: (i, 0)),
                  pl.BlockSpec((POOL_HALO, E), lambda i: (jnp.minimum((i + 1) * hb, last_halo), 0)),
                  pl.BlockSpec((tm, E), lambda i: (i, 0)),
                  pl.BlockSpec((POOL_HALO, E), lambda i: (jnp.maximum(i * hb - 1, 0), 0)),
                  pl.BlockSpec((tm, E), lambda i: (i, 1)),
                  pl.BlockSpec((POOL_HALO, E), lambda i: (jnp.minimum((i + 1) * hb, last_halo), 1)),
                  pl.BlockSpec((n_grp, gc, gc), lambda i: (0, 0, 0)),
                  pl.BlockSpec((1, E), lambda i: (0, 0))],
        out_specs=[pl.BlockSpec((tm, E2), lambda i: (i, 0)),
                   pl.BlockSpec((n_grp, gc, gc), lambda i: (0, 0, 0)),
                   pl.BlockSpec((1, E), lambda i: (0, 0))],
        out_shape=[jax.ShapeDtypeStruct((S, E2), BF16),
                   jax.ShapeDtypeStruct((n_grp, gc, gc), F32),
                   jax.ShapeDtypeStruct((1, E), F32)],
        scratch_shapes=[pltpu.VMEM((POOL_HALO + tm, E), F32), pltpu.VMEM((tm + POOL_HALO, E), F32)],
        compiler_params=_params("arbitrary"))(dz, dz, proj, proj, proj, proj, wg, scale)


def _rope_k(kvp, cos2, sin2, name):
    S, E2 = kvp.shape
    E = E2 // 2
    tm = min(256, S)

    def body(k_ref, c_ref, s_ref, ko_ref):
        cosv, sinv = c_ref[...], s_ref[...]
        for h in range(E // HEAD_DIM):
            hs = slice(h * HEAD_DIM, (h + 1) * HEAD_DIM)
            ko_ref[:, hs] = _rope(k_ref[:, hs], cosv, sinv)

    return pl.pallas_call(
        body, name=name, grid=(S // tm,),
        in_specs=[pl.BlockSpec((tm, E), lambda i: (i, 0)),
                  pl.BlockSpec((tm, HEAD_DIM), lambda i: (i, 0)), pl.BlockSpec((tm, HEAD_DIM), lambda i: (i, 0))],
        out_specs=pl.BlockSpec((tm, E), lambda i: (i, 0)),
        out_shape=jax.ShapeDtypeStruct((S, E), F32),
        compiler_params=_params("parallel"))(kvp, cos2, sin2)


ATTN_ROWS = 2048


def _rows(r, b, dil, n=BAND):
    start = r + b * BAND * dil
    return pl.ds(start, n) if dil == 1 else pl.ds(start, n, stride=dil)


def _band_mask(first):
    row = lax.broadcasted_iota(jnp.int32, (BAND, 2 * BAND), 0)
    col = lax.broadcasted_iota(jnp.int32, (BAND, 2 * BAND), 1)
    mask = (col >= row) & (col <= row + BAND)
    return mask if first is None else mask & (jnp.logical_not(first) | (col >= BAND))


def _lane_column(tile, lane, h):
    return jnp.sum(jnp.where(lane == h, tile, 0.0), axis=-1, keepdims=True)


def _keys(cur_ref, prev_ref, r, b, dil, nq):
    if b > 0:
        return cur_ref[_rows(r, b - 1, dil, 2 * BAND), :].astype(BF16)
    return jnp.concatenate([prev_ref[_rows(r, nq - 1, dil), :], cur_ref[_rows(r, 0, dil), :]], axis=0).astype(BF16)


def _block_of(ref, n, R):
    return ref.at[pl.ds(pl.multiple_of(n * R, R), R), :]


def _attn_fwd(proj, kr, kvp, cos2, sin2, group, dil, name):
    S, PW = proj.shape
    E = kr.shape[1]
    H = E // HEAD_DIM
    R = min(ATTN_ROWS, S)
    nq = R // (BAND * dil)
    q_col0, v_col0 = group * H, H

    def body(q_ref, kc_ref, vc_ref, cos_ref, sin_ref, o_ref, lse_ref, kp_ref, vp_ref):
        h, n = pl.program_id(0), pl.program_id(1)
        edge, inner = _band_mask(n == 0), _band_mask(None)
        lane = lax.broadcasted_iota(jnp.int32, (BAND, HEAD_DIM), 1)
        c_ref, s_ref, l_ref = _block_of(cos_ref, n, R), _block_of(sin_ref, n, R), _block_of(lse_ref, n, R)

        @pl.when(h == 0)
        def _():
            l_ref[...] = jnp.zeros((R, HEAD_DIM), F32)

        @pl.when(n == 0)
        def _():
            kp_ref[...] = jnp.zeros_like(kp_ref)
            vp_ref[...] = jnp.zeros_like(vp_ref)

        for r in range(dil):
            for b in range(nq):
                rows = _rows(r, b, dil)
                qr = _rope(q_ref[rows, :], c_ref[rows, :], s_ref[rows, :]).astype(BF16)
                kcat = _keys(kc_ref, kp_ref, r, b, dil, nq)
                vcat = _keys(vc_ref, vp_ref, r, b, dil, nq)
                s = lax.dot_general(qr, kcat, NT, preferred_element_type=F32) * ATTN_SCALE
                s = jnp.where(edge if b == 0 else inner, s, NEG_INF)
                m = jnp.max(s, axis=-1, keepdims=True)
                p = jnp.exp(s - m)
                l = jnp.sum(p, axis=-1, keepdims=True)
                o_ref[rows, :] = jnp.dot(p.astype(BF16), vcat, preferred_element_type=F32) / l
                l_ref[rows, :] = jnp.where(lane == h, m + jnp.log(l), l_ref[rows, :])
        kp_ref[...] = kc_ref[...]
        vp_ref[...] = vc_ref[...]

    blk = (R, HEAD_DIM)
    whole = pl.BlockSpec((S, HEAD_DIM), lambda h, n: (0, 0))
    return pl.pallas_call(
        body, name=name, grid=(H, S // R),
        in_specs=[pl.BlockSpec(blk, lambda h, n: (n, q_col0 + h)),
                  pl.BlockSpec(blk, lambda h, n: (n, h)),
                  pl.BlockSpec(blk, lambda h, n: (n, v_col0 + h)),
                  whole, whole],
        out_specs=[pl.BlockSpec(blk, lambda h, n: (n, h)), whole],
        out_shape=[jax.ShapeDtypeStruct((S, E), F32), jax.ShapeDtypeStruct((S, HEAD_DIM), F32)],
        scratch_shapes=[pltpu.VMEM(blk, F32), pltpu.VMEM(blk, F32)],
        compiler_params=_params("arbitrary", "arbitrary"))(proj, kr, kvp, cos2, sin2)


def _attn_bwd(proj, kr, kvp, cos2, sin2, do, lse, dlt, dproj, group, dil, name):
    S, PW = proj.shape
    E = kr.shape[1]
    H = E // HEAD_DIM
    R = min(ATTN_ROWS, S)
    nq = R // (BAND * dil)
    nsb = S // R
    q_col0, v_col0 = group * H, H

    def body(q_ref, kc_ref, vc_ref, cos_ref, sin_ref, do_ref, lse_ref, dlt_ref, dproj_ref,
             dq_ref, dk_ref, dv_ref, kp_ref, vp_ref, dq_scr, pk_ref, pv_ref, ck_ref, cv_ref):
        h, n = pl.program_id(0), pl.program_id(1)

        @pl.when(n == 0)
        def _():
            for ref in (kp_ref, vp_ref, ck_ref, cv_ref):
                ref[...] = jnp.zeros_like(ref)

        pk_ref[...] = ck_ref[...]
        pv_ref[...] = cv_ref[...]

        @pl.when(n < nsb)
        def _():
            ck_ref[...] = jnp.zeros_like(ck_ref)
            cv_ref[...] = jnp.zeros_like(cv_ref)
            edge, inner = _band_mask(n == 0), _band_mask(None)
            lane = lax.broadcasted_iota(jnp.int32, (BAND, HEAD_DIM), 1)
            c_ref, s_ref = _block_of(cos_ref, n, R), _block_of(sin_ref, n, R)
            l_ref, dl_ref = _block_of(lse_ref, n, R), _block_of(dlt_ref, n, R)
            for r in range(dil):
                for b in range(nq):
                    rows = _rows(r, b, dil)
                    cosv, sinv = c_ref[rows, :], s_ref[rows, :]
                    qr = _rope(q_ref[rows, :], cosv, sinv).astype(BF16)
                    kcat = _keys(kc_ref, kp_ref, r, b, dil, nq)
                    vcat = _keys(vc_ref, vp_ref, r, b, dil, nq)
                    s = lax.dot_general(qr, kcat, NT, preferred_element_type=F32) * ATTN_SCALE
                    s = jnp.where(edge if b == 0 else inner, s, NEG_INF)
                    p = jnp.exp(s - _lane_column(l_ref[rows, :], lane, h))
                    dob = do_ref[rows, :].astype(BF16)
                    dpr = lax.dot_general(dob, vcat, NT, preferred_element_type=F32)
                    ds = (p * (dpr - _lane_column(dl_ref[rows, :], lane, h)) * ATTN_SCALE).astype(BF16)
                    dq = jnp.dot(ds, kcat, preferred_element_type=F32)
                    dq_scr[rows, :] = _rope_bwd(dq, cosv, sinv)
                    dkc = lax.dot_general(ds, qr, TN, preferred_element_type=F32)
                    dvc = lax.dot_general(p.astype(BF16), dob, TN, preferred_element_type=F32)
                    if b > 0:
                        both = _rows(r, b - 1, dil, 2 * BAND)
                        ck_ref[both, :] += dkc
                        cv_ref[both, :] += dvc
                    else:
                        last = _rows(r, nq - 1, dil)
                        pk_ref[last, :] += dkc[0:BAND]
                        pv_ref[last, :] += dvc[0:BAND]
                        ck_ref[rows, :] += dkc[BAND:2 * BAND]
                        cv_ref[rows, :] += dvc[BAND:2 * BAND]
            dq_ref[...] = dq_scr[...].astype(BF16)
            kp_ref[...] = kc_ref[...]
            vp_ref[...] = vc_ref[...]

        dk_ref[...] = pk_ref[...].astype(BF16)
        dv_ref[...] = pv_ref[...].astype(BF16)

    top = nsb - 1
    blk = (R, HEAD_DIM)
    cur = lambda h, n: (jnp.minimum(n, top), h)
    late = lambda h, n: (jnp.maximum(n - 1, 0), h)
    whole = pl.BlockSpec((S, HEAD_DIM), lambda h, n: (0, 0))
    return pl.pallas_call(
        body, name=name, grid=(H, nsb + 1),
        in_specs=[pl.BlockSpec(blk, lambda h, n: (jnp.minimum(n, top), q_col0 + h)),
                  pl.BlockSpec(blk, cur),
                  pl.BlockSpec(blk, lambda h, n: (jnp.minimum(n, top), v_col0 + h)),
                  whole, whole, pl.BlockSpec(blk, cur), whole, whole, ANY],
        out_specs=[pl.BlockSpec(blk, lambda h, n: (jnp.minimum(n, top), q_col0 + h)),
                   pl.BlockSpec(blk, late), pl.BlockSpec(blk, late)],
        out_shape=[jax.ShapeDtypeStruct(dproj.shape, BF16), jax.ShapeDtypeStruct((S, E), BF16),
                   jax.ShapeDtypeStruct((S, E), BF16)],
        scratch_shapes=[pltpu.VMEM(blk, F32)] * 7,
        input_output_aliases={8: 0},
        compiler_params=_params("parallel", "arbitrary"))(proj, kr, kvp, cos2, sin2, do, lse, dlt, dproj)


def _group_weights(l_refs, h):
    ls = [r[:, h:h + 1] for r in l_refs]
    mx = jnp.maximum(jnp.maximum(ls[0], ls[1]), ls[2])
    es = [jnp.exp(l - mx) for l in ls]
    inv = 1.0 / (es[0] + es[1] + es[2])
    return [e * inv for e in es]


def _merge_fwd(outs, lses, proj, name):
    S, E = outs[0].shape
    tm = min(256, S)
    gate_col = proj.shape[1] // E - 1

    def body(o0, o1, o2, l0, l1, l2, gt_ref, z_ref):
        for h in range(E // HEAD_DIM):
            hs = slice(h * HEAD_DIM, (h + 1) * HEAD_DIM)
            a = _group_weights((l0, l1, l2), h)
            merged = a[0] * o0[:, hs] + a[1] * o1[:, hs] + a[2] * o2[:, hs]
            gate = gt_ref[:, hs]
            z_ref[:, hs] = (merged * (gate * _sigmoid(gate))).astype(BF16)

    wide = pl.BlockSpec((tm, E), lambda i: (i, 0))
    thin = pl.BlockSpec((tm, HEAD_DIM), lambda i: (i, 0))
    return pl.pallas_call(
        body, name=name, grid=(S // tm,),
        in_specs=[wide, wide, wide, thin, thin, thin, pl.BlockSpec((tm, E), lambda i: (i, gate_col))],
        out_specs=wide,
        out_shape=jax.ShapeDtypeStruct((S, E), BF16),
        compiler_params=_params("parallel"))(*outs, *lses, proj)


def _merge_bwd(dz, outs, lses, proj, name):
    S, E = outs[0].shape
    tm = min(256, S)
    gate_col = proj.shape[1] // E - 1

    def body(dz_ref, o0, o1, o2, l0, l1, l2, gt_ref, d0, d1, d2, t0, t1, t2, dg_ref):
        o_refs, d_refs, t_refs = (o0, o1, o2), (d0, d1, d2), (t0, t1, t2)
        lane = lax.broadcasted_iota(jnp.int32, (tm, HEAD_DIM), 1)
        tiles = [jnp.zeros((tm, HEAD_DIM), F32) for _ in range(3)]
        for h in range(E // HEAD_DIM):
            hs = slice(h * HEAD_DIM, (h + 1) * HEAD_DIM)
            a = _group_weights((l0, l1, l2), h)
            merged = a[0] * o0[:, hs] + a[1] * o1[:, hs] + a[2] * o2[:, hs]
            gate = gt_ref[:, hs]
            sg = _sigmoid(gate)
            dzh = dz_ref[:, hs]
            dmerged = dzh * (gate * sg)
            dg_ref[:, hs] = (dzh * merged * (sg * (1.0 + gate * (1.0 - sg)))).astype(BF16)
            tot = jnp.sum(dmerged * merged, axis=-1, keepdims=True)
            for g in range(3):
                d_refs[g][:, hs] = a[g] * dmerged
                tiles[g] = jnp.where(lane == h, a[g] * tot, tiles[g])
        for g in range(3):
            t_refs[g][...] = tiles[g]

    wide = pl.BlockSpec((tm, E), lambda i: (i, 0))
    thin = pl.BlockSpec((tm, HEAD_DIM), lambda i: (i, 0))
    res = pl.pallas_call(
        body, name=name, grid=(S // tm,),
        in_specs=[wide, wide, wide, wide, thin, thin, thin, pl.BlockSpec((tm, E), lambda i: (i, gate_col))],
        out_specs=[wide, wide, wide, thin, thin, thin, pl.BlockSpec((tm, E), lambda i: (i, gate_col))],
        out_shape=[jax.ShapeDtypeStruct((S, E), F32)] * 3 + [jax.ShapeDtypeStruct((S, HEAD_DIM), F32)] * 3
        + [jax.ShapeDtypeStruct(proj.shape, BF16)],
        compiler_params=_params("parallel"))(dz, *outs, *lses, proj)
    return res[0:3], res[3:6], res[6]


def _kv_bwd(dks, dvs, cos2, sin2, name):
    S, E = dks[0].shape
    n = len(dks)
    tm = min(256, S)

    def body(*refs):
        dk_refs, dv_refs = refs[0:n], refs[n:2 * n]
        c_ref, s_ref, o_ref = refs[2 * n:]
        cosv, sinv = c_ref[...], s_ref[...]
        for h in range(E // HEAD_DIM):
            hs = slice(h * HEAD_DIM, (h + 1) * HEAD_DIM)
            dk = dk_refs[0][:, hs].astype(F32)
            dv = dv_refs[0][:, hs].astype(F32)
            for j in range(1, n):
                dk = dk + dk_refs[j][:, hs].astype(F32)
                dv = dv + dv_refs[j][:, hs].astype(F32)
            o_ref[:, hs] = _rope_bwd(dk, cosv, sinv).astype(BF16)
            o_ref[:, E + h * HEAD_DIM:E + (h + 1) * HEAD_DIM] = dv.astype(BF16)

    wide = pl.BlockSpec((tm, E), lambda i: (i, 0))
    thin = pl.BlockSpec((tm, HEAD_DIM), lambda i: (i, 0))
    return pl.pallas_call(
        body, name=name, grid=(S // tm,),
        in_specs=[wide] * (2 * n) + [thin, thin],
        out_specs=pl.BlockSpec((tm, 2 * E), lambda i: (i, 0)),
        out_shape=jax.ShapeDtypeStruct((S, 2 * E), BF16),
        compiler_params=_params("parallel"))(*dks, *dvs, cos2, sin2)


def _norm_bwd(dhn, x, gain, dres, name):
    S, D = x.shape
    tm = min(256, S)

    def body(dh_ref, x_ref, g_ref, r_ref, dx_ref, dg_ref):
        @pl.when(pl.program_id(0) == 0)
        def _():
            dg_ref[...] = jnp.zeros_like(dg_ref)
        xf = x_ref[...]
        inv = lax.rsqrt(jnp.mean(xf * xf, axis=-1, keepdims=True) + RMS_EPS)
        xhat = xf * inv
        dh = dh_ref[...]
        dg_ref[...] += jnp.sum(dh * xhat, axis=0, keepdims=True)
        dxh = dh * g_ref[...]
        dx_ref[...] = r_ref[...] + inv * (dxh - xhat * jnp.mean(dxh * xhat, axis=-1, keepdims=True))

    tile = pl.BlockSpec((tm, D), lambda i: (i, 0))
    vec = pl.BlockSpec((1, D), lambda i: (0, 0))
    return pl.pallas_call(
        body, name=name, grid=(S // tm,),
        in_specs=[tile, tile, vec, tile],
        out_specs=[tile, vec],
        out_shape=[jax.ShapeDtypeStruct((S, D), F32), jax.ShapeDtypeStruct((1, D), F32)],
        compiler_params=_params("arbitrary"))(dhn, x, gain, dres)


def _final_norm_loss(x, target, gain, name):
    S, D = x.shape
    tm = min(256, S)

    def body(x_ref, t_ref, g_ref, loss_ref, dx_ref, dg_ref):
        @pl.when(pl.program_id(0) == 0)
        def _():
            loss_ref[...] = jnp.zeros_like(loss_ref)
            dg_ref[...] = jnp.zeros_like(dg_ref)
        xf = x_ref[...]
        inv = lax.rsqrt(jnp.mean(xf * xf, axis=-1, keepdims=True) + RMS_EPS)
        xhat = xf * inv
        g = g_ref[...]
        err = xhat * g - t_ref[...]
        loss_ref[...] += 0.5 * jnp.sum(jnp.mean(err * err, axis=-1, keepdims=True), axis=0, keepdims=True)
        dy = err / D
        dg_ref[...] += jnp.sum(dy * xhat, axis=0, keepdims=True)
        dxh = dy * g
        dx_ref[...] = inv * (dxh - xhat * jnp.mean(dxh * xhat, axis=-1, keepdims=True))

    tile = pl.BlockSpec((tm, D), lambda i: (i, 0))
    vec = pl.BlockSpec((1, D), lambda i: (0, 0))
    return pl.pallas_call(
        body, name=name, grid=(S // tm,),
        in_specs=[tile, tile, vec],
        out_specs=[pl.BlockSpec((1, 1), lambda i: (0, 0)), tile, vec],
        out_shape=[jax.ShapeDtypeStruct((1, 1), F32), jax.ShapeDtypeStruct((S, D), F32),
                   jax.ShapeDtypeStruct((1, D), F32)],
        compiler_params=_params("arbitrary"))(x, target, gain)


def _adamw_math(g, w, m, v):
    m = ADAM_B1 * m + (1.0 - ADAM_B1) * g
    v = ADAM_B2 * v + (1.0 - ADAM_B2) * (g * g)
    m_hat = m / (1.0 - ADAM_B1 ** ADAM_STEP)
    v_hat = v / (1.0 - ADAM_B2 ** ADAM_STEP)
    delta = -ADAM_LR * (m_hat / (jnp.sqrt(v_hat) + ADAM_EPS) + ADAM_WD * w)
    return delta, m, v


def _adamw_rows(g, w, m, v, name):
    def body(g_ref, w_ref, m_ref, v_ref, d_ref, mo_ref, vo_ref):
        d_ref[...], mo_ref[...], vo_ref[...] = _adamw_math(g_ref[...], w_ref[...], m_ref[...], v_ref[...])

    whole = pl.BlockSpec(memory_space=pltpu.VMEM)
    return pl.pallas_call(
        body, name=name, in_specs=[whole] * 4, out_specs=[whole] * 3,
        out_shape=[jax.ShapeDtypeStruct(g.shape, F32)] * 3)(g, w, m, v)


def _adamw_blocks(own, others, w, m, v, layer, earlier, name):
    L, R, C = w.shape
    n = others.shape[0]
    tr = R
    while tr * C > 128 * 1024 and tr % 16 == 0:
        tr //= 2

    def body(o_ref, p_ref, w_ref, m_ref, v_ref, *rest):
        g_ref, d_ref, mo_ref, vo_ref = rest[-4:]
        g = o_ref[...].astype(F32)
        for j in range(n):
            g = g + p_ref[j].astype(F32)
        g_ref[...] = g
        d_ref[...], mo_ref[...], vo_ref[...] = _adamw_math(g, w_ref[...], m_ref[...], v_ref[...])

    tile = pl.BlockSpec((None, tr, C), lambda i: (layer, i, 0))
    kept = [] if earlier is None else list(earlier)
    return pl.pallas_call(
        body, name=name, grid=(R // tr,),
        in_specs=[pl.BlockSpec((tr, C), lambda i: (i, 0)), pl.BlockSpec((n, tr, C), lambda i: (0, i, 0)),
                  tile, tile, tile] + [ANY] * len(kept),
        out_specs=[tile] * 4,
        out_shape=[jax.ShapeDtypeStruct((L, R, C), F32)] * 4,
        input_output_aliases={5 + j: j for j in range(len(kept))},
        compiler_params=_params("parallel"))(own, others, w, m, v, *kept)


def _position():
    return lax.axis_index("x"), lax.axis_index("y"), lax.axis_index("c")


def _block_index(px, py, pc):
    return 4 * px + 2 * py + pc


def _all_gather(shards, name):
    n = len(shards)

    def body(*refs):
        ins, outs = refs[0:n], refs[n:2 * n]
        send_sems, recv_sems, local_sems = refs[2 * n:]
        x, y, c = _position()
        me, sibling = (x, y, c), (x, y, 1 - c)
        chips = [(1 - x, y), (x, 1 - y), (1 - x, 1 - y)]

        def copy(a, k, block, to, src=None):
            rows = outs[a].at[_block_index(*block)]
            return pltpu.make_async_remote_copy(
                src_ref=rows if src is None else src, dst_ref=rows,
                send_sem=send_sems.at[a, k], recv_sem=recv_sems.at[a, k], device_id=to, device_id_type=MESH)

        mine, first, passed = [], [], []
        for a in range(n):
            cp = pltpu.make_async_copy(ins[a], outs[a].at[_block_index(*me)], local_sems.at[a])
            cp.start()
            mine.append(cp)
            first.append(copy(a, 0, me, sibling, src=ins[a]))
            first += [copy(a, 1 + j, me, (*chip, c), src=ins[a]) for j, chip in enumerate(chips)]
        for cp in first:
            cp.start()
        for j, chip in enumerate(chips):
            for a in range(n):
                copy(a, 1 + j, (*chip, c), me).wait_recv()
                fwd = copy(a, 4 + j, (*chip, c), sibling)
                fwd.start()
                passed.append(fwd)
        for a in range(n):
            copy(a, 0, sibling, me).wait_recv()
            for j, chip in enumerate(chips):
                copy(a, 4 + j, (*chip, 1 - c), me).wait_recv()
        for cp in first + passed:
            cp.wait_send()
        for cp in mine:
            cp.wait()

    return pl.pallas_call(
        body, name=name,
        in_specs=[ANY] * n, out_specs=[ANY] * n,
        out_shape=[jax.ShapeDtypeStruct((N_DEV,) + s.shape, s.dtype) for s in shards],
        scratch_shapes=[pltpu.SemaphoreType.DMA((n, 7)), pltpu.SemaphoreType.DMA((n, 7)),
                        pltpu.SemaphoreType.DMA((n,))],
    )(*shards)


def _peers(x, y, c):
    return [((1 - x) if k & 4 else x, (1 - y) if k & 2 else y, (1 - c) if k & 1 else c) for k in range(1, N_DEV)]


HBM = pl.BlockSpec(memory_space=pltpu.HBM)
SEM = pl.BlockSpec(memory_space=pltpu.SEMAPHORE)
EFFECT = pltpu.SideEffectType.DATAFLOW_SIDE_EFFECTING


ALL_PEERS = (1, 2, 3, 4, 5, 6, 7)
SIBLING_AND_SAME_CORES = (1, 2, 4, 6)


def _push_copy(src_refs, land_refs, send_sems, recv_sems, a, i, relations, per_peer, by_sender, arriving):
    peer = _peers(*_position())[relations[i] - 1]
    me_idx, p_idx = _block_index(*_position()), _block_index(*peer)
    src = src_refs[a].at[p_idx] if per_peer else src_refs[a]
    if by_sender:
        slot = p_idx if arriving else me_idx
    else:
        slot = relations[i] - 1
    sem = a * len(relations) + i
    return pltpu.make_async_remote_copy(
        src_ref=src, dst_ref=land_refs[a].at[slot], send_sem=send_sems.at[sem], recv_sem=recv_sems.at[sem],
        device_id=peer, device_id_type=MESH)


def _push_start(srcs, lands, relations, per_peer, by_sender, after, name):
    n = len(srcs)

    def body(*refs):
        src_refs, land_refs = refs[0:n], refs[n:2 * n]
        send_sems, recv_sems = refs[2 * n + 1], refs[2 * n + 2]
        token = refs[-1]
        for a in range(n):
            for i in range(len(relations)):
                _push_copy(src_refs, land_refs, send_sems, recv_sems, a, i, relations, per_peer, by_sender, False).start()
        token[...] = jnp.zeros_like(token)

    args = [pltpu.with_memory_space_constraint(t, pltpu.HBM) for t in list(srcs) + list(lands)]
    res = pl.pallas_call(
        body, name=name,
        in_specs=[HBM] * (2 * n) + [ANY],
        out_specs=[SEM, SEM] + [HBM] * (2 * n) + [pl.BlockSpec(memory_space=pltpu.VMEM)],
        out_shape=[pltpu.SemaphoreType.DMA((n * len(relations),)), pltpu.SemaphoreType.DMA((n * len(relations),))]
        + [pltpu.HBM(t.shape, t.dtype) for t in args] + [jax.ShapeDtypeStruct((8, 128), F32)],
        input_output_aliases={i: 2 + i for i in range(2 * n)},
        compiler_params=pltpu.CompilerParams(has_side_effects=EFFECT))(*args, after)
    return res[0], res[1], res[2:2 + n], res[2 + n:2 + 2 * n], res[-1]


def _push_wait(started, relations, per_peer, by_sender, after, name):
    send_sems, recv_sems, srcs, lands, _ = started
    n = len(srcs)

    def body(*refs):
        src_refs, land_refs = refs[0:n], refs[n:2 * n]
        send_s, recv_s = refs[2 * n], refs[2 * n + 1]
        for a in range(n):
            for i in range(len(relations)):
                _push_copy(src_refs, land_refs, send_s, recv_s, a, i, relations, per_peer, by_sender, False).wait_send()
                _push_copy(src_refs, land_refs, send_s, recv_s, a, i, relations, per_peer, by_sender, True).wait_recv()

    res = pl.pallas_call(
        body, name=name,
        in_specs=[HBM] * (2 * n) + [SEM, SEM, ANY],
        out_specs=[HBM] * (2 * n),
        out_shape=[pltpu.HBM(t.shape, t.dtype) for t in list(srcs) + list(lands)],
        input_output_aliases={i: i for i in range(2 * n)},
        compiler_params=pltpu.CompilerParams(has_side_effects=EFFECT))(*srcs, *lands, send_sems, recv_sems, after)
    return res[0:n], res[n:2 * n]


def _pass_on(lands, name):
    n = len(lands)

    def body(*refs):
        outs = refs[n:2 * n]
        send_sems, recv_sems = refs[2 * n:]
        x, y, c = _position()
        chips = [(1 - x, y), (x, 1 - y), (1 - x, 1 - y)]
        copies = []
        for a in range(n):
            for j, chip in enumerate(chips):
                def copy(core):
                    rows = outs[a].at[_block_index(*chip, core)]
                    return pltpu.make_async_remote_copy(
                        src_ref=rows, dst_ref=rows, send_sem=send_sems.at[a, j], recv_sem=recv_sems.at[a, j],
                        device_id=(x, y, 1 - c), device_id_type=MESH)
                copy(c).start()
                copies.append((copy(c), copy(1 - c)))
        for sending, arriving in copies:
            sending.wait_send()
            arriving.wait_recv()

    return pl.pallas_call(
        body, name=name,
        in_specs=[ANY] * n, out_specs=[ANY] * n,
        out_shape=[jax.ShapeDtypeStruct(t.shape, t.dtype) for t in lands],
        input_output_aliases={a: a for a in range(n)},
        scratch_shapes=[pltpu.SemaphoreType.DMA((n, 3)), pltpu.SemaphoreType.DMA((n, 3))],
    )(*lands)


def _all_reduce_rows(v, name):
    R, D = v.shape

    def body(v_ref, o_ref, buf_ref, send_sems, recv_sems):
        x, y, c = _position()
        me_idx = _block_index(x, y, c)
        buf_ref[me_idx] = v_ref[...]
        copies = []
        for k in range(1, N_DEV):
            px = (1 - x) if k & 4 else x
            py = (1 - y) if k & 2 else y
            pc = (1 - c) if k & 1 else c
            rc = pltpu.make_async_remote_copy(
                src_ref=v_ref, dst_ref=buf_ref.at[me_idx],
                send_sem=send_sems.at[k - 1], recv_sem=recv_sems.at[k - 1],
                device_id=(px, py, pc), device_id_type=MESH)
            rc.start()
            copies.append((rc, pltpu.make_async_remote_copy(
                src_ref=v_ref, dst_ref=buf_ref.at[_block_index(px, py, pc)],
                send_sem=send_sems.at[k - 1], recv_sem=recv_sems.at[k - 1],
                device_id=(px, py, pc), device_id_type=MESH)))
        for rc, arrival in copies:
            rc.wait_send()
            arrival.wait_recv()
        acc = buf_ref[0]
        for j in range(1, N_DEV):
            acc = acc + buf_ref[j]
        o_ref[...] = acc

    return pl.pallas_call(
        body, name=name,
        in_specs=[pl.BlockSpec(memory_space=pltpu.VMEM)],
        out_specs=pl.BlockSpec(memory_space=pltpu.VMEM),
        out_shape=jax.ShapeDtypeStruct((R, D), F32),
        scratch_shapes=[pltpu.VMEM((N_DEV, R, D), F32),
                        pltpu.SemaphoreType.DMA((7,)), pltpu.SemaphoreType.DMA((7,))],
    )(v)


def _rope_tables(S):
    inv_freq = 1.0 / (ROPE_THETA ** (jnp.arange(0, HEAD_DIM, 2, dtype=F32) / HEAD_DIM))
    ang = jnp.arange(S, dtype=F32)[:, None] * inv_freq[None, :]
    cos, sin = jnp.cos(ang), jnp.sin(ang)
    return jnp.concatenate([cos, cos], axis=1), jnp.concatenate([-sin, sin], axis=1)


def _local_step(xs, target, vecs, n_a, n_b, get_weights, put_grads):
    S, D = xs.shape
    E = D
    cos2, sin2 = _rope_tables(S)
    ts = min(512, S)

    def col_blocks(w):
        cb = w.shape[2]
        tn = min(cb, 1024)
        per = cb // tn
        return (None, D, tn), (lambda i, j: (j // per, 0, j % per)), N_DEV * per, tn

    def grad_in(hn, dproj, cb, name):
        return _matmul_tn(hn, dproj, (ts, D), lambda j, s: (s, 0), (ts, cb), lambda j, s: (s, j),
                          (N_DEV, D, cb), (None, D, cb), lambda j, s: (j, 0, 0), (D, cb), N_DEV, name)

    def grad_out(z, dx, name, col=0):
        rows = z.shape[1]
        ta = min(1024, rows)
        out = _matmul_tn(z, dx, (ts, ta), lambda a, s: (s, a), (ts, E), lambda a, s: (s, col),
                         (rows, E), (ta, E), lambda a, s: (a, 0), (ta, E), rows // ta, name)
        return out.reshape(N_DEV, rows // N_DEV, E)

    x = xs
    a_saved, b_saved = [], []
    for i in range(n_a):
        w = get_weights(f"a{i}", x)
        blk, idx, nblocks, tn = col_blocks(w["w_in"])
        proj, hn = _norm_matmul(x, vecs["norm_a"][i:i + 1], w["w_in"], blk, idx, nblocks, tn, f"a{i}_in")
        z = _a_mid_fwd(proj, w["w_grp"], vecs["scale_a"][i:i + 1], f"a{i}_mid")
        x_next = _matmul_res(z, w["w_out"], x, f"a{i}_out")
        a_saved.append((x, hn, proj, z, w))
        x = x_next
    x_kv = x
    w_kv = get_weights("kv", x)["w_kv"]
    tn = min(E, 1024)
    kvp, hn_kv = _norm_matmul(x, vecs["norm_kv"], w_kv, (D, tn), lambda i, j: (0, j), 2 * E // tn, tn, "kv_in")
    kr = _rope_k(kvp, cos2, sin2, "kv_rope")
    after = kr
    for i in range(n_b):
        w = get_weights(f"b{i}", after)
        blk, idx, nblocks, tn = col_blocks(w["w_in"])
        proj, hn = _norm_matmul(x, vecs["norm_b"][i:i + 1], w["w_in"], blk, idx, nblocks, tn, f"b{i}_in")
        outs, lses = [], []
        for g, dil in enumerate(DILATIONS):
            o, l = _attn_fwd(proj, kr, kvp, cos2, sin2, g, dil, f"b{i}_attn{g}")
            outs.append(o)
            lses.append(l)
        z = _merge_fwd(outs, lses, proj, f"b{i}_merge")
        x_next = _matmul_res(z, w["w_out"], x, f"b{i}_out")
        b_saved.append((x, hn, proj, z, outs, lses, w))
        x = x_next
        after = x
    loss, dx, dg_f = _final_norm_loss(x, target, vecs["norm_f"], "final")

    vec = {"norm_a": [None] * n_a, "scale_a": [None] * n_a, "norm_b": [None] * n_b, "norm_f": [dg_f]}
    dks, dvs = [], []
    for i in reversed(range(n_b)):
        x_in, hn, proj, z, outs, lses, w = b_saved[i]
        dw_out = grad_out(z, dx, f"b{i}_dwout")
        dz = _matmul_nt_rows(dx, w["w_out"], f"b{i}_dz")
        dos, dlts, dproj = _merge_bwd(dz, outs, lses, proj, f"b{i}_dmerge")
        for g, dil in enumerate(DILATIONS):
            dproj, dk, dv = _attn_bwd(proj, kr, kvp, cos2, sin2, dos[g], lses[g], dlts[g], dproj, g, dil,
                                      f"b{i}_dattn{g}")
            dks.append(dk)
            dvs.append(dv)
        cb = w["w_in"].shape[2]
        tok = put_grads(f"b{i}", {"w_out": dw_out, "w_in": grad_in(hn, dproj, cb, f"b{i}_dwin")})
        dhn = _matmul_nt_cols(dproj, w["w_in"], (None, D, cb), lambda t, j: (j, 0, 0), N_DEV, cb, D, f"b{i}_dhn")
        dx, vec["norm_b"][i] = _norm_bwd(dhn, x_in, vecs["norm_b"][i:i + 1] + tok[0:1, 0:1], dx, f"b{i}_dnorm")

    dkv = _kv_bwd(dks, dvs, cos2, sin2, "kv_dsum")
    tok = put_grads("kv", {"w_k": grad_out(hn_kv, dkv, "kv_dwk", 0), "w_v": grad_out(hn_kv, dkv, "kv_dwv", 1)})
    dhn = _matmul_nt_cols(dkv, w_kv, (D, E), lambda t, j: (0, j), 2, E, D, "kv_dhn")
    dx, dg_kv = _norm_bwd(dhn, x_kv, vecs["norm_kv"] + tok[0:1, 0:1], dx, "kv_dnorm")
    vec["norm_kv"] = [dg_kv]

    for i in reversed(range(n_a)):
        x_in, hn, proj, z, w = a_saved[i]
        dw_out = grad_out(z, dx, f"a{i}_dwout")
        dz = _matmul_nt_rows(dx, w["w_out"], f"a{i}_dz")
        dproj, dwg, dsc = _a_mid_bwd(dz, proj, w["w_grp"], vecs["scale_a"][i:i + 1], f"a{i}_dmid")
        n_grp, gc, _ = dwg.shape
        dwg = dwg.reshape(n_grp, N_DEV, gc // N_DEV, gc).transpose(1, 0, 2, 3).astype(BF16)
        vec["scale_a"][i] = dsc
        cb = w["w_in"].shape[2]
        tok = put_grads(f"a{i}", {"w_out": dw_out, "w_grp": dwg, "w_in": grad_in(hn, dproj, cb, f"a{i}_dwin")})
        dhn = _matmul_nt_cols(dproj, w["w_in"], (None, D, cb), lambda t, j: (j, 0, 0), N_DEV, cb, D, f"a{i}_dhn")
        dx, vec["norm_a"][i] = _norm_bwd(dhn, x_in, vecs["norm_a"][i:i + 1] + tok[0:1, 0:1], dx, f"a{i}_dnorm")

    return loss, dx, {k: jnp.concatenate(v, axis=0) for k, v in vec.items()}


VECTORS = ("norm_a", "scale_a", "norm_kv", "norm_b", "norm_f")
SHARDED_VECTORS = ("norm_a", "scale_a")
GROUPS = {
    "a0": (("w_in", "w_in_a", 0), ("w_grp", "w_grp_a", 0), ("w_out", "w_out_a", 0)),
    "a1": (("w_in", "w_in_a", 1), ("w_grp", "w_grp_a", 1), ("w_out", "w_out_a", 1)),
    "kv": (("w_k", "w_k", None), ("w_v", "w_v", None)),
    "b0": (("w_in", "w_in_b", 0), ("w_out", "w_out_b", 0)),
    "b1": (("w_in", "w_in_b", 1), ("w_out", "w_out_b", 1)),
}
PREFETCHED = ("a1", "kv", "b0", "b1")


def kernel(x, norm_a, w_in_a, w_grp_a, scale_a, w_out_a, norm_kv, w_k, w_v, norm_b, w_in_b, w_out_b, norm_f, loss_target, m_norm_a, m_w_in_a, m_w_grp_a, m_scale_a, m_w_out_a, m_norm_kv, m_w_k, m_w_v, m_norm_b, m_w_in_b, m_w_out_b, m_norm_f, v_norm_a, v_w_in_a, v_w_grp_a, v_scale_a, v_w_out_a, v_norm_kv, v_w_k, v_w_v, v_norm_b, v_w_in_b, v_w_out_b, v_norm_f):
    w = dict(norm_a=norm_a, w_in_a=w_in_a, w_grp_a=w_grp_a, scale_a=scale_a, w_out_a=w_out_a, norm_kv=norm_kv,
             w_k=w_k, w_v=w_v, norm_b=norm_b, w_in_b=w_in_b, w_out_b=w_out_b, norm_f=norm_f)
    m = dict(norm_a=m_norm_a, w_in_a=m_w_in_a, w_grp_a=m_w_grp_a, scale_a=m_scale_a, w_out_a=m_w_out_a,
             norm_kv=m_norm_kv, w_k=m_w_k, w_v=m_w_v, norm_b=m_norm_b, w_in_b=m_w_in_b, w_out_b=m_w_out_b,
             norm_f=m_norm_f)
    v = dict(norm_a=v_norm_a, w_in_a=v_w_in_a, w_grp_a=v_w_grp_a, scale_a=v_scale_a, w_out_a=v_w_out_a,
             norm_kv=v_norm_kv, w_k=v_w_k, w_v=v_w_v, norm_b=v_norm_b, w_in_b=v_w_in_b, w_out_b=v_w_out_b,
             norm_f=v_norm_f)
    D = x.shape[2]
    me = _block_index(*_position())

    def shard(group):
        return [w[p].astype(BF16) if layer is None else w[p][layer].astype(BF16) for _, p, layer in GROUPS[group]]

    def as_weights(group, gathered):
        out = dict(zip([n for n, _, _ in GROUPS[group]], gathered))
        if "w_grp" in out:
            g = out["w_grp"]
            out["w_grp"] = g.transpose(1, 0, 2, 3).reshape(g.shape[1], g.shape[3], g.shape[3])
        if "w_k" in out:
            out = {"w_kv": jnp.concatenate([out["w_k"].reshape(D, D), out["w_v"].reshape(D, D)], axis=1)}
        return out

    first = _all_gather(shard("a0") + [w[k] for k in SHARDED_VECTORS], "gather_first")
    n_first = len(GROUPS["a0"])
    vecs = {k: g.transpose(1, 0, 2).reshape(w[k].shape[0], D) for k, g in zip(SHARDED_VECTORS, first[n_first:])}
    vecs.update(norm_kv=norm_kv[None, :], norm_b=norm_b, norm_f=norm_f[None, :])
    srcs, lands = [], []
    for group in PREFETCHED:
        for s in shard(group):
            srcs.append(s)
            lands.append(lax.dynamic_update_index_in_dim(lax.empty((N_DEV,) + s.shape, s.dtype), s[None], me, 0))
    inflight, at = {}, 0
    token = None
    for group in PREFETCHED:
        n = len(GROUPS[group])
        inflight[group] = _push_start(srcs[at:at + n], lands[at:at + n], SIBLING_AND_SAME_CORES, False, True,
                                      first[0] if token is None else token, f"gather_{group}_start")
        token = inflight[group][4]
        at += n
    vecs["norm_a"] = vecs["norm_a"] + token[0:1, 0:1]

    def get_weights(group, after):
        if group == "a0":
            return as_weights(group, first[0:n_first])
        half = _push_wait(inflight[group], SIBLING_AND_SAME_CORES, False, True, after, f"gather_{group}_wait")[1]
        return as_weights(group, _pass_on(half, f"gather_{group}_pass"))

    sent = {}

    def put_grads(group, grads):
        blocks = [grads[n] for n, _, _ in GROUPS[group]]
        lands = [lax.empty((N_DEV - 1,) + b.shape[1:], b.dtype) for b in blocks]
        sent[group] = _push_start(blocks, lands, ALL_PEERS, True, False, jnp.zeros((8, 128), F32),
                                  f"exchange_{group}_start")
        return sent[group][4]

    loss, dx, vec = _local_step(x[0], loss_target[0], vecs, w_in_a.shape[0], w_in_b.shape[0], get_weights, put_grads)
    rows = _all_reduce_rows(jnp.concatenate([vec[k] for k in VECTORS], axis=0), "reduce_vectors")

    out = {}
    after = dx
    for group in sent:
        blocks, arrived = _push_wait(sent[group], ALL_PEERS, True, False, after, f"exchange_{group}_wait")
        for (_, p, layer), blk, got in zip(GROUPS[group], blocks, arrived):
            cols = w[p].shape[-1]
            own = lax.dynamic_index_in_dim(blk, me, 0, keepdims=False).reshape(-1, cols)
            n_layers = 1 if layer is None else w[p].shape[0]
            stacked = lambda t: t.reshape(n_layers, -1, cols)
            res = _adamw_blocks(own, got.reshape(N_DEV - 1, -1, cols), stacked(w[p]), stacked(m[p]), stacked(v[p]),
                                0 if layer is None else layer, out.get(p), f"adamw_{group}_{p}")
            out[p] = res
            after = res[1]
    out = {p: [r.reshape(w[p].shape) for r in res] for p, res in out.items()}
    start = 0
    for k in VECTORS:
        n_rows = vec[k].shape[0]
        g = rows[start:start + n_rows]
        start += n_rows
        if k in SHARDED_VECTORS:
            g = lax.dynamic_slice_in_dim(g, me * (D // N_DEV), D // N_DEV, axis=1)
        res = _adamw_rows(g, w[k].reshape(g.shape), m[k].reshape(g.shape), v[k].reshape(g.shape), f"adamw_{k}")
        out[k] = [r.reshape(w[k].shape) for r in [g] + list(res)]

    names = ("norm_a", "w_in_a", "w_grp_a", "scale_a", "w_out_a", "norm_kv", "w_k", "w_v", "norm_b", "w_in_b",
             "w_out_b", "norm_f")
    total = lax.psum(loss[0, 0], ("x", "y", "c"))
    return (total, dx[None], *[out[k][0] for k in names], *[out[k][1] for k in names],
            *[out[k][2] for k in names], *[out[k][3] for k in names])
```

```python
import math

import jax
import jax.numpy as jnp
from jax import lax
from jax.experimental import pallas as pl
from jax.experimental.pallas import tpu as pltpu

F32 = jnp.float32
BF16 = jnp.bfloat16

N_DEV = 8
MESH = pl.DeviceIdType.MESH
RMS_EPS = 1e-6
HEAD_DIM = 128
HALF_HEAD = HEAD_DIM // 2
BAND = 128
DILATIONS = (1, 4, 16)
POOL_WINDOWS = (2, 4, 8, 16)
POOL_HALO = 16
ROPE_THETA = 10000.0
NEG_INF = -1e30
ATTN_SCALE = 1.0 / math.sqrt(HEAD_DIM)
ADAM_LR, ADAM_B1, ADAM_B2, ADAM_EPS, ADAM_WD, ADAM_STEP = 0.001, 0.9, 0.999, 1e-08, 0.01, 10
VMEM_LIMIT_BYTES = 56 * 1024 * 1024
ANY = pl.BlockSpec(memory_space=pl.ANY)
NT = (((1,), (1,)), ((), ()))
TN = (((0,), (0,)), ((), ()))


def _params(*semantics):
    return pltpu.CompilerParams(dimension_semantics=semantics, vmem_limit_bytes=VMEM_LIMIT_BYTES)


def _sigmoid(t):
    return 1.0 / (1.0 + jnp.exp(-t))


def _rope(t, cos2, sin2):
    return t * cos2 + pltpu.roll(t, HALF_HEAD, 1) * sin2


def _rope_bwd(dt, cos2, sin2):
    return dt * cos2 + pltpu.roll(dt * sin2, HALF_HEAD, 1)


def _norm_matmul(x, gain, w, w_block, w_index, n_col_blocks, tn, name):
    S, D = x.shape
    tm = min(1024 if tn <= 512 else 512, S)

    def body(x_ref, g_ref, w_ref, o_ref, hn_ref, hs_ref):
        @pl.when(pl.program_id(1) == 0)
        def _():
            xf = x_ref[...]
            inv = lax.rsqrt(jnp.mean(xf * xf, axis=-1, keepdims=True) + RMS_EPS)
            hb = ((xf * inv) * g_ref[...]).astype(BF16)
            hs_ref[...] = hb
            hn_ref[...] = hb
        o_ref[...] = jnp.dot(hs_ref[...], w_ref[...], preferred_element_type=F32)

    return pl.pallas_call(
        body, name=name, grid=(S // tm, n_col_blocks),
        in_specs=[pl.BlockSpec((tm, D), lambda i, j: (i, 0)),
                  pl.BlockSpec((1, D), lambda i, j: (0, 0)),
                  pl.BlockSpec(w_block, w_index)],
        out_specs=[pl.BlockSpec((tm, tn), lambda i, j: (i, j)),
                   pl.BlockSpec((tm, D), lambda i, j: (i, 0))],
        out_shape=[jax.ShapeDtypeStruct((S, n_col_blocks * tn), F32), jax.ShapeDtypeStruct((S, D), BF16)],
        scratch_shapes=[pltpu.VMEM((tm, D), BF16)],
        compiler_params=_params("parallel", "arbitrary"))(x, gain, w)


def _matmul_res(a, w, res, name):
    S, K = a.shape
    nd, rb, N = w.shape
    tm = min(512, S)

    def body(a_ref, w_ref, r_ref, o_ref):
        acc = jnp.dot(a_ref[:, 0:rb], w_ref[0], preferred_element_type=F32)
        for k in range(1, nd):
            acc = acc + jnp.dot(a_ref[:, k * rb:(k + 1) * rb], w_ref[k], preferred_element_type=F32)
        o_ref[...] = r_ref[...] + acc

    return pl.pallas_call(
        body, name=name, grid=(S // tm,),
        in_specs=[pl.BlockSpec((tm, K), lambda i: (i, 0)),
                  pl.BlockSpec((nd, rb, N), lambda i: (0, 0, 0)),
                  pl.BlockSpec((tm, N), lambda i: (i, 0))],
        out_specs=pl.BlockSpec((tm, N), lambda i: (i, 0)),
        out_shape=jax.ShapeDtypeStruct((S, N), F32),
        compiler_params=_params("parallel"))(a, w, res)


def _matmul_nt_rows(dy, w, name):
    S, N = dy.shape
    nd, rb, _ = w.shape
    tm = min(512, S)

    def body(d_ref, w_ref, o_ref):
        db = d_ref[...].astype(BF16)
        for k in range(nd):
            o_ref[:, k * rb:(k + 1) * rb] = lax.dot_general(db, w_ref[k], NT, preferred_element_type=F32)

    return pl.pallas_call(
        body, name=name, grid=(S // tm,),
        in_specs=[pl.BlockSpec((tm, N), lambda i: (i, 0)),
                  pl.BlockSpec((nd, rb, N), lambda i: (0, 0, 0))],
        out_specs=pl.BlockSpec((tm, nd * rb), lambda i: (i, 0)),
        out_shape=jax.ShapeDtypeStruct((S, nd * rb), F32),
        compiler_params=_params("parallel"))(dy, w)


def _matmul_nt_cols(dp, w, w_block, w_index, n_red, tc, n_out, name):
    S = dp.shape[0]
    tm = min(1024, S)

    def body(d_ref, w_ref, o_ref):
        @pl.when(pl.program_id(1) == 0)
        def _():
            o_ref[...] = jnp.zeros_like(o_ref)
        o_ref[...] += lax.dot_general(d_ref[...], w_ref[...], NT, preferred_element_type=F32)

    return pl.pallas_call(
        body, name=name, grid=(S // tm, n_red),
        in_specs=[pl.BlockSpec((tm, tc), lambda i, j: (i, j)),
                  pl.BlockSpec(w_block, w_index)],
        out_specs=pl.BlockSpec((tm, n_out), lambda i, j: (i, 0)),
        out_shape=jax.ShapeDtypeStruct((S, n_out), F32),
        compiler_params=_params("parallel", "arbitrary"))(dp, w)


def _matmul_tn(a, b, a_block, a_index, b_block, b_index, out_shape, out_block, out_index, acc_shape, n_outer, name):
    S = a.shape[0]
    ts = a_block[0]
    n_tok = S // ts

    def body(a_ref, b_ref, o_ref, acc_ref):
        s = pl.program_id(1)

        @pl.when(s == 0)
        def _():
            acc_ref[...] = jnp.zeros_like(acc_ref)
        acc_ref[...] += lax.dot_general(a_ref[...].astype(BF16), b_ref[...].astype(BF16), TN,
                                        preferred_element_type=F32)

        @pl.when(s == n_tok - 1)
        def _():
            o_ref[...] = acc_ref[...].astype(o_ref.dtype)

    return pl.pallas_call(
        body, name=name, grid=(n_outer, n_tok),
        in_specs=[pl.BlockSpec(a_block, a_index), pl.BlockSpec(b_block, b_index)],
        out_specs=pl.BlockSpec(out_block, out_index),
        out_shape=jax.ShapeDtypeStruct(out_shape, BF16),
        scratch_shapes=[pltpu.VMEM(acc_shape, F32)],
        compiler_params=_params("parallel", "arbitrary"))(a, b)


def _pool(scr_ref, u, row0, tm, E):
    gc = E // len(POOL_WINDOWS)
    t1 = row0 + lax.broadcasted_iota(jnp.int32, (tm, 1), 0) + 1
    out = []
    for g, win in enumerate(POOL_WINDOWS):
        cs = slice(g * gc, (g + 1) * gc)
        acc = u[:, cs]
        for k in range(1, win):
            acc = acc + scr_ref[pl.ds(POOL_HALO - k, tm), cs]
        count = jnp.minimum(t1, win).astype(F32)
        out.append(acc / count - u[:, cs])
    return out


def _a_mid_fwd(proj, wg, scale, name):
    S, E2 = proj.shape
    E = E2 // 2
    gc = E // len(POOL_WINDOWS)
    tm = min(256, S)
    hb = tm // POOL_HALO

    def body(u_ref, uh_ref, gt_ref, wg_ref, sc_ref, z_ref, scr_ref):
        i = pl.program_id(0)
        scr_ref[0:POOL_HALO, :] = jnp.where(i > 0, uh_ref[...], 0.0)
        u = u_ref[...]
        scr_ref[POOL_HALO:POOL_HALO + tm, :] = u
        pooled = _pool(scr_ref, u, i * tm, tm, E)
        for g in range(len(POOL_WINDOWS)):
            cs = slice(g * gc, (g + 1) * gc)
            y = jnp.dot(pooled[g].astype(BF16), wg_ref[g], preferred_element_type=F32) * sc_ref[:, cs]
            gate = gt_ref[:, cs]
            z_ref[:, cs] = (y * (gate * _sigmoid(gate))).astype(BF16)

    return pl.pallas_call(
        body, name=name, grid=(S // tm,),
        in_specs=[pl.BlockSpec((tm, E), lambda i: (i, 0)),
                  pl.BlockSpec((POOL_HALO, E), lambda i: (jnp.maximum(i * hb - 1, 0), 0)),
                  pl.BlockSpec((tm, E), lambda i: (i, 1)),
                  pl.BlockSpec((len(POOL_WINDOWS), gc, gc), lambda i: (0, 0, 0)),
                  pl.BlockSpec((1, E), lambda i: (0, 0))],
        out_specs=pl.BlockSpec((tm, E), lambda i: (i, 0)),
        out_shape=jax.ShapeDtypeStruct((S, E), BF16),
        scratch_shapes=[pltpu.VMEM((POOL_HALO + tm, E), F32)],
        compiler_params=_params("parallel"))(proj, proj, proj, wg, scale)


def _a_mid_bwd(dz, proj, wg, scale, name):
    S, E2 = proj.shape
    E = E2 // 2
    n_grp = len(POOL_WINDOWS)
    gc = E // n_grp
    tm = min(256, S)
    hb = tm // POOL_HALO
    n_tiles = S // tm
    last_halo = S // POOL_HALO - 1

    def body(dz_ref, dzh_ref, u_ref, uh_ref, gt_ref, gth_ref, wg_ref, sc_ref, dp_ref, dwg_ref, dsc_ref, scr_ref, q_ref):
        i = pl.program_id(0)

        @pl.when(i == 0)
        def _():
            dwg_ref[...] = jnp.zeros_like(dwg_ref)
            dsc_ref[...] = jnp.zeros_like(dsc_ref)

        scr_ref[0:POOL_HALO, :] = jnp.where(i > 0, uh_ref[...], 0.0)
        u = u_ref[...]
        scr_ref[POOL_HALO:POOL_HALO + tm, :] = u
        pooled = _pool(scr_ref, u, i * tm, tm, E)
        t1 = i * tm + lax.broadcasted_iota(jnp.int32, (tm, 1), 0) + 1
        t1h = (i + 1) * tm + lax.broadcasted_iota(jnp.int32, (POOL_HALO, 1), 0) + 1
        not_last = i < n_tiles - 1
        for g, win in enumerate(POOL_WINDOWS):
            cs = slice(g * gc, (g + 1) * gc)
            w = wg_ref[g]
            sc = sc_ref[:, cs]
            pb = pooled[g].astype(BF16)
            ypre = jnp.dot(pb, w, preferred_element_type=F32)
            gate = gt_ref[:, cs]
            sg = _sigmoid(gate)
            silu = gate * sg
            dzg = dz_ref[:, cs]
            dy = dzg * silu
            dp_ref[:, E + g * gc:E + (g + 1) * gc] = (dzg * (ypre * sc) * (sg * (1.0 + gate * (1.0 - sg)))).astype(BF16)
            dsc_ref[:, cs] += jnp.sum(dy * ypre, axis=0, keepdims=True)
            dyp = (dy * sc).astype(BF16)
            dwg_ref[g] += lax.dot_general(pb, dyp, TN, preferred_element_type=F32)
            dpool = lax.dot_general(dyp, w, NT, preferred_element_type=F32)
            gate_h = gth_ref[:, cs]
            dyp_h = (dzh_ref[:, cs] * (gate_h * _sigmoid(gate_h)) * sc).astype(BF16)
            dpool_h = lax.dot_general(dyp_h, w, NT, preferred_element_type=F32)
            q_ref[0:tm, cs] = dpool / jnp.minimum(t1, win).astype(F32)
            q_ref[tm:tm + POOL_HALO, cs] = jnp.where(not_last, dpool_h / jnp.minimum(t1h, win).astype(F32), 0.0)
            acc = q_ref[0:tm, cs] - dpool
            for k in range(1, win):
                acc = acc + q_ref[pl.ds(k, tm), cs]
            dp_ref[:, cs] = acc.astype(BF16)

    return pl.pallas_call(
        body, name=name, grid=(n_tiles,),
        in_specs=[pl.BlockSpec((tm, E), lambda i: (i, 0)),
                  pl.BlockSpec((POOL_HALO, E), lambda i: (jnp.minimum((i + 1) * hb, last_halo), 0)),
                  pl.BlockSpec((tm, E), lambda i: (i, 0)),
                  pl.BlockSpec((POOL_HALO, E), lambda i: (jnp.maximum(i * hb - 1, 0), 0)),
                  pl.BlockSpec((tm, E), lambda i: (i, 1)),
                  pl.BlockSpec((POOL_HALO, E), lambda i: (jnp.minimum((i + 1) * hb, last_halo), 1)),
                  pl.BlockSpec((n_grp, gc, gc), lambda i: (0, 0, 0)),
                  pl.BlockSpec((1, E), lambda i: (0, 0))],
        out_specs=[pl.BlockSpec((tm, E2), lambda i: (i, 0)),
                   pl.BlockSpec((n_grp, gc, gc), lambda i: (0, 0, 0)),
                   pl.BlockSpec((1, E), lambda i: (0, 0))],
        out_shape=[jax.ShapeDtypeStruct((S, E2), BF16),
                   jax.ShapeDtypeStruct((n_grp, gc, gc), F32),
                   jax.ShapeDtypeStruct((1, E), F32)],
        scratch_shapes=[pltpu.VMEM((POOL_HALO + tm, E), F32), pltpu.VMEM((tm + POOL_HALO, E), F32)],
        compiler_params=_params("arbitrary"))(dz, dz, proj, proj, proj, proj, wg, scale)


def _rope_k(kvp, cos2, sin2, name):
    S, E2 = kvp.shape
    E = E2 // 2
    tm = min(256, S)

    def body(k_ref, c_ref, s_ref, ko_ref):
        cosv, sinv = c_ref[...], s_ref[...]
        for h in range(E // HEAD_DIM):
            hs = slice(h * HEAD_DIM, (h + 1) * HEAD_DIM)
            ko_ref[:, hs] = _rope(k_ref[:, hs], cosv, sinv)

    return pl.pallas_call(
        body, name=name, grid=(S // tm,),
        in_specs=[pl.BlockSpec((tm, E), lambda i: (i, 0)),
                  pl.BlockSpec((tm, HEAD_DIM), lambda i: (i, 0)), pl.BlockSpec((tm, HEAD_DIM), lambda i: (i, 0))],
        out_specs=pl.BlockSpec((tm, E), lambda i: (i, 0)),
        out_shape=jax.ShapeDtypeStruct((S, E), F32),
        compiler_params=_params("parallel"))(kvp, cos2, sin2)


def _rows(r, b, dil, n=BAND):
    start = r + b * BAND * dil
    return pl.ds(start, n) if dil == 1 else pl.ds(start, n, stride=dil)


def _band_mask(first):
    row = lax.broadcasted_iota(jnp.int32, (BAND, 2 * BAND), 0)
    col = lax.broadcasted_iota(jnp.int32, (BAND, 2 * BAND), 1)
    mask = (col >= row) & (col <= row + BAND)
    return mask & (col >= BAND) if first else mask


def _lane_column(tile, lane, h):
    return jnp.sum(jnp.where(lane == h, tile, 0.0), axis=-1, keepdims=True)


def _keys(ref, r, b, dil):
    if b > 0:
        return ref[_rows(r, b - 1, dil, 2 * BAND), :].astype(BF16)
    return jnp.concatenate([jnp.zeros((BAND, HEAD_DIM), BF16), ref[_rows(r, 0, dil), :].astype(BF16)], axis=0)


def _attn_fwd(proj, kr, kvp, cos2, sin2, group, dil, name):
    S, PW = proj.shape
    E = kr.shape[1]
    H = E // HEAD_DIM
    nb = S // (BAND * dil)

    def body(q_ref, k_ref, v_ref, c_ref, s_ref, o_ref, l_ref):
        h = pl.program_id(0)
        lane = lax.broadcasted_iota(jnp.int32, (BAND, HEAD_DIM), 1)
        edge, inner = _band_mask(True), _band_mask(False)

        @pl.when(h == 0)
        def _():
            l_ref[...] = jnp.zeros_like(l_ref)

        for r in range(dil):
            for b in range(nb):
                rows = _rows(r, b, dil)
                qr = _rope(q_ref[rows, :], c_ref[rows, :], s_ref[rows, :]).astype(BF16)
                kcat, vcat = _keys(k_ref, r, b, dil), _keys(v_ref, r, b, dil)
                s = lax.dot_general(qr, kcat, NT, preferred_element_type=F32) * ATTN_SCALE
                s = jnp.where(edge if b == 0 else inner, s, NEG_INF)
                m = jnp.max(s, axis=-1, keepdims=True)
                p = jnp.exp(s - m)
                l = jnp.sum(p, axis=-1, keepdims=True)
                o_ref[rows, :] = jnp.dot(p.astype(BF16), vcat, preferred_element_type=F32) / l
                l_ref[rows, :] = jnp.where(lane == h, m + jnp.log(l), l_ref[rows, :])

    col = (S, HEAD_DIM)
    whole = pl.BlockSpec(col, lambda h: (0, 0))
    return pl.pallas_call(
        body, name=name, grid=(H,),
        in_specs=[pl.BlockSpec(col, lambda h: (0, group * H + h)), pl.BlockSpec(col, lambda h: (0, h)),
                  pl.BlockSpec(col, lambda h: (0, H + h)), whole, whole],
        out_specs=[pl.BlockSpec(col, lambda h: (0, h)), whole],
        out_shape=[jax.ShapeDtypeStruct((S, E), F32), jax.ShapeDtypeStruct((S, HEAD_DIM), F32)],
        compiler_params=_params("arbitrary"))(proj, kr, kvp, cos2, sin2)


def _attn_bwd(proj, kr, kvp, cos2, sin2, do, lse, dlt, dproj, group, dil, name):
    S, PW = proj.shape
    E = kr.shape[1]
    H = E // HEAD_DIM
    nb = S // (BAND * dil)

    def body(q_ref, k_ref, v_ref, c_ref, s_ref, do_ref, l_ref, dl_ref, dproj_ref,
             dq_ref, dk_ref, dv_ref, dq_scr, dk_scr, dv_scr):
        h = pl.program_id(0)
        lane = lax.broadcasted_iota(jnp.int32, (BAND, HEAD_DIM), 1)
        edge, inner = _band_mask(True), _band_mask(False)
        dk_scr[...] = jnp.zeros_like(dk_scr)
        dv_scr[...] = jnp.zeros_like(dv_scr)
        for r in range(dil):
            for b in range(nb):
                rows = _rows(r, b, dil)
                cosv, sinv = c_ref[rows, :], s_ref[rows, :]
                qr = _rope(q_ref[rows, :], cosv, sinv).astype(BF16)
                kcat, vcat = _keys(k_ref, r, b, dil), _keys(v_ref, r, b, dil)
                s = lax.dot_general(qr, kcat, NT, preferred_element_type=F32) * ATTN_SCALE
                s = jnp.where(edge if b == 0 else inner, s, NEG_INF)
                p = jnp.exp(s - _lane_column(l_ref[rows, :], lane, h))
                dob = do_ref[rows, :].astype(BF16)
                dpr = lax.dot_general(dob, vcat, NT, preferred_element_type=F32)
                ds = (p * (dpr - _lane_column(dl_ref[rows, :], lane, h)) * ATTN_SCALE).astype(BF16)
                dq = jnp.dot(ds, kcat, preferred_element_type=F32)
                dq_scr[rows, :] = _rope_bwd(dq, cosv, sinv)
                dkc = lax.dot_general(ds, qr, TN, preferred_element_type=F32)
                dvc = lax.dot_general(p.astype(BF16), dob, TN, preferred_element_type=F32)
                if b > 0:
                    both = _rows(r, b - 1, dil, 2 * BAND)
                    dk_scr[both, :] += dkc
                    dv_scr[both, :] += dvc
                else:
                    dk_scr[rows, :] += dkc[BAND:2 * BAND]
                    dv_scr[rows, :] += dvc[BAND:2 * BAND]
        dq_ref[...] = dq_scr[...].astype(BF16)
        dk_ref[...] = dk_scr[...].astype(BF16)
        dv_ref[...] = dv_scr[...].astype(BF16)

    col = (S, HEAD_DIM)
    whole = pl.BlockSpec(col, lambda h: (0, 0))
    head = pl.BlockSpec(col, lambda h: (0, h))
    return pl.pallas_call(
        body, name=name, grid=(H,),
        in_specs=[pl.BlockSpec(col, lambda h: (0, group * H + h)), head, pl.BlockSpec(col, lambda h: (0, H + h)),
                  whole, whole, head, whole, whole, ANY],
        out_specs=[pl.BlockSpec(col, lambda h: (0, group * H + h)), head, head],
        out_shape=[jax.ShapeDtypeStruct(dproj.shape, BF16), jax.ShapeDtypeStruct((S, E), BF16),
                   jax.ShapeDtypeStruct((S, E), BF16)],
        scratch_shapes=[pltpu.VMEM(col, F32)] * 3,
        input_output_aliases={8: 0},
        compiler_params=_params("parallel"))(proj, kr, kvp, cos2, sin2, do, lse, dlt, dproj)


def _group_weights(l_refs, h):
    ls = [r[:, h:h + 1] for r in l_refs]
    mx = jnp.maximum(jnp.maximum(ls[0], ls[1]), ls[2])
    es = [jnp.exp(l - mx) for l in ls]
    inv = 1.0 / (es[0] + es[1] + es[2])
    return [e * inv for e in es]


def _merge_fwd(outs, lses, proj, name):
    S, E = outs[0].shape
    tm = min(256, S)
    gate_col = proj.shape[1] // E - 1

    def body(o0, o1, o2, l0, l1, l2, gt_ref, z_ref):
        for h in range(E // HEAD_DIM):
            hs = slice(h * HEAD_DIM, (h + 1) * HEAD_DIM)
            a = _group_weights((l0, l1, l2), h)
            merged = a[0] * o0[:, hs] + a[1] * o1[:, hs] + a[2] * o2[:, hs]
            gate = gt_ref[:, hs]
            z_ref[:, hs] = (merged * (gate * _sigmoid(gate))).astype(BF16)

    wide = pl.BlockSpec((tm, E), lambda i: (i, 0))
    thin = pl.BlockSpec((tm, HEAD_DIM), lambda i: (i, 0))
    return pl.pallas_call(
        body, name=name, grid=(S // tm,),
        in_specs=[wide, wide, wide, thin, thin, thin, pl.BlockSpec((tm, E), lambda i: (i, gate_col))],
        out_specs=wide,
        out_shape=jax.ShapeDtypeStruct((S, E), BF16),
        compiler_params=_params("parallel"))(*outs, *lses, proj)


def _merge_bwd(dz, outs, lses, proj, name):
    S, E = outs[0].shape
    tm = min(256, S)
    gate_col = proj.shape[1] // E - 1

    def body(dz_ref, o0, o1, o2, l0, l1, l2, gt_ref, d0, d1, d2, t0, t1, t2, dg_ref):
        o_refs, d_refs, t_refs = (o0, o1, o2), (d0, d1, d2), (t0, t1, t2)
        lane = lax.broadcasted_iota(jnp.int32, (tm, HEAD_DIM), 1)
        tiles = [jnp.zeros((tm, HEAD_DIM), F32) for _ in range(3)]
        for h in range(E // HEAD_DIM):
            hs = slice(h * HEAD_DIM, (h + 1) * HEAD_DIM)
            a = _group_weights((l0, l1, l2), h)
            merged = a[0] * o0[:, hs] + a[1] * o1[:, hs] + a[2] * o2[:, hs]
            gate = gt_ref[:, hs]
            sg = _sigmoid(gate)
            dzh = dz_ref[:, hs]
            dmerged = dzh * (gate * sg)
            dg_ref[:, hs] = (dzh * merged * (sg * (1.0 + gate * (1.0 - sg)))).astype(BF16)
            tot = jnp.sum(dmerged * merged, axis=-1, keepdims=True)
            for g in range(3):
                d_refs[g][:, hs] = a[g] * dmerged
                tiles[g] = jnp.where(lane == h, a[g] * tot, tiles[g])
        for g in range(3):
            t_refs[g][...] = tiles[g]

    wide = pl.BlockSpec((tm, E), lambda i: (i, 0))
    thin = pl.BlockSpec((tm, HEAD_DIM), lambda i: (i, 0))
    res = pl.pallas_call(
        body, name=name, grid=(S // tm,),
        in_specs=[wide, wide, wide, wide, thin, thin, thin, pl.BlockSpec((tm, E), lambda i: (i, gate_col))],
        out_specs=[wide, wide, wide, thin, thin, thin, pl.BlockSpec((tm, E), lambda i: (i, gate_col))],
        out_shape=[jax.ShapeDtypeStruct((S, E), F32)] * 3 + [jax.ShapeDtypeStruct((S, HEAD_DIM), F32)] * 3
        + [jax.ShapeDtypeStruct(proj.shape, BF16)],
        compiler_params=_params("parallel"))(dz, *outs, *lses, proj)
    return res[0:3], res[3:6], res[6]


def _kv_bwd(dks, dvs, cos2, sin2, name):
    S, E = dks[0].shape
    n = len(dks)
    tm = min(256, S)

    def body(*refs):
        dk_refs, dv_refs = refs[0:n], refs[n:2 * n]
        c_ref, s_ref, o_ref = refs[2 * n:]
        cosv, sinv = c_ref[...], s_ref[...]
        for h in range(E // HEAD_DIM):
            hs = slice(h * HEAD_DIM, (h + 1) * HEAD_DIM)
            dk = dk_refs[0][:, hs].astype(F32)
            dv = dv_refs[0][:, hs].astype(F32)
            for j in range(1, n):
                dk = dk + dk_refs[j][:, hs].astype(F32)
                dv = dv + dv_refs[j][:, hs].astype(F32)
            o_ref[:, hs] = _rope_bwd(dk, cosv, sinv).astype(BF16)
            o_ref[:, E + h * HEAD_DIM:E + (h + 1) * HEAD_DIM] = dv.astype(BF16)

    wide = pl.BlockSpec((tm, E), lambda i: (i, 0))
    thin = pl.BlockSpec((tm, HEAD_DIM), lambda i: (i, 0))
    return pl.pallas_call(
        body, name=name, grid=(S // tm,),
        in_specs=[wide] * (2 * n) + [thin, thin],
        out_specs=pl.BlockSpec((tm, 2 * E), lambda i: (i, 0)),
        out_shape=jax.ShapeDtypeStruct((S, 2 * E), BF16),
        compiler_params=_params("parallel"))(*dks, *dvs, cos2, sin2)


def _norm_bwd(dhn, x, gain, dres, name):
    S, D = x.shape
    tm = min(256, S)

    def body(dh_ref, x_ref, g_ref, r_ref, dx_ref, dg_ref):
        @pl.when(pl.program_id(0) == 0)
        def _():
            dg_ref[...] = jnp.zeros_like(dg_ref)
        xf = x_ref[...]
        inv = lax.rsqrt(jnp.mean(xf * xf, axis=-1, keepdims=True) + RMS_EPS)
        xhat = xf * inv
        dh = dh_ref[...]
        dg_ref[...] += jnp.sum(dh * xhat, axis=0, keepdims=True)
        dxh = dh * g_ref[...]
        dx_ref[...] = r_ref[...] + inv * (dxh - xhat * jnp.mean(dxh * xhat, axis=-1, keepdims=True))

    tile = pl.BlockSpec((tm, D), lambda i: (i, 0))
    vec = pl.BlockSpec((1, D), lambda i: (0, 0))
    return pl.pallas_call(
        body, name=name, grid=(S // tm,),
        in_specs=[tile, tile, vec, tile],
        out_specs=[tile, vec],
        out_shape=[jax.ShapeDtypeStruct((S, D), F32), jax.ShapeDtypeStruct((1, D), F32)],
        compiler_params=_params("arbitrary"))(dhn, x, gain, dres)


def _final_norm_loss(x, target, gain, name):
    S, D = x.shape
    tm = min(256, S)

    def body(x_ref, t_ref, g_ref, loss_ref, dx_ref, dg_ref):
        @pl.when(pl.program_id(0) == 0)
        def _():
            loss_ref[...] = jnp.zeros_like(loss_ref)
            dg_ref[...] = jnp.zeros_like(dg_ref)
        xf = x_ref[...]
        inv = lax.rsqrt(jnp.mean(xf * xf, axis=-1, keepdims=True) + RMS_EPS)
        xhat = xf * inv
        g = g_ref[...]
        err = xhat * g - t_ref[...]
        loss_ref[...] += 0.5 * jnp.sum(jnp.mean(err * err, axis=-1, keepdims=True), axis=0, keepdims=True)
        dy = err / D
        dg_ref[...] += jnp.sum(dy * xhat, axis=0, keepdims=True)
        dxh = dy * g
        dx_ref[...] = inv * (dxh - xhat * jnp.mean(dxh * xhat, axis=-1, keepdims=True))

    tile = pl.BlockSpec((tm, D), lambda i: (i, 0))
    vec = pl.BlockSpec((1, D), lambda i: (0, 0))
    return pl.pallas_call(
        body, name=name, grid=(S // tm,),
        in_specs=[tile, tile, vec],
        out_specs=[pl.BlockSpec((1, 1), lambda i: (0, 0)), tile, vec],
        out_shape=[jax.ShapeDtypeStruct((1, 1), F32), jax.ShapeDtypeStruct((S, D), F32),
                   jax.ShapeDtypeStruct((1, D), F32)],
        compiler_params=_params("arbitrary"))(x, target, gain)


def _adamw_math(g, w, m, v):
    m = ADAM_B1 * m + (1.0 - ADAM_B1) * g
    v = ADAM_B2 * v + (1.0 - ADAM_B2) * (g * g)
    m_hat = m / (1.0 - ADAM_B1 ** ADAM_STEP)
    v_hat = v / (1.0 - ADAM_B2 ** ADAM_STEP)
    delta = -ADAM_LR * (m_hat / (jnp.sqrt(v_hat) + ADAM_EPS) + ADAM_WD * w)
    return delta, m, v


def _adamw_rows(g, w, m, v, name):
    def body(g_ref, w_ref, m_ref, v_ref, d_ref, mo_ref, vo_ref):
        d_ref[...], mo_ref[...], vo_ref[...] = _adamw_math(g_ref[...], w_ref[...], m_ref[...], v_ref[...])

    whole = pl.BlockSpec(memory_space=pltpu.VMEM)
    return pl.pallas_call(
        body, name=name, in_specs=[whole] * 4, out_specs=[whole] * 3,
        out_shape=[jax.ShapeDtypeStruct(g.shape, F32)] * 3)(g, w, m, v)


def _adamw_blocks(own, others, w, m, v, layer, earlier, name):
    L, R, C = w.shape
    n = others.shape[0]
    tr = R
    while tr * C > 128 * 1024 and tr % 16 == 0:
        tr //= 2

    def body(o_ref, p_ref, w_ref, m_ref, v_ref, *rest):
        g_ref, d_ref, mo_ref, vo_ref = rest[-4:]
        g = o_ref[...].astype(F32)
        for j in range(n):
            g = g + p_ref[j].astype(F32)
        g_ref[...] = g
        d_ref[...], mo_ref[...], vo_ref[...] = _adamw_math(g, w_ref[...], m_ref[...], v_ref[...])

    tile = pl.BlockSpec((None, tr, C), lambda i: (layer, i, 0))
    kept = [] if earlier is None else list(earlier)
    return pl.pallas_call(
        body, name=name, grid=(R // tr,),
        in_specs=[pl.BlockSpec((tr, C), lambda i: (i, 0)), pl.BlockSpec((n, tr, C), lambda i: (0, i, 0)),
                  tile, tile, tile] + [ANY] * len(kept),
        out_specs=[tile] * 4,
        out_shape=[jax.ShapeDtypeStruct((L, R, C), F32)] * 4,
        input_output_aliases={5 + j: j for j in range(len(kept))},
        compiler_params=_params("parallel"))(own, others, w, m, v, *kept)


def _position():
    return lax.axis_index("x"), lax.axis_index("y"), lax.axis_index("c")


def _block_index(px, py, pc):
    return 4 * px + 2 * py + pc


def _all_gather(shards, name):
    n = len(shards)

    def body(*refs):
        ins, outs = refs[0:n], refs[n:2 * n]
        send_sems, recv_sems, local_sems = refs[2 * n:]
        x, y, c = _position()
        me, sibling = (x, y, c), (x, y, 1 - c)
        chips = [(1 - x, y), (x, 1 - y), (1 - x, 1 - y)]

        def copy(a, k, block, to, src=None):
            rows = outs[a].at[_block_index(*block)]
            return pltpu.make_async_remote_copy(
                src_ref=rows if src is None else src, dst_ref=rows,
                send_sem=send_sems.at[a, k], recv_sem=recv_sems.at[a, k], device_id=to, device_id_type=MESH)

        mine, first, passed = [], [], []
        for a in range(n):
            cp = pltpu.make_async_copy(ins[a], outs[a].at[_block_index(*me)], local_sems.at[a])
            cp.start()
            mine.append(cp)
            first.append(copy(a, 0, me, sibling, src=ins[a]))
            first += [copy(a, 1 + j, me, (*chip, c), src=ins[a]) for j, chip in enumerate(chips)]
        for cp in first:
            cp.start()
        for j, chip in enumerate(chips):
            for a in range(n):
                copy(a, 1 + j, (*chip, c), me).wait_recv()
                fwd = copy(a, 4 + j, (*chip, c), sibling)
                fwd.start()
                passed.append(fwd)
        for a in range(n):
            copy(a, 0, sibling, me).wait_recv()
            for j, chip in enumerate(chips):
                copy(a, 4 + j, (*chip, 1 - c), me).wait_recv()
        for cp in first + passed:
            cp.wait_send()
        for cp in mine:
            cp.wait()

    return pl.pallas_call(
        body, name=name,
        in_specs=[ANY] * n, out_specs=[ANY] * n,
        out_shape=[jax.ShapeDtypeStruct((N_DEV,) + s.shape, s.dtype) for s in shards],
        scratch_shapes=[pltpu.SemaphoreType.DMA((n, 7)), pltpu.SemaphoreType.DMA((n, 7)),
                        pltpu.SemaphoreType.DMA((n,))],
    )(*shards)


def _peers(x, y, c):
    return [((1 - x) if k & 4 else x, (1 - y) if k & 2 else y, (1 - c) if k & 1 else c) for k in range(1, N_DEV)]


HBM = pl.BlockSpec(memory_space=pltpu.HBM)
SEM = pl.BlockSpec(memory_space=pltpu.SEMAPHORE)
EFFECT = pltpu.SideEffectType.DATAFLOW_SIDE_EFFECTING


ALL_PEERS = (1, 2, 3, 4, 5, 6, 7)
SIBLING_AND_SAME_CORES = (1, 2, 4, 6)


def _push_copy(src_refs, land_refs, send_sems, recv_sems, a, i, relations, per_peer, by_sender, arriving):
    peer = _peers(*_position())[relations[i] - 1]
    me_idx, p_idx = _block_index(*_position()), _block_index(*peer)
    src = src_refs[a].at[p_idx] if per_peer else src_refs[a]
    if by_sender:
        slot = p_idx if arriving else me_idx
    else:
        slot = relations[i] - 1
    sem = a * len(relations) + i
    return pltpu.make_async_remote_copy(
        src_ref=src, dst_ref=land_refs[a].at[slot], send_sem=send_sems.at[sem], recv_sem=recv_sems.at[sem],
        device_id=peer, device_id_type=MESH)


def _push_start(srcs, lands, relations, per_peer, by_sender, after, name):
    n = len(srcs)

    def body(*refs):
        src_refs, land_refs = refs[0:n], refs[n:2 * n]
        send_sems, recv_sems = refs[2 * n + 1], refs[2 * n + 2]
        token = refs[-1]
        for a in range(n):
            for i in range(len(relations)):
                _push_copy(src_refs, land_refs, send_sems, recv_sems, a, i, relations, per_peer, by_sender, False).start()
        token[...] = jnp.zeros_like(token)

    args = [pltpu.with_memory_space_constraint(t, pltpu.HBM) for t in list(srcs) + list(lands)]
    res = pl.pallas_call(
        body, name=name,
        in_specs=[HBM] * (2 * n) + [ANY],
        out_specs=[SEM, SEM] + [HBM] * (2 * n) + [pl.BlockSpec(memory_space=pltpu.VMEM)],
        out_shape=[pltpu.SemaphoreType.DMA((n * len(relations),)), pltpu.SemaphoreType.DMA((n * len(relations),))]
        + [pltpu.HBM(t.shape, t.dtype) for t in args] + [jax.ShapeDtypeStruct((8, 128), F32)],
        input_output_aliases={i: 2 + i for i in range(2 * n)},
        compiler_params=pltpu.CompilerParams(has_side_effects=EFFECT))(*args, after)
    return res[0], res[1], res[2:2 + n], res[2 + n:2 + 2 * n], res[-1]


def _push_wait(started, relations, per_peer, by_sender, after, name):
    send_sems, recv_sems, srcs, lands, _ = started
    n = len(srcs)

    def body(*refs):
        src_refs, land_refs = refs[0:n], refs[n:2 * n]
        send_s, recv_s = refs[2 * n], refs[2 * n + 1]
        for a in range(n):
            for i in range(len(relations)):
                _push_copy(src_refs, land_refs, send_s, recv_s, a, i, relations, per_peer, by_sender, False).wait_send()
                _push_copy(src_refs, land_refs, send_s, recv_s, a, i, relations, per_peer, by_sender, True).wait_recv()

    res = pl.pallas_call(
        body, name=name,
        in_specs=[HBM] * (2 * n) + [SEM, SEM, ANY],
        out_specs=[HBM] * (2 * n),
        out_shape=[pltpu.HBM(t.shape, t.dtype) for t in list(srcs) + list(lands)],
        input_output_aliases={i: i for i in range(2 * n)},
        compiler_params=pltpu.CompilerParams(has_side_effects=EFFECT))(*srcs, *lands, send_sems, recv_sems, after)
    return res[0:n], res[n:2 * n]


def _pass_on(lands, name):
    n = len(lands)

    def body(*refs):
        outs = refs[n:2 * n]
        send_sems, recv_sems = refs[2 * n:]
        x, y, c = _position()
        chips = [(1 - x, y), (x, 1 - y), (1 - x, 1 - y)]
        copies = []
        for a in range(n):
            for j, chip in enumerate(chips):
                def copy(core):
                    rows = outs[a].at[_block_index(*chip, core)]
                    return pltpu.make_async_remote_copy(
                        src_ref=rows, dst_ref=rows, send_sem=send_sems.at[a, j], recv_sem=recv_sems.at[a, j],
                        device_id=(x, y, 1 - c), device_id_type=MESH)
                copy(c).start()
                copies.append((copy(c), copy(1 - c)))
        for sending, arriving in copies:
            sending.wait_send()
            arriving.wait_recv()

    return pl.pallas_call(
        body, name=name,
        in_specs=[ANY] * n, out_specs=[ANY] * n,
        out_shape=[jax.ShapeDtypeStruct(t.shape, t.dtype) for t in lands],
        input_output_aliases={a: a for a in range(n)},
        scratch_shapes=[pltpu.SemaphoreType.DMA((n, 3)), pltpu.SemaphoreType.DMA((n, 3))],
    )(*lands)


def _all_reduce_rows(v, name):
    R, D = v.shape

    def body(v_ref, o_ref, buf_ref, send_sems, recv_sems):
        x, y, c = _position()
        me_idx = _block_index(x, y, c)
        buf_ref[me_idx] = v_ref[...]
        copies = []
        for k in range(1, N_DEV):
            px = (1 - x) if k & 4 else x
            py = (1 - y) if k & 2 else y
            pc = (1 - c) if k & 1 else c
            rc = pltpu.make_async_remote_copy(
                src_ref=v_ref, dst_ref=buf_ref.at[me_idx],
                send_sem=send_sems.at[k - 1], recv_sem=recv_sems.at[k - 1],
                device_id=(px, py, pc), device_id_type=MESH)
            rc.start()
            copies.append((rc, pltpu.make_async_remote_copy(
                src_ref=v_ref, dst_ref=buf_ref.at[_block_index(px, py, pc)],
                send_sem=send_sems.at[k - 1], recv_sem=recv_sems.at[k - 1],
                device_id=(px, py, pc), device_id_type=MESH)))
        for rc, arrival in copies:
            rc.wait_send()
            arrival.wait_recv()
        acc = buf_ref[0]
        for j in range(1, N_DEV):
            acc = acc + buf_ref[j]
        o_ref[...] = acc

    return pl.pallas_call(
        body, name=name,
        in_specs=[pl.BlockSpec(memory_space=pltpu.VMEM)],
        out_specs=pl.BlockSpec(memory_space=pltpu.VMEM),
        out_shape=jax.ShapeDtypeStruct((R, D), F32),
        scratch_shapes=[pltpu.VMEM((N_DEV, R, D), F32),
                        pltpu.SemaphoreType.DMA((7,)), pltpu.SemaphoreType.DMA((7,))],
    )(v)


def _rope_tables(S):
    inv_freq = 1.0 / (ROPE_THETA ** (jnp.arange(0, HEAD_DIM, 2, dtype=F32) / HEAD_DIM))
    ang = jnp.arange(S, dtype=F32)[:, None] * inv_freq[None, :]
    cos, sin = jnp.cos(ang), jnp.sin(ang)
    return jnp.concatenate([cos, cos], axis=1), jnp.concatenate([-sin, sin], axis=1)


def _local_step(xs, target, vecs, n_a, n_b, get_weights, put_grads):
    S, D = xs.shape
    E = D
    cos2, sin2 = _rope_tables(S)
    ts = min(512, S)

    def col_blocks(w):
        cb = w.shape[2]
        tn = min(cb, 1024)
        per = cb // tn
        return (None, D, tn), (lambda i, j: (j // per, 0, j % per)), N_DEV * per, tn

    def grad_in(hn, dproj, cb, name):
        return _matmul_tn(hn, dproj, (ts, D), lambda j, s: (s, 0), (ts, cb), lambda j, s: (s, j),
                          (N_DEV, D, cb), (None, D, cb), lambda j, s: (j, 0, 0), (D, cb), N_DEV, name)

    def grad_out(z, dx, name, col=0):
        rows = z.shape[1]
        ta = min(1024, rows)
        out = _matmul_tn(z, dx, (ts, ta), lambda a, s: (s, a), (ts, E), lambda a, s: (s, col),
                         (rows, E), (ta, E), lambda a, s: (a, 0), (ta, E), rows // ta, name)
        return out.reshape(N_DEV, rows // N_DEV, E)

    x = xs
    a_saved, b_saved = [], []
    for i in range(n_a):
        w = get_weights(f"a{i}", x)
        blk, idx, nblocks, tn = col_blocks(w["w_in"])
        proj, hn = _norm_matmul(x, vecs["norm_a"][i:i + 1], w["w_in"], blk, idx, nblocks, tn, f"a{i}_in")
        z = _a_mid_fwd(proj, w["w_grp"], vecs["scale_a"][i:i + 1], f"a{i}_mid")
        x_next = _matmul_res(z, w["w_out"], x, f"a{i}_out")
        a_saved.append((x, hn, proj, z, w))
        x = x_next
    x_kv = x
    w_kv = get_weights("kv", x)["w_kv"]
    tn = min(E, 1024)
    kvp, hn_kv = _norm_matmul(x, vecs["norm_kv"], w_kv, (D, tn), lambda i, j: (0, j), 2 * E // tn, tn, "kv_in")
    kr = _rope_k(kvp, cos2, sin2, "kv_rope")
    after = kr
    for i in range(n_b):
        w = get_weights(f"b{i}", after)
        blk, idx, nblocks, tn = col_blocks(w["w_in"])
        proj, hn = _norm_matmul(x, vecs["norm_b"][i:i + 1], w["w_in"], blk, idx, nblocks, tn, f"b{i}_in")
        outs, lses = [], []
        for g, dil in enumerate(DILATIONS):
            o, l = _attn_fwd(proj, kr, kvp, cos2, sin2, g, dil, f"b{i}_attn{g}")
            outs.append(o)
            lses.append(l)
        z = _merge_fwd(outs, lses, proj, f"b{i}_merge")
        x_next = _matmul_res(z, w["w_out"], x, f"b{i}_out")
        b_saved.append((x, hn, proj, z, outs, lses, w))
        x = x_next
        after = x
    loss, dx, dg_f = _final_norm_loss(x, target, vecs["norm_f"], "final")

    vec = {"norm_a": [None] * n_a, "scale_a": [None] * n_a, "norm_b": [None] * n_b, "norm_f": [dg_f]}
    dks, dvs = [], []
    for i in reversed(range(n_b)):
        x_in, hn, proj, z, outs, lses, w = b_saved[i]
        dw_out = grad_out(z, dx, f"b{i}_dwout")
        dz = _matmul_nt_rows(dx, w["w_out"], f"b{i}_dz")
        dos, dlts, dproj = _merge_bwd(dz, outs, lses, proj, f"b{i}_dmerge")
        for g, dil in enumerate(DILATIONS):
            dproj, dk, dv = _attn_bwd(proj, kr, kvp, cos2, sin2, dos[g], lses[g], dlts[g], dproj, g, dil,
                                      f"b{i}_dattn{g}")
            dks.append(dk)
            dvs.append(dv)
        cb = w["w_in"].shape[2]
        tok = put_grads(f"b{i}", {"w_out": dw_out, "w_in": grad_in(hn, dproj, cb, f"b{i}_dwin")})
        dhn = _matmul_nt_cols(dproj, w["w_in"], (None, D, cb), lambda t, j: (j, 0, 0), N_DEV, cb, D, f"b{i}_dhn")
        dx, vec["norm_b"][i] = _norm_bwd(dhn, x_in, vecs["norm_b"][i:i + 1] + tok[0:1, 0:1], dx, f"b{i}_dnorm")

    dkv = _kv_bwd(dks, dvs, cos2, sin2, "kv_dsum")
    tok = put_grads("kv", {"w_k": grad_out(hn_kv, dkv, "kv_dwk", 0), "w_v": grad_out(hn_kv, dkv, "kv_dwv", 1)})
    dhn = _matmul_nt_cols(dkv, w_kv, (D, E), lambda t, j: (0, j), 2, E, D, "kv_dhn")
    dx, dg_kv = _norm_bwd(dhn, x_kv, vecs["norm_kv"] + tok[0:1, 0:1], dx, "kv_dnorm")
    vec["norm_kv"] = [dg_kv]

    for i in reversed(range(n_a)):
        x_in, hn, proj, z, w = a_saved[i]
        dw_out = grad_out(z, dx, f"a{i}_dwout")
        dz = _matmul_nt_rows(dx, w["w_out"], f"a{i}_dz")
        dproj, dwg, dsc = _a_mid_bwd(dz, proj, w["w_grp"], vecs["scale_a"][i:i + 1], f"a{i}_dmid")
        n_grp, gc, _ = dwg.shape
        dwg = dwg.reshape(n_grp, N_DEV, gc // N_DEV, gc).transpose(1, 0, 2, 3).astype(BF16)
        vec["scale_a"][i] = dsc
        cb = w["w_in"].shape[2]
        tok = put_grads(f"a{i}", {"w_out": dw_out, "w_grp": dwg, "w_in": grad_in(hn, dproj, cb, f"a{i}_dwin")})
        dhn = _matmul_nt_cols(dproj, w["w_in"], (None, D, cb), lambda t, j: (j, 0, 0), N_DEV, cb, D, f"a{i}_dhn")
        dx, vec["norm_a"][i] = _norm_bwd(dhn, x_in, vecs["norm_a"][i:i + 1] + tok[0:1, 0:1], dx, f"a{i}_dnorm")

    return loss, dx, {k: jnp.concatenate(v, axis=0) for k, v in vec.items()}


VECTORS = ("norm_a", "scale_a", "norm_kv", "norm_b", "norm_f")
SHARDED_VECTORS = ("norm_a", "scale_a")
GROUPS = {
    "a0": (("w_in", "w_in_a", 0), ("w_grp", "w_grp_a", 0), ("w_out", "w_out_a", 0)),
    "a1": (("w_in", "w_in_a", 1), ("w_grp", "w_grp_a", 1), ("w_out", "w_out_a", 1)),
    "kv": (("w_k", "w_k", None), ("w_v", "w_v", None)),
    "b0": (("w_in", "w_in_b", 0), ("w_out", "w_out_b", 0)),
    "b1": (("w_in", "w_in_b", 1), ("w_out", "w_out_b", 1)),
}
PREFETCHED = ("a1", "kv", "b0", "b1")


def kernel(x, norm_a, w_in_a, w_grp_a, scale_a, w_out_a, norm_kv, w_k, w_v, norm_b, w_in_b, w_out_b, norm_f, loss_target, m_norm_a, m_w_in_a, m_w_grp_a, m_scale_a, m_w_out_a, m_norm_kv, m_w_k, m_w_v, m_norm_b, m_w_in_b, m_w_out_b, m_norm_f, v_norm_a, v_w_in_a, v_w_grp_a, v_scale_a, v_w_out_a, v_norm_kv, v_w_k, v_w_v, v_norm_b, v_w_in_b, v_w_out_b, v_norm_f):
    w = dict(norm_a=norm_a, w_in_a=w_in_a, w_grp_a=w_grp_a, scale_a=scale_a, w_out_a=w_out_a, norm_kv=norm_kv,
             w_k=w_k, w_v=w_v, norm_b=norm_b, w_in_b=w_in_b, w_out_b=w_out_b, norm_f=norm_f)
    m = dict(norm_a=m_norm_a, w_in_a=m_w_in_a, w_grp_a=m_w_grp_a, scale_a=m_scale_a, w_out_a=m_w_out_a,
             norm_kv=m_norm_kv, w_k=m_w_k, w_v=m_w_v, norm_b=m_norm_b, w_in_b=m_w_in_b, w_out_b=m_w_out_b,
             norm_f=m_norm_f)
    v = dict(norm_a=v_norm_a, w_in_a=v_w_in_a, w_grp_a=v_w_grp_a, scale_a=v_scale_a, w_out_a=v_w_out_a,
             norm_kv=v_norm_kv, w_k=v_w_k, w_v=v_w_v, norm_b=v_norm_b, w_in_b=v_w_in_b, w_out_b=v_w_out_b,
             norm_f=v_norm_f)
    D = x.shape[2]
    me = _block_index(*_position())

    def shard(group):
        return [w[p].astype(BF16) if layer is None else w[p][layer].astype(BF16) for _, p, layer in GROUPS[group]]

    def as_weights(group, gathered):
        out = dict(zip([n for n, _, _ in GROUPS[group]], gathered))
        if "w_grp" in out:
            g = out["w_grp"]
            out["w_grp"] = g.transpose(1, 0, 2, 3).reshape(g.shape[1], g.shape[3], g.shape[3])
        if "w_k" in out:
            out = {"w_kv": jnp.concatenate([out["w_k"].reshape(D, D), out["w_v"].reshape(D, D)], axis=1)}
        return out

    first = _all_gather(shard("a0") + [w[k] for k in SHARDED_VECTORS], "gather_first")
    n_first = len(GROUPS["a0"])
    vecs = {k: g.transpose(1, 0, 2).reshape(w[k].shape[0], D) for k, g in zip(SHARDED_VECTORS, first[n_first:])}
    vecs.update(norm_kv=norm_kv[None, :], norm_b=norm_b, norm_f=norm_f[None, :])
    srcs, lands = [], []
    for group in PREFETCHED:
        for s in shard(group):
            srcs.append(s)
            lands.append(lax.dynamic_update_index_in_dim(lax.empty((N_DEV,) + s.shape, s.dtype), s[None], me, 0))
    inflight, at = {}, 0
    token = None
    for group in PREFETCHED:
        n = len(GROUPS[group])
        inflight[group] = _push_start(srcs[at:at + n], lands[at:at + n], SIBLING_AND_SAME_CORES, False, True,
                                      first[0] if token is None else token, f"gather_{group}_start")
        token = inflight[group][4]
        at += n
    vecs["norm_a"] = vecs["norm_a"] + token[0:1, 0:1]

    def get_weights(group, after):
        if group == "a0":
            return as_weights(group, first[0:n_first])
        half = _push_wait(inflight[group], SIBLING_AND_SAME_CORES, False, True, after, f"gather_{group}_wait")[1]
        return as_weights(group, _pass_on(half, f"gather_{group}_pass"))

    sent = {}

    def put_grads(group, grads):
        blocks = [grads[n] for n, _, _ in GROUPS[group]]
        lands = [lax.empty((N_DEV - 1,) + b.shape[1:], b.dtype) for b in blocks]
        sent[group] = _push_start(blocks, lands, ALL_PEERS, True, False, jnp.zeros((8, 128), F32),
                                  f"exchange_{group}_start")
        return sent[group][4]

    loss, dx, vec = _local_step(x[0], loss_target[0], vecs, w_in_a.shape[0], w_in_b.shape[0], get_weights, put_grads)
    rows = _all_reduce_rows(jnp.concatenate([vec[k] for k in VECTORS], axis=0), "reduce_vectors")

    out = {}
    after = dx
    for group in sent:
        blocks, arrived = _push_wait(sent[group], ALL_PEERS, True, False, after, f"exchange_{group}_wait")
        for (_, p, layer), blk, got in zip(GROUPS[group], blocks, arrived):
            cols = w[p].shape[-1]
            own = lax.dynamic_index_in_dim(blk, me, 0, keepdims=False).reshape(-1, cols)
            n_layers = 1 if layer is None else w[p].shape[0]
            stacked = lambda t: t.reshape(n_layers, -1, cols)
            res = _adamw_blocks(own, got.reshape(N_DEV - 1, -1, cols), stacked(w[p]), stacked(m[p]), stacked(v[p]),
                                0 if layer is None else layer, out.get(p), f"adamw_{group}_{p}")
            out[p] = res
            after = res[1]
    out = {p: [r.reshape(w[p].shape) for r in res] for p, res in out.items()}
    start = 0
    for k in VECTORS:
        n_rows = vec[k].shape[0]
        g = rows[start:start + n_rows]
        start += n_rows
        if k in SHARDED_VECTORS:
            g = lax.dynamic_slice_in_dim(g, me * (D // N_DEV), D // N_DEV, axis=1)
        res = _adamw_rows(g, w[k].reshape(g.shape), m[k].reshape(g.shape), v[k].reshape(g.shape), f"adamw_{k}")
        out[k] = [r.reshape(w[k].shape) for r in [g] + list(res)]

    names = ("norm_a", "w_in_a", "w_grp_a", "scale_a", "w_out_a", "norm_kv", "w_k", "w_v", "norm_b", "w_in_b",
             "w_out_b", "norm_f")
    total = lax.psum(loss[0, 0], ("x", "y", "c"))
    return (total, dx[None], *[out[k][0] for k in names], *[out[k][1] for k in names],
            *[out[k][2] for k in names], *[out[k][3] for k in names])
```

```python
import math

import jax
import jax.numpy as jnp
from jax import lax
from jax.experimental import pallas as pl
from jax.experimental.pallas import tpu as pltpu

F32 = jnp.float32
BF16 = jnp.bfloat16

N_DEV = 8
MESH = pl.DeviceIdType.MESH
RMS_EPS = 1e-6
HEAD_DIM = 128
HALF_HEAD = HEAD_DIM // 2
BAND = 128
DILATIONS = (1, 4, 16)
POOL_WINDOWS = (2, 4, 8, 16)
POOL_HALO = 16
ROPE_THETA = 10000.0
NEG_INF = -1e30
ATTN_SCALE = 1.0 / math.sqrt(HEAD_DIM)
ADAM_LR, ADAM_B1, ADAM_B2, ADAM_EPS, ADAM_WD, ADAM_STEP = 0.001, 0.9, 0.999, 1e-08, 0.01, 10
VMEM_LIMIT_BYTES = 56 * 1024 * 1024
ANY = pl.BlockSpec(memory_space=pl.ANY)
NT = (((1,), (1,)), ((), ()))
TN = (((0,), (0,)), ((), ()))


def _params(*semantics):
    return pltpu.CompilerParams(dimension_semantics=semantics, vmem_limit_bytes=VMEM_LIMIT_BYTES)


def _sigmoid(t):
    return 1.0 / (1.0 + jnp.exp(-t))


def _rope(t, cos2, sin2):
    return t * cos2 + pltpu.roll(t, HALF_HEAD, 1) * sin2


def _rope_bwd(dt, cos2, sin2):
    return dt * cos2 + pltpu.roll(dt * sin2, HALF_HEAD, 1)


def _norm_matmul(x, gain, w, w_block, w_index, n_col_blocks, tn, name):
    S, D = x.shape
    tm = min(1024 if tn <= 512 else 512, S)

    def body(x_ref, g_ref, w_ref, o_ref, hn_ref, hs_ref):
        @pl.when(pl.program_id(1) == 0)
        def _():
            xf = x_ref[...]
            inv = lax.rsqrt(jnp.mean(xf * xf, axis=-1, keepdims=True) + RMS_EPS)
            hb = ((xf * inv) * g_ref[...]).astype(BF16)
            hs_ref[...] = hb
            hn_ref[...] = hb
        o_ref[...] = jnp.dot(hs_ref[...], w_ref[...], preferred_element_type=F32)

    return pl.pallas_call(
        body, name=name, grid=(S // tm, n_col_blocks),
        in_specs=[pl.BlockSpec((tm, D), lambda i, j: (i, 0)),
                  pl.BlockSpec((1, D), lambda i, j: (0, 0)),
                  pl.BlockSpec(w_block, w_index)],
        out_specs=[pl.BlockSpec((tm, tn), lambda i, j: (i, j)),
                   pl.BlockSpec((tm, D), lambda i, j: (i, 0))],
        out_shape=[jax.ShapeDtypeStruct((S, n_col_blocks * tn), F32), jax.ShapeDtypeStruct((S, D), BF16)],
        scratch_shapes=[pltpu.VMEM((tm, D), BF16)],
        compiler_params=_params("parallel", "arbitrary"))(x, gain, w)


def _matmul_res(a, w, res, name):
    S, K = a.shape
    nd, rb, N = w.shape
    tm = min(512, S)

    def body(a_ref, w_ref, r_ref, o_ref):
        acc = jnp.dot(a_ref[:, 0:rb], w_ref[0], preferred_element_type=F32)
        for k in range(1, nd):
            acc = acc + jnp.dot(a_ref[:, k * rb:(k + 1) * rb], w_ref[k], preferred_element_type=F32)
        o_ref[...] = r_ref[...] + acc

    return pl.pallas_call(
        body, name=name, grid=(S // tm,),
        in_specs=[pl.BlockSpec((tm, K), lambda i: (i, 0)),
                  pl.BlockSpec((nd, rb, N), lambda i: (0, 0, 0)),
                  pl.BlockSpec((tm, N), lambda i: (i, 0))],
        out_specs=pl.BlockSpec((tm, N), lambda i: (i, 0)),
        out_shape=jax.ShapeDtypeStruct((S, N), F32),
        compiler_params=_params("parallel"))(a, w, res)


def _matmul_nt_rows(dy, w, name):
    S, N = dy.shape
    nd, rb, _ = w.shape
    tm = min(512, S)

    def body(d_ref, w_ref, o_ref):
        db = d_ref[...].astype(BF16)
        for k in range(nd):
            o_ref[:, k * rb:(k + 1) * rb] = lax.dot_general(db, w_ref[k], NT, preferred_element_type=F32)

    return pl.pallas_call(
        body, name=name, grid=(S // tm,),
        in_specs=[pl.BlockSpec((tm, N), lambda i: (i, 0)),
                  pl.BlockSpec((nd, rb, N), lambda i: (0, 0, 0))],
        out_specs=pl.BlockSpec((tm, nd * rb), lambda i: (i, 0)),
        out_shape=jax.ShapeDtypeStruct((S, nd * rb), F32),
        compiler_params=_params("parallel"))(dy, w)


def _matmul_nt_cols(dp, w, w_block, w_index, n_red, tc, n_out, name):
    S = dp.shape[0]
    tm = min(1024, S)

    def body(d_ref, w_ref, o_ref):
        @pl.when(pl.program_id(1) == 0)
        def _():
            o_ref[...] = jnp.zeros_like(o_ref)
        o_ref[...] += lax.dot_general(d_ref[...], w_ref[...], NT, preferred_element_type=F32)

    return pl.pallas_call(
        body, name=name, grid=(S // tm, n_red),
        in_specs=[pl.BlockSpec((tm, tc), lambda i, j: (i, j)),
                  pl.BlockSpec(w_block, w_index)],
        out_specs=pl.BlockSpec((tm, n_out), lambda i, j: (i, 0)),
        out_shape=jax.ShapeDtypeStruct((S, n_out), F32),
        compiler_params=_params("parallel", "arbitrary"))(dp, w)


def _matmul_tn(a, b, a_block, a_index, b_block, b_index, out_shape, out_block, out_index, acc_shape, n_outer, name):
    S = a.shape[0]
    ts = a_block[0]
    n_tok = S // ts

    def body(a_ref, b_ref, o_ref, acc_ref):
        s = pl.program_id(1)

        @pl.when(s == 0)
        def _():
            acc_ref[...] = jnp.zeros_like(acc_ref)
        acc_ref[...] += lax.dot_general(a_ref[...].astype(BF16), b_ref[...].astype(BF16), TN,
                                        preferred_element_type=F32)

        @pl.when(s == n_tok - 1)
        def _():
            o_ref[...] = acc_ref[...].astype(o_ref.dtype)

    return pl.pallas_call(
        body, name=name, grid=(n_outer, n_tok),
        in_specs=[pl.BlockSpec(a_block, a_index), pl.BlockSpec(b_block, b_index)],
        out_specs=pl.BlockSpec(out_block, out_index),
        out_shape=jax.ShapeDtypeStruct(out_shape, BF16),
        scratch_shapes=[pltpu.VMEM(acc_shape, F32)],
        compiler_params=_params("parallel", "arbitrary"))(a, b)


def _pool(scr_ref, u, row0, tm, E):
    gc = E // len(POOL_WINDOWS)
    t1 = row0 + lax.broadcasted_iota(jnp.int32, (tm, 1), 0) + 1
    out = []
    for g, win in enumerate(POOL_WINDOWS):
        cs = slice(g * gc, (g + 1) * gc)
        acc = u[:, cs]
        for k in range(1, win):
            acc = acc + scr_ref[pl.ds(POOL_HALO - k, tm), cs]
        count = jnp.minimum(t1, win).astype(F32)
        out.append(acc / count - u[:, cs])
    return out


def _a_mid_fwd(proj, wg, scale, name):
    S, E2 = proj.shape
    E = E2 // 2
    gc = E // len(POOL_WINDOWS)
    tm = min(256, S)
    hb = tm // POOL_HALO

    def body(u_ref, uh_ref, gt_ref, wg_ref, sc_ref, z_ref, scr_ref):
        i = pl.program_id(0)
        scr_ref[0:POOL_HALO, :] = jnp.where(i > 0, uh_ref[...], 0.0)
        u = u_ref[...]
        scr_ref[POOL_HALO:POOL_HALO + tm, :] = u
        pooled = _pool(scr_ref, u, i * tm, tm, E)
        for g in range(len(POOL_WINDOWS)):
            cs = slice(g * gc, (g + 1) * gc)
            y = jnp.dot(pooled[g].astype(BF16), wg_ref[g], preferred_element_type=F32) * sc_ref[:, cs]
            gate = gt_ref[:, cs]
            z_ref[:, cs] = (y * (gate * _sigmoid(gate))).astype(BF16)

    return pl.pallas_call(
        body, name=name, grid=(S // tm,),
        in_specs=[pl.BlockSpec((tm, E), lambda i: (i, 0)),
                  pl.BlockSpec((POOL_HALO, E), lambda i: (jnp.maximum(i * hb - 1, 0), 0)),
                  pl.BlockSpec((tm, E), lambda i: (i, 1)),
                  pl.BlockSpec((len(POOL_WINDOWS), gc, gc), lambda i: (0, 0, 0)),
                  pl.BlockSpec((1, E), lambda i: (0, 0))],
        out_specs=pl.BlockSpec((tm, E), lambda i: (i, 0)),
        out_shape=jax.ShapeDtypeStruct((S, E), BF16),
        scratch_shapes=[pltpu.VMEM((POOL_HALO + tm, E), F32)],
        compiler_params=_params("parallel"))(proj, proj, proj, wg, scale)


def _a_mid_bwd(dz, proj, wg, scale, name):
    S, E2 = proj.shape
    E = E2 // 2
    n_grp = len(POOL_WINDOWS)
    gc = E // n_grp
    tm = min(256, S)
    hb = tm // POOL_HALO
    n_tiles = S // tm
    last_halo = S // POOL_HALO - 1

    def body(dz_ref, dzh_ref, u_ref, uh_ref, gt_ref, gth_ref, wg_ref, sc_ref, dp_ref, dwg_ref, dsc_ref, scr_ref, q_ref):
        i = pl.program_id(0)

        @pl.when(i == 0)
        def _():
            dwg_ref[...] = jnp.zeros_like(dwg_ref)
            dsc_ref[...] = jnp.zeros_like(dsc_ref)

        scr_ref[0:POOL_HALO, :] = jnp.where(i > 0, uh_ref[...], 0.0)
        u = u_ref[...]
        scr_ref[POOL_HALO:POOL_HALO + tm, :] = u
        pooled = _pool(scr_ref, u, i * tm, tm, E)
        t1 = i * tm + lax.broadcasted_iota(jnp.int32, (tm, 1), 0) + 1
        t1h = (i + 1) * tm + lax.broadcasted_iota(jnp.int32, (POOL_HALO, 1), 0) + 1
        not_last = i < n_tiles - 1
        for g, win in enumerate(POOL_WINDOWS):
            cs = slice(g * gc, (g + 1) * gc)
            w = wg_ref[g]
            sc = sc_ref[:, cs]
            pb = pooled[g].astype(BF16)
            ypre = jnp.dot(pb, w, preferred_element_type=F32)
            gate = gt_ref[:, cs]
            sg = _sigmoid(gate)
            silu = gate * sg
            dzg = dz_ref[:, cs]
            dy = dzg * silu
            dp_ref[:, E + g * gc:E + (g + 1) * gc] = (dzg * (ypre * sc) * (sg * (1.0 + gate * (1.0 - sg)))).astype(BF16)
            dsc_ref[:, cs] += jnp.sum(dy * ypre, axis=0, keepdims=True)
            dyp = (dy * sc).astype(BF16)
            dwg_ref[g] += lax.dot_general(pb, dyp, TN, preferred_element_type=F32)
            dpool = lax.dot_general(dyp, w, NT, preferred_element_type=F32)
            gate_h = gth_ref[:, cs]
            dyp_h = (dzh_ref[:, cs] * (gate_h * _sigmoid(gate_h)) * sc).astype(BF16)
            dpool_h = lax.dot_general(dyp_h, w, NT, preferred_element_type=F32)
            q_ref[0:tm, cs] = dpool / jnp.minimum(t1, win).astype(F32)
            q_ref[tm:tm + POOL_HALO, cs] = jnp.where(not_last, dpool_h / jnp.minimum(t1h, win).astype(F32), 0.0)
            acc = q_ref[0:tm, cs] - dpool
            for k in range(1, win):
                acc = acc + q_ref[pl.ds(k, tm), cs]
            dp_ref[:, cs] = acc.astype(BF16)

    return pl.pallas_call(
        body, name=name, grid=(n_tiles,),
        in_specs=[pl.BlockSpec((tm, E), lambda i: (i, 0)),
                  pl.BlockSpec((POOL_HALO, E), lambda i: (jnp.minimum((i + 1) * hb, last_halo), 0)),
                  pl.BlockSpec((tm, E), lambda i: (i, 0)),
                  pl.BlockSpec((POOL_HALO, E), lambda i: (jnp.maximum(i * hb - 1, 0), 0)),
                  pl.BlockSpec((tm, E), lambda i: (i, 1)),
                  pl.BlockSpec((POOL_HALO, E), lambda i: (jnp.minimum((i + 1) * hb, last_halo), 1)),
                  pl.BlockSpec((n_grp, gc, gc), lambda i: (0, 0, 0)),
                  pl.BlockSpec((1, E), lambda i: (0, 0))],
        out_specs=[pl.BlockSpec((tm, E2), lambda i: (i, 0)),
                   pl.BlockSpec((n_grp, gc, gc), lambda i: (0, 0, 0)),
                   pl.BlockSpec((1, E), lambda i: (0, 0))],
        out_shape=[jax.ShapeDtypeStruct((S, E2), BF16),
                   jax.ShapeDtypeStruct((n_grp, gc, gc), F32),
                   jax.ShapeDtypeStruct((1, E), F32)],
        scratch_shapes=[pltpu.VMEM((POOL_HALO + tm, E), F32), pltpu.VMEM((tm + POOL_HALO, E), F32)],
        compiler_params=_params("arbitrary"))(dz, dz, proj, proj, proj, proj, wg, scale)


def _rope_k(kvp, cos2, sin2, name):
    S, E2 = kvp.shape
    E = E2 // 2
    tm = min(256, S)

    def body(k_ref, c_ref, s_ref, ko_ref):
        cosv, sinv = c_ref[...], s_ref[...]
        for h in range(E // HEAD_DIM):
            hs = slice(h * HEAD_DIM, (h + 1) * HEAD_DIM)
            ko_ref[:, hs] = _rope(k_ref[:, hs], cosv, sinv)

    return pl.pallas_call(
        body, name=name, grid=(S // tm,),
        in_specs=[pl.BlockSpec((tm, E), lambda i: (i, 0)),
                  pl.BlockSpec((tm, HEAD_DIM), lambda i: (i, 0)), pl.BlockSpec((tm, HEAD_DIM), lambda i: (i, 0))],
        out_specs=pl.BlockSpec((tm, E), lambda i: (i, 0)),
        out_shape=jax.ShapeDtypeStruct((S, E), F32),
        compiler_params=_params("parallel"))(kvp, cos2, sin2)


def _rows(r, b, dil, n=BAND):
    start = r + b * BAND * dil
    return pl.ds(start, n) if dil == 1 else pl.ds(start, n, stride=dil)


def _band_mask(first):
    row = lax.broadcasted_iota(jnp.int32, (BAND, 2 * BAND), 0)
    col = lax.broadcasted_iota(jnp.int32, (BAND, 2 * BAND), 1)
    mask = (col >= row) & (col <= row + BAND)
    return mask & (col >= BAND) if first else mask


def _lane_column(tile, lane, h):
    return jnp.sum(jnp.where(lane == h, tile, 0.0), axis=-1, keepdims=True)


def _keys(ref, r, b, dil):
    if b > 0:
        return ref[_rows(r, b - 1, dil, 2 * BAND), :].astype(BF16)
    return jnp.concatenate([jnp.zeros((BAND, HEAD_DIM), BF16), ref[_rows(r, 0, dil), :].astype(BF16)], axis=0)


def _attn_fwd(proj, kr, kvp, cos2, sin2, group, dil, name):
    S, PW = proj.shape
    E = kr.shape[1]
    H = E // HEAD_DIM
    nb = S // (BAND * dil)

    def body(q_ref, k_ref, v_ref, c_ref, s_ref, o_ref, l_ref):
        h = pl.program_id(0)
        lane = lax.broadcasted_iota(jnp.int32, (BAND, HEAD_DIM), 1)
        edge, inner = _band_mask(True), _band_mask(False)

        @pl.when(h == 0)
        def _():
            l_ref[...] = jnp.zeros_like(l_ref)

        def scores(r, b):
            rows = _rows(r, b, dil)
            qr = _rope(q_ref[rows, :], c_ref[rows, :], s_ref[rows, :]).astype(BF16)
            return lax.dot_general(qr, _keys(k_ref, r, b, dil), NT, preferred_element_type=F32)

        units = [(r, b) for r in range(dil) for b in range(nb)]
        ahead = scores(*units[0])
        for i, (r, b) in enumerate(units):
            s = ahead
            if i + 1 < len(units):
                ahead = scores(*units[i + 1])
            rows = _rows(r, b, dil)
            s = jnp.where(edge if b == 0 else inner, s * ATTN_SCALE, NEG_INF)
            m = jnp.max(s, axis=-1, keepdims=True)
            p = jnp.exp(s - m)
            l = jnp.sum(p, axis=-1, keepdims=True)
            o_ref[rows, :] = jnp.dot(p.astype(BF16), _keys(v_ref, r, b, dil), preferred_element_type=F32) / l
            l_ref[rows, :] = jnp.where(lane == h, m + jnp.log(l), l_ref[rows, :])

    col = (S, HEAD_DIM)
    whole = pl.BlockSpec(col, lambda h: (0, 0))
    return pl.pallas_call(
        body, name=name, grid=(H,),
        in_specs=[pl.BlockSpec(col, lambda h: (0, group * H + h)), pl.BlockSpec(col, lambda h: (0, h)),
                  pl.BlockSpec(col, lambda h: (0, H + h)), whole, whole],
        out_specs=[pl.BlockSpec(col, lambda h: (0, h)), whole],
        out_shape=[jax.ShapeDtypeStruct((S, E), F32), jax.ShapeDtypeStruct((S, HEAD_DIM), F32)],
        compiler_params=_params("arbitrary"))(proj, kr, kvp, cos2, sin2)


def _attn_bwd(proj, kr, kvp, cos2, sin2, do, lse, dlt, dproj, group, dil, name):
    S, PW = proj.shape
    E = kr.shape[1]
    H = E // HEAD_DIM
    nb = S // (BAND * dil)

    def body(q_ref, k_ref, v_ref, c_ref, s_ref, do_ref, l_ref, dl_ref, dproj_ref,
             dq_ref, dk_ref, dv_ref, dq_scr, dk_scr, dv_scr):
        h = pl.program_id(0)
        lane = lax.broadcasted_iota(jnp.int32, (BAND, HEAD_DIM), 1)
        edge, inner = _band_mask(True), _band_mask(False)
        dk_scr[...] = jnp.zeros_like(dk_scr)
        dv_scr[...] = jnp.zeros_like(dv_scr)
        def scores(r, b):
            rows = _rows(r, b, dil)
            qr = _rope(q_ref[rows, :], c_ref[rows, :], s_ref[rows, :]).astype(BF16)
            return qr, lax.dot_general(qr, _keys(k_ref, r, b, dil), NT, preferred_element_type=F32)

        units = [(r, b) for r in range(dil) for b in range(nb)]
        ahead = scores(*units[0])
        for i, (r, b) in enumerate(units):
            qr, s = ahead
            if i + 1 < len(units):
                ahead = scores(*units[i + 1])
            rows = _rows(r, b, dil)
            dob = do_ref[rows, :].astype(BF16)
            dpr = lax.dot_general(dob, _keys(v_ref, r, b, dil), NT, preferred_element_type=F32)
            s = jnp.where(edge if b == 0 else inner, s * ATTN_SCALE, NEG_INF)
            p = jnp.exp(s - _lane_column(l_ref[rows, :], lane, h))
            ds = (p * (dpr - _lane_column(dl_ref[rows, :], lane, h)) * ATTN_SCALE).astype(BF16)
            dq = jnp.dot(ds, _keys(k_ref, r, b, dil), preferred_element_type=F32)
            dq_scr[rows, :] = _rope_bwd(dq, c_ref[rows, :], s_ref[rows, :])
            dkc = lax.dot_general(ds, qr, TN, preferred_element_type=F32)
            dvc = lax.dot_general(p.astype(BF16), dob, TN, preferred_element_type=F32)
            if b > 0:
                both = _rows(r, b - 1, dil, 2 * BAND)
                dk_scr[both, :] += dkc
                dv_scr[both, :] += dvc
            else:
                dk_scr[rows, :] += dkc[BAND:2 * BAND]
                dv_scr[rows, :] += dvc[BAND:2 * BAND]
        dq_ref[...] = dq_scr[...].astype(BF16)
        dk_ref[...] = dk_scr[...].astype(BF16)
        dv_ref[...] = dv_scr[...].astype(BF16)

    col = (S, HEAD_DIM)
    whole = pl.BlockSpec(col, lambda h: (0, 0))
    head = pl.BlockSpec(col, lambda h: (0, h))
    return pl.pallas_call(
        body, name=name, grid=(H,),
        in_specs=[pl.BlockSpec(col, lambda h: (0, group * H + h)), head, pl.BlockSpec(col, lambda h: (0, H + h)),
                  whole, whole, head, whole, whole, ANY],
        out_specs=[pl.BlockSpec(col, lambda h: (0, group * H + h)), head, head],
        out_shape=[jax.ShapeDtypeStruct(dproj.shape, BF16), jax.ShapeDtypeStruct((S, E), BF16),
                   jax.ShapeDtypeStruct((S, E), BF16)],
        scratch_shapes=[pltpu.VMEM(col, F32)] * 3,
        input_output_aliases={8: 0},
        compiler_params=_params("parallel"))(proj, kr, kvp, cos2, sin2, do, lse, dlt, dproj)


def _group_weights(l_refs, h):
    ls = [r[:, h:h + 1] for r in l_refs]
    mx = jnp.maximum(jnp.maximum(ls[0], ls[1]), ls[2])
    es = [jnp.exp(l - mx) for l in ls]
    inv = 1.0 / (es[0] + es[1] + es[2])
    return [e * inv for e in es]


def _merge_fwd(outs, lses, proj, name):
    S, E = outs[0].shape
    tm = min(256, S)
    gate_col = proj.shape[1] // E - 1

    def body(o0, o1, o2, l0, l1, l2, gt_ref, z_ref):
        for h in range(E // HEAD_DIM):
            hs = slice(h * HEAD_DIM, (h + 1) * HEAD_DIM)
            a = _group_weights((l0, l1, l2), h)
            merged = a[0] * o0[:, hs] + a[1] * o1[:, hs] + a[2] * o2[:, hs]
            gate = gt_ref[:, hs]
            z_ref[:, hs] = (merged * (gate * _sigmoid(gate))).astype(BF16)

    wide = pl.BlockSpec((tm, E), lambda i: (i, 0))
    thin = pl.BlockSpec((tm, HEAD_DIM), lambda i: (i, 0))
    return pl.pallas_call(
        body, name=name, grid=(S // tm,),
        in_specs=[wide, wide, wide, thin, thin, thin, pl.BlockSpec((tm, E), lambda i: (i, gate_col))],
        out_specs=wide,
        out_shape=jax.ShapeDtypeStruct((S, E), BF16),
        compiler_params=_params("parallel"))(*outs, *lses, proj)


def _merge_bwd(dz, outs, lses, proj, name):
    S, E = outs[0].shape
    tm = min(256, S)
    gate_col = proj.shape[1] // E - 1

    def body(dz_ref, o0, o1, o2, l0, l1, l2, gt_ref, d0, d1, d2, t0, t1, t2, dg_ref):
        o_refs, d_refs, t_refs = (o0, o1, o2), (d0, d1, d2), (t0, t1, t2)
        lane = lax.broadcasted_iota(jnp.int32, (tm, HEAD_DIM), 1)
        tiles = [jnp.zeros((tm, HEAD_DIM), F32) for _ in range(3)]
        for h in range(E // HEAD_DIM):
            hs = slice(h * HEAD_DIM, (h + 1) * HEAD_DIM)
            a = _group_weights((l0, l1, l2), h)
            merged = a[0] * o0[:, hs] + a[1] * o1[:, hs] + a[2] * o2[:, hs]
            gate = gt_ref[:, hs]
            sg = _sigmoid(gate)
            dzh = dz_ref[:, hs]
            dmerged = dzh * (gate * sg)
            dg_ref[:, hs] = (dzh * merged * (sg * (1.0 + gate * (1.0 - sg)))).astype(BF16)
            tot = jnp.sum(dmerged * merged, axis=-1, keepdims=True)
            for g in range(3):
                d_refs[g][:, hs] = a[g] * dmerged
                tiles[g] = jnp.where(lane == h, a[g] * tot, tiles[g])
        for g in range(3):
            t_refs[g][...] = tiles[g]

    wide = pl.BlockSpec((tm, E), lambda i: (i, 0))
    thin = pl.BlockSpec((tm, HEAD_DIM), lambda i: (i, 0))
    res = pl.pallas_call(
        body, name=name, grid=(S // tm,),
        in_specs=[wide, wide, wide, wide, thin, thin, thin, pl.BlockSpec((tm, E), lambda i: (i, gate_col))],
        out_specs=[wide, wide, wide, thin, thin, thin, pl.BlockSpec((tm, E), lambda i: (i, gate_col))],
        out_shape=[jax.ShapeDtypeStruct((S, E), F32)] * 3 + [jax.ShapeDtypeStruct((S, HEAD_DIM), F32)] * 3
        + [jax.ShapeDtypeStruct(proj.shape, BF16)],
        compiler_params=_params("parallel"))(dz, *outs, *lses, proj)
    return res[0:3], res[3:6], res[6]


def _kv_bwd(dks, dvs, cos2, sin2, name):
    S, E = dks[0].shape
    n = len(dks)
    tm = min(256, S)

    def body(*refs):
        dk_refs, dv_refs = refs[0:n], refs[n:2 * n]
        c_ref, s_ref, o_ref = refs[2 * n:]
        cosv, sinv = c_ref[...], s_ref[...]
        for h in range(E // HEAD_DIM):
            hs = slice(h * HEAD_DIM, (h + 1) * HEAD_DIM)
            dk = dk_refs[0][:, hs].astype(F32)
            dv = dv_refs[0][:, hs].astype(F32)
            for j in range(1, n):
                dk = dk + dk_refs[j][:, hs].astype(F32)
                dv = dv + dv_refs[j][:, hs].astype(F32)
            o_ref[:, hs] = _rope_bwd(dk, cosv, sinv).astype(BF16)
            o_ref[:, E + h * HEAD_DIM:E + (h + 1) * HEAD_DIM] = dv.astype(BF16)

    wide = pl.BlockSpec((tm, E), lambda i: (i, 0))
    thin = pl.BlockSpec((tm, HEAD_DIM), lambda i: (i, 0))
    return pl.pallas_call(
        body, name=name, grid=(S // tm,),
        in_specs=[wide] * (2 * n) + [thin, thin],
        out_specs=pl.BlockSpec((tm, 2 * E), lambda i: (i, 0)),
        out_shape=jax.ShapeDtypeStruct((S, 2 * E), BF16),
        compiler_params=_params("parallel"))(*dks, *dvs, cos2, sin2)


def _norm_bwd(dhn, x, gain, dres, name):
    S, D = x.shape
    tm = min(256, S)

    def body(dh_ref, x_ref, g_ref, r_ref, dx_ref, dg_ref):
        @pl.when(pl.program_id(0) == 0)
        def _():
            dg_ref[...] = jnp.zeros_like(dg_ref)
        xf = x_ref[...]
        inv = lax.rsqrt(jnp.mean(xf * xf, axis=-1, keepdims=True) + RMS_EPS)
        xhat = xf * inv
        dh = dh_ref[...]
        dg_ref[...] += jnp.sum(dh * xhat, axis=0, keepdims=True)
        dxh = dh * g_ref[...]
        dx_ref[...] = r_ref[...] + inv * (dxh - xhat * jnp.mean(dxh * xhat, axis=-1, keepdims=True))

    tile = pl.BlockSpec((tm, D), lambda i: (i, 0))
    vec = pl.BlockSpec((1, D), lambda i: (0, 0))
    return pl.pallas_call(
        body, name=name, grid=(S // tm,),
        in_specs=[tile, tile, vec, tile],
        out_specs=[tile, vec],
        out_shape=[jax.ShapeDtypeStruct((S, D), F32), jax.ShapeDtypeStruct((1, D), F32)],
        compiler_params=_params("arbitrary"))(dhn, x, gain, dres)


def _final_norm_loss(x, target, gain, name):
    S, D = x.shape
    tm = min(256, S)

    def body(x_ref, t_ref, g_ref, loss_ref, dx_ref, dg_ref):
        @pl.when(pl.program_id(0) == 0)
        def _():
            loss_ref[...] = jnp.zeros_like(loss_ref)
            dg_ref[...] = jnp.zeros_like(dg_ref)
        xf = x_ref[...]
        inv = lax.rsqrt(jnp.mean(xf * xf, axis=-1, keepdims=True) + RMS_EPS)
        xhat = xf * inv
        g = g_ref[...]
        err = xhat * g - t_ref[...]
        loss_ref[...] += 0.5 * jnp.sum(jnp.mean(err * err, axis=-1, keepdims=True), axis=0, keepdims=True)
        dy = err / D
        dg_ref[...] += jnp.sum(dy * xhat, axis=0, keepdims=True)
        dxh = dy * g
        dx_ref[...] = inv * (dxh - xhat * jnp.mean(dxh * xhat, axis=-1, keepdims=True))

    tile = pl.BlockSpec((tm, D), lambda i: (i, 0))
    vec = pl.BlockSpec((1, D), lambda i: (0, 0))
    return pl.pallas_call(
        body, name=name, grid=(S // tm,),
        in_specs=[tile, tile, vec],
        out_specs=[pl.BlockSpec((1, 1), lambda i: (0, 0)), tile, vec],
        out_shape=[jax.ShapeDtypeStruct((1, 1), F32), jax.ShapeDtypeStruct((S, D), F32),
                   jax.ShapeDtypeStruct((1, D), F32)],
        compiler_params=_params("arbitrary"))(x, target, gain)


def _adamw_math(g, w, m, v):
    m = ADAM_B1 * m + (1.0 - ADAM_B1) * g
    v = ADAM_B2 * v + (1.0 - ADAM_B2) * (g * g)
    m_hat = m / (1.0 - ADAM_B1 ** ADAM_STEP)
    v_hat = v / (1.0 - ADAM_B2 ** ADAM_STEP)
    delta = -ADAM_LR * (m_hat / (jnp.sqrt(v_hat) + ADAM_EPS) + ADAM_WD * w)
    return delta, m, v


def _adamw_rows(g, w, m, v, name):
    def body(g_ref, w_ref, m_ref, v_ref, d_ref, mo_ref, vo_ref):
        d_ref[...], mo_ref[...], vo_ref[...] = _adamw_math(g_ref[...], w_ref[...], m_ref[...], v_ref[...])

    whole = pl.BlockSpec(memory_space=pltpu.VMEM)
    return pl.pallas_call(
        body, name=name, in_specs=[whole] * 4, out_specs=[whole] * 3,
        out_shape=[jax.ShapeDtypeStruct(g.shape, F32)] * 3)(g, w, m, v)


def _adamw_blocks(own, others, w, m, v, layer, earlier, name):
    L, R, C = w.shape
    n = others.shape[0]
    tr = R
    while tr * C > 128 * 1024 and tr % 16 == 0:
        tr //= 2

    def body(o_ref, p_ref, w_ref, m_ref, v_ref, *rest):
        g_ref, d_ref, mo_ref, vo_ref = rest[-4:]
        g = o_ref[...].astype(F32)
        for j in range(n):
            g = g + p_ref[j].astype(F32)
        g_ref[...] = g
        d_ref[...], mo_ref[...], vo_ref[...] = _adamw_math(g, w_ref[...], m_ref[...], v_ref[...])

    tile = pl.BlockSpec((None, tr, C), lambda i: (layer, i, 0))
    kept = [] if earlier is None else list(earlier)
    return pl.pallas_call(
        body, name=name, grid=(R // tr,),
        in_specs=[pl.BlockSpec((tr, C), lambda i: (i, 0)), pl.BlockSpec((n, tr, C), lambda i: (0, i, 0)),
                  tile, tile, tile] + [ANY] * len(kept),
        out_specs=[tile] * 4,
        out_shape=[jax.ShapeDtypeStruct((L, R, C), F32)] * 4,
        input_output_aliases={5 + j: j for j in range(len(kept))},
        compiler_params=_params("parallel"))(own, others, w, m, v, *kept)


def _position():
    return lax.axis_index("x"), lax.axis_index("y"), lax.axis_index("c")


def _block_index(px, py, pc):
    return 4 * px + 2 * py + pc


def _all_gather(shards, name):
    n = len(shards)

    def body(*refs):
        ins, outs = refs[0:n], refs[n:2 * n]
        send_sems, recv_sems, local_sems = refs[2 * n:]
        x, y, c = _position()
        me, sibling = (x, y, c), (x, y, 1 - c)
        chips = [(1 - x, y), (x, 1 - y), (1 - x, 1 - y)]

        def copy(a, k, block, to, src=None):
            rows = outs[a].at[_block_index(*block)]
            return pltpu.make_async_remote_copy(
                src_ref=rows if src is None else src, dst_ref=rows,
                send_sem=send_sems.at[a, k], recv_sem=recv_sems.at[a, k], device_id=to, device_id_type=MESH)

        mine, first, passed = [], [], []
        for a in range(n):
            cp = pltpu.make_async_copy(ins[a], outs[a].at[_block_index(*me)], local_sems.at[a])
            cp.start()
            mine.append(cp)
            first.append(copy(a, 0, me, sibling, src=ins[a]))
            first += [copy(a, 1 + j, me, (*chip, c), src=ins[a]) for j, chip in enumerate(chips)]
        for cp in first:
            cp.start()
        for j, chip in enumerate(chips):
            for a in range(n):
                copy(a, 1 + j, (*chip, c), me).wait_recv()
                fwd = copy(a, 4 + j, (*chip, c), sibling)
                fwd.start()
                passed.append(fwd)
        for a in range(n):
            copy(a, 0, sibling, me).wait_recv()
            for j, chip in enumerate(chips):
                copy(a, 4 + j, (*chip, 1 - c), me).wait_recv()
        for cp in first + passed:
            cp.wait_send()
        for cp in mine:
            cp.wait()

    return pl.pallas_call(
        body, name=name,
        in_specs=[ANY] * n, out_specs=[ANY] * n,
        out_shape=[jax.ShapeDtypeStruct((N_DEV,) + s.shape, s.dtype) for s in shards],
        scratch_shapes=[pltpu.SemaphoreType.DMA((n, 7)), pltpu.SemaphoreType.DMA((n, 7)),
                        pltpu.SemaphoreType.DMA((n,))],
    )(*shards)


def _peers(x, y, c):
    return [((1 - x) if k & 4 else x, (1 - y) if k & 2 else y, (1 - c) if k & 1 else c) for k in range(1, N_DEV)]


HBM = pl.BlockSpec(memory_space=pltpu.HBM)
SEM = pl.BlockSpec(memory_space=pltpu.SEMAPHORE)
EFFECT = pltpu.SideEffectType.DATAFLOW_SIDE_EFFECTING


ALL_PEERS = (1, 2, 3, 4, 5, 6, 7)
SIBLING_AND_SAME_CORES = (1, 2, 4, 6)


def _push_copy(src_refs, land_refs, send_sems, recv_sems, a, i, relations, per_peer, by_sender, arriving):
    peer = _peers(*_position())[relations[i] - 1]
    me_idx, p_idx = _block_index(*_position()), _block_index(*peer)
    src = src_refs[a].at[p_idx] if per_peer else src_refs[a]
    if by_sender:
        slot = p_idx if arriving else me_idx
    else:
        slot = relations[i] - 1
    sem = a * len(relations) + i
    return pltpu.make_async_remote_copy(
        src_ref=src, dst_ref=land_refs[a].at[slot], send_sem=send_sems.at[sem], recv_sem=recv_sems.at[sem],
        device_id=peer, device_id_type=MESH)


def _push_start(srcs, lands, relations, per_peer, by_sender, after, name):
    n = len(srcs)

    def body(*refs):
        src_refs, land_refs = refs[0:n], refs[n:2 * n]
        send_sems, recv_sems = refs[2 * n + 1], refs[2 * n + 2]
        token = refs[-1]
        for a in range(n):
            for i in range(len(relations)):
                _push_copy(src_refs, land_refs, send_sems, recv_sems, a, i, relations, per_peer, by_sender, False).start()
        token[...] = jnp.zeros_like(token)

    args = [pltpu.with_memory_space_constraint(t, pltpu.HBM) for t in list(srcs) + list(lands)]
    res = pl.pallas_call(
        body, name=name,
        in_specs=[HBM] * (2 * n) + [ANY],
        out_specs=[SEM, SEM] + [HBM] * (2 * n) + [pl.BlockSpec(memory_space=pltpu.VMEM)],
        out_shape=[pltpu.SemaphoreType.DMA((n * len(relations),)), pltpu.SemaphoreType.DMA((n * len(relations),))]
        + [pltpu.HBM(t.shape, t.dtype) for t in args] + [jax.ShapeDtypeStruct((8, 128), F32)],
        input_output_aliases={i: 2 + i for i in range(2 * n)},
        compiler_params=pltpu.CompilerParams(has_side_effects=EFFECT))(*args, after)
    return res[0], res[1], res[2:2 + n], res[2 + n:2 + 2 * n], res[-1]


def _push_wait(started, relations, per_peer, by_sender, after, name):
    send_sems, recv_sems, srcs, lands, _ = started
    n = len(srcs)

    def body(*refs):
        src_refs, land_refs = refs[0:n], refs[n:2 * n]
        send_s, recv_s = refs[2 * n], refs[2 * n + 1]
        for a in range(n):
            for i in range(len(relations)):
                _push_copy(src_refs, land_refs, send_s, recv_s, a, i, relations, per_peer, by_sender, False).wait_send()
                _push_copy(src_refs, land_refs, send_s, recv_s, a, i, relations, per_peer, by_sender, True).wait_recv()

    res = pl.pallas_call(
        body, name=name,
        in_specs=[HBM] * (2 * n) + [SEM, SEM, ANY],
        out_specs=[HBM] * (2 * n),
        out_shape=[pltpu.HBM(t.shape, t.dtype) for t in list(srcs) + list(lands)],
        input_output_aliases={i: i for i in range(2 * n)},
        compiler_params=pltpu.CompilerParams(has_side_effects=EFFECT))(*srcs, *lands, send_sems, recv_sems, after)
    return res[0:n], res[n:2 * n]


def _pass_on(lands, name):
    n = len(lands)

    def body(*refs):
        outs = refs[n:2 * n]
        send_sems, recv_sems = refs[2 * n:]
        x, y, c = _position()
        chips = [(1 - x, y), (x, 1 - y), (1 - x, 1 - y)]
        copies = []
        for a in range(n):
            for j, chip in enumerate(chips):
                def copy(core):
                    rows = outs[a].at[_block_index(*chip, core)]
                    return pltpu.make_async_remote_copy(
                        src_ref=rows, dst_ref=rows, send_sem=send_sems.at[a, j], recv_sem=recv_sems.at[a, j],
                        device_id=(x, y, 1 - c), device_id_type=MESH)
                copy(c).start()
                copies.append((copy(c), copy(1 - c)))
        for sending, arriving in copies:
            sending.wait_send()
            arriving.wait_recv()

    return pl.pallas_call(
        body, name=name,
        in_specs=[ANY] * n, out_specs=[ANY] * n,
        out_shape=[jax.ShapeDtypeStruct(t.shape, t.dtype) for t in lands],
        input_output_aliases={a: a for a in range(n)},
        scratch_shapes=[pltpu.SemaphoreType.DMA((n, 3)), pltpu.SemaphoreType.DMA((n, 3))],
    )(*lands)


def _all_reduce_rows(v, name):
    R, D = v.shape

    def body(v_ref, o_ref, buf_ref, send_sems, recv_sems):
        x, y, c = _position()
        me_idx = _block_index(x, y, c)
        buf_ref[me_idx] = v_ref[...]
        copies = []
        for k in range(1, N_DEV):
            px = (1 - x) if k & 4 else x
            py = (1 - y) if k & 2 else y
            pc = (1 - c) if k & 1 else c
            rc = pltpu.make_async_remote_copy(
                src_ref=v_ref, dst_ref=buf_ref.at[me_idx],
                send_sem=send_sems.at[k - 1], recv_sem=recv_sems.at[k - 1],
                device_id=(px, py, pc), device_id_type=MESH)
            rc.start()
            copies.append((rc, pltpu.make_async_remote_copy(
                src_ref=v_ref, dst_ref=buf_ref.at[_block_index(px, py, pc)],
                send_sem=send_sems.at[k - 1], recv_sem=recv_sems.at[k - 1],
                device_id=(px, py, pc), device_id_type=MESH)))
        for rc, arrival in copies:
            rc.wait_send()
            arrival.wait_recv()
        acc = buf_ref[0]
        for j in range(1, N_DEV):
            acc = acc + buf_ref[j]
        o_ref[...] = acc

    return pl.pallas_call(
        body, name=name,
        in_specs=[pl.BlockSpec(memory_space=pltpu.VMEM)],
        out_specs=pl.BlockSpec(memory_space=pltpu.VMEM),
        out_shape=jax.ShapeDtypeStruct((R, D), F32),
        scratch_shapes=[pltpu.VMEM((N_DEV, R, D), F32),
                        pltpu.SemaphoreType.DMA((7,)), pltpu.SemaphoreType.DMA((7,))],
    )(v)


def _rope_tables(S):
    inv_freq = 1.0 / (ROPE_THETA ** (jnp.arange(0, HEAD_DIM, 2, dtype=F32) / HEAD_DIM))
    ang = jnp.arange(S, dtype=F32)[:, None] * inv_freq[None, :]
    cos, sin = jnp.cos(ang), jnp.sin(ang)
    return jnp.concatenate([cos, cos], axis=1), jnp.concatenate([-sin, sin], axis=1)


def _local_step(xs, target, vecs, n_a, n_b, get_weights, put_grads):
    S, D = xs.shape
    E = D
    cos2, sin2 = _rope_tables(S)
    ts = min(1024, S)

    def col_blocks(w):
        cb = w.shape[2]
        tn = min(cb, 1024)
        per = cb // tn
        return (None, D, tn), (lambda i, j: (j // per, 0, j % per)), N_DEV * per, tn

    def grad_in(hn, dproj, cb, name):
        return _matmul_tn(hn, dproj, (ts, D), lambda j, s: (s, 0), (ts, cb), lambda j, s: (s, j),
                          (N_DEV, D, cb), (None, D, cb), lambda j, s: (j, 0, 0), (D, cb), N_DEV, name)

    def grad_out(z, dx, name, col=0):
        rows = z.shape[1]
        ta = min(1024, rows)
        out = _matmul_tn(z, dx, (ts, ta), lambda a, s: (s, a), (ts, E), lambda a, s: (s, col),
                         (rows, E), (ta, E), lambda a, s: (a, 0), (ta, E), rows // ta, name)
        return out.reshape(N_DEV, rows // N_DEV, E)

    x = xs
    a_saved, b_saved = [], []
    for i in range(n_a):
        w = get_weights(f"a{i}", x)
        blk, idx, nblocks, tn = col_blocks(w["w_in"])
        proj, hn = _norm_matmul(x, vecs["norm_a"][i:i + 1], w["w_in"], blk, idx, nblocks, tn, f"a{i}_in")
        z = _a_mid_fwd(proj, w["w_grp"], vecs["scale_a"][i:i + 1], f"a{i}_mid")
        x_next = _matmul_res(z, w["w_out"], x, f"a{i}_out")
        a_saved.append((x, hn, proj, z, w))
        x = x_next
    x_kv = x
    w_kv = get_weights("kv", x)["w_kv"]
    tn = min(E, 1024)
    kvp, hn_kv = _norm_matmul(x, vecs["norm_kv"], w_kv, (D, tn), lambda i, j: (0, j), 2 * E // tn, tn, "kv_in")
    kr = _rope_k(kvp, cos2, sin2, "kv_rope")
    after = kr
    for i in range(n_b):
        w = get_weights(f"b{i}", after)
        blk, idx, nblocks, tn = col_blocks(w["w_in"])
        proj, hn = _norm_matmul(x, vecs["norm_b"][i:i + 1], w["w_in"], blk, idx, nblocks, tn, f"b{i}_in")
        outs, lses = [], []
        for g, dil in enumerate(DILATIONS):
            o, l = _attn_fwd(proj, kr, kvp, cos2, sin2, g, dil, f"b{i}_attn{g}")
            outs.append(o)
            lses.append(l)
        z = _merge_fwd(outs, lses, proj, f"b{i}_merge")
        x_next = _matmul_res(z, w["w_out"], x, f"b{i}_out")
        b_saved.append((x, hn, proj, z, outs, lses, w))
        x = x_next
        after = x
    loss, dx, dg_f = _final_norm_loss(x, target, vecs["norm_f"], "final")

    vec = {"norm_a": [None] * n_a, "scale_a": [None] * n_a, "norm_b": [None] * n_b, "norm_f": [dg_f]}
    dks, dvs = [], []
    for i in reversed(range(n_b)):
        x_in, hn, proj, z, outs, lses, w = b_saved[i]
        dw_out = grad_out(z, dx, f"b{i}_dwout")
        dz = _matmul_nt_rows(dx, w["w_out"], f"b{i}_dz")
        dos, dlts, dproj = _merge_bwd(dz, outs, lses, proj, f"b{i}_dmerge")
        for g, dil in enumerate(DILATIONS):
            dproj, dk, dv = _attn_bwd(proj, kr, kvp, cos2, sin2, dos[g], lses[g], dlts[g], dproj, g, dil,
                                      f"b{i}_dattn{g}")
            dks.append(dk)
            dvs.append(dv)
        cb = w["w_in"].shape[2]
        tok = put_grads(f"b{i}", {"w_out": dw_out, "w_in": grad_in(hn, dproj, cb, f"b{i}_dwin")})
        dhn = _matmul_nt_cols(dproj, w["w_in"], (None, D, cb), lambda t, j: (j, 0, 0), N_DEV, cb, D, f"b{i}_dhn")
        dx, vec["norm_b"][i] = _norm_bwd(dhn, x_in, vecs["norm_b"][i:i + 1] + tok[0:1, 0:1], dx, f"b{i}_dnorm")

    dkv = _kv_bwd(dks, dvs, cos2, sin2, "kv_dsum")
    tok = put_grads("kv", {"w_k": grad_out(hn_kv, dkv, "kv_dwk", 0), "w_v": grad_out(hn_kv, dkv, "kv_dwv", 1)})
    dhn = _matmul_nt_cols(dkv, w_kv, (D, E), lambda t, j: (0, j), 2, E, D, "kv_dhn")
    dx, dg_kv = _norm_bwd(dhn, x_kv, vecs["norm_kv"] + tok[0:1, 0:1], dx, "kv_dnorm")
    vec["norm_kv"] = [dg_kv]

    for i in reversed(range(n_a)):
        x_in, hn, proj, z, w = a_saved[i]
        dw_out = grad_out(z, dx, f"a{i}_dwout")
        dz = _matmul_nt_rows(dx, w["w_out"], f"a{i}_dz")
        dproj, dwg, dsc = _a_mid_bwd(dz, proj, w["w_grp"], vecs["scale_a"][i:i + 1], f"a{i}_dmid")
        n_grp, gc, _ = dwg.shape
        dwg = dwg.reshape(n_grp, N_DEV, gc // N_DEV, gc).transpose(1, 0, 2, 3).astype(BF16)
        vec["scale_a"][i] = dsc
        cb = w["w_in"].shape[2]
        tok = put_grads(f"a{i}", {"w_out": dw_out, "w_grp": dwg, "w_in": grad_in(hn, dproj, cb, f"a{i}_dwin")})
        dhn = _matmul_nt_cols(dproj, w["w_in"], (None, D, cb), lambda t, j: (j, 0, 0), N_DEV, cb, D, f"a{i}_dhn")
        dx, vec["norm_a"][i] = _norm_bwd(dhn, x_in, vecs["norm_a"][i:i + 1] + tok[0:1, 0:1], dx, f"a{i}_dnorm")

    return loss, dx, {k: jnp.concatenate(v, axis=0) for k, v in vec.items()}


VECTORS = ("norm_a", "scale_a", "norm_kv", "norm_b", "norm_f")
SHARDED_VECTORS = ("norm_a", "scale_a")
GROUPS = {
    "a0": (("w_in", "w_in_a", 0), ("w_grp", "w_grp_a", 0), ("w_out", "w_out_a", 0)),
    "a1": (("w_in", "w_in_a", 1), ("w_grp", "w_grp_a", 1), ("w_out", "w_out_a", 1)),
    "kv": (("w_k", "w_k", None), ("w_v", "w_v", None)),
    "b0": (("w_in", "w_in_b", 0), ("w_out", "w_out_b", 0)),
    "b1": (("w_in", "w_in_b", 1), ("w_out", "w_out_b", 1)),
}
PREFETCHED = ("a1", "kv", "b0", "b1")


def kernel(x, norm_a, w_in_a, w_grp_a, scale_a, w_out_a, norm_kv, w_k, w_v, norm_b, w_in_b, w_out_b, norm_f, loss_target, m_norm_a, m_w_in_a, m_w_grp_a, m_scale_a, m_w_out_a, m_norm_kv, m_w_k, m_w_v, m_norm_b, m_w_in_b, m_w_out_b, m_norm_f, v_norm_a, v_w_in_a, v_w_grp_a, v_scale_a, v_w_out_a, v_norm_kv, v_w_k, v_w_v, v_norm_b, v_w_in_b, v_w_out_b, v_norm_f):
    w = dict(norm_a=norm_a, w_in_a=w_in_a, w_grp_a=w_grp_a, scale_a=scale_a, w_out_a=w_out_a, norm_kv=norm_kv,
             w_k=w_k, w_v=w_v, norm_b=norm_b, w_in_b=w_in_b, w_out_b=w_out_b, norm_f=norm_f)
    m = dict(norm_a=m_norm_a, w_in_a=m_w_in_a, w_grp_a=m_w_grp_a, scale_a=m_scale_a, w_out_a=m_w_out_a,
             norm_kv=m_norm_kv, w_k=m_w_k, w_v=m_w_v, norm_b=m_norm_b, w_in_b=m_w_in_b, w_out_b=m_w_out_b,
             norm_f=m_norm_f)
    v = dict(norm_a=v_norm_a, w_in_a=v_w_in_a, w_grp_a=v_w_grp_a, scale_a=v_scale_a, w_out_a=v_w_out_a,
             norm_kv=v_norm_kv, w_k=v_w_k, w_v=v_w_v, norm_b=v_norm_b, w_in_b=v_w_in_b, w_out_b=v_w_out_b,
             norm_f=v_norm_f)
    D = x.shape[2]
    me = _block_index(*_position())

    def shard(group):
        return [w[p].astype(BF16) if layer is None else w[p][layer].astype(BF16) for _, p, layer in GROUPS[group]]

    def as_weights(group, gathered):
        out = dict(zip([n for n, _, _ in GROUPS[group]], gathered))
        if "w_grp" in out:
            g = out["w_grp"]
            out["w_grp"] = g.transpose(1, 0, 2, 3).reshape(g.shape[1], g.shape[3], g.shape[3])
        if "w_k" in out:
            out = {"w_kv": jnp.concatenate([out["w_k"].reshape(D, D), out["w_v"].reshape(D, D)], axis=1)}
        return out

    first = _all_gather(shard("a0") + [w[k] for k in SHARDED_VECTORS], "gather_first")
    n_first = len(GROUPS["a0"])
    vecs = {k: g.transpose(1, 0, 2).reshape(w[k].shape[0], D) for k, g in zip(SHARDED_VECTORS, first[n_first:])}
    vecs.update(norm_kv=norm_kv[None, :], norm_b=norm_b, norm_f=norm_f[None, :])
    srcs, lands = [], []
    for group in PREFETCHED:
        for s in shard(group):
            srcs.append(s)
            lands.append(lax.dynamic_update_index_in_dim(lax.empty((N_DEV,) + s.shape, s.dtype), s[None], me, 0))
    inflight, at = {}, 0
    token = None
    for group in PREFETCHED:
        n = len(GROUPS[group])
        inflight[group] = _push_start(srcs[at:at + n], lands[at:at + n], SIBLING_AND_SAME_CORES, False, True,
                                      first[0] if token is None else token, f"gather_{group}_start")
        token = inflight[group][4]
        at += n
    vecs["norm_a"] = vecs["norm_a"] + token[0:1, 0:1]

    def get_weights(group, after):
        if group == "a0":
            return as_weights(group, first[0:n_first])
        half = _push_wait(inflight[group], SIBLING_AND_SAME_CORES, False, True, after, f"gather_{group}_wait")[1]
        return as_weights(group, _pass_on(half, f"gather_{group}_pass"))

    sent = {}

    def put_grads(group, grads):
        blocks = [grads[n] for n, _, _ in GROUPS[group]]
        lands = [lax.empty((N_DEV - 1,) + b.shape[1:], b.dtype) for b in blocks]
        sent[group] = _push_start(blocks, lands, ALL_PEERS, True, False, jnp.zeros((8, 128), F32),
                                  f"exchange_{group}_start")
        return sent[group][4]

    loss, dx, vec = _local_step(x[0], loss_target[0], vecs, w_in_a.shape[0], w_in_b.shape[0], get_weights, put_grads)
    rows = _all_reduce_rows(jnp.concatenate([vec[k] for k in VECTORS], axis=0), "reduce_vectors")

    out = {}
    after = dx
    for group in sent:
        blocks, arrived = _push_wait(sent[group], ALL_PEERS, True, False, after, f"exchange_{group}_wait")
        for (_, p, layer), blk, got in zip(GROUPS[group], blocks, arrived):
            cols = w[p].shape[-1]
            own = lax.dynamic_index_in_dim(blk, me, 0, keepdims=False).reshape(-1, cols)
            n_layers = 1 if layer is None else w[p].shape[0]
            stacked = lambda t: t.reshape(n_layers, -1, cols)
            res = _adamw_blocks(own, got.reshape(N_DEV - 1, -1, cols), stacked(w[p]), stacked(m[p]), stacked(v[p]),
                                0 if layer is None else layer, out.get(p), f"adamw_{group}_{p}")
            out[p] = res
            after = res[1]
    out = {p: [r.reshape(w[p].shape) for r in res] for p, res in out.items()}
    start = 0
    for k in VECTORS:
        n_rows = vec[k].shape[0]
        g = rows[start:start + n_rows]
        start += n_rows
        if k in SHARDED_VECTORS:
            g = lax.dynamic_slice_in_dim(g, me * (D // N_DEV), D // N_DEV, axis=1)
        res = _adamw_rows(g, w[k].reshape(g.shape), m[k].reshape(g.shape), v[k].reshape(g.shape), f"adamw_{k}")
        out[k] = [r.reshape(w[k].shape) for r in [g] + list(res)]

    names = ("norm_a", "w_in_a", "w_grp_a", "scale_a", "w_out_a", "norm_kv", "w_k", "w_v", "norm_b", "w_in_b",
             "w_out_b", "norm_f")
    total = lax.psum(loss[0, 0], ("x", "y", "c"))
    return (total, dx[None], *[out[k][0] for k in names], *[out[k][1] for k in names],
            *[out[k][2] for k in names], *[out[k][3] for k in names])
```

```python
import math

import jax
import jax.numpy as jnp
from jax import lax
from jax.experimental import pallas as pl
from jax.experimental.pallas import tpu as pltpu

F32 = jnp.float32
BF16 = jnp.bfloat16

N_DEV = 8
MESH = pl.DeviceIdType.MESH
RMS_EPS = 1e-6
HEAD_DIM = 128
HALF_HEAD = HEAD_DIM // 2
BAND = 128
DILATIONS = (1, 4, 16)
POOL_WINDOWS = (2, 4, 8, 16)
POOL_HALO = 16
ROPE_THETA = 10000.0
NEG_INF = -1e30
ATTN_SCALE = 1.0 / math.sqrt(HEAD_DIM)
LOG2_E = math.log2(math.e)
ADAM_LR, ADAM_B1, ADAM_B2, ADAM_EPS, ADAM_WD, ADAM_STEP = 0.001, 0.9, 0.999, 1e-08, 0.01, 10
VMEM_LIMIT_BYTES = 56 * 1024 * 1024
ANY = pl.BlockSpec(memory_space=pl.ANY)
NT = (((1,), (1,)), ((), ()))
TN = (((0,), (0,)), ((), ()))


def _params(*semantics):
    return pltpu.CompilerParams(dimension_semantics=semantics, vmem_limit_bytes=VMEM_LIMIT_BYTES)


def _sigmoid(t):
    return 1.0 / (1.0 + jnp.exp(-t))


def _rope(t, cos2, sin2):
    return t * cos2 + pltpu.roll(t, HALF_HEAD, 1) * sin2


def _rope_bwd(dt, cos2, sin2):
    return dt * cos2 + pltpu.roll(dt * sin2, HALF_HEAD, 1)


def _norm_matmul(x, gain, w, w_block, w_index, n_col_blocks, tn, name):
    S, D = x.shape
    tm = min(1024, S)

    def body(x_ref, g_ref, w_ref, o_ref, hn_ref, hs_ref):
        @pl.when(pl.program_id(1) == 0)
        def _():
            xf = x_ref[...]
            inv = lax.rsqrt(jnp.mean(xf * xf, axis=-1, keepdims=True) + RMS_EPS)
            hb = ((xf * inv) * g_ref[...]).astype(BF16)
            hs_ref[...] = hb
            hn_ref[...] = hb
        o_ref[...] = jnp.dot(hs_ref[...], w_ref[...], preferred_element_type=F32)

    return pl.pallas_call(
        body, name=name, grid=(S // tm, n_col_blocks),
        in_specs=[pl.BlockSpec((tm, D), lambda i, j: (i, 0)),
                  pl.BlockSpec((1, D), lambda i, j: (0, 0)),
                  pl.BlockSpec(w_block, w_index)],
        out_specs=[pl.BlockSpec((tm, tn), lambda i, j: (i, j)),
                   pl.BlockSpec((tm, D), lambda i, j: (i, 0))],
        out_shape=[jax.ShapeDtypeStruct((S, n_col_blocks * tn), F32), jax.ShapeDtypeStruct((S, D), BF16)],
        scratch_shapes=[pltpu.VMEM((tm, D), BF16)],
        compiler_params=_params("parallel", "arbitrary"))(x, gain, w)


def _matmul_res(a, w, res, name):
    S, K = a.shape
    nd, rb, N = w.shape
    tm = min(512, S)

    def body(a_ref, w_ref, r_ref, o_ref):
        acc = jnp.dot(a_ref[:, 0:rb], w_ref[0], preferred_element_type=F32)
        for k in range(1, nd):
            acc = acc + jnp.dot(a_ref[:, k * rb:(k + 1) * rb], w_ref[k], preferred_element_type=F32)
        o_ref[...] = r_ref[...] + acc

    return pl.pallas_call(
        body, name=name, grid=(S // tm,),
        in_specs=[pl.BlockSpec((tm, K), lambda i: (i, 0)),
                  pl.BlockSpec((nd, rb, N), lambda i: (0, 0, 0)),
                  pl.BlockSpec((tm, N), lambda i: (i, 0))],
        out_specs=pl.BlockSpec((tm, N), lambda i: (i, 0)),
        out_shape=jax.ShapeDtypeStruct((S, N), F32),
        compiler_params=_params("parallel"))(a, w, res)


def _matmul_nt_rows(dy, w, name):
    S, N = dy.shape
    nd, rb, _ = w.shape
    tm = min(512, S)

    def body(d_ref, w_ref, o_ref):
        db = d_ref[...].astype(BF16)
        for k in range(nd):
            o_ref[:, k * rb:(k + 1) * rb] = lax.dot_general(db, w_ref[k], NT, preferred_element_type=F32)

    return pl.pallas_call(
        body, name=name, grid=(S // tm,),
        in_specs=[pl.BlockSpec((tm, N), lambda i: (i, 0)),
                  pl.BlockSpec((nd, rb, N), lambda i: (0, 0, 0))],
        out_specs=pl.BlockSpec((tm, nd * rb), lambda i: (i, 0)),
        out_shape=jax.ShapeDtypeStruct((S, nd * rb), F32),
        compiler_params=_params("parallel"))(dy, w)


def _matmul_nt_cols(dp, w, w_block, w_index, n_red, tc, n_out, name):
    S = dp.shape[0]
    tm = min(1024, S)

    def body(d_ref, w_ref, o_ref):
        @pl.when(pl.program_id(1) == 0)
        def _():
            o_ref[...] = jnp.zeros_like(o_ref)
        o_ref[...] += lax.dot_general(d_ref[...], w_ref[...], NT, preferred_element_type=F32)

    return pl.pallas_call(
        body, name=name, grid=(S // tm, n_red),
        in_specs=[pl.BlockSpec((tm, tc), lambda i, j: (i, j)),
                  pl.BlockSpec(w_block, w_index)],
        out_specs=pl.BlockSpec((tm, n_out), lambda i, j: (i, 0)),
        out_shape=jax.ShapeDtypeStruct((S, n_out), F32),
        compiler_params=_params("parallel", "arbitrary"))(dp, w)


def _matmul_tn(a, b, a_block, a_index, b_block, b_index, out_shape, out_block, out_index, acc_shape, n_outer, name):
    S = a.shape[0]
    ts = a_block[0]
    n_tok = S // ts

    def body(a_ref, b_ref, o_ref, acc_ref):
        s = pl.program_id(1)

        @pl.when(s == 0)
        def _():
            acc_ref[...] = jnp.zeros_like(acc_ref)
        acc_ref[...] += lax.dot_general(a_ref[...].astype(BF16), b_ref[...].astype(BF16), TN,
                                        preferred_element_type=F32)

        @pl.when(s == n_tok - 1)
        def _():
            o_ref[...] = acc_ref[...].astype(o_ref.dtype)

    return pl.pallas_call(
        body, name=name, grid=(n_outer, n_tok),
        in_specs=[pl.BlockSpec(a_block, a_index), pl.BlockSpec(b_block, b_index)],
        out_specs=pl.BlockSpec(out_block, out_index),
        out_shape=jax.ShapeDtypeStruct(out_shape, BF16),
        scratch_shapes=[pltpu.VMEM(acc_shape, F32)],
        compiler_params=_params("parallel", "arbitrary"))(a, b)


def _pool(scr_ref, u, row0, tm, E):
    gc = E // len(POOL_WINDOWS)
    t1 = row0 + lax.broadcasted_iota(jnp.int32, (tm, 1), 0) + 1
    out = []
    for g, win in enumerate(POOL_WINDOWS):
        cs = slice(g * gc, (g + 1) * gc)
        acc = u[:, cs]
        for k in range(1, win):
            acc = acc + scr_ref[pl.ds(POOL_HALO - k, tm), cs]
        count = jnp.minimum(t1, win).astype(F32)
        out.append(acc / count - u[:, cs])
    return out


def _a_mid_fwd(proj, wg, scale, name):
    S, E2 = proj.shape
    E = E2 // 2
    gc = E // len(POOL_WINDOWS)
    tm = min(256, S)
    hb = tm // POOL_HALO

    def body(u_ref, uh_ref, gt_ref, wg_ref, sc_ref, z_ref, scr_ref):
        i = pl.program_id(0)
        scr_ref[0:POOL_HALO, :] = jnp.where(i > 0, uh_ref[...], 0.0)
        u = u_ref[...]
        scr_ref[POOL_HALO:POOL_HALO + tm, :] = u
        pooled = _pool(scr_ref, u, i * tm, tm, E)
        for g in range(len(POOL_WINDOWS)):
            cs = slice(g * gc, (g + 1) * gc)
            y = jnp.dot(pooled[g].astype(BF16), wg_ref[g], preferred_element_type=F32) * sc_ref[:, cs]
            gate = gt_ref[:, cs]
            z_ref[:, cs] = (y * (gate * _sigmoid(gate))).astype(BF16)

    return pl.pallas_call(
        body, name=name, grid=(S // tm,),
        in_specs=[pl.BlockSpec((tm, E), lambda i: (i, 0)),
                  pl.BlockSpec((POOL_HALO, E), lambda i: (jnp.maximum(i * hb - 1, 0), 0)),
                  pl.BlockSpec((tm, E), lambda i: (i, 1)),
                  pl.BlockSpec((len(POOL_WINDOWS), gc, gc), lambda i: (0, 0, 0)),
                  pl.BlockSpec((1, E), lambda i: (0, 0))],
        out_specs=pl.BlockSpec((tm, E), lambda i: (i, 0)),
        out_shape=jax.ShapeDtypeStruct((S, E), BF16),
        scratch_shapes=[pltpu.VMEM((POOL_HALO + tm, E), F32)],
        compiler_params=_params("parallel"))(proj, proj, proj, wg, scale)


def _a_mid_bwd(dz, proj, wg, scale, name):
    S, E2 = proj.shape
    E = E2 // 2
    n_grp = len(POOL_WINDOWS)
    gc = E // n_grp
    tm = min(256, S)
    hb = tm // POOL_HALO
    n_tiles = S // tm
    last_halo = S // POOL_HALO - 1

    def body(dz_ref, dzh_ref, u_ref, uh_ref, gt_ref, gth_ref, wg_ref, sc_ref, dp_ref, dwg_ref, dsc_ref, scr_ref, q_ref):
        i = pl.program_id(0)

        @pl.when(i == 0)
        def _():
            dwg_ref[...] = jnp.zeros_like(dwg_ref)
            dsc_ref[...] = jnp.zeros_like(dsc_ref)

        scr_ref[0:POOL_HALO, :] = jnp.where(i > 0, uh_ref[...], 0.0)
        u = u_ref[...]
        scr_ref[POOL_HALO:POOL_HALO + tm, :] = u
        pooled = _pool(scr_ref, u, i * tm, tm, E)
        t1 = i * tm + lax.broadcasted_iota(jnp.int32, (tm, 1), 0) + 1
        t1h = (i + 1) * tm + lax.broadcasted_iota(jnp.int32, (POOL_HALO, 1), 0) + 1
        not_last = i < n_tiles - 1
        for g, win in enumerate(POOL_WINDOWS):
            cs = slice(g * gc, (g + 1) * gc)
            w = wg_ref[g]
            sc = sc_ref[:, cs]
            pb = pooled[g].astype(BF16)
            ypre = jnp.dot(pb, w, preferred_element_type=F32)
            gate = gt_ref[:, cs]
            sg = _sigmoid(gate)
            silu = gate * sg
            dzg = dz_ref[:, cs]
            dy = dzg * silu
            dp_ref[:, E + g * gc:E + (g + 1) * gc] = (dzg * (ypre * sc) * (sg * (1.0 + gate * (1.0 - sg)))).astype(BF16)
            dsc_ref[:, cs] += jnp.sum(dy * ypre, axis=0, keepdims=True)
            dyp = (dy * sc).astype(BF16)
            dwg_ref[g] += lax.dot_general(pb, dyp, TN, preferred_element_type=F32)
            dpool = lax.dot_general(dyp, w, NT, preferred_element_type=F32)
            gate_h = gth_ref[:, cs]
            dyp_h = (dzh_ref[:, cs] * (gate_h * _sigmoid(gate_h)) * sc).astype(BF16)
            dpool_h = lax.dot_general(dyp_h, w, NT, preferred_element_type=F32)
            q_ref[0:tm, cs] = dpool / jnp.minimum(t1, win).astype(F32)
            q_ref[tm:tm + POOL_HALO, cs] = jnp.where(not_last, dpool_h / jnp.minimum(t1h, win).astype(F32), 0.0)
            acc = q_ref[0:tm, cs] - dpool
            for k in range(1, win):
                acc = acc + q_ref[pl.ds(k, tm), cs]
            dp_ref[:, cs] = acc.astype(BF16)

    return pl.pallas_call(
        body, name=name, grid=(n_tiles,),
        in_specs=[pl.BlockSpec((tm, E), lambda i: (i, 0)),
                  pl.BlockSpec((POOL_HALO, E), lambda i: (jnp.minimum((i + 1) * hb, last_halo), 0)),
                  pl.BlockSpec((tm, E), lambda i: (i, 0)),
                  pl.BlockSpec((POOL_HALO, E), lambda i: (jnp.maximum(i * hb - 1, 0), 0)),
                  pl.BlockSpec((tm, E), lambda i: (i, 1)),
                  pl.BlockSpec((POOL_HALO, E), lambda i: (jnp.minimum((i + 1) * hb, last_halo), 1)),
                  pl.BlockSpec((n_grp, gc, gc), lambda i: (0, 0, 0)),
                  pl.BlockSpec((1, E), lambda i: (0, 0))],
        out_specs=[pl.BlockSpec((tm, E2), lambda i: (i, 0)),
                   pl.BlockSpec((n_grp, gc, gc), lambda i: (0, 0, 0)),
                   pl.BlockSpec((1, E), lambda i: (0, 0))],
        out_shape=[jax.ShapeDtypeStruct((S, E2), BF16),
                   jax.ShapeDtypeStruct((n_grp, gc, gc), F32),
                   jax.ShapeDtypeStruct((1, E), F32)],
        scratch_shapes=[pltpu.VMEM((POOL_HALO + tm, E), F32), pltpu.VMEM((tm + POOL_HALO, E), F32)],
        compiler_params=_params("arbitrary"))(dz, dz, proj, proj, proj, proj, wg, scale)


def _rope_k(kvp, cos2, sin2, name):
    S, E2 = kvp.shape
    E = E2 // 2
    tm = min(256, S)

    def body(k_ref, c_ref, s_ref, ko_ref):
        cosv, sinv = c_ref[...], s_ref[...]
        for h in range(E // HEAD_DIM):
            hs = slice(h * HEAD_DIM, (h + 1) * HEAD_DIM)
            ko_ref[:, hs] = _rope(k_ref[:, hs], cosv, sinv)

    return pl.pallas_call(
        body, name=name, grid=(S // tm,),
        in_specs=[pl.BlockSpec((tm, E), lambda i: (i, 0)),
                  pl.BlockSpec((tm, HEAD_DIM), lambda i: (i, 0)), pl.BlockSpec((tm, HEAD_DIM), lambda i: (i, 0))],
        out_specs=pl.BlockSpec((tm, E), lambda i: (i, 0)),
        out_shape=jax.ShapeDtypeStruct((S, E), F32),
        compiler_params=_params("parallel"))(kvp, cos2, sin2)


def _rows(r, b, dil, n=BAND):
    start = r + b * BAND * dil
    return pl.ds(start, n) if dil == 1 else pl.ds(start, n, stride=dil)


def _by_class(table, dil):
    S, W = table.shape
    return table if dil == 1 else table.reshape(S // (BAND * dil), BAND, dil, W).transpose(2, 0, 1, 3).reshape(S, W)


def _table_rows(r, b, nb):
    return pl.ds((r * nb + b) * BAND, BAND)


def _band_mask(first):
    row = lax.broadcasted_iota(jnp.int32, (BAND, 2 * BAND), 0)
    col = lax.broadcasted_iota(jnp.int32, (BAND, 2 * BAND), 1)
    mask = (col >= row) & (col <= row + BAND)
    return mask & (col >= BAND) if first else mask


def _lane_column(tile, lane, h):
    return jnp.sum(jnp.where(lane == h, tile, 0.0), axis=-1, keepdims=True)


def _keys(ref, r, b, dil):
    if b > 0:
        return ref[_rows(r, b - 1, dil, 2 * BAND), :].astype(BF16)
    return jnp.concatenate([jnp.zeros((BAND, HEAD_DIM), BF16), ref[_rows(r, 0, dil), :].astype(BF16)], axis=0)


def _attn_fwd(proj, kr, kvp, cos2, sin2, group, dil, name):
    S, PW = proj.shape
    E = kr.shape[1]
    H = E // HEAD_DIM
    nb = S // (BAND * dil)

    def body(q_ref, k_ref, v_ref, c_ref, s_ref, o_ref, l_ref):
        h = pl.program_id(0)
        lane = lax.broadcasted_iota(jnp.int32, (BAND, HEAD_DIM), 1)
        edge, inner = _band_mask(True), _band_mask(False)

        @pl.when(h == 0)
        def _():
            l_ref[...] = jnp.zeros_like(l_ref)

        def scores(r, b):
            rows, trows = _rows(r, b, dil), _table_rows(r, b, nb)
            qr = _rope(q_ref[rows, :], c_ref[trows, :], s_ref[trows, :]).astype(BF16)
            return lax.dot_general(qr, _keys(k_ref, r, b, dil), NT, preferred_element_type=F32)

        units = [(r, b) for r in range(dil) for b in range(nb)]
        ahead = scores(*units[0])
        for i, (r, b) in enumerate(units):
            s = ahead
            if i + 1 < len(units):
                ahead = scores(*units[i + 1])
            rows = _rows(r, b, dil)
            s = jnp.where(edge if b == 0 else inner, s, NEG_INF)
            m = jnp.max(s, axis=-1, keepdims=True)
            p = jnp.exp2((s - m) * (ATTN_SCALE * LOG2_E))
            l = jnp.sum(p, axis=-1, keepdims=True)
            o_ref[rows, :] = jnp.dot(p.astype(BF16), _keys(v_ref, r, b, dil), preferred_element_type=F32) / l
            l_ref[rows, :] = jnp.where(lane == h, m * ATTN_SCALE + jnp.log(l), l_ref[rows, :])

    col = (S, HEAD_DIM)
    whole = pl.BlockSpec(col, lambda h: (0, 0))
    return pl.pallas_call(
        body, name=name, grid=(H,),
        in_specs=[pl.BlockSpec(col, lambda h: (0, group * H + h)), pl.BlockSpec(col, lambda h: (0, h)),
                  pl.BlockSpec(col, lambda h: (0, H + h)), whole, whole],
        out_specs=[pl.BlockSpec(col, lambda h: (0, h)), whole],
        out_shape=[jax.ShapeDtypeStruct((S, E), F32), jax.ShapeDtypeStruct((S, HEAD_DIM), F32)],
        compiler_params=_params("arbitrary"))(proj, kr, kvp, _by_class(cos2, dil), _by_class(sin2, dil))


def _attn_bwd(proj, kr, kvp, cos2, sin2, do, lse, dlt, dproj, group, dil, name):
    S, PW = proj.shape
    E = kr.shape[1]
    H = E // HEAD_DIM
    nb = S // (BAND * dil)

    def body(q_ref, k_ref, v_ref, c_ref, s_ref, do_ref, l_ref, dl_ref, dproj_ref,
             dq_ref, dk_ref, dv_ref, dq_scr, dk_scr, dv_scr):
        h = pl.program_id(0)
        lane = lax.broadcasted_iota(jnp.int32, (BAND, HEAD_DIM), 1)
        edge, inner = _band_mask(True), _band_mask(False)
        dk_scr[...] = jnp.zeros_like(dk_scr)
        dv_scr[...] = jnp.zeros_like(dv_scr)
        def scores(r, b):
            rows, trows = _rows(r, b, dil), _table_rows(r, b, nb)
            qr = _rope(q_ref[rows, :], c_ref[trows, :], s_ref[trows, :]).astype(BF16)
            return qr, lax.dot_general(qr, _keys(k_ref, r, b, dil), NT, preferred_element_type=F32)

        units = [(r, b) for r in range(dil) for b in range(nb)]
        ahead = scores(*units[0])
        for i, (r, b) in enumerate(units):
            qr, s = ahead
            if i + 1 < len(units):
                ahead = scores(*units[i + 1])
            rows = _rows(r, b, dil)
            dob = do_ref[rows, :].astype(BF16)
            dpr = lax.dot_general(dob, _keys(v_ref, r, b, dil), NT, preferred_element_type=F32)
            s = jnp.where(edge if b == 0 else inner, s, NEG_INF)
            p = jnp.exp2(s * (ATTN_SCALE * LOG2_E) - _lane_column(l_ref[rows, :], lane, h) * LOG2_E)
            ds = (p * (dpr - _lane_column(dl_ref[rows, :], lane, h)) * ATTN_SCALE).astype(BF16)
            dq = jnp.dot(ds, _keys(k_ref, r, b, dil), preferred_element_type=F32)
            trows = _table_rows(r, b, nb)
            dq_scr[rows, :] = _rope_bwd(dq, c_ref[trows, :], s_ref[trows, :])
            dkc = lax.dot_general(ds, qr, TN, preferred_element_type=F32)
            dvc = lax.dot_general(p.astype(BF16), dob, TN, preferred_element_type=F32)
            if b > 0:
                both = _rows(r, b - 1, dil, 2 * BAND)
                dk_scr[both, :] += dkc
                dv_scr[both, :] += dvc
            else:
                dk_scr[rows, :] += dkc[BAND:2 * BAND]
                dv_scr[rows, :] += dvc[BAND:2 * BAND]
        dq_ref[...] = dq_scr[...].astype(BF16)
        dk_ref[...] = dk_scr[...].astype(BF16)
        dv_ref[...] = dv_scr[...].astype(BF16)

    col = (S, HEAD_DIM)
    whole = pl.BlockSpec(col, lambda h: (0, 0))
    head = pl.BlockSpec(col, lambda h: (0, h))
    return pl.pallas_call(
        body, name=name, grid=(H,),
        in_specs=[pl.BlockSpec(col, lambda h: (0, group * H + h)), head, pl.BlockSpec(col, lambda h: (0, H + h)),
                  whole, whole, head, whole, whole, ANY],
        out_specs=[pl.BlockSpec(col, lambda h: (0, group * H + h)), head, head],
        out_shape=[jax.ShapeDtypeStruct(dproj.shape, BF16), jax.ShapeDtypeStruct((S, E), BF16),
                   jax.ShapeDtypeStruct((S, E), BF16)],
        scratch_shapes=[pltpu.VMEM(col, F32)] * 3,
        input_output_aliases={8: 0},
        compiler_params=_params("parallel"))(proj, kr, kvp, _by_class(cos2, dil), _by_class(sin2, dil), do, lse, dlt,
                                             dproj)


def _group_weights(l_refs, h):
    ls = [r[:, h:h + 1] for r in l_refs]
    mx = jnp.maximum(jnp.maximum(ls[0], ls[1]), ls[2])
    es = [jnp.exp(l - mx) for l in ls]
    inv = 1.0 / (es[0] + es[1] + es[2])
    return [e * inv for e in es]


def _merge_fwd(outs, lses, proj, name):
    S, E = outs[0].shape
    tm = min(256, S)
    gate_col = proj.shape[1] // E - 1

    def body(o0, o1, o2, l0, l1, l2, gt_ref, z_ref):
        for h in range(E // HEAD_DIM):
            hs = slice(h * HEAD_DIM, (h + 1) * HEAD_DIM)
            a = _group_weights((l0, l1, l2), h)
            merged = a[0] * o0[:, hs] + a[1] * o1[:, hs] + a[2] * o2[:, hs]
            gate = gt_ref[:, hs]
            z_ref[:, hs] = (merged * (gate * _sigmoid(gate))).astype(BF16)

    wide = pl.BlockSpec((tm, E), lambda i: (i, 0))
    thin = pl.BlockSpec((tm, HEAD_DIM), lambda i: (i, 0))
    return pl.pallas_call(
        body, name=name, grid=(S // tm,),
        in_specs=[wide, wide, wide, thin, thin, thin, pl.BlockSpec((tm, E), lambda i: (i, gate_col))],
        out_specs=wide,
        out_shape=jax.ShapeDtypeStruct((S, E), BF16),
        compiler_params=_params("parallel"))(*outs, *lses, proj)


def _merge_bwd(dz, outs, lses, proj, name):
    S, E = outs[0].shape
    tm = min(256, S)
    gate_col = proj.shape[1] // E - 1

    def body(dz_ref, o0, o1, o2, l0, l1, l2, gt_ref, d0, d1, d2, t0, t1, t2, dg_ref):
        o_refs, d_refs, t_refs = (o0, o1, o2), (d0, d1, d2), (t0, t1, t2)
        lane = lax.broadcasted_iota(jnp.int32, (tm, HEAD_DIM), 1)
        tiles = [jnp.zeros((tm, HEAD_DIM), F32) for _ in range(3)]
        for h in range(E // HEAD_DIM):
            hs = slice(h * HEAD_DIM, (h + 1) * HEAD_DIM)
            a = _group_weights((l0, l1, l2), h)
            merged = a[0] * o0[:, hs] + a[1] * o1[:, hs] + a[2] * o2[:, hs]
            gate = gt_ref[:, hs]
            sg = _sigmoid(gate)
            dzh = dz_ref[:, hs]
            dmerged = dzh * (gate * sg)
            dg_ref[:, hs] = (dzh * merged * (sg * (1.0 + gate * (1.0 - sg)))).astype(BF16)
            tot = jnp.sum(dmerged * merged, axis=-1, keepdims=True)
            for g in range(3):
                d_refs[g][:, hs] = a[g] * dmerged
                tiles[g] = jnp.where(lane == h, a[g] * tot, tiles[g])
        for g in range(3):
            t_refs[g][...] = tiles[g]

    wide = pl.BlockSpec((tm, E), lambda i: (i, 0))
    thin = pl.BlockSpec((tm, HEAD_DIM), lambda i: (i, 0))
    res = pl.pallas_call(
        body, name=name, grid=(S // tm,),
        in_specs=[wide, wide, wide, wide, thin, thin, thin, pl.BlockSpec((tm, E), lambda i: (i, gate_col))],
        out_specs=[wide, wide, wide, thin, thin, thin, pl.BlockSpec((tm, E), lambda i: (i, gate_col))],
        out_shape=[jax.ShapeDtypeStruct((S, E), F32)] * 3 + [jax.ShapeDtypeStruct((S, HEAD_DIM), F32)] * 3
        + [jax.ShapeDtypeStruct(proj.shape, BF16)],
        compiler_params=_params("parallel"))(dz, *outs, *lses, proj)
    return res[0:3], res[3:6], res[6]


def _kv_bwd(dks, dvs, cos2, sin2, name):
    S, E = dks[0].shape
    n = len(dks)
    tm = min(256, S)

    def body(*refs):
        dk_refs, dv_refs = refs[0:n], refs[n:2 * n]
        c_ref, s_ref, o_ref = refs[2 * n:]
        cosv, sinv = c_ref[...], s_ref[...]
        for h in range(E // HEAD_DIM):
            hs = slice(h * HEAD_DIM, (h + 1) * HEAD_DIM)
            dk = dk_refs[0][:, hs].astype(F32)
            dv = dv_refs[0][:, hs].astype(F32)
            for j in range(1, n):
                dk = dk + dk_refs[j][:, hs].astype(F32)
                dv = dv + dv_refs[j][:, hs].astype(F32)
            o_ref[:, hs] = _rope_bwd(dk, cosv, sinv).astype(BF16)
            o_ref[:, E + h * HEAD_DIM:E + (h + 1) * HEAD_DIM] = dv.astype(BF16)

    wide = pl.BlockSpec((tm, E), lambda i: (i, 0))
    thin = pl.BlockSpec((tm, HEAD_DIM), lambda i: (i, 0))
    return pl.pallas_call(
        body, name=name, grid=(S // tm,),
        in_specs=[wide] * (2 * n) + [thin, thin],
        out_specs=pl.BlockSpec((tm, 2 * E), lambda i: (i, 0)),
        out_shape=jax.ShapeDtypeStruct((S, 2 * E), BF16),
        compiler_params=_params("parallel"))(*dks, *dvs, cos2, sin2)


def _norm_bwd(dhn, x, gain, dres, name):
    S, D = x.shape
    tm = min(256, S)

    def body(dh_ref, x_ref, g_ref, r_ref, dx_ref, dg_ref):
        @pl.when(pl.program_id(0) == 0)
        def _():
            dg_ref[...] = jnp.zeros_like(dg_ref)
        xf = x_ref[...]
        inv = lax.rsqrt(jnp.mean(xf * xf, axis=-1, keepdims=True) + RMS_EPS)
        xhat = xf * inv
        dh = dh_ref[...]
        dg_ref[...] += jnp.sum(dh * xhat, axis=0, keepdims=True)
        dxh = dh * g_ref[...]
        dx_ref[...] = r_ref[...] + inv * (dxh - xhat * jnp.mean(dxh * xhat, axis=-1, keepdims=True))

    tile = pl.BlockSpec((tm, D), lambda i: (i, 0))
    vec = pl.BlockSpec((1, D), lambda i: (0, 0))
    return pl.pallas_call(
        body, name=name, grid=(S // tm,),
        in_specs=[tile, tile, vec, tile],
        out_specs=[tile, vec],
        out_shape=[jax.ShapeDtypeStruct((S, D), F32), jax.ShapeDtypeStruct((1, D), F32)],
        compiler_params=_params("arbitrary"))(dhn, x, gain, dres)


def _final_norm_loss(x, target, gain, name):
    S, D = x.shape
    tm = min(256, S)

    def body(x_ref, t_ref, g_ref, loss_ref, dx_ref, dg_ref):
        @pl.when(pl.program_id(0) == 0)
        def _():
            loss_ref[...] = jnp.zeros_like(loss_ref)
            dg_ref[...] = jnp.zeros_like(dg_ref)
        xf = x_ref[...]
        inv = lax.rsqrt(jnp.mean(xf * xf, axis=-1, keepdims=True) + RMS_EPS)
        xhat = xf * inv
        g = g_ref[...]
        err = xhat * g - t_ref[...]
        loss_ref[...] += 0.5 * jnp.sum(jnp.mean(err * err, axis=-1, keepdims=True), axis=0, keepdims=True)
        dy = err / D
        dg_ref[...] += jnp.sum(dy * xhat, axis=0, keepdims=True)
        dxh = dy * g
        dx_ref[...] = inv * (dxh - xhat * jnp.mean(dxh * xhat, axis=-1, keepdims=True))

    tile = pl.BlockSpec((tm, D), lambda i: (i, 0))
    vec = pl.BlockSpec((1, D), lambda i: (0, 0))
    return pl.pallas_call(
        body, name=name, grid=(S // tm,),
        in_specs=[tile, tile, vec],
        out_specs=[pl.BlockSpec((1, 1), lambda i: (0, 0)), tile, vec],
        out_shape=[jax.ShapeDtypeStruct((1, 1), F32), jax.ShapeDtypeStruct((S, D), F32),
                   jax.ShapeDtypeStruct((1, D), F32)],
        compiler_params=_params("arbitrary"))(x, target, gain)


def _adamw_math(g, w, m, v):
    m = ADAM_B1 * m + (1.0 - ADAM_B1) * g
    v = ADAM_B2 * v + (1.0 - ADAM_B2) * (g * g)
    m_hat = m / (1.0 - ADAM_B1 ** ADAM_STEP)
    v_hat = v / (1.0 - ADAM_B2 ** ADAM_STEP)
    delta = -ADAM_LR * (m_hat / (jnp.sqrt(v_hat) + ADAM_EPS) + ADAM_WD * w)
    return delta, m, v


def _adamw_rows(g, w, m, v, name):
    def body(g_ref, w_ref, m_ref, v_ref, d_ref, mo_ref, vo_ref):
        d_ref[...], mo_ref[...], vo_ref[...] = _adamw_math(g_ref[...], w_ref[...], m_ref[...], v_ref[...])

    whole = pl.BlockSpec(memory_space=pltpu.VMEM)
    return pl.pallas_call(
        body, name=name, in_specs=[whole] * 4, out_specs=[whole] * 3,
        out_shape=[jax.ShapeDtypeStruct(g.shape, F32)] * 3)(g, w, m, v)


def _adamw_blocks(own, others, w, m, v, layer, earlier, name):
    L, R, C = w.shape
    n = others.shape[0]
    tr = R
    while tr * C > 128 * 1024 and tr % 16 == 0:
        tr //= 2

    def body(o_ref, p_ref, w_ref, m_ref, v_ref, *rest):
        g_ref, d_ref, mo_ref, vo_ref = rest[-4:]
        g = o_ref[...].astype(F32)
        for j in range(n):
            g = g + p_ref[j].astype(F32)
        g_ref[...] = g
        d_ref[...], mo_ref[...], vo_ref[...] = _adamw_math(g, w_ref[...], m_ref[...], v_ref[...])

    tile = pl.BlockSpec((None, tr, C), lambda i: (layer, i, 0))
    kept = [] if earlier is None else list(earlier)
    return pl.pallas_call(
        body, name=name, grid=(R // tr,),
        in_specs=[pl.BlockSpec((tr, C), lambda i: (i, 0)), pl.BlockSpec((n, tr, C), lambda i: (0, i, 0)),
                  tile, tile, tile] + [ANY] * len(kept),
        out_specs=[tile] * 4,
        out_shape=[jax.ShapeDtypeStruct((L, R, C), F32)] * 4,
        input_output_aliases={5 + j: j for j in range(len(kept))},
        compiler_params=_params("parallel"))(own, others, w, m, v, *kept)


def _position():
    return lax.axis_index("x"), lax.axis_index("y"), lax.axis_index("c")


def _block_index(px, py, pc):
    return 4 * px + 2 * py + pc


def _all_gather(shards, name):
    n = len(shards)

    def body(*refs):
        ins, outs = refs[0:n], refs[n:2 * n]
        send_sems, recv_sems, local_sems = refs[2 * n:]
        x, y, c = _position()
        me, sibling = (x, y, c), (x, y, 1 - c)
        chips = [(1 - x, y), (x, 1 - y), (1 - x, 1 - y)]

        def copy(a, k, block, to, src=None):
            rows = outs[a].at[_block_index(*block)]
            return pltpu.make_async_remote_copy(
                src_ref=rows if src is None else src, dst_ref=rows,
                send_sem=send_sems.at[a, k], recv_sem=recv_sems.at[a, k], device_id=to, device_id_type=MESH)

        mine, first, passed = [], [], []
        for a in range(n):
            cp = pltpu.make_async_copy(ins[a], outs[a].at[_block_index(*me)], local_sems.at[a])
            cp.start()
            mine.append(cp)
            first.append(copy(a, 0, me, sibling, src=ins[a]))
            first += [copy(a, 1 + j, me, (*chip, c), src=ins[a]) for j, chip in enumerate(chips)]
        for cp in first:
            cp.start()
        for j, chip in enumerate(chips):
            for a in range(n):
                copy(a, 1 + j, (*chip, c), me).wait_recv()
                fwd = copy(a, 4 + j, (*chip, c), sibling)
                fwd.start()
                passed.append(fwd)
        for a in range(n):
            copy(a, 0, sibling, me).wait_recv()
            for j, chip in enumerate(chips):
                copy(a, 4 + j, (*chip, 1 - c), me).wait_recv()
        for cp in first + passed:
            cp.wait_send()
        for cp in mine:
            cp.wait()

    return pl.pallas_call(
        body, name=name,
        in_specs=[ANY] * n, out_specs=[ANY] * n,
        out_shape=[jax.ShapeDtypeStruct((N_DEV,) + s.shape, s.dtype) for s in shards],
        scratch_shapes=[pltpu.SemaphoreType.DMA((n, 7)), pltpu.SemaphoreType.DMA((n, 7)),
                        pltpu.SemaphoreType.DMA((n,))],
    )(*shards)


def _peers(x, y, c):
    return [((1 - x) if k & 4 else x, (1 - y) if k & 2 else y, (1 - c) if k & 1 else c) for k in range(1, N_DEV)]


HBM = pl.BlockSpec(memory_space=pltpu.HBM)
SEM = pl.BlockSpec(memory_space=pltpu.SEMAPHORE)
EFFECT = pltpu.SideEffectType.DATAFLOW_SIDE_EFFECTING


ALL_PEERS = (1, 2, 3, 4, 5, 6, 7)
SIBLING_AND_SAME_CORES = (1, 2, 4, 6)


def _push_copy(src_refs, land_refs, send_sems, recv_sems, a, i, relations, per_peer, by_sender, arriving):
    peer = _peers(*_position())[relations[i] - 1]
    me_idx, p_idx = _block_index(*_position()), _block_index(*peer)
    src = src_refs[a].at[p_idx] if per_peer else src_refs[a]
    if by_sender:
        slot = p_idx if arriving else me_idx
    else:
        slot = relations[i] - 1
    sem = a * len(relations) + i
    return pltpu.make_async_remote_copy(
        src_ref=src, dst_ref=land_refs[a].at[slot], send_sem=send_sems.at[sem], recv_sem=recv_sems.at[sem],
        device_id=peer, device_id_type=MESH)


def _push_start(srcs, lands, relations, per_peer, by_sender, after, name):
    n = len(srcs)

    def body(*refs):
        src_refs, land_refs = refs[0:n], refs[n:2 * n]
        send_sems, recv_sems = refs[2 * n + 1], refs[2 * n + 2]
        token = refs[-1]
        for a in range(n):
            for i in range(len(relations)):
                _push_copy(src_refs, land_refs, send_sems, recv_sems, a, i, relations, per_peer, by_sender, False).start()
        token[...] = jnp.zeros_like(token)

    args = [pltpu.with_memory_space_constraint(t, pltpu.HBM) for t in list(srcs) + list(lands)]
    res = pl.pallas_call(
        body, name=name,
        in_specs=[HBM] * (2 * n) + [ANY],
        out_specs=[SEM, SEM] + [HBM] * (2 * n) + [pl.BlockSpec(memory_space=pltpu.VMEM)],
        out_shape=[pltpu.SemaphoreType.DMA((n * len(relations),)), pltpu.SemaphoreType.DMA((n * len(relations),))]
        + [pltpu.HBM(t.shape, t.dtype) for t in args] + [jax.ShapeDtypeStruct((8, 128), F32)],
        input_output_aliases={i: 2 + i for i in range(2 * n)},
        compiler_params=pltpu.CompilerParams(has_side_effects=EFFECT))(*args, after)
    return res[0], res[1], res[2:2 + n], res[2 + n:2 + 2 * n], res[-1]


def _push_wait(started, relations, per_peer, by_sender, after, name):
    send_sems, recv_sems, srcs, lands, _ = started
    n = len(srcs)

    def body(*refs):
        src_refs, land_refs = refs[0:n], refs[n:2 * n]
        send_s, recv_s = refs[2 * n], refs[2 * n + 1]
        for a in range(n):
            for i in range(len(relations)):
                _push_copy(src_refs, land_refs, send_s, recv_s, a, i, relations, per_peer, by_sender, False).wait_send()
                _push_copy(src_refs, land_refs, send_s, recv_s, a, i, relations, per_peer, by_sender, True).wait_recv()

    res = pl.pallas_call(
        body, name=name,
        in_specs=[HBM] * (2 * n) + [SEM, SEM, ANY],
        out_specs=[HBM] * (2 * n),
        out_shape=[pltpu.HBM(t.shape, t.dtype) for t in list(srcs) + list(lands)],
        input_output_aliases={i: i for i in range(2 * n)},
        compiler_params=pltpu.CompilerParams(has_side_effects=EFFECT))(*srcs, *lands, send_sems, recv_sems, after)
    return res[0:n], res[n:2 * n]


def _pass_on(lands, name):
    n = len(lands)

    def body(*refs):
        outs = refs[n:2 * n]
        send_sems, recv_sems = refs[2 * n:]
        x, y, c = _position()
        chips = [(1 - x, y), (x, 1 - y), (1 - x, 1 - y)]
        copies = []
        for a in range(n):
            for j, chip in enumerate(chips):
                def copy(core):
                    rows = outs[a].at[_block_index(*chip, core)]
                    return pltpu.make_async_remote_copy(
                        src_ref=rows, dst_ref=rows, send_sem=send_sems.at[a, j], recv_sem=recv_sems.at[a, j],
                        device_id=(x, y, 1 - c), device_id_type=MESH)
                copy(c).start()
                copies.append((copy(c), copy(1 - c)))
        for sending, arriving in copies:
            sending.wait_send()
            arriving.wait_recv()

    return pl.pallas_call(
        body, name=name,
        in_specs=[ANY] * n, out_specs=[ANY] * n,
        out_shape=[jax.ShapeDtypeStruct(t.shape, t.dtype) for t in lands],
        input_output_aliases={a: a for a in range(n)},
        scratch_shapes=[pltpu.SemaphoreType.DMA((n, 3)), pltpu.SemaphoreType.DMA((n, 3))],
    )(*lands)


def _all_reduce_rows(v, name):
    R, D = v.shape

    def body(v_ref, o_ref, buf_ref, send_sems, recv_sems):
        x, y, c = _position()
        me_idx = _block_index(x, y, c)
        buf_ref[me_idx] = v_ref[...]
        copies = []
        for k in range(1, N_DEV):
            px = (1 - x) if k & 4 else x
            py = (1 - y) if k & 2 else y
            pc = (1 - c) if k & 1 else c
            rc = pltpu.make_async_remote_copy(
                src_ref=v_ref, dst_ref=buf_ref.at[me_idx],
                send_sem=send_sems.at[k - 1], recv_sem=recv_sems.at[k - 1],
                device_id=(px, py, pc), device_id_type=MESH)
            rc.start()
            copies.append((rc, pltpu.make_async_remote_copy(
                src_ref=v_ref, dst_ref=buf_ref.at[_block_index(px, py, pc)],
                send_sem=send_sems.at[k - 1], recv_sem=recv_sems.at[k - 1],
                device_id=(px, py, pc), device_id_type=MESH)))
        for rc, arrival in copies:
            rc.wait_send()
            arrival.wait_recv()
        acc = buf_ref[0]
        for j in range(1, N_DEV):
            acc = acc + buf_ref[j]
        o_ref[...] = acc

    return pl.pallas_call(
        body, name=name,
        in_specs=[pl.BlockSpec(memory_space=pltpu.VMEM)],
        out_specs=pl.BlockSpec(memory_space=pltpu.VMEM),
        out_shape=jax.ShapeDtypeStruct((R, D), F32),
        scratch_shapes=[pltpu.VMEM((N_DEV, R, D), F32),
                        pltpu.SemaphoreType.DMA((7,)), pltpu.SemaphoreType.DMA((7,))],
    )(v)


def _rope_tables(S):
    inv_freq = 1.0 / (ROPE_THETA ** (jnp.arange(0, HEAD_DIM, 2, dtype=F32) / HEAD_DIM))
    ang = jnp.arange(S, dtype=F32)[:, None] * inv_freq[None, :]
    cos, sin = jnp.cos(ang), jnp.sin(ang)
    return jnp.concatenate([cos, cos], axis=1), jnp.concatenate([-sin, sin], axis=1)


def _local_step(xs, target, vecs, n_a, n_b, get_weights, put_grads):
    S, D = xs.shape
    E = D
    cos2, sin2 = _rope_tables(S)
    ts = min(1024, S)

    def col_blocks(w):
        cb = w.shape[2]
        tn = min(cb, 1024)
        per = cb // tn
        return (None, D, tn), (lambda i, j: (j // per, 0, j % per)), N_DEV * per, tn

    def grad_in(hn, dproj, cb, name):
        return _matmul_tn(hn, dproj, (ts, D), lambda j, s: (s, 0), (ts, cb), lambda j, s: (s, j),
                          (N_DEV, D, cb), (None, D, cb), lambda j, s: (j, 0, 0), (D, cb), N_DEV, name)

    def grad_out(z, dx, name, col=0):
        rows = z.shape[1]
        ta = min(1024, rows)
        out = _matmul_tn(z, dx, (ts, ta), lambda a, s: (s, a), (ts, E), lambda a, s: (s, col),
                         (rows, E), (ta, E), lambda a, s: (a, 0), (ta, E), rows // ta, name)
        return out.reshape(N_DEV, rows // N_DEV, E)

    x = xs
    a_saved, b_saved = [], []
    for i in range(n_a):
        w = get_weights(f"a{i}", x)
        blk, idx, nblocks, tn = col_blocks(w["w_in"])
        proj, hn = _norm_matmul(x, vecs["norm_a"][i:i + 1], w["w_in"], blk, idx, nblocks, tn, f"a{i}_in")
        z = _a_mid_fwd(proj, w["w_grp"], vecs["scale_a"][i:i + 1], f"a{i}_mid")
        x_next = _matmul_res(z, w["w_out"], x, f"a{i}_out")
        a_saved.append((x, hn, proj, z, w))
        x = x_next
    x_kv = x
    w_kv = get_weights("kv", x)["w_kv"]
    tn = min(E, 1024)
    kvp, hn_kv = _norm_matmul(x, vecs["norm_kv"], w_kv, (D, tn), lambda i, j: (0, j), 2 * E // tn, tn, "kv_in")
    kr = _rope_k(kvp, cos2, sin2, "kv_rope")
    after = kr
    for i in range(n_b):
        w = get_weights(f"b{i}", after)
        blk, idx, nblocks, tn = col_blocks(w["w_in"])
        proj, hn = _norm_matmul(x, vecs["norm_b"][i:i + 1], w["w_in"], blk, idx, nblocks, tn, f"b{i}_in")
        outs, lses = [], []
        for g, dil in enumerate(DILATIONS):
            o, l = _attn_fwd(proj, kr, kvp, cos2, sin2, g, dil, f"b{i}_attn{g}")
            outs.append(o)
            lses.append(l)
        z = _merge_fwd(outs, lses, proj, f"b{i}_merge")
        x_next = _matmul_res(z, w["w_out"], x, f"b{i}_out")
        b_saved.append((x, hn, proj, z, outs, lses, w))
        x = x_next
        after = x
    loss, dx, dg_f = _final_norm_loss(x, target, vecs["norm_f"], "final")

    vec = {"norm_a": [None] * n_a, "scale_a": [None] * n_a, "norm_b": [None] * n_b, "norm_f": [dg_f]}
    dks, dvs = [], []
    for i in reversed(range(n_b)):
        x_in, hn, proj, z, outs, lses, w = b_saved[i]
        dw_out = grad_out(z, dx, f"b{i}_dwout")
        dz = _matmul_nt_rows(dx, w["w_out"], f"b{i}_dz")
        dos, dlts, dproj = _merge_bwd(dz, outs, lses, proj, f"b{i}_dmerge")
        for g, dil in enumerate(DILATIONS):
            dproj, dk, dv = _attn_bwd(proj, kr, kvp, cos2, sin2, dos[g], lses[g], dlts[g], dproj, g, dil,
                                      f"b{i}_dattn{g}")
            dks.append(dk)
            dvs.append(dv)
        cb = w["w_in"].shape[2]
        tok = put_grads(f"b{i}", {"w_out": dw_out, "w_in": grad_in(hn, dproj, cb, f"b{i}_dwin")})
        dhn = _matmul_nt_cols(dproj, w["w_in"], (None, D, cb), lambda t, j: (j, 0, 0), N_DEV, cb, D, f"b{i}_dhn")
        dx, vec["norm_b"][i] = _norm_bwd(dhn, x_in, vecs["norm_b"][i:i + 1] + tok[0:1, 0:1], dx, f"b{i}_dnorm")

    dkv = _kv_bwd(dks, dvs, cos2, sin2, "kv_dsum")
    tok = put_grads("kv", {"w_k": grad_out(hn_kv, dkv, "kv_dwk", 0), "w_v": grad_out(hn_kv, dkv, "kv_dwv", 1)})
    dhn = _matmul_nt_cols(dkv, w_kv, (D, E), lambda t, j: (0, j), 2, E, D, "kv_dhn")
    dx, dg_kv = _norm_bwd(dhn, x_kv, vecs["norm_kv"] + tok[0:1, 0:1], dx, "kv_dnorm")
    vec["norm_kv"] = [dg_kv]

    for i in reversed(range(n_a)):
        x_in, hn, proj, z, w = a_saved[i]
        dw_out = grad_out(z, dx, f"a{i}_dwout")
        dz = _matmul_nt_rows(dx, w["w_out"], f"a{i}_dz")
        dproj, dwg, dsc = _a_mid_bwd(dz, proj, w["w_grp"], vecs["scale_a"][i:i + 1], f"a{i}_dmid")
        n_grp, gc, _ = dwg.shape
        dwg = dwg.reshape(n_grp, N_DEV, gc // N_DEV, gc).transpose(1, 0, 2, 3).astype(BF16)
        vec["scale_a"][i] = dsc
        cb = w["w_in"].shape[2]
        tok = put_grads(f"a{i}", {"w_out": dw_out, "w_grp": dwg, "w_in": grad_in(hn, dproj, cb, f"a{i}_dwin")})
        dhn = _matmul_nt_cols(dproj, w["w_in"], (None, D, cb), lambda t, j: (j, 0, 0), N_DEV, cb, D, f"a{i}_dhn")
        dx, vec["norm_a"][i] = _norm_bwd(dhn, x_in, vecs["norm_a"][i:i + 1] + tok[0:1, 0:1], dx, f"a{i}_dnorm")

    return loss, dx, {k: jnp.concatenate(v, axis=0) for k, v in vec.items()}


VECTORS = ("norm_a", "scale_a", "norm_kv", "norm_b", "norm_f")
SHARDED_VECTORS = ("norm_a", "scale_a")
GROUPS = {
    "a0": (("w_in", "w_in_a", 0), ("w_grp", "w_grp_a", 0), ("w_out", "w_out_a", 0)),
    "a1": (("w_in", "w_in_a", 1), ("w_grp", "w_grp_a", 1), ("w_out", "w_out_a", 1)),
    "kv": (("w_k", "w_k", None), ("w_v", "w_v", None)),
    "b0": (("w_in", "w_in_b", 0), ("w_out", "w_out_b", 0)),
    "b1": (("w_in", "w_in_b", 1), ("w_out", "w_out_b", 1)),
}
PREFETCHED = ("a1", "kv", "b0", "b1")


def kernel(x, norm_a, w_in_a, w_grp_a, scale_a, w_out_a, norm_kv, w_k, w_v, norm_b, w_in_b, w_out_b, norm_f, loss_target, m_norm_a, m_w_in_a, m_w_grp_a, m_scale_a, m_w_out_a, m_norm_kv, m_w_k, m_w_v, m_norm_b, m_w_in_b, m_w_out_b, m_norm_f, v_norm_a, v_w_in_a, v_w_grp_a, v_scale_a, v_w_out_a, v_norm_kv, v_w_k, v_w_v, v_norm_b, v_w_in_b, v_w_out_b, v_norm_f):
    w = dict(norm_a=norm_a, w_in_a=w_in_a, w_grp_a=w_grp_a, scale_a=scale_a, w_out_a=w_out_a, norm_kv=norm_kv,
             w_k=w_k, w_v=w_v, norm_b=norm_b, w_in_b=w_in_b, w_out_b=w_out_b, norm_f=norm_f)
    m = dict(norm_a=m_norm_a, w_in_a=m_w_in_a, w_grp_a=m_w_grp_a, scale_a=m_scale_a, w_out_a=m_w_out_a,
             norm_kv=m_norm_kv, w_k=m_w_k, w_v=m_w_v, norm_b=m_norm_b, w_in_b=m_w_in_b, w_out_b=m_w_out_b,
             norm_f=m_norm_f)
    v = dict(norm_a=v_norm_a, w_in_a=v_w_in_a, w_grp_a=v_w_grp_a, scale_a=v_scale_a, w_out_a=v_w_out_a,
             norm_kv=v_norm_kv, w_k=v_w_k, w_v=v_w_v, norm_b=v_norm_b, w_in_b=v_w_in_b, w_out_b=v_w_out_b,
             norm_f=v_norm_f)
    D = x.shape[2]
    me = _block_index(*_position())

    def shard(group):
        return [w[p].astype(BF16) if layer is None else w[p][layer].astype(BF16) for _, p, layer in GROUPS[group]]

    def as_weights(group, gathered):
        out = dict(zip([n for n, _, _ in GROUPS[group]], gathered))
        if "w_grp" in out:
            g = out["w_grp"]
            out["w_grp"] = g.transpose(1, 0, 2, 3).reshape(g.shape[1], g.shape[3], g.shape[3])
        if "w_k" in out:
            out = {"w_kv": jnp.concatenate([out["w_k"].reshape(D, D), out["w_v"].reshape(D, D)], axis=1)}
        return out

    first = _all_gather(shard("a0") + [w[k] for k in SHARDED_VECTORS], "gather_first")
    n_first = len(GROUPS["a0"])
    vecs = {k: g.transpose(1, 0, 2).reshape(w[k].shape[0], D) for k, g in zip(SHARDED_VECTORS, first[n_first:])}
    vecs.update(norm_kv=norm_kv[None, :], norm_b=norm_b, norm_f=norm_f[None, :])
    srcs, lands = [], []
    for group in PREFETCHED:
        for s in shard(group):
            srcs.append(s)
            lands.append(lax.dynamic_update_index_in_dim(lax.empty((N_DEV,) + s.shape, s.dtype), s[None], me, 0))
    inflight, at = {}, 0
    token = None
    for group in PREFETCHED:
        n = len(GROUPS[group])
        inflight[group] = _push_start(srcs[at:at + n], lands[at:at + n], SIBLING_AND_SAME_CORES, False, True,
                                      first[0] if token is None else token, f"gather_{group}_start")
        token = inflight[group][4]
        at += n
    vecs["norm_a"] = vecs["norm_a"] + token[0:1, 0:1]

    def get_weights(group, after):
        if group == "a0":
            return as_weights(group, first[0:n_first])
        half = _push_wait(inflight[group], SIBLING_AND_SAME_CORES, False, True, after, f"gather_{group}_wait")[1]
        return as_weights(group, _pass_on(half, f"gather_{group}_pass"))

    sent = {}

    def put_grads(group, grads):
        blocks = [grads[n] for n, _, _ in GROUPS[group]]
        lands = [lax.empty((N_DEV - 1,) + b.shape[1:], b.dtype) for b in blocks]
        sent[group] = _push_start(blocks, lands, ALL_PEERS, True, False, jnp.zeros((8, 128), F32),
                                  f"exchange_{group}_start")
        return sent[group][4]

    loss, dx, vec = _local_step(x[0], loss_target[0], vecs, w_in_a.shape[0], w_in_b.shape[0], get_weights, put_grads)
    rows = _all_reduce_rows(jnp.concatenate([vec[k] for k in VECTORS], axis=0), "reduce_vectors")

    out = {}
    after = dx
    for group in sent:
        blocks, arrived = _push_wait(sent[group], ALL_PEERS, True, False, after, f"exchange_{group}_wait")
        for (_, p, layer), blk, got in zip(GROUPS[group], blocks, arrived):
            cols = w[p].shape[-1]
            own = lax.dynamic_index_in_dim(blk, me, 0, keepdims=False).reshape(-1, cols)
            n_layers = 1 if layer is None else w[p].shape[0]
            stacked = lambda t: t.reshape(n_layers, -1, cols)
            res = _adamw_blocks(own, got.reshape(N_DEV - 1, -1, cols), stacked(w[p]), stacked(m[p]), stacked(v[p]),
                                0 if layer is None else layer, out.get(p), f"adamw_{group}_{p}")
            out[p] = res
            after = res[1]
    out = {p: [r.reshape(w[p].shape) for r in res] for p, res in out.items()}
    start = 0
    for k in VECTORS:
        n_rows = vec[k].shape[0]
        g = rows[start:start + n_rows]
        start += n_rows
        if k in SHARDED_VECTORS:
            g = lax.dynamic_slice_in_dim(g, me * (D // N_DEV), D // N_DEV, axis=1)
        res = _adamw_rows(g, w[k].reshape(g.shape), m[k].reshape(g.shape), v[k].reshape(g.shape), f"adamw_{k}")
        out[k] = [r.reshape(w[k].shape) for r in [g] + list(res)]

    names = ("norm_a", "w_in_a", "w_grp_a", "scale_a", "w_out_a", "norm_kv", "w_k", "w_v", "norm_b", "w_in_b",
             "w_out_b", "norm_f")
    total = lax.psum(loss[0, 0], ("x", "y", "c"))
    return (total, dx[None], *[out[k][0] for k in names], *[out[k][1] for k in names],
            *[out[k][2] for k in names], *[out[k][3] for k in names])
```

```python
import math

import jax
import jax.numpy as jnp
from jax import lax
from jax.experimental import pallas as pl
from jax.experimental.pallas import tpu as pltpu

F32 = jnp.float32
BF16 = jnp.bfloat16

N_DEV = 8
MESH = pl.DeviceIdType.MESH
RMS_EPS = 1e-6
HEAD_DIM = 128
HALF_HEAD = HEAD_DIM // 2
BAND = 128
DILATIONS = (1, 4, 16)
POOL_WINDOWS = (2, 4, 8, 16)
POOL_HALO = 16
ROPE_THETA = 10000.0
NEG_INF = -1e30
ATTN_SCALE = 1.0 / math.sqrt(HEAD_DIM)
LOG2_E = math.log2(math.e)
ADAM_LR, ADAM_B1, ADAM_B2, ADAM_EPS, ADAM_WD, ADAM_STEP = 0.001, 0.9, 0.999, 1e-08, 0.01, 10
VMEM_LIMIT_BYTES = 56 * 1024 * 1024
ANY = pl.BlockSpec(memory_space=pl.ANY)
NT = (((1,), (1,)), ((), ()))
TN = (((0,), (0,)), ((), ()))


def _params(*semantics):
    return pltpu.CompilerParams(dimension_semantics=semantics, vmem_limit_bytes=VMEM_LIMIT_BYTES)


def _sigmoid(t):
    return 1.0 / (1.0 + jnp.exp(-t))


def _rope(t, cos2, sin2):
    return t * cos2 + pltpu.roll(t, HALF_HEAD, 1) * sin2


def _rope_bwd(dt, cos2, sin2):
    return dt * cos2 + pltpu.roll(dt * sin2, HALF_HEAD, 1)


def _norm_matmul(x, gain, w, w_block, w_index, n_col_blocks, tn, name):
    S, D = x.shape
    tm = min(1024, S)

    def body(x_ref, g_ref, w_ref, o_ref, hn_ref, hs_ref):
        @pl.when(pl.program_id(1) == 0)
        def _():
            xf = x_ref[...]
            inv = lax.rsqrt(jnp.mean(xf * xf, axis=-1, keepdims=True) + RMS_EPS)
            hb = ((xf * inv) * g_ref[...]).astype(BF16)
            hs_ref[...] = hb
            hn_ref[...] = hb
        o_ref[...] = jnp.dot(hs_ref[...], w_ref[...], preferred_element_type=F32)

    return pl.pallas_call(
        body, name=name, grid=(S // tm, n_col_blocks),
        in_specs=[pl.BlockSpec((tm, D), lambda i, j: (i, 0)),
                  pl.BlockSpec((1, D), lambda i, j: (0, 0)),
                  pl.BlockSpec(w_block, w_index)],
        out_specs=[pl.BlockSpec((tm, tn), lambda i, j: (i, j)),
                   pl.BlockSpec((tm, D), lambda i, j: (i, 0))],
        out_shape=[jax.ShapeDtypeStruct((S, n_col_blocks * tn), F32), jax.ShapeDtypeStruct((S, D), BF16)],
        scratch_shapes=[pltpu.VMEM((tm, D), BF16)],
        compiler_params=_params("parallel", "arbitrary"))(x, gain, w)


def _matmul_res(a, w, res, name):
    S, K = a.shape
    nd, rb, N = w.shape
    tm = min(512, S)

    def body(a_ref, w_ref, r_ref, o_ref):
        acc = jnp.dot(a_ref[:, 0:rb], w_ref[0], preferred_element_type=F32)
        for k in range(1, nd):
            acc = acc + jnp.dot(a_ref[:, k * rb:(k + 1) * rb], w_ref[k], preferred_element_type=F32)
        o_ref[...] = r_ref[...] + acc

    return pl.pallas_call(
        body, name=name, grid=(S // tm,),
        in_specs=[pl.BlockSpec((tm, K), lambda i: (i, 0)),
                  pl.BlockSpec((nd, rb, N), lambda i: (0, 0, 0)),
                  pl.BlockSpec((tm, N), lambda i: (i, 0))],
        out_specs=pl.BlockSpec((tm, N), lambda i: (i, 0)),
        out_shape=jax.ShapeDtypeStruct((S, N), F32),
        compiler_params=_params("parallel"))(a, w, res)


def _matmul_nt_rows(dy, w, name):
    S, N = dy.shape
    nd, rb, _ = w.shape
    tm = min(512, S)

    def body(d_ref, w_ref, o_ref):
        db = d_ref[...].astype(BF16)
        for k in range(nd):
            o_ref[:, k * rb:(k + 1) * rb] = lax.dot_general(db, w_ref[k], NT, preferred_element_type=F32)

    return pl.pallas_call(
        body, name=name, grid=(S // tm,),
        in_specs=[pl.BlockSpec((tm, N), lambda i: (i, 0)),
                  pl.BlockSpec((nd, rb, N), lambda i: (0, 0, 0))],
        out_specs=pl.BlockSpec((tm, nd * rb), lambda i: (i, 0)),
        out_shape=jax.ShapeDtypeStruct((S, nd * rb), F32),
        compiler_params=_params("parallel"))(dy, w)


def _matmul_nt_dnorm(dp, w, w_block, w_index, n_red, tc, x, gain, dres, name):
    S, D = x.shape
    tm = min(512, S)

    def body(d_ref, w_ref, x_ref, g_ref, r_ref, dx_ref, dg_ref, acc_ref):
        i, j = pl.program_id(0), pl.program_id(1)

        @pl.when((i == 0) & (j == 0))
        def _():
            dg_ref[...] = jnp.zeros_like(dg_ref)

        @pl.when(j == 0)
        def _():
            acc_ref[...] = jnp.zeros_like(acc_ref)
        acc_ref[...] += lax.dot_general(d_ref[...], w_ref[...], NT, preferred_element_type=F32)

        @pl.when(j == n_red - 1)
        def _():
            xf = x_ref[...]
            inv = lax.rsqrt(jnp.mean(xf * xf, axis=-1, keepdims=True) + RMS_EPS)
            xhat = xf * inv
            dh = acc_ref[...]
            dg_ref[...] += jnp.sum(dh * xhat, axis=0, keepdims=True)
            dxh = dh * g_ref[...]
            dx_ref[...] = r_ref[...] + inv * (dxh - xhat * jnp.mean(dxh * xhat, axis=-1, keepdims=True))

    tile = pl.BlockSpec((tm, D), lambda i, j: (i, 0))
    vec = pl.BlockSpec((1, D), lambda i, j: (0, 0))
    return pl.pallas_call(
        body, name=name, grid=(S // tm, n_red),
        in_specs=[pl.BlockSpec((tm, tc), lambda i, j: (i, j)), pl.BlockSpec(w_block, w_index), tile, vec, tile],
        out_specs=[tile, vec],
        out_shape=[jax.ShapeDtypeStruct((S, D), F32), jax.ShapeDtypeStruct((1, D), F32)],
        scratch_shapes=[pltpu.VMEM((tm, D), F32)],
        compiler_params=_params("arbitrary", "arbitrary"))(dp, w, x, gain, dres)


def _matmul_tn(a, b, a_block, a_index, b_block, b_index, out_shape, out_block, out_index, acc_shape, n_outer, name):
    S = a.shape[0]
    ts = a_block[0]
    n_tok = S // ts

    def body(a_ref, b_ref, o_ref, acc_ref):
        s = pl.program_id(1)

        @pl.when(s == 0)
        def _():
            acc_ref[...] = jnp.zeros_like(acc_ref)
        acc_ref[...] += lax.dot_general(a_ref[...].astype(BF16), b_ref[...].astype(BF16), TN,
                                        preferred_element_type=F32)

        @pl.when(s == n_tok - 1)
        def _():
            o_ref[...] = acc_ref[...].astype(o_ref.dtype)

    return pl.pallas_call(
        body, name=name, grid=(n_outer, n_tok),
        in_specs=[pl.BlockSpec(a_block, a_index), pl.BlockSpec(b_block, b_index)],
        out_specs=pl.BlockSpec(out_block, out_index),
        out_shape=jax.ShapeDtypeStruct(out_shape, BF16),
        scratch_shapes=[pltpu.VMEM(acc_shape, F32)],
        compiler_params=_params("parallel", "arbitrary"))(a, b)


def _pool(scr_ref, u, row0, tm, E):
    gc = E // len(POOL_WINDOWS)
    t1 = row0 + lax.broadcasted_iota(jnp.int32, (tm, 1), 0) + 1
    out = []
    for g, win in enumerate(POOL_WINDOWS):
        cs = slice(g * gc, (g + 1) * gc)
        acc = u[:, cs]
        for k in range(1, win):
            acc = acc + scr_ref[pl.ds(POOL_HALO - k, tm), cs]
        count = jnp.minimum(t1, win).astype(F32)
        out.append(acc / count - u[:, cs])
    return out


def _a_mid_fwd(proj, wg, scale, name):
    S, E2 = proj.shape
    E = E2 // 2
    gc = E // len(POOL_WINDOWS)
    tm = min(256, S)
    hb = tm // POOL_HALO

    def body(u_ref, uh_ref, gt_ref, wg_ref, sc_ref, z_ref, scr_ref):
        i = pl.program_id(0)
        scr_ref[0:POOL_HALO, :] = jnp.where(i > 0, uh_ref[...], 0.0)
        u = u_ref[...]
        scr_ref[POOL_HALO:POOL_HALO + tm, :] = u
        pooled = _pool(scr_ref, u, i * tm, tm, E)
        for g in range(len(POOL_WINDOWS)):
            cs = slice(g * gc, (g + 1) * gc)
            y = jnp.dot(pooled[g].astype(BF16), wg_ref[g], preferred_element_type=F32) * sc_ref[:, cs]
            gate = gt_ref[:, cs]
            z_ref[:, cs] = (y * (gate * _sigmoid(gate))).astype(BF16)

    return pl.pallas_call(
        body, name=name, grid=(S // tm,),
        in_specs=[pl.BlockSpec((tm, E), lambda i: (i, 0)),
                  pl.BlockSpec((POOL_HALO, E), lambda i: (jnp.maximum(i * hb - 1, 0), 0)),
                  pl.BlockSpec((tm, E), lambda i: (i, 1)),
                  pl.BlockSpec((len(POOL_WINDOWS), gc, gc), lambda i: (0, 0, 0)),
                  pl.BlockSpec((1, E), lambda i: (0, 0))],
        out_specs=pl.BlockSpec((tm, E), lambda i: (i, 0)),
        out_shape=jax.ShapeDtypeStruct((S, E), BF16),
        scratch_shapes=[pltpu.VMEM((POOL_HALO + tm, E), F32)],
        compiler_params=_params("parallel"))(proj, proj, proj, wg, scale)


def _a_mid_bwd(dz, proj, wg, scale, name):
    S, E2 = proj.shape
    E = E2 // 2
    n_grp = len(POOL_WINDOWS)
    gc = E // n_grp
    tm = min(256, S)
    hb = tm // POOL_HALO
    n_tiles = S // tm
    last_halo = S // POOL_HALO - 1

    def body(dz_ref, dzh_ref, u_ref, uh_ref, gt_ref, gth_ref, wg_ref, sc_ref, dp_ref, dwg_ref, dsc_ref, scr_ref, q_ref):
        i = pl.program_id(0)

        @pl.when(i == 0)
        def _():
            dwg_ref[...] = jnp.zeros_like(dwg_ref)
            dsc_ref[...] = jnp.zeros_like(dsc_ref)

        scr_ref[0:POOL_HALO, :] = jnp.where(i > 0, uh_ref[...], 0.0)
        u = u_ref[...]
        scr_ref[POOL_HALO:POOL_HALO + tm, :] = u
        pooled = _pool(scr_ref, u, i * tm, tm, E)
        t1 = i * tm + lax.broadcasted_iota(jnp.int32, (tm, 1), 0) + 1
        t1h = (i + 1) * tm + lax.broadcasted_iota(jnp.int32, (POOL_HALO, 1), 0) + 1
        not_last = i < n_tiles - 1
        for g, win in enumerate(POOL_WINDOWS):
            cs = slice(g * gc, (g + 1) * gc)
            w = wg_ref[g]
            sc = sc_ref[:, cs]
            pb = pooled[g].astype(BF16)
            ypre = jnp.dot(pb, w, preferred_element_type=F32)
            gate = gt_ref[:, cs]
            sg = _sigmoid(gate)
            silu = gate * sg
            dzg = dz_ref[:, cs]
            dy = dzg * silu
            dp_ref[:, E + g * gc:E + (g + 1) * gc] = (dzg * (ypre * sc) * (sg * (1.0 + gate * (1.0 - sg)))).astype(BF16)
            dsc_ref[:, cs] += jnp.sum(dy * ypre, axis=0, keepdims=True)
            dyp = (dy * sc).astype(BF16)
            dwg_ref[g] += lax.dot_general(pb, dyp, TN, preferred_element_type=F32)
            dpool = lax.dot_general(dyp, w, NT, preferred_element_type=F32)
            gate_h = gth_ref[:, cs]
            dyp_h = (dzh_ref[:, cs] * (gate_h * _sigmoid(gate_h)) * sc).astype(BF16)
            dpool_h = lax.dot_general(dyp_h, w, NT, preferred_element_type=F32)
            q_ref[0:tm, cs] = dpool / jnp.minimum(t1, win).astype(F32)
            q_ref[tm:tm + POOL_HALO, cs] = jnp.where(not_last, dpool_h / jnp.minimum(t1h, win).astype(F32), 0.0)
            acc = q_ref[0:tm, cs] - dpool
            for k in range(1, win):
                acc = acc + q_ref[pl.ds(k, tm), cs]
            dp_ref[:, cs] = acc.astype(BF16)

    return pl.pallas_call(
        body, name=name, grid=(n_tiles,),
        in_specs=[pl.BlockSpec((tm, E), lambda i: (i, 0)),
                  pl.BlockSpec((POOL_HALO, E), lambda i: (jnp.minimum((i + 1) * hb, last_halo), 0)),
                  pl.BlockSpec((tm, E), lambda i: (i, 0)),
                  pl.BlockSpec((POOL_HALO, E), lambda i: (jnp.maximum(i * hb - 1, 0), 0)),
                  pl.BlockSpec((tm, E), lambda i: (i, 1)),
                  pl.BlockSpec((POOL_HALO, E), lambda i: (jnp.minimum((i + 1) * hb, last_halo), 1)),
                  pl.BlockSpec((n_grp, gc, gc), lambda i: (0, 0, 0)),
                  pl.BlockSpec((1, E), lambda i: (0, 0))],
        out_specs=[pl.BlockSpec((tm, E2), lambda i: (i, 0)),
                   pl.BlockSpec((n_grp, gc, gc), lambda i: (0, 0, 0)),
                   pl.BlockSpec((1, E), lambda i: (0, 0))],
        out_shape=[jax.ShapeDtypeStruct((S, E2), BF16),
                   jax.ShapeDtypeStruct((n_grp, gc, gc), F32),
                   jax.ShapeDtypeStruct((1, E), F32)],
        scratch_shapes=[pltpu.VMEM((POOL_HALO + tm, E), F32), pltpu.VMEM((tm + POOL_HALO, E), F32)],
        compiler_params=_params("arbitrary"))(dz, dz, proj, proj, proj, proj, wg, scale)


def _rope_k(kvp, cos2, sin2, name):
    S, E2 = kvp.shape
    E = E2 // 2
    tm = min(256, S)

    def body(k_ref, c_ref, s_ref, ko_ref):
        cosv, sinv = c_ref[...], s_ref[...]
        for h in range(E // HEAD_DIM):
            hs = slice(h * HEAD_DIM, (h + 1) * HEAD_DIM)
            ko_ref[:, hs] = _rope(k_ref[:, hs], cosv, sinv)

    return pl.pallas_call(
        body, name=name, grid=(S // tm,),
        in_specs=[pl.BlockSpec((tm, E), lambda i: (i, 0)),
                  pl.BlockSpec((tm, HEAD_DIM), lambda i: (i, 0)), pl.BlockSpec((tm, HEAD_DIM), lambda i: (i, 0))],
        out_specs=pl.BlockSpec((tm, E), lambda i: (i, 0)),
        out_shape=jax.ShapeDtypeStruct((S, E), F32),
        compiler_params=_params("parallel"))(kvp, cos2, sin2)


def _rows(r, b, dil, n=BAND):
    start = r + b * BAND * dil
    return pl.ds(start, n) if dil == 1 else pl.ds(start, n, stride=dil)


def _by_class(table, dil):
    S, W = table.shape
    return table if dil == 1 else table.reshape(S // (BAND * dil), BAND, dil, W).transpose(2, 0, 1, 3).reshape(S, W)


def _table_rows(r, b, nb):
    return pl.ds((r * nb + b) * BAND, BAND)


def _band_mask(first):
    row = lax.broadcasted_iota(jnp.int32, (BAND, 2 * BAND), 0)
    col = lax.broadcasted_iota(jnp.int32, (BAND, 2 * BAND), 1)
    mask = (col >= row) & (col <= row + BAND)
    return mask & (col >= BAND) if first else mask


def _lane_column(tile, lane, h):
    return jnp.sum(jnp.where(lane == h, tile, 0.0), axis=-1, keepdims=True)


def _keys(ref, r, b, dil):
    if b > 0:
        return ref[_rows(r, b - 1, dil, 2 * BAND), :].astype(BF16)
    return jnp.concatenate([jnp.zeros((BAND, HEAD_DIM), BF16), ref[_rows(r, 0, dil), :].astype(BF16)], axis=0)


def _attn_fwd(proj, kr, kvp, cos2, sin2, group, dil, name):
    S, PW = proj.shape
    E = kr.shape[1]
    H = E // HEAD_DIM
    nb = S // (BAND * dil)

    def body(q_ref, k_ref, v_ref, c_ref, s_ref, o_ref, l_ref):
        h = pl.program_id(0)
        lane = lax.broadcasted_iota(jnp.int32, (BAND, HEAD_DIM), 1)
        edge, inner = _band_mask(True), _band_mask(False)

        @pl.when(h == 0)
        def _():
            l_ref[...] = jnp.zeros_like(l_ref)

        def scores(r, b):
            rows, trows = _rows(r, b, dil), _table_rows(r, b, nb)
            qr = _rope(q_ref[rows, :], c_ref[trows, :], s_ref[trows, :]).astype(BF16)
            return lax.dot_general(qr, _keys(k_ref, r, b, dil), NT, preferred_element_type=F32)

        units = [(r, b) for r in range(dil) for b in range(nb)]
        ahead = scores(*units[0])
        for i, (r, b) in enumerate(units):
            s = ahead
            if i + 1 < len(units):
                ahead = scores(*units[i + 1])
            rows = _rows(r, b, dil)
            s = jnp.where(edge if b == 0 else inner, s, NEG_INF)
            m = jnp.max(s, axis=-1, keepdims=True)
            p = jnp.exp2((s - m) * (ATTN_SCALE * LOG2_E))
            l = jnp.sum(p, axis=-1, keepdims=True)
            o_ref[rows, :] = jnp.dot(p.astype(BF16), _keys(v_ref, r, b, dil), preferred_element_type=F32) / l
            l_ref[rows, :] = jnp.where(lane == h, m * ATTN_SCALE + jnp.log(l), l_ref[rows, :])

    col = (S, HEAD_DIM)
    whole = pl.BlockSpec(col, lambda h: (0, 0))
    return pl.pallas_call(
        body, name=name, grid=(H,),
        in_specs=[pl.BlockSpec(col, lambda h: (0, group * H + h)), pl.BlockSpec(col, lambda h: (0, h)),
                  pl.BlockSpec(col, lambda h: (0, H + h)), whole, whole],
        out_specs=[pl.BlockSpec(col, lambda h: (0, h)), whole],
        out_shape=[jax.ShapeDtypeStruct((S, E), F32), jax.ShapeDtypeStruct((S, HEAD_DIM), F32)],
        compiler_params=_params("arbitrary"))(proj, kr, kvp, _by_class(cos2, dil), _by_class(sin2, dil))


def _attn_bwd(proj, kr, kvp, cos2, sin2, do, lse, dlt, dproj, group, dil, name):
    S, PW = proj.shape
    E = kr.shape[1]
    H = E // HEAD_DIM
    nb = S // (BAND * dil)

    def body(q_ref, k_ref, v_ref, c_ref, s_ref, do_ref, l_ref, dl_ref, dproj_ref,
             dq_ref, dk_ref, dv_ref, dq_scr, dk_scr, dv_scr):
        h = pl.program_id(0)
        lane = lax.broadcasted_iota(jnp.int32, (BAND, HEAD_DIM), 1)
        edge, inner = _band_mask(True), _band_mask(False)
        dk_scr[...] = jnp.zeros_like(dk_scr)
        dv_scr[...] = jnp.zeros_like(dv_scr)
        def scores(r, b):
            rows, trows = _rows(r, b, dil), _table_rows(r, b, nb)
            qr = _rope(q_ref[rows, :], c_ref[trows, :], s_ref[trows, :]).astype(BF16)
            return qr, lax.dot_general(qr, _keys(k_ref, r, b, dil), NT, preferred_element_type=F32)

        units = [(r, b) for r in range(dil) for b in range(nb)]
        ahead = scores(*units[0])
        for i, (r, b) in enumerate(units):
            qr, s = ahead
            if i + 1 < len(units):
                ahead = scores(*units[i + 1])
            rows = _rows(r, b, dil)
            dob = do_ref[rows, :].astype(BF16)
            dpr = lax.dot_general(dob, _keys(v_ref, r, b, dil), NT, preferred_element_type=F32)
            s = jnp.where(edge if b == 0 else inner, s, NEG_INF)
            p = jnp.exp2(s * (ATTN_SCALE * LOG2_E) - _lane_column(l_ref[rows, :], lane, h) * LOG2_E)
            ds = (p * (dpr - _lane_column(dl_ref[rows, :], lane, h)) * ATTN_SCALE).astype(BF16)
            dq = jnp.dot(ds, _keys(k_ref, r, b, dil), preferred_element_type=F32)
            trows = _table_rows(r, b, nb)
            dq_scr[rows, :] = _rope_bwd(dq, c_ref[trows, :], s_ref[trows, :])
            dkc = lax.dot_general(ds, qr, TN, preferred_element_type=F32)
            dvc = lax.dot_general(p.astype(BF16), dob, TN, preferred_element_type=F32)
            if b > 0:
                both = _rows(r, b - 1, dil, 2 * BAND)
                dk_scr[both, :] += dkc
                dv_scr[both, :] += dvc
            else:
                dk_scr[rows, :] += dkc[BAND:2 * BAND]
                dv_scr[rows, :] += dvc[BAND:2 * BAND]
        dq_ref[...] = dq_scr[...].astype(BF16)
        dk_ref[...] = dk_scr[...].astype(BF16)
        dv_ref[...] = dv_scr[...].astype(BF16)

    col = (S, HEAD_DIM)
    whole = pl.BlockSpec(col, lambda h: (0, 0))
    head = pl.BlockSpec(col, lambda h: (0, h))
    return pl.pallas_call(
        body, name=name, grid=(H,),
        in_specs=[pl.BlockSpec(col, lambda h: (0, group * H + h)), head, pl.BlockSpec(col, lambda h: (0, H + h)),
                  whole, whole, head, whole, whole, ANY],
        out_specs=[pl.BlockSpec(col, lambda h: (0, group * H + h)), head, head],
        out_shape=[jax.ShapeDtypeStruct(dproj.shape, BF16), jax.ShapeDtypeStruct((S, E), BF16),
                   jax.ShapeDtypeStruct((S, E), BF16)],
        scratch_shapes=[pltpu.VMEM(col, F32)] * 3,
        input_output_aliases={8: 0},
        compiler_params=_params("parallel"))(proj, kr, kvp, _by_class(cos2, dil), _by_class(sin2, dil), do, lse, dlt,
                                             dproj)


def _group_weights(l_refs, h):
    ls = [r[:, h:h + 1] for r in l_refs]
    mx = jnp.maximum(jnp.maximum(ls[0], ls[1]), ls[2])
    es = [jnp.exp(l - mx) for l in ls]
    inv = 1.0 / (es[0] + es[1] + es[2])
    return [e * inv for e in es]


def _merge_fwd(outs, lses, proj, name):
    S, E = outs[0].shape
    tm = min(256, S)
    gate_col = proj.shape[1] // E - 1

    def body(o0, o1, o2, l0, l1, l2, gt_ref, z_ref):
        for h in range(E // HEAD_DIM):
            hs = slice(h * HEAD_DIM, (h + 1) * HEAD_DIM)
            a = _group_weights((l0, l1, l2), h)
            merged = a[0] * o0[:, hs] + a[1] * o1[:, hs] + a[2] * o2[:, hs]
            gate = gt_ref[:, hs]
            z_ref[:, hs] = (merged * (gate * _sigmoid(gate))).astype(BF16)

    wide = pl.BlockSpec((tm, E), lambda i: (i, 0))
    thin = pl.BlockSpec((tm, HEAD_DIM), lambda i: (i, 0))
    return pl.pallas_call(
        body, name=name, grid=(S // tm,),
        in_specs=[wide, wide, wide, thin, thin, thin, pl.BlockSpec((tm, E), lambda i: (i, gate_col))],
        out_specs=wide,
        out_shape=jax.ShapeDtypeStruct((S, E), BF16),
        compiler_params=_params("parallel"))(*outs, *lses, proj)


def _merge_bwd(dx, w_out, outs, lses, proj, name):
    S, E = outs[0].shape
    D = dx.shape[1]
    tm = min(256, S)
    gate_col = proj.shape[1] // E - 1

    def body(dx_ref, w_ref, o0, o1, o2, l0, l1, l2, gt_ref, d0, d1, d2, t0, t1, t2, dg_ref):
        o_refs, d_refs, t_refs = (o0, o1, o2), (d0, d1, d2), (t0, t1, t2)
        lane = lax.broadcasted_iota(jnp.int32, (tm, HEAD_DIM), 1)
        tiles = [jnp.zeros((tm, HEAD_DIM), F32) for _ in range(3)]
        dxb = dx_ref[...].astype(BF16)
        for h in range(E // HEAD_DIM):
            hs = slice(h * HEAD_DIM, (h + 1) * HEAD_DIM)
            a = _group_weights((l0, l1, l2), h)
            merged = a[0] * o0[:, hs] + a[1] * o1[:, hs] + a[2] * o2[:, hs]
            gate = gt_ref[:, hs]
            sg = _sigmoid(gate)
            dzh = lax.dot_general(dxb, w_ref[hs, :], NT, preferred_element_type=F32)
            dmerged = dzh * (gate * sg)
            dg_ref[:, hs] = (dzh * merged * (sg * (1.0 + gate * (1.0 - sg)))).astype(BF16)
            tot = jnp.sum(dmerged * merged, axis=-1, keepdims=True)
            for g in range(3):
                d_refs[g][:, hs] = a[g] * dmerged
                tiles[g] = jnp.where(lane == h, a[g] * tot, tiles[g])
        for g in range(3):
            t_refs[g][...] = tiles[g]

    wide = pl.BlockSpec((tm, E), lambda i: (i, 0))
    thin = pl.BlockSpec((tm, HEAD_DIM), lambda i: (i, 0))
    res = pl.pallas_call(
        body, name=name, grid=(S // tm,),
        in_specs=[pl.BlockSpec((tm, D), lambda i: (i, 0)), pl.BlockSpec((E, D), lambda i: (0, 0)),
                  wide, wide, wide, thin, thin, thin, pl.BlockSpec((tm, E), lambda i: (i, gate_col))],
        out_specs=[wide, wide, wide, thin, thin, thin, pl.BlockSpec((tm, E), lambda i: (i, gate_col))],
        out_shape=[jax.ShapeDtypeStruct((S, E), F32)] * 3 + [jax.ShapeDtypeStruct((S, HEAD_DIM), F32)] * 3
        + [jax.ShapeDtypeStruct(proj.shape, BF16)],
        compiler_params=_params("parallel"))(dx, w_out.reshape(E, D), *outs, *lses, proj)
    return res[0:3], res[3:6], res[6]


def _kv_bwd(dks, dvs, cos2, sin2, name):
    S, E = dks[0].shape
    n = len(dks)
    tm = min(256, S)

    def body(*refs):
        dk_refs, dv_refs = refs[0:n], refs[n:2 * n]
        c_ref, s_ref, o_ref = refs[2 * n:]
        cosv, sinv = c_ref[...], s_ref[...]
        for h in range(E // HEAD_DIM):
            hs = slice(h * HEAD_DIM, (h + 1) * HEAD_DIM)
            dk = dk_refs[0][:, hs].astype(F32)
            dv = dv_refs[0][:, hs].astype(F32)
            for j in range(1, n):
                dk = dk + dk_refs[j][:, hs].astype(F32)
                dv = dv + dv_refs[j][:, hs].astype(F32)
            o_ref[:, hs] = _rope_bwd(dk, cosv, sinv).astype(BF16)
            o_ref[:, E + h * HEAD_DIM:E + (h + 1) * HEAD_DIM] = dv.astype(BF16)

    wide = pl.BlockSpec((tm, E), lambda i: (i, 0))
    thin = pl.BlockSpec((tm, HEAD_DIM), lambda i: (i, 0))
    return pl.pallas_call(
        body, name=name, grid=(S // tm,),
        in_specs=[wide] * (2 * n) + [thin, thin],
        out_specs=pl.BlockSpec((tm, 2 * E), lambda i: (i, 0)),
        out_shape=jax.ShapeDtypeStruct((S, 2 * E), BF16),
        compiler_params=_params("parallel"))(*dks, *dvs, cos2, sin2)


def _final_norm_loss(x, target, gain, name):
    S, D = x.shape
    tm = min(256, S)

    def body(x_ref, t_ref, g_ref, loss_ref, dx_ref, dg_ref):
        @pl.when(pl.program_id(0) == 0)
        def _():
            loss_ref[...] = jnp.zeros_like(loss_ref)
            dg_ref[...] = jnp.zeros_like(dg_ref)
        xf = x_ref[...]
        inv = lax.rsqrt(jnp.mean(xf * xf, axis=-1, keepdims=True) + RMS_EPS)
        xhat = xf * inv
        g = g_ref[...]
        err = xhat * g - t_ref[...]
        loss_ref[...] += 0.5 * jnp.sum(jnp.mean(err * err, axis=-1, keepdims=True), axis=0, keepdims=True)
        dy = err / D
        dg_ref[...] += jnp.sum(dy * xhat, axis=0, keepdims=True)
        dxh = dy * g
        dx_ref[...] = inv * (dxh - xhat * jnp.mean(dxh * xhat, axis=-1, keepdims=True))

    tile = pl.BlockSpec((tm, D), lambda i: (i, 0))
    vec = pl.BlockSpec((1, D), lambda i: (0, 0))
    return pl.pallas_call(
        body, name=name, grid=(S // tm,),
        in_specs=[tile, tile, vec],
        out_specs=[pl.BlockSpec((1, 1), lambda i: (0, 0)), tile, vec],
        out_shape=[jax.ShapeDtypeStruct((1, 1), F32), jax.ShapeDtypeStruct((S, D), F32),
                   jax.ShapeDtypeStruct((1, D), F32)],
        compiler_params=_params("arbitrary"))(x, target, gain)


def _adamw_math(g, w, m, v):
    m = ADAM_B1 * m + (1.0 - ADAM_B1) * g
    v = ADAM_B2 * v + (1.0 - ADAM_B2) * (g * g)
    m_hat = m / (1.0 - ADAM_B1 ** ADAM_STEP)
    v_hat = v / (1.0 - ADAM_B2 ** ADAM_STEP)
    delta = -ADAM_LR * (m_hat / (jnp.sqrt(v_hat) + ADAM_EPS) + ADAM_WD * w)
    return delta, m, v


def _adamw_rows(g, w, m, v, name):
    def body(g_ref, w_ref, m_ref, v_ref, d_ref, mo_ref, vo_ref):
        d_ref[...], mo_ref[...], vo_ref[...] = _adamw_math(g_ref[...], w_ref[...], m_ref[...], v_ref[...])

    whole = pl.BlockSpec(memory_space=pltpu.VMEM)
    return pl.pallas_call(
        body, name=name, in_specs=[whole] * 4, out_specs=[whole] * 3,
        out_shape=[jax.ShapeDtypeStruct(g.shape, F32)] * 3)(g, w, m, v)


def _adamw_blocks(own, others, w, m, v, layer, earlier, name):
    L, R, C = w.shape
    n = others.shape[0]
    tr = R
    while tr * C > 128 * 1024 and tr % 16 == 0:
        tr //= 2

    def body(o_ref, p_ref, w_ref, m_ref, v_ref, *rest):
        g_ref, d_ref, mo_ref, vo_ref = rest[-4:]
        g = o_ref[...].astype(F32)
        for j in range(n):
            g = g + p_ref[j].astype(F32)
        g_ref[...] = g
        d_ref[...], mo_ref[...], vo_ref[...] = _adamw_math(g, w_ref[...], m_ref[...], v_ref[...])

    tile = pl.BlockSpec((None, tr, C), lambda i: (layer, i, 0))
    kept = [] if earlier is None else list(earlier)
    return pl.pallas_call(
        body, name=name, grid=(R // tr,),
        in_specs=[pl.BlockSpec((tr, C), lambda i: (i, 0)), pl.BlockSpec((n, tr, C), lambda i: (0, i, 0)),
                  tile, tile, tile] + [ANY] * len(kept),
        out_specs=[tile] * 4,
        out_shape=[jax.ShapeDtypeStruct((L, R, C), F32)] * 4,
        input_output_aliases={5 + j: j for j in range(len(kept))},
        compiler_params=_params("parallel"))(own, others, w, m, v, *kept)


def _position():
    return lax.axis_index("x"), lax.axis_index("y"), lax.axis_index("c")


def _block_index(px, py, pc):
    return 4 * px + 2 * py + pc


def _all_gather(shards, name):
    n = len(shards)

    def body(*refs):
        ins, outs = refs[0:n], refs[n:2 * n]
        send_sems, recv_sems, local_sems = refs[2 * n:]
        x, y, c = _position()
        me, sibling = (x, y, c), (x, y, 1 - c)
        chips = [(1 - x, y), (x, 1 - y), (1 - x, 1 - y)]

        def copy(a, k, block, to, src=None):
            rows = outs[a].at[_block_index(*block)]
            return pltpu.make_async_remote_copy(
                src_ref=rows if src is None else src, dst_ref=rows,
                send_sem=send_sems.at[a, k], recv_sem=recv_sems.at[a, k], device_id=to, device_id_type=MESH)

        mine, first, passed = [], [], []
        for a in range(n):
            cp = pltpu.make_async_copy(ins[a], outs[a].at[_block_index(*me)], local_sems.at[a])
            cp.start()
            mine.append(cp)
            first.append(copy(a, 0, me, sibling, src=ins[a]))
            first += [copy(a, 1 + j, me, (*chip, c), src=ins[a]) for j, chip in enumerate(chips)]
        for cp in first:
            cp.start()
        for j, chip in enumerate(chips):
            for a in range(n):
                copy(a, 1 + j, (*chip, c), me).wait_recv()
                fwd = copy(a, 4 + j, (*chip, c), sibling)
                fwd.start()
                passed.append(fwd)
        for a in range(n):
            copy(a, 0, sibling, me).wait_recv()
            for j, chip in enumerate(chips):
                copy(a, 4 + j, (*chip, 1 - c), me).wait_recv()
        for cp in first + passed:
            cp.wait_send()
        for cp in mine:
            cp.wait()

    return pl.pallas_call(
        body, name=name,
        in_specs=[ANY] * n, out_specs=[ANY] * n,
        out_shape=[jax.ShapeDtypeStruct((N_DEV,) + s.shape, s.dtype) for s in shards],
        scratch_shapes=[pltpu.SemaphoreType.DMA((n, 7)), pltpu.SemaphoreType.DMA((n, 7)),
                        pltpu.SemaphoreType.DMA((n,))],
    )(*shards)


def _peers(x, y, c):
    return [((1 - x) if k & 4 else x, (1 - y) if k & 2 else y, (1 - c) if k & 1 else c) for k in range(1, N_DEV)]


HBM = pl.BlockSpec(memory_space=pltpu.HBM)
SEM = pl.BlockSpec(memory_space=pltpu.SEMAPHORE)
EFFECT = pltpu.SideEffectType.DATAFLOW_SIDE_EFFECTING


ALL_PEERS = (1, 2, 3, 4, 5, 6, 7)
SIBLING_AND_SAME_CORES = (1, 2, 4, 6)


def _push_copy(src_refs, land_refs, send_sems, recv_sems, a, i, relations, per_peer, by_sender, arriving):
    peer = _peers(*_position())[relations[i] - 1]
    me_idx, p_idx = _block_index(*_position()), _block_index(*peer)
    src = src_refs[a].at[p_idx] if per_peer else src_refs[a]
    if by_sender:
        slot = p_idx if arriving else me_idx
    else:
        slot = relations[i] - 1
    sem = a * len(relations) + i
    return pltpu.make_async_remote_copy(
        src_ref=src, dst_ref=land_refs[a].at[slot], send_sem=send_sems.at[sem], recv_sem=recv_sems.at[sem],
        device_id=peer, device_id_type=MESH)


def _push_start(srcs, lands, relations, per_peer, by_sender, after, name):
    n = len(srcs)

    def body(*refs):
        src_refs, land_refs = refs[0:n], refs[n:2 * n]
        send_sems, recv_sems = refs[2 * n + 1], refs[2 * n + 2]
        token = refs[-1]
        for a in range(n):
            for i in range(len(relations)):
                _push_copy(src_refs, land_refs, send_sems, recv_sems, a, i, relations, per_peer, by_sender, False).start()
        token[...] = jnp.zeros_like(token)

    args = [pltpu.with_memory_space_constraint(t, pltpu.HBM) for t in list(srcs) + list(lands)]
    res = pl.pallas_call(
        body, name=name,
        in_specs=[HBM] * (2 * n) + [ANY],
        out_specs=[SEM, SEM] + [HBM] * (2 * n) + [pl.BlockSpec(memory_space=pltpu.VMEM)],
        out_shape=[pltpu.SemaphoreType.DMA((n * len(relations),)), pltpu.SemaphoreType.DMA((n * len(relations),))]
        + [pltpu.HBM(t.shape, t.dtype) for t in args] + [jax.ShapeDtypeStruct((8, 128), F32)],
        input_output_aliases={i: 2 + i for i in range(2 * n)},
        compiler_params=pltpu.CompilerParams(has_side_effects=EFFECT))(*args, after)
    return res[0], res[1], res[2:2 + n], res[2 + n:2 + 2 * n], res[-1]


def _push_wait(started, relations, per_peer, by_sender, after, name):
    send_sems, recv_sems, srcs, lands, _ = started
    n = len(srcs)

    def body(*refs):
        src_refs, land_refs = refs[0:n], refs[n:2 * n]
        send_s, recv_s = refs[2 * n], refs[2 * n + 1]
        for a in range(n):
            for i in range(len(relations)):
                _push_copy(src_refs, land_refs, send_s, recv_s, a, i, relations, per_peer, by_sender, False).wait_send()
                _push_copy(src_refs, land_refs, send_s, recv_s, a, i, relations, per_peer, by_sender, True).wait_recv()

    res = pl.pallas_call(
        body, name=name,
        in_specs=[HBM] * (2 * n) + [SEM, SEM, ANY],
        out_specs=[HBM] * (2 * n),
        out_shape=[pltpu.HBM(t.shape, t.dtype) for t in list(srcs) + list(lands)],
        input_output_aliases={i: i for i in range(2 * n)},
        compiler_params=pltpu.CompilerParams(has_side_effects=EFFECT))(*srcs, *lands, send_sems, recv_sems, after)
    return res[0:n], res[n:2 * n]


def _pass_on(lands, name):
    n = len(lands)

    def body(*refs):
        outs = refs[n:2 * n]
        send_sems, recv_sems = refs[2 * n:]
        x, y, c = _position()
        chips = [(1 - x, y), (x, 1 - y), (1 - x, 1 - y)]
        copies = []
        for a in range(n):
            for j, chip in enumerate(chips):
                def copy(core):
                    rows = outs[a].at[_block_index(*chip, core)]
                    return pltpu.make_async_remote_copy(
                        src_ref=rows, dst_ref=rows, send_sem=send_sems.at[a, j], recv_sem=recv_sems.at[a, j],
                        device_id=(x, y, 1 - c), device_id_type=MESH)
                copy(c).start()
                copies.append((copy(c), copy(1 - c)))
        for sending, arriving in copies:
            sending.wait_send()
            arriving.wait_recv()

    return pl.pallas_call(
        body, name=name,
        in_specs=[ANY] * n, out_specs=[ANY] * n,
        out_shape=[jax.ShapeDtypeStruct(t.shape, t.dtype) for t in lands],
        input_output_aliases={a: a for a in range(n)},
        scratch_shapes=[pltpu.SemaphoreType.DMA((n, 3)), pltpu.SemaphoreType.DMA((n, 3))],
    )(*lands)


def _all_reduce_rows(v, name):
    R, D = v.shape

    def body(v_ref, o_ref, buf_ref, send_sems, recv_sems):
        x, y, c = _position()
        me_idx = _block_index(x, y, c)
        buf_ref[me_idx] = v_ref[...]
        copies = []
        for k in range(1, N_DEV):
            px = (1 - x) if k & 4 else x
            py = (1 - y) if k & 2 else y
            pc = (1 - c) if k & 1 else c
            rc = pltpu.make_async_remote_copy(
                src_ref=v_ref, dst_ref=buf_ref.at[me_idx],
                send_sem=send_sems.at[k - 1], recv_sem=recv_sems.at[k - 1],
                device_id=(px, py, pc), device_id_type=MESH)
            rc.start()
            copies.append((rc, pltpu.make_async_remote_copy(
                src_ref=v_ref, dst_ref=buf_ref.at[_block_index(px, py, pc)],
                send_sem=send_sems.at[k - 1], recv_sem=recv_sems.at[k - 1],
                device_id=(px, py, pc), device_id_type=MESH)))
        for rc, arrival in copies:
            rc.wait_send()
            arrival.wait_recv()
        acc = buf_ref[0]
        for j in range(1, N_DEV):
            acc = acc + buf_ref[j]
        o_ref[...] = acc

    return pl.pallas_call(
        body, name=name,
        in_specs=[pl.BlockSpec(memory_space=pltpu.VMEM)],
        out_specs=pl.BlockSpec(memory_space=pltpu.VMEM),
        out_shape=jax.ShapeDtypeStruct((R, D), F32),
        scratch_shapes=[pltpu.VMEM((N_DEV, R, D), F32),
                        pltpu.SemaphoreType.DMA((7,)), pltpu.SemaphoreType.DMA((7,))],
    )(v)


def _rope_tables(S):
    inv_freq = 1.0 / (ROPE_THETA ** (jnp.arange(0, HEAD_DIM, 2, dtype=F32) / HEAD_DIM))
    ang = jnp.arange(S, dtype=F32)[:, None] * inv_freq[None, :]
    cos, sin = jnp.cos(ang), jnp.sin(ang)
    return jnp.concatenate([cos, cos], axis=1), jnp.concatenate([-sin, sin], axis=1)


def _local_step(xs, target, vecs, n_a, n_b, get_weights, put_grads):
    S, D = xs.shape
    E = D
    cos2, sin2 = _rope_tables(S)
    ts = min(1024, S)

    def col_blocks(w):
        cb = w.shape[2]
        tn = min(cb, 1024)
        per = cb // tn
        return (None, D, tn), (lambda i, j: (j // per, 0, j % per)), N_DEV * per, tn

    def grad_in(hn, dproj, cb, name):
        return _matmul_tn(hn, dproj, (ts, D), lambda j, s: (s, 0), (ts, cb), lambda j, s: (s, j),
                          (N_DEV, D, cb), (None, D, cb), lambda j, s: (j, 0, 0), (D, cb), N_DEV, name)

    def grad_out(z, dx, name, col=0):
        rows = z.shape[1]
        ta = min(1024, rows)
        out = _matmul_tn(z, dx, (ts, ta), lambda a, s: (s, a), (ts, E), lambda a, s: (s, col),
                         (rows, E), (ta, E), lambda a, s: (a, 0), (ta, E), rows // ta, name)
        return out.reshape(N_DEV, rows // N_DEV, E)

    x = xs
    a_saved, b_saved = [], []
    for i in range(n_a):
        w = get_weights(f"a{i}", x)
        blk, idx, nblocks, tn = col_blocks(w["w_in"])
        proj, hn = _norm_matmul(x, vecs["norm_a"][i:i + 1], w["w_in"], blk, idx, nblocks, tn, f"a{i}_in")
        z = _a_mid_fwd(proj, w["w_grp"], vecs["scale_a"][i:i + 1], f"a{i}_mid")
        x_next = _matmul_res(z, w["w_out"], x, f"a{i}_out")
        a_saved.append((x, hn, proj, z, w))
        x = x_next
    x_kv = x
    w_kv = get_weights("kv", x)["w_kv"]
    tn = min(E, 1024)
    kvp, hn_kv = _norm_matmul(x, vecs["norm_kv"], w_kv, (D, tn), lambda i, j: (0, j), 2 * E // tn, tn, "kv_in")
    kr = _rope_k(kvp, cos2, sin2, "kv_rope")
    after = kr
    for i in range(n_b):
        w = get_weights(f"b{i}", after)
        blk, idx, nblocks, tn = col_blocks(w["w_in"])
        proj, hn = _norm_matmul(x, vecs["norm_b"][i:i + 1], w["w_in"], blk, idx, nblocks, tn, f"b{i}_in")
        outs, lses = [], []
        for g, dil in enumerate(DILATIONS):
            o, l = _attn_fwd(proj, kr, kvp, cos2, sin2, g, dil, f"b{i}_attn{g}")
            outs.append(o)
            lses.append(l)
        z = _merge_fwd(outs, lses, proj, f"b{i}_merge")
        x_next = _matmul_res(z, w["w_out"], x, f"b{i}_out")
        b_saved.append((x, hn, proj, z, outs, lses, w))
        x = x_next
        after = x
    loss, dx, dg_f = _final_norm_loss(x, target, vecs["norm_f"], "final")

    vec = {"norm_a": [None] * n_a, "scale_a": [None] * n_a, "norm_b": [None] * n_b, "norm_f": [dg_f]}
    dks, dvs = [], []
    for i in reversed(range(n_b)):
        x_in, hn, proj, z, outs, lses, w = b_saved[i]
        dw_out = grad_out(z, dx, f"b{i}_dwout")
        dos, dlts, dproj = _merge_bwd(dx, w["w_out"], outs, lses, proj, f"b{i}_dmerge")
        for g, dil in enumerate(DILATIONS):
            dproj, dk, dv = _attn_bwd(proj, kr, kvp, cos2, sin2, dos[g], lses[g], dlts[g], dproj, g, dil,
                                      f"b{i}_dattn{g}")
            dks.append(dk)
            dvs.append(dv)
        cb = w["w_in"].shape[2]
        tok = put_grads(f"b{i}", {"w_out": dw_out, "w_in": grad_in(hn, dproj, cb, f"b{i}_dwin")})
        dx, vec["norm_b"][i] = _matmul_nt_dnorm(dproj, w["w_in"], (None, D, cb), lambda t, j: (j, 0, 0), N_DEV, cb, x_in,
                                                vecs["norm_b"][i:i + 1] + tok[0:1, 0:1], dx, f"b{i}_dhn")

    dkv = _kv_bwd(dks, dvs, cos2, sin2, "kv_dsum")
    tok = put_grads("kv", {"w_k": grad_out(hn_kv, dkv, "kv_dwk", 0), "w_v": grad_out(hn_kv, dkv, "kv_dwv", 1)})
    tk = min(E, 1024)
    dx, dg_kv = _matmul_nt_dnorm(dkv, w_kv, (D, tk), lambda t, j: (0, j), 2 * E // tk, tk, x_kv,
                                 vecs["norm_kv"] + tok[0:1, 0:1], dx, "kv_dhn")
    vec["norm_kv"] = [dg_kv]

    for i in reversed(range(n_a)):
        x_in, hn, proj, z, w = a_saved[i]
        dw_out = grad_out(z, dx, f"a{i}_dwout")
        dz = _matmul_nt_rows(dx, w["w_out"], f"a{i}_dz")
        dproj, dwg, dsc = _a_mid_bwd(dz, proj, w["w_grp"], vecs["scale_a"][i:i + 1], f"a{i}_dmid")
        n_grp, gc, _ = dwg.shape
        dwg = dwg.reshape(n_grp, N_DEV, gc // N_DEV, gc).transpose(1, 0, 2, 3).astype(BF16)
        vec["scale_a"][i] = dsc
        cb = w["w_in"].shape[2]
        tok = put_grads(f"a{i}", {"w_out": dw_out, "w_grp": dwg, "w_in": grad_in(hn, dproj, cb, f"a{i}_dwin")})
        dx, vec["norm_a"][i] = _matmul_nt_dnorm(dproj, w["w_in"], (None, D, cb), lambda t, j: (j, 0, 0), N_DEV, cb, x_in,
                                                vecs["norm_a"][i:i + 1] + tok[0:1, 0:1], dx, f"a{i}_dhn")

    return loss, dx, {k: jnp.concatenate(v, axis=0) for k, v in vec.items()}


VECTORS = ("norm_a", "scale_a", "norm_kv", "norm_b", "norm_f")
SHARDED_VECTORS = ("norm_a", "scale_a")
GROUPS = {
    "a0": (("w_in", "w_in_a", 0), ("w_grp", "w_grp_a", 0), ("w_out", "w_out_a", 0)),
    "a1": (("w_in", "w_in_a", 1), ("w_grp", "w_grp_a", 1), ("w_out", "w_out_a", 1)),
    "kv": (("w_k", "w_k", None), ("w_v", "w_v", None)),
    "b0": (("w_in", "w_in_b", 0), ("w_out", "w_out_b", 0)),
    "b1": (("w_in", "w_in_b", 1), ("w_out", "w_out_b", 1)),
}
PREFETCHED = ("a1", "kv", "b0", "b1")


def kernel(x, norm_a, w_in_a, w_grp_a, scale_a, w_out_a, norm_kv, w_k, w_v, norm_b, w_in_b, w_out_b, norm_f, loss_target, m_norm_a, m_w_in_a, m_w_grp_a, m_scale_a, m_w_out_a, m_norm_kv, m_w_k, m_w_v, m_norm_b, m_w_in_b, m_w_out_b, m_norm_f, v_norm_a, v_w_in_a, v_w_grp_a, v_scale_a, v_w_out_a, v_norm_kv, v_w_k, v_w_v, v_norm_b, v_w_in_b, v_w_out_b, v_norm_f):
    w = dict(norm_a=norm_a, w_in_a=w_in_a, w_grp_a=w_grp_a, scale_a=scale_a, w_out_a=w_out_a, norm_kv=norm_kv,
             w_k=w_k, w_v=w_v, norm_b=norm_b, w_in_b=w_in_b, w_out_b=w_out_b, norm_f=norm_f)
    m = dict(norm_a=m_norm_a, w_in_a=m_w_in_a, w_grp_a=m_w_grp_a, scale_a=m_scale_a, w_out_a=m_w_out_a,
             norm_kv=m_norm_kv, w_k=m_w_k, w_v=m_w_v, norm_b=m_norm_b, w_in_b=m_w_in_b, w_out_b=m_w_out_b,
             norm_f=m_norm_f)
    v = dict(norm_a=v_norm_a, w_in_a=v_w_in_a, w_grp_a=v_w_grp_a, scale_a=v_scale_a, w_out_a=v_w_out_a,
             norm_kv=v_norm_kv, w_k=v_w_k, w_v=v_w_v, norm_b=v_norm_b, w_in_b=v_w_in_b, w_out_b=v_w_out_b,
             norm_f=v_norm_f)
    D = x.shape[2]
    me = _block_index(*_position())

    def shard(group):
        return [w[p].astype(BF16) if layer is None else w[p][layer].astype(BF16) for _, p, layer in GROUPS[group]]

    def as_weights(group, gathered):
        out = dict(zip([n for n, _, _ in GROUPS[group]], gathered))
        if "w_grp" in out:
            g = out["w_grp"]
            out["w_grp"] = g.transpose(1, 0, 2, 3).reshape(g.shape[1], g.shape[3], g.shape[3])
        if "w_k" in out:
            out = {"w_kv": jnp.concatenate([out["w_k"].reshape(D, D), out["w_v"].reshape(D, D)], axis=1)}
        return out

    first = _all_gather(shard("a0") + [w[k] for k in SHARDED_VECTORS], "gather_first")
    n_first = len(GROUPS["a0"])
    vecs = {k: g.transpose(1, 0, 2).reshape(w[k].shape[0], D) for k, g in zip(SHARDED_VECTORS, first[n_first:])}
    vecs.update(norm_kv=norm_kv[None, :], norm_b=norm_b, norm_f=norm_f[None, :])
    srcs, lands = [], []
    for group in PREFETCHED:
        for s in shard(group):
            srcs.append(s)
            lands.append(lax.dynamic_update_index_in_dim(lax.empty((N_DEV,) + s.shape, s.dtype), s[None], me, 0))
    inflight, at = {}, 0
    token = None
    for group in PREFETCHED:
        n = len(GROUPS[group])
        inflight[group] = _push_start(srcs[at:at + n], lands[at:at + n], SIBLING_AND_SAME_CORES, False, True,
                                      first[0] if token is None else token, f"gather_{group}_start")
        token = inflight[group][4]
        at += n
    vecs["norm_a"] = vecs["norm_a"] + token[0:1, 0:1]

    def get_weights(group, after):
        if group == "a0":
            return as_weights(group, first[0:n_first])
        half = _push_wait(inflight[group], SIBLING_AND_SAME_CORES, False, True, after, f"gather_{group}_wait")[1]
        return as_weights(group, _pass_on(half, f"gather_{group}_pass"))

    sent = {}

    def put_grads(group, grads):
        blocks = [grads[n] for n, _, _ in GROUPS[group]]
        lands = [lax.empty((N_DEV - 1,) + b.shape[1:], b.dtype) for b in blocks]
        sent[group] = _push_start(blocks, lands, ALL_PEERS, True, False, jnp.zeros((8, 128), F32),
                                  f"exchange_{group}_start")
        return sent[group][4]

    loss, dx, vec = _local_step(x[0], loss_target[0], vecs, w_in_a.shape[0], w_in_b.shape[0], get_weights, put_grads)
    rows = _all_reduce_rows(jnp.concatenate([vec[k] for k in VECTORS], axis=0), "reduce_vectors")

    out = {}
    after = dx
    for group in sent:
        blocks, arrived = _push_wait(sent[group], ALL_PEERS, True, False, after, f"exchange_{group}_wait")
        for (_, p, layer), blk, got in zip(GROUPS[group], blocks, arrived):
            cols = w[p].shape[-1]
            own = lax.dynamic_index_in_dim(blk, me, 0, keepdims=False).reshape(-1, cols)
            n_layers = 1 if layer is None else w[p].shape[0]
            stacked = lambda t: t.reshape(n_layers, -1, cols)
            res = _adamw_blocks(own, got.reshape(N_DEV - 1, -1, cols), stacked(w[p]), stacked(m[p]), stacked(v[p]),
                                0 if layer is None else layer, out.get(p), f"adamw_{group}_{p}")
            out[p] = res
            after = res[1]
    out = {p: [r.reshape(w[p].shape) for r in res] for p, res in out.items()}
    start = 0
    for k in VECTORS:
        n_rows = vec[k].shape[0]
        g = rows[start:start + n_rows]
        start += n_rows
        if k in SHARDED_VECTORS:
            g = lax.dynamic_slice_in_dim(g, me * (D // N_DEV), D // N_DEV, axis=1)
        res = _adamw_rows(g, w[k].reshape(g.shape), m[k].reshape(g.shape), v[k].reshape(g.shape), f"adamw_{k}")
        out[k] = [r.reshape(w[k].shape) for r in [g] + list(res)]

    names = ("norm_a", "w_in_a", "w_grp_a", "scale_a", "w_out_a", "norm_kv", "w_k", "w_v", "norm_b", "w_in_b",
             "w_out_b", "norm_f")
    total = lax.psum(loss[0, 0], ("x", "y", "c"))
    return (total, dx[None], *[out[k][0] for k in names], *[out[k][1] for k in names],
            *[out[k][2] for k in names], *[out[k][3] for k in names])
```

```python
import math

import jax
import jax.numpy as jnp
from jax import lax
from jax.experimental import pallas as pl
from jax.experimental.pallas import tpu as pltpu

F32 = jnp.float32
BF16 = jnp.bfloat16

N_DEV = 8
MESH = pl.DeviceIdType.MESH
RMS_EPS = 1e-6
HEAD_DIM = 128
HALF_HEAD = HEAD_DIM // 2
BAND = 128
DILATIONS = (1, 4, 16)
POOL_WINDOWS = (2, 4, 8, 16)
POOL_HALO = 16
ROPE_THETA = 10000.0
NEG_INF = -1e30
ATTN_SCALE = 1.0 / math.sqrt(HEAD_DIM)
LOG2_E = math.log2(math.e)
ADAM_LR, ADAM_B1, ADAM_B2, ADAM_EPS, ADAM_WD, ADAM_STEP = 0.001, 0.9, 0.999, 1e-08, 0.01, 10
VMEM_LIMIT_BYTES = 56 * 1024 * 1024
ANY = pl.BlockSpec(memory_space=pl.ANY)
NT = (((1,), (1,)), ((), ()))
TN = (((0,), (0,)), ((), ()))


def _params(*semantics):
    return pltpu.CompilerParams(dimension_semantics=semantics, vmem_limit_bytes=VMEM_LIMIT_BYTES)


def _sigmoid(t):
    return 1.0 / (1.0 + jnp.exp(-t))


def _rope(t, cos2, sin2):
    return t * cos2 + pltpu.roll(t, HALF_HEAD, 1) * sin2


def _rope_bwd(dt, cos2, sin2):
    return dt * cos2 + pltpu.roll(dt * sin2, HALF_HEAD, 1)


def _norm_matmul(x, gain, w, w_block, w_index, n_col_blocks, tn, name):
    S, D = x.shape
    tm = min(1024, S)

    def body(x_ref, g_ref, w_ref, o_ref, hn_ref, hs_ref):
        @pl.when(pl.program_id(1) == 0)
        def _():
            xf = x_ref[...]
            inv = lax.rsqrt(jnp.mean(xf * xf, axis=-1, keepdims=True) + RMS_EPS)
            hb = ((xf * inv) * g_ref[...]).astype(BF16)
            hs_ref[...] = hb
            hn_ref[...] = hb
        o_ref[...] = jnp.dot(hs_ref[...], w_ref[...], preferred_element_type=F32)

    return pl.pallas_call(
        body, name=name, grid=(S // tm, n_col_blocks),
        in_specs=[pl.BlockSpec((tm, D), lambda i, j: (i, 0)),
                  pl.BlockSpec((1, D), lambda i, j: (0, 0)),
                  pl.BlockSpec(w_block, w_index)],
        out_specs=[pl.BlockSpec((tm, tn), lambda i, j: (i, j)),
                   pl.BlockSpec((tm, D), lambda i, j: (i, 0))],
        out_shape=[jax.ShapeDtypeStruct((S, n_col_blocks * tn), F32), jax.ShapeDtypeStruct((S, D), BF16)],
        scratch_shapes=[pltpu.VMEM((tm, D), BF16)],
        compiler_params=_params("parallel", "arbitrary"))(x, gain, w)


def _matmul_res(a, w, res, name):
    S, K = a.shape
    nd, rb, N = w.shape
    tm = min(512, S)

    def body(a_ref, w_ref, r_ref, o_ref):
        acc = jnp.dot(a_ref[:, 0:rb], w_ref[0], preferred_element_type=F32)
        for k in range(1, nd):
            acc = acc + jnp.dot(a_ref[:, k * rb:(k + 1) * rb], w_ref[k], preferred_element_type=F32)
        o_ref[...] = r_ref[...] + acc

    return pl.pallas_call(
        body, name=name, grid=(S // tm,),
        in_specs=[pl.BlockSpec((tm, K), lambda i: (i, 0)),
                  pl.BlockSpec((nd, rb, N), lambda i: (0, 0, 0)),
                  pl.BlockSpec((tm, N), lambda i: (i, 0))],
        out_specs=pl.BlockSpec((tm, N), lambda i: (i, 0)),
        out_shape=jax.ShapeDtypeStruct((S, N), F32),
        compiler_params=_params("parallel"))(a, w, res)


def _matmul_nt_dnorm(dp, w, w_block, w_index, n_red, tc, x, gain, dres, name):
    S, D = x.shape
    tm = min(512, S)

    def body(d_ref, w_ref, x_ref, g_ref, r_ref, dx_ref, dg_ref, acc_ref):
        i, j = pl.program_id(0), pl.program_id(1)

        @pl.when((i == 0) & (j == 0))
        def _():
            dg_ref[...] = jnp.zeros_like(dg_ref)

        @pl.when(j == 0)
        def _():
            acc_ref[...] = jnp.zeros_like(acc_ref)
        acc_ref[...] += lax.dot_general(d_ref[...], w_ref[...], NT, preferred_element_type=F32)

        @pl.when(j == n_red - 1)
        def _():
            xf = x_ref[...]
            inv = lax.rsqrt(jnp.mean(xf * xf, axis=-1, keepdims=True) + RMS_EPS)
            xhat = xf * inv
            dh = acc_ref[...]
            dg_ref[...] += jnp.sum(dh * xhat, axis=0, keepdims=True)
            dxh = dh * g_ref[...]
            dx_ref[...] = r_ref[...] + inv * (dxh - xhat * jnp.mean(dxh * xhat, axis=-1, keepdims=True))

    tile = pl.BlockSpec((tm, D), lambda i, j: (i, 0))
    vec = pl.BlockSpec((1, D), lambda i, j: (0, 0))
    return pl.pallas_call(
        body, name=name, grid=(S // tm, n_red),
        in_specs=[pl.BlockSpec((tm, tc), lambda i, j: (i, j)), pl.BlockSpec(w_block, w_index), tile, vec, tile],
        out_specs=[tile, vec],
        out_shape=[jax.ShapeDtypeStruct((S, D), F32), jax.ShapeDtypeStruct((1, D), F32)],
        scratch_shapes=[pltpu.VMEM((tm, D), F32)],
        compiler_params=_params("arbitrary", "arbitrary"))(dp, w, x, gain, dres)


def _matmul_tn(a, b, a_block, a_index, b_block, b_index, out_shape, out_block, out_index, acc_shape, n_outer, name):
    S = a.shape[0]
    ts = a_block[0]
    n_tok = S // ts

    def body(a_ref, b_ref, o_ref, acc_ref):
        s = pl.program_id(1)

        @pl.when(s == 0)
        def _():
            acc_ref[...] = jnp.zeros_like(acc_ref)
        acc_ref[...] += lax.dot_general(a_ref[...].astype(BF16), b_ref[...].astype(BF16), TN,
                                        preferred_element_type=F32)

        @pl.when(s == n_tok - 1)
        def _():
            o_ref[...] = acc_ref[...].astype(o_ref.dtype)

    return pl.pallas_call(
        body, name=name, grid=(n_outer, n_tok),
        in_specs=[pl.BlockSpec(a_block, a_index), pl.BlockSpec(b_block, b_index)],
        out_specs=pl.BlockSpec(out_block, out_index),
        out_shape=jax.ShapeDtypeStruct(out_shape, BF16),
        scratch_shapes=[pltpu.VMEM(acc_shape, F32)],
        compiler_params=_params("parallel", "arbitrary"))(a, b)


def _pool(scr_ref, u, row0, tm, E):
    gc = E // len(POOL_WINDOWS)
    t1 = row0 + lax.broadcasted_iota(jnp.int32, (tm, 1), 0) + 1
    out = []
    for g, win in enumerate(POOL_WINDOWS):
        cs = slice(g * gc, (g + 1) * gc)
        acc = u[:, cs]
        for k in range(1, win):
            acc = acc + scr_ref[pl.ds(POOL_HALO - k, tm), cs]
        count = jnp.minimum(t1, win).astype(F32)
        out.append(acc / count - u[:, cs])
    return out


def _a_mid_fwd(proj, wg, scale, name):
    S, E2 = proj.shape
    E = E2 // 2
    gc = E // len(POOL_WINDOWS)
    tm = min(256, S)
    hb = tm // POOL_HALO

    def body(u_ref, uh_ref, gt_ref, wg_ref, sc_ref, z_ref, scr_ref):
        i = pl.program_id(0)
        scr_ref[0:POOL_HALO, :] = jnp.where(i > 0, uh_ref[...], 0.0)
        u = u_ref[...]
        scr_ref[POOL_HALO:POOL_HALO + tm, :] = u
        pooled = _pool(scr_ref, u, i * tm, tm, E)
        for g in range(len(POOL_WINDOWS)):
            cs = slice(g * gc, (g + 1) * gc)
            y = jnp.dot(pooled[g].astype(BF16), wg_ref[g], preferred_element_type=F32) * sc_ref[:, cs]
            gate = gt_ref[:, cs]
            z_ref[:, cs] = (y * (gate * _sigmoid(gate))).astype(BF16)

    return pl.pallas_call(
        body, name=name, grid=(S // tm,),
        in_specs=[pl.BlockSpec((tm, E), lambda i: (i, 0)),
                  pl.BlockSpec((POOL_HALO, E), lambda i: (jnp.maximum(i * hb - 1, 0), 0)),
                  pl.BlockSpec((tm, E), lambda i: (i, 1)),
                  pl.BlockSpec((len(POOL_WINDOWS), gc, gc), lambda i: (0, 0, 0)),
                  pl.BlockSpec((1, E), lambda i: (0, 0))],
        out_specs=pl.BlockSpec((tm, E), lambda i: (i, 0)),
        out_shape=jax.ShapeDtypeStruct((S, E), BF16),
        scratch_shapes=[pltpu.VMEM((POOL_HALO + tm, E), F32)],
        compiler_params=_params("parallel"))(proj, proj, proj, wg, scale)


def _a_mid_bwd(dx, w_out, proj, wg, scale, name):
    S, E2 = proj.shape
    E = E2 // 2
    D = dx.shape[1]
    n_grp = len(POOL_WINDOWS)
    gc = E // n_grp
    tm = min(256, S)
    hb = tm // POOL_HALO
    n_tiles = S // tm
    last_halo = S // POOL_HALO - 1

    def body(dx_ref, dxh_ref, wo_ref, u_ref, uh_ref, gt_ref, gth_ref, wg_ref, sc_ref, dp_ref, dwg_ref, dsc_ref,
             scr_ref, q_ref):
        i = pl.program_id(0)
        dxb = jnp.concatenate([dx_ref[...].astype(BF16), dxh_ref[...].astype(BF16)], axis=0)

        @pl.when(i == 0)
        def _():
            dwg_ref[...] = jnp.zeros_like(dwg_ref)
            dsc_ref[...] = jnp.zeros_like(dsc_ref)

        scr_ref[0:POOL_HALO, :] = jnp.where(i > 0, uh_ref[...], 0.0)
        u = u_ref[...]
        scr_ref[POOL_HALO:POOL_HALO + tm, :] = u
        pooled = _pool(scr_ref, u, i * tm, tm, E)
        t1 = i * tm + lax.broadcasted_iota(jnp.int32, (tm, 1), 0) + 1
        t1h = (i + 1) * tm + lax.broadcasted_iota(jnp.int32, (POOL_HALO, 1), 0) + 1
        not_last = i < n_tiles - 1
        for g, win in enumerate(POOL_WINDOWS):
            cs = slice(g * gc, (g + 1) * gc)
            w = wg_ref[g]
            sc = sc_ref[:, cs]
            pb = pooled[g].astype(BF16)
            ypre = jnp.dot(pb, w, preferred_element_type=F32)
            gate = gt_ref[:, cs]
            sg = _sigmoid(gate)
            silu = gate * sg
            dz_all = lax.dot_general(dxb, wo_ref[cs, :], NT, preferred_element_type=F32)
            dzg = dz_all[0:tm]
            dy = dzg * silu
            dp_ref[:, E + g * gc:E + (g + 1) * gc] = (dzg * (ypre * sc) * (sg * (1.0 + gate * (1.0 - sg)))).astype(BF16)
            dsc_ref[:, cs] += jnp.sum(dy * ypre, axis=0, keepdims=True)
            dyp = (dy * sc).astype(BF16)
            dwg_ref[g] += lax.dot_general(pb, dyp, TN, preferred_element_type=F32)
            dpool = lax.dot_general(dyp, w, NT, preferred_element_type=F32)
            gate_h = gth_ref[:, cs]
            dyp_h = (dz_all[tm:tm + POOL_HALO] * (gate_h * _sigmoid(gate_h)) * sc).astype(BF16)
            dpool_h = lax.dot_general(dyp_h, w, NT, preferred_element_type=F32)
            q_ref[0:tm, cs] = dpool / jnp.minimum(t1, win).astype(F32)
            q_ref[tm:tm + POOL_HALO, cs] = jnp.where(not_last, dpool_h / jnp.minimum(t1h, win).astype(F32), 0.0)
            acc = q_ref[0:tm, cs] - dpool
            for k in range(1, win):
                acc = acc + q_ref[pl.ds(k, tm), cs]
            dp_ref[:, cs] = acc.astype(BF16)

    return pl.pallas_call(
        body, name=name, grid=(n_tiles,),
        in_specs=[pl.BlockSpec((tm, D), lambda i: (i, 0)),
                  pl.BlockSpec((POOL_HALO, D), lambda i: (jnp.minimum((i + 1) * hb, last_halo), 0)),
                  pl.BlockSpec((E, D), lambda i: (0, 0)),
                  pl.BlockSpec((tm, E), lambda i: (i, 0)),
                  pl.BlockSpec((POOL_HALO, E), lambda i: (jnp.maximum(i * hb - 1, 0), 0)),
                  pl.BlockSpec((tm, E), lambda i: (i, 1)),
                  pl.BlockSpec((POOL_HALO, E), lambda i: (jnp.minimum((i + 1) * hb, last_halo), 1)),
                  pl.BlockSpec((n_grp, gc, gc), lambda i: (0, 0, 0)),
                  pl.BlockSpec((1, E), lambda i: (0, 0))],
        out_specs=[pl.BlockSpec((tm, E2), lambda i: (i, 0)),
                   pl.BlockSpec((n_grp, gc, gc), lambda i: (0, 0, 0)),
                   pl.BlockSpec((1, E), lambda i: (0, 0))],
        out_shape=[jax.ShapeDtypeStruct((S, E2), BF16),
                   jax.ShapeDtypeStruct((n_grp, gc, gc), F32),
                   jax.ShapeDtypeStruct((1, E), F32)],
        scratch_shapes=[pltpu.VMEM((POOL_HALO + tm, E), F32), pltpu.VMEM((tm + POOL_HALO, E), F32)],
        compiler_params=_params("arbitrary"))(dx, dx, w_out.reshape(E, D), proj, proj, proj, proj, wg, scale)


def _rope_k(kvp, cos2, sin2, name):
    S, E2 = kvp.shape
    E = E2 // 2
    tm = min(256, S)

    def body(k_ref, c_ref, s_ref, ko_ref):
        cosv, sinv = c_ref[...], s_ref[...]
        for h in range(E // HEAD_DIM):
            hs = slice(h * HEAD_DIM, (h + 1) * HEAD_DIM)
            ko_ref[:, hs] = _rope(k_ref[:, hs], cosv, sinv)

    return pl.pallas_call(
        body, name=name, grid=(S // tm,),
        in_specs=[pl.BlockSpec((tm, E), lambda i: (i, 0)),
                  pl.BlockSpec((tm, HEAD_DIM), lambda i: (i, 0)), pl.BlockSpec((tm, HEAD_DIM), lambda i: (i, 0))],
        out_specs=pl.BlockSpec((tm, E), lambda i: (i, 0)),
        out_shape=jax.ShapeDtypeStruct((S, E), F32),
        compiler_params=_params("parallel"))(kvp, cos2, sin2)


CLASSES = 16


def _to_class_order(a):
    S, W = a.shape
    return a.reshape(S // CLASSES, CLASSES, W).transpose(1, 0, 2).reshape(S, W)


def _from_class_order(a):
    S, W = a.shape
    return a.reshape(CLASSES, S // CLASSES, W).transpose(1, 0, 2).reshape(S, W)


def _runs(r, b, dil, M, back=0):
    nj = CLASSES // dil
    c = BAND // nj
    return [((r + dil * j) * M + (b - back) * c, (1 + back) * c) for j in range(nj)]


def _load(ref, runs):
    parts = [ref[pl.ds(start, n), :] for start, n in runs]
    return parts[0] if len(parts) == 1 else jnp.concatenate(parts, axis=0)


def _store(ref, runs, val, add=False):
    at = 0
    for start, n in runs:
        if add:
            ref[pl.ds(start, n), :] += val[at:at + n]
        else:
            ref[pl.ds(start, n), :] = val[at:at + n]
        at += n


def _keys(ref, r, b, dil, M):
    if b > 0:
        return _load(ref, _runs(r, b, dil, M, back=1)).astype(BF16)
    parts = []
    for start, n in _runs(r, 0, dil, M):
        parts += [jnp.zeros((n, HEAD_DIM), BF16), ref[pl.ds(start, n), :].astype(BF16)]
    return jnp.concatenate(parts, axis=0)


def _band_mask(dil, first):
    nj = CLASSES // dil
    c = BAND // nj
    row = lax.broadcasted_iota(jnp.int32, (BAND, 2 * BAND), 0)
    col = lax.broadcasted_iota(jnp.int32, (BAND, 2 * BAND), 1)
    q_place = (row % c) * nj + row // c
    k_place = (col % (2 * c) - c) * nj + col // (2 * c)
    mask = (q_place >= k_place) & (q_place <= k_place + BAND)
    return mask & (col % (2 * c) >= c) if first else mask


def _lane_column(tile, lane, h):
    return jnp.sum(jnp.where(lane == h, tile, 0.0), axis=-1, keepdims=True)


def _attn_fwd(proj, kr, kvp, cos2, sin2, group, dil, name):
    S, PW = proj.shape
    E = kr.shape[1]
    H = E // HEAD_DIM
    M = S // CLASSES
    nb = S // (BAND * dil)

    def body(q_ref, k_ref, v_ref, c_ref, s_ref, o_ref, l_ref):
        h = pl.program_id(0)
        lane = lax.broadcasted_iota(jnp.int32, (BAND, HEAD_DIM), 1)
        edge, inner = _band_mask(dil, True), _band_mask(dil, False)

        @pl.when(h == 0)
        def _():
            l_ref[...] = jnp.zeros_like(l_ref)

        def scores(r, b):
            runs = _runs(r, b, dil, M)
            qr = _rope(_load(q_ref, runs), _load(c_ref, runs), _load(s_ref, runs)).astype(BF16)
            return lax.dot_general(qr, _keys(k_ref, r, b, dil, M), NT, preferred_element_type=F32)

        units = [(r, b) for r in range(dil) for b in range(nb)]
        ahead = scores(*units[0])
        for i, (r, b) in enumerate(units):
            s = ahead
            if i + 1 < len(units):
                ahead = scores(*units[i + 1])
            runs = _runs(r, b, dil, M)
            s = jnp.where(edge if b == 0 else inner, s, NEG_INF)
            m = jnp.max(s, axis=-1, keepdims=True)
            p = jnp.exp2((s - m) * (ATTN_SCALE * LOG2_E))
            l = jnp.sum(p, axis=-1, keepdims=True)
            _store(o_ref, runs, jnp.dot(p.astype(BF16), _keys(v_ref, r, b, dil, M), preferred_element_type=F32) / l)
            _store(l_ref, runs, jnp.where(lane == h, m * ATTN_SCALE + jnp.log(l), _load(l_ref, runs)))

    col = (S, HEAD_DIM)
    whole = pl.BlockSpec(col, lambda h: (0, 0))
    return pl.pallas_call(
        body, name=name, grid=(H,),
        in_specs=[pl.BlockSpec(col, lambda h: (0, group * H + h)), pl.BlockSpec(col, lambda h: (0, h)),
                  pl.BlockSpec(col, lambda h: (0, H + h)), whole, whole],
        out_specs=[pl.BlockSpec(col, lambda h: (0, h)), whole],
        out_shape=[jax.ShapeDtypeStruct((S, E), F32), jax.ShapeDtypeStruct((S, HEAD_DIM), F32)],
        compiler_params=_params("arbitrary"))(proj, kr, kvp, cos2, sin2)


def _attn_bwd(proj, kr, kvp, cos2, sin2, do, lse, dlt, dproj, group, dil, name):
    S, PW = proj.shape
    E = kr.shape[1]
    H = E // HEAD_DIM
    M = S // CLASSES
    nb = S // (BAND * dil)

    def body(q_ref, k_ref, v_ref, c_ref, s_ref, do_ref, l_ref, dl_ref, dproj_ref,
             dq_ref, dk_ref, dv_ref, dq_scr, dk_scr, dv_scr):
        h = pl.program_id(0)
        lane = lax.broadcasted_iota(jnp.int32, (BAND, HEAD_DIM), 1)
        edge, inner = _band_mask(dil, True), _band_mask(dil, False)
        dk_scr[...] = jnp.zeros_like(dk_scr)
        dv_scr[...] = jnp.zeros_like(dv_scr)

        def scores(r, b):
            runs = _runs(r, b, dil, M)
            qr = _rope(_load(q_ref, runs), _load(c_ref, runs), _load(s_ref, runs)).astype(BF16)
            return qr, lax.dot_general(qr, _keys(k_ref, r, b, dil, M), NT, preferred_element_type=F32)

        units = [(r, b) for r in range(dil) for b in range(nb)]
        ahead = scores(*units[0])
        for i, (r, b) in enumerate(units):
            qr, s = ahead
            if i + 1 < len(units):
                ahead = scores(*units[i + 1])
            runs = _runs(r, b, dil, M)
            dob = _load(do_ref, runs).astype(BF16)
            dpr = lax.dot_general(dob, _keys(v_ref, r, b, dil, M), NT, preferred_element_type=F32)
            s = jnp.where(edge if b == 0 else inner, s, NEG_INF)
            p = jnp.exp2(s * (ATTN_SCALE * LOG2_E) - _lane_column(_load(l_ref, runs), lane, h) * LOG2_E)
            ds = (p * (dpr - _lane_column(_load(dl_ref, runs), lane, h)) * ATTN_SCALE).astype(BF16)
            dq = jnp.dot(ds, _keys(k_ref, r, b, dil, M), preferred_element_type=F32)
            _store(dq_scr, runs, _rope_bwd(dq, _load(c_ref, runs), _load(s_ref, runs)))
            dkc = lax.dot_general(ds, qr, TN, preferred_element_type=F32)
            dvc = lax.dot_general(p.astype(BF16), dob, TN, preferred_element_type=F32)
            if b > 0:
                both = _runs(r, b, dil, M, back=1)
                _store(dk_scr, both, dkc, add=True)
                _store(dv_scr, both, dvc, add=True)
            else:
                n = runs[0][1]
                own = jnp.concatenate([dkc[(2 * j + 1) * n:(2 * j + 2) * n] for j in range(len(runs))], axis=0)
                _store(dk_scr, runs, own, add=True)
                own = jnp.concatenate([dvc[(2 * j + 1) * n:(2 * j + 2) * n] for j in range(len(runs))], axis=0)
                _store(dv_scr, runs, own, add=True)
        dq_ref[...] = dq_scr[...].astype(BF16)
        dk_ref[...] = dk_scr[...].astype(BF16)
        dv_ref[...] = dv_scr[...].astype(BF16)

    col = (S, HEAD_DIM)
    whole = pl.BlockSpec(col, lambda h: (0, 0))
    head = pl.BlockSpec(col, lambda h: (0, h))
    return pl.pallas_call(
        body, name=name, grid=(H,),
        in_specs=[pl.BlockSpec(col, lambda h: (0, group * H + h)), head, pl.BlockSpec(col, lambda h: (0, H + h)),
                  whole, whole, head, whole, whole, ANY],
        out_specs=[pl.BlockSpec(col, lambda h: (0, group * H + h)), head, head],
        out_shape=[jax.ShapeDtypeStruct(dproj.shape, BF16), jax.ShapeDtypeStruct((S, E), BF16),
                   jax.ShapeDtypeStruct((S, E), BF16)],
        scratch_shapes=[pltpu.VMEM(col, F32)] * 3,
        input_output_aliases={8: 0},
        compiler_params=_params("parallel"))(proj, kr, kvp, cos2, sin2, do, lse, dlt, dproj)


def _group_weights(l_refs, h):
    ls = [r[:, h:h + 1] for r in l_refs]
    mx = jnp.maximum(jnp.maximum(ls[0], ls[1]), ls[2])
    es = [jnp.exp(l - mx) for l in ls]
    inv = 1.0 / (es[0] + es[1] + es[2])
    return [e * inv for e in es]


def _merge_out(outs, lses, proj, w_out, res, name):
    S, E = outs[0].shape
    D = res.shape[1]
    tm = min(256, S)
    gate_col = proj.shape[1] // E - 1

    def body(o0, o1, o2, l0, l1, l2, gt_ref, w_ref, r_ref, x_ref, z_ref):
        for h in range(E // HEAD_DIM):
            hs = slice(h * HEAD_DIM, (h + 1) * HEAD_DIM)
            a = _group_weights((l0, l1, l2), h)
            merged = a[0] * o0[:, hs] + a[1] * o1[:, hs] + a[2] * o2[:, hs]
            gate = gt_ref[:, hs]
            z_ref[:, hs] = (merged * (gate * _sigmoid(gate))).astype(BF16)
        x_ref[...] = r_ref[...] + jnp.dot(z_ref[...], w_ref[...], preferred_element_type=F32)

    wide = pl.BlockSpec((tm, E), lambda i: (i, 0))
    thin = pl.BlockSpec((tm, HEAD_DIM), lambda i: (i, 0))
    return pl.pallas_call(
        body, name=name, grid=(S // tm,),
        in_specs=[wide, wide, wide, thin, thin, thin, pl.BlockSpec((tm, E), lambda i: (i, gate_col)),
                  pl.BlockSpec((E, D), lambda i: (0, 0)), pl.BlockSpec((tm, D), lambda i: (i, 0))],
        out_specs=[pl.BlockSpec((tm, D), lambda i: (i, 0)), wide],
        out_shape=[jax.ShapeDtypeStruct((S, D), F32), jax.ShapeDtypeStruct((S, E), BF16)],
        compiler_params=_params("parallel"))(*outs, *lses, proj, w_out.reshape(E, D), res)


def _merge_bwd(dx, w_out, outs, lses, proj, name):
    S, E = outs[0].shape
    D = dx.shape[1]
    tm = min(256, S)
    gate_col = proj.shape[1] // E - 1

    def body(dx_ref, w_ref, o0, o1, o2, l0, l1, l2, gt_ref, d0, d1, d2, t0, t1, t2, dg_ref):
        o_refs, d_refs, t_refs = (o0, o1, o2), (d0, d1, d2), (t0, t1, t2)
        lane = lax.broadcasted_iota(jnp.int32, (tm, HEAD_DIM), 1)
        tiles = [jnp.zeros((tm, HEAD_DIM), F32) for _ in range(3)]
        dxb = dx_ref[...].astype(BF16)
        for h in range(E // HEAD_DIM):
            hs = slice(h * HEAD_DIM, (h + 1) * HEAD_DIM)
            a = _group_weights((l0, l1, l2), h)
            merged = a[0] * o0[:, hs] + a[1] * o1[:, hs] + a[2] * o2[:, hs]
            gate = gt_ref[:, hs]
            sg = _sigmoid(gate)
            dzh = lax.dot_general(dxb, w_ref[hs, :], NT, preferred_element_type=F32)
            dmerged = dzh * (gate * sg)
            dg_ref[:, hs] = (dzh * merged * (sg * (1.0 + gate * (1.0 - sg)))).astype(BF16)
            tot = jnp.sum(dmerged * merged, axis=-1, keepdims=True)
            for g in range(3):
                d_refs[g][:, hs] = a[g] * dmerged
                tiles[g] = jnp.where(lane == h, a[g] * tot, tiles[g])
        for g in range(3):
            t_refs[g][...] = tiles[g]

    wide = pl.BlockSpec((tm, E), lambda i: (i, 0))
    thin = pl.BlockSpec((tm, HEAD_DIM), lambda i: (i, 0))
    res = pl.pallas_call(
        body, name=name, grid=(S // tm,),
        in_specs=[pl.BlockSpec((tm, D), lambda i: (i, 0)), pl.BlockSpec((E, D), lambda i: (0, 0)),
                  wide, wide, wide, thin, thin, thin, pl.BlockSpec((tm, E), lambda i: (i, gate_col))],
        out_specs=[wide, wide, wide, thin, thin, thin, pl.BlockSpec((tm, E), lambda i: (i, gate_col))],
        out_shape=[jax.ShapeDtypeStruct((S, E), F32)] * 3 + [jax.ShapeDtypeStruct((S, HEAD_DIM), F32)] * 3
        + [jax.ShapeDtypeStruct(proj.shape, BF16)],
        compiler_params=_params("parallel"))(dx, w_out.reshape(E, D), *outs, *lses, proj)
    return res[0:3], res[3:6], res[6]


def _kv_bwd(dks, dvs, cos2, sin2, name):
    S, E = dks[0].shape
    n = len(dks)
    tm = min(256, S)

    def body(*refs):
        dk_refs, dv_refs = refs[0:n], refs[n:2 * n]
        c_ref, s_ref, o_ref = refs[2 * n:]
        cosv, sinv = c_ref[...], s_ref[...]
        for h in range(E // HEAD_DIM):
            hs = slice(h * HEAD_DIM, (h + 1) * HEAD_DIM)
            dk = dk_refs[0][:, hs].astype(F32)
            dv = dv_refs[0][:, hs].astype(F32)
            for j in range(1, n):
                dk = dk + dk_refs[j][:, hs].astype(F32)
                dv = dv + dv_refs[j][:, hs].astype(F32)
            o_ref[:, hs] = _rope_bwd(dk, cosv, sinv).astype(BF16)
            o_ref[:, E + h * HEAD_DIM:E + (h + 1) * HEAD_DIM] = dv.astype(BF16)

    wide = pl.BlockSpec((tm, E), lambda i: (i, 0))
    thin = pl.BlockSpec((tm, HEAD_DIM), lambda i: (i, 0))
    return pl.pallas_call(
        body, name=name, grid=(S // tm,),
        in_specs=[wide] * (2 * n) + [thin, thin],
        out_specs=pl.BlockSpec((tm, 2 * E), lambda i: (i, 0)),
        out_shape=jax.ShapeDtypeStruct((S, 2 * E), BF16),
        compiler_params=_params("parallel"))(*dks, *dvs, cos2, sin2)


def _final_norm_loss(x, target, gain, name):
    S, D = x.shape
    tm = min(256, S)

    def body(x_ref, t_ref, g_ref, loss_ref, dx_ref, dg_ref):
        @pl.when(pl.program_id(0) == 0)
        def _():
            loss_ref[...] = jnp.zeros_like(loss_ref)
            dg_ref[...] = jnp.zeros_like(dg_ref)
        xf = x_ref[...]
        inv = lax.rsqrt(jnp.mean(xf * xf, axis=-1, keepdims=True) + RMS_EPS)
        xhat = xf * inv
        g = g_ref[...]
        err = xhat * g - t_ref[...]
        loss_ref[...] += 0.5 * jnp.sum(jnp.mean(err * err, axis=-1, keepdims=True), axis=0, keepdims=True)
        dy = err / D
        dg_ref[...] += jnp.sum(dy * xhat, axis=0, keepdims=True)
        dxh = dy * g
        dx_ref[...] = inv * (dxh - xhat * jnp.mean(dxh * xhat, axis=-1, keepdims=True))

    tile = pl.BlockSpec((tm, D), lambda i: (i, 0))
    vec = pl.BlockSpec((1, D), lambda i: (0, 0))
    return pl.pallas_call(
        body, name=name, grid=(S // tm,),
        in_specs=[tile, tile, vec],
        out_specs=[pl.BlockSpec((1, 1), lambda i: (0, 0)), tile, vec],
        out_shape=[jax.ShapeDtypeStruct((1, 1), F32), jax.ShapeDtypeStruct((S, D), F32),
                   jax.ShapeDtypeStruct((1, D), F32)],
        compiler_params=_params("arbitrary"))(x, target, gain)


def _adamw_math(g, w, m, v):
    m = ADAM_B1 * m + (1.0 - ADAM_B1) * g
    v = ADAM_B2 * v + (1.0 - ADAM_B2) * (g * g)
    m_hat = m / (1.0 - ADAM_B1 ** ADAM_STEP)
    v_hat = v / (1.0 - ADAM_B2 ** ADAM_STEP)
    delta = -ADAM_LR * (m_hat / (jnp.sqrt(v_hat) + ADAM_EPS) + ADAM_WD * w)
    return delta, m, v


def _adamw_rows(g, w, m, v, name):
    def body(g_ref, w_ref, m_ref, v_ref, d_ref, mo_ref, vo_ref):
        d_ref[...], mo_ref[...], vo_ref[...] = _adamw_math(g_ref[...], w_ref[...], m_ref[...], v_ref[...])

    whole = pl.BlockSpec(memory_space=pltpu.VMEM)
    return pl.pallas_call(
        body, name=name, in_specs=[whole] * 4, out_specs=[whole] * 3,
        out_shape=[jax.ShapeDtypeStruct(g.shape, F32)] * 3)(g, w, m, v)


def _adamw_blocks(own, others, w, m, v, layer, earlier, name):
    L, R, C = w.shape
    n = others.shape[0]
    tr = R
    while tr * C > 128 * 1024 and tr % 16 == 0:
        tr //= 2

    def body(o_ref, p_ref, w_ref, m_ref, v_ref, *rest):
        g_ref, d_ref, mo_ref, vo_ref = rest[-4:]
        g = o_ref[...].astype(F32)
        for j in range(n):
            g = g + p_ref[j].astype(F32)
        g_ref[...] = g
        d_ref[...], mo_ref[...], vo_ref[...] = _adamw_math(g, w_ref[...], m_ref[...], v_ref[...])

    tile = pl.BlockSpec((None, tr, C), lambda i: (layer, i, 0))
    kept = [] if earlier is None else list(earlier)
    return pl.pallas_call(
        body, name=name, grid=(R // tr,),
        in_specs=[pl.BlockSpec((tr, C), lambda i: (i, 0)), pl.BlockSpec((n, tr, C), lambda i: (0, i, 0)),
                  tile, tile, tile] + [ANY] * len(kept),
        out_specs=[tile] * 4,
        out_shape=[jax.ShapeDtypeStruct((L, R, C), F32)] * 4,
        input_output_aliases={5 + j: j for j in range(len(kept))},
        compiler_params=_params("parallel"))(own, others, w, m, v, *kept)


def _position():
    return lax.axis_index("x"), lax.axis_index("y"), lax.axis_index("c")


def _block_index(px, py, pc):
    return 4 * px + 2 * py + pc


def _all_gather(shards, name):
    n = len(shards)

    def body(*refs):
        ins, outs = refs[0:n], refs[n:2 * n]
        send_sems, recv_sems, local_sems = refs[2 * n:]
        x, y, c = _position()
        me, sibling = (x, y, c), (x, y, 1 - c)
        chips = [(1 - x, y), (x, 1 - y), (1 - x, 1 - y)]

        def copy(a, k, block, to, src=None):
            rows = outs[a].at[_block_index(*block)]
            return pltpu.make_async_remote_copy(
                src_ref=rows if src is None else src, dst_ref=rows,
                send_sem=send_sems.at[a, k], recv_sem=recv_sems.at[a, k], device_id=to, device_id_type=MESH)

        mine, first, passed = [], [], []
        for a in range(n):
            cp = pltpu.make_async_copy(ins[a], outs[a].at[_block_index(*me)], local_sems.at[a])
            cp.start()
            mine.append(cp)
            first.append(copy(a, 0, me, sibling, src=ins[a]))
            first += [copy(a, 1 + j, me, (*chip, c), src=ins[a]) for j, chip in enumerate(chips)]
        for cp in first:
            cp.start()
        for j, chip in enumerate(chips):
            for a in range(n):
                copy(a, 1 + j, (*chip, c), me).wait_recv()
                fwd = copy(a, 4 + j, (*chip, c), sibling)
                fwd.start()
                passed.append(fwd)
        for a in range(n):
            copy(a, 0, sibling, me).wait_recv()
            for j, chip in enumerate(chips):
                copy(a, 4 + j, (*chip, 1 - c), me).wait_recv()
        for cp in first + passed:
            cp.wait_send()
        for cp in mine:
            cp.wait()

    return pl.pallas_call(
        body, name=name,
        in_specs=[ANY] * n, out_specs=[ANY] * n,
        out_shape=[jax.ShapeDtypeStruct((N_DEV,) + s.shape, s.dtype) for s in shards],
        scratch_shapes=[pltpu.SemaphoreType.DMA((n, 7)), pltpu.SemaphoreType.DMA((n, 7)),
                        pltpu.SemaphoreType.DMA((n,))],
    )(*shards)


def _peers(x, y, c):
    return [((1 - x) if k & 4 else x, (1 - y) if k & 2 else y, (1 - c) if k & 1 else c) for k in range(1, N_DEV)]


HBM = pl.BlockSpec(memory_space=pltpu.HBM)
SEM = pl.BlockSpec(memory_space=pltpu.SEMAPHORE)
EFFECT = pltpu.SideEffectType.DATAFLOW_SIDE_EFFECTING


ALL_PEERS = (1, 2, 3, 4, 5, 6, 7)
SIBLING_AND_SAME_CORES = (1, 2, 4, 6)


def _push_copy(src_refs, land_refs, send_sems, recv_sems, a, i, relations, per_peer, by_sender, arriving):
    peer = _peers(*_position())[relations[i] - 1]
    me_idx, p_idx = _block_index(*_position()), _block_index(*peer)
    src = src_refs[a].at[p_idx] if per_peer else src_refs[a]
    if by_sender:
        slot = p_idx if arriving else me_idx
    else:
        slot = relations[i] - 1
    sem = a * len(relations) + i
    return pltpu.make_async_remote_copy(
        src_ref=src, dst_ref=land_refs[a].at[slot], send_sem=send_sems.at[sem], recv_sem=recv_sems.at[sem],
        device_id=peer, device_id_type=MESH)


def _push_start(srcs, lands, relations, per_peer, by_sender, after, name):
    n = len(srcs)

    def body(*refs):
        src_refs, land_refs = refs[0:n], refs[n:2 * n]
        send_sems, recv_sems = refs[2 * n + 1], refs[2 * n + 2]
        token = refs[-1]
        for a in range(n):
            for i in range(len(relations)):
                _push_copy(src_refs, land_refs, send_sems, recv_sems, a, i, relations, per_peer, by_sender, False).start()
        token[...] = jnp.zeros_like(token)

    args = [pltpu.with_memory_space_constraint(t, pltpu.HBM) for t in list(srcs) + list(lands)]
    res = pl.pallas_call(
        body, name=name,
        in_specs=[HBM] * (2 * n) + [ANY],
        out_specs=[SEM, SEM] + [HBM] * (2 * n) + [pl.BlockSpec(memory_space=pltpu.VMEM)],
        out_shape=[pltpu.SemaphoreType.DMA((n * len(relations),)), pltpu.SemaphoreType.DMA((n * len(relations),))]
        + [pltpu.HBM(t.shape, t.dtype) for t in args] + [jax.ShapeDtypeStruct((8, 128), F32)],
        input_output_aliases={i: 2 + i for i in range(2 * n)},
        compiler_params=pltpu.CompilerParams(has_side_effects=EFFECT))(*args, after)
    return res[0], res[1], res[2:2 + n], res[2 + n:2 + 2 * n], res[-1]


def _push_wait(started, relations, per_peer, by_sender, after, name):
    send_sems, recv_sems, srcs, lands, _ = started
    n = len(srcs)

    def body(*refs):
        src_refs, land_refs = refs[0:n], refs[n:2 * n]
        send_s, recv_s = refs[2 * n], refs[2 * n + 1]
        for a in range(n):
            for i in range(len(relations)):
                _push_copy(src_refs, land_refs, send_s, recv_s, a, i, relations, per_peer, by_sender, False).wait_send()
                _push_copy(src_refs, land_refs, send_s, recv_s, a, i, relations, per_peer, by_sender, True).wait_recv()

    res = pl.pallas_call(
        body, name=name,
        in_specs=[HBM] * (2 * n) + [SEM, SEM, ANY],
        out_specs=[HBM] * (2 * n),
        out_shape=[pltpu.HBM(t.shape, t.dtype) for t in list(srcs) + list(lands)],
        input_output_aliases={i: i for i in range(2 * n)},
        compiler_params=pltpu.CompilerParams(has_side_effects=EFFECT))(*srcs, *lands, send_sems, recv_sems, after)
    return res[0:n], res[n:2 * n]


def _pass_on(lands, name):
    n = len(lands)

    def body(*refs):
        outs = refs[n:2 * n]
        send_sems, recv_sems = refs[2 * n:]
        x, y, c = _position()
        chips = [(1 - x, y), (x, 1 - y), (1 - x, 1 - y)]
        copies = []
        for a in range(n):
            for j, chip in enumerate(chips):
                def copy(core):
                    rows = outs[a].at[_block_index(*chip, core)]
                    return pltpu.make_async_remote_copy(
                        src_ref=rows, dst_ref=rows, send_sem=send_sems.at[a, j], recv_sem=recv_sems.at[a, j],
                        device_id=(x, y, 1 - c), device_id_type=MESH)
                copy(c).start()
                copies.append((copy(c), copy(1 - c)))
        for sending, arriving in copies:
            sending.wait_send()
            arriving.wait_recv()

    return pl.pallas_call(
        body, name=name,
        in_specs=[ANY] * n, out_specs=[ANY] * n,
        out_shape=[jax.ShapeDtypeStruct(t.shape, t.dtype) for t in lands],
        input_output_aliases={a: a for a in range(n)},
        scratch_shapes=[pltpu.SemaphoreType.DMA((n, 3)), pltpu.SemaphoreType.DMA((n, 3))],
    )(*lands)


def _all_reduce_rows(v, name):
    R, D = v.shape

    def body(v_ref, o_ref, buf_ref, send_sems, recv_sems):
        x, y, c = _position()
        me_idx = _block_index(x, y, c)
        buf_ref[me_idx] = v_ref[...]
        copies = []
        for k in range(1, N_DEV):
            px = (1 - x) if k & 4 else x
            py = (1 - y) if k & 2 else y
            pc = (1 - c) if k & 1 else c
            rc = pltpu.make_async_remote_copy(
                src_ref=v_ref, dst_ref=buf_ref.at[me_idx],
                send_sem=send_sems.at[k - 1], recv_sem=recv_sems.at[k - 1],
                device_id=(px, py, pc), device_id_type=MESH)
            rc.start()
            copies.append((rc, pltpu.make_async_remote_copy(
                src_ref=v_ref, dst_ref=buf_ref.at[_block_index(px, py, pc)],
                send_sem=send_sems.at[k - 1], recv_sem=recv_sems.at[k - 1],
                device_id=(px, py, pc), device_id_type=MESH)))
        for rc, arrival in copies:
            rc.wait_send()
            arrival.wait_recv()
        acc = buf_ref[0]
        for j in range(1, N_DEV):
            acc = acc + buf_ref[j]
        o_ref[...] = acc

    return pl.pallas_call(
        body, name=name,
        in_specs=[pl.BlockSpec(memory_space=pltpu.VMEM)],
        out_specs=pl.BlockSpec(memory_space=pltpu.VMEM),
        out_shape=jax.ShapeDtypeStruct((R, D), F32),
        scratch_shapes=[pltpu.VMEM((N_DEV, R, D), F32),
                        pltpu.SemaphoreType.DMA((7,)), pltpu.SemaphoreType.DMA((7,))],
    )(v)


def _rope_tables(S):
    inv_freq = 1.0 / (ROPE_THETA ** (jnp.arange(0, HEAD_DIM, 2, dtype=F32) / HEAD_DIM))
    ang = jnp.arange(S, dtype=F32)[:, None] * inv_freq[None, :]
    cos, sin = jnp.cos(ang), jnp.sin(ang)
    return jnp.concatenate([cos, cos], axis=1), jnp.concatenate([-sin, sin], axis=1)


def _local_step(xs, target, vecs, n_a, n_b, get_weights, put_grads):
    S, D = xs.shape
    E = D
    cos2, sin2 = _rope_tables(S)
    ts = min(1024, S)

    def col_blocks(w):
        cb = w.shape[2]
        tn = min(cb, 1024)
        per = cb // tn
        return (None, D, tn), (lambda i, j: (j // per, 0, j % per)), N_DEV * per, tn

    def grad_in(hn, dproj, cb, name):
        return _matmul_tn(hn, dproj, (ts, D), lambda j, s: (s, 0), (ts, cb), lambda j, s: (s, j),
                          (N_DEV, D, cb), (None, D, cb), lambda j, s: (j, 0, 0), (D, cb), N_DEV, name)

    def grad_out(z, dx, name, col=0):
        rows = z.shape[1]
        ta = min(1024, rows)
        out = _matmul_tn(z, dx, (ts, ta), lambda a, s: (s, a), (ts, E), lambda a, s: (s, col),
                         (rows, E), (ta, E), lambda a, s: (a, 0), (ta, E), rows // ta, name)
        return out.reshape(N_DEV, rows // N_DEV, E)

    x = xs
    a_saved, b_saved = [], []
    for i in range(n_a):
        w = get_weights(f"a{i}", x)
        blk, idx, nblocks, tn = col_blocks(w["w_in"])
        proj, hn = _norm_matmul(x, vecs["norm_a"][i:i + 1], w["w_in"], blk, idx, nblocks, tn, f"a{i}_in")
        z = _a_mid_fwd(proj, w["w_grp"], vecs["scale_a"][i:i + 1], f"a{i}_mid")
        x_next = _matmul_res(z, w["w_out"], x, f"a{i}_out")
        a_saved.append((x, hn, proj, z, w))
        x = x_next
    w_kv = get_weights("kv", x)["w_kv"]
    x = x_kv = _to_class_order(x)
    target = _to_class_order(target)
    cos2, sin2 = _to_class_order(cos2), _to_class_order(sin2)
    tn = min(E, 1024)
    kvp, hn_kv = _norm_matmul(x, vecs["norm_kv"], w_kv, (D, tn), lambda i, j: (0, j), 2 * E // tn, tn, "kv_in")
    kr = _rope_k(kvp, cos2, sin2, "kv_rope")
    after = kr
    for i in range(n_b):
        w = get_weights(f"b{i}", after)
        blk, idx, nblocks, tn = col_blocks(w["w_in"])
        proj, hn = _norm_matmul(x, vecs["norm_b"][i:i + 1], w["w_in"], blk, idx, nblocks, tn, f"b{i}_in")
        outs, lses = [], []
        for g, dil in enumerate(DILATIONS):
            o, l = _attn_fwd(proj, kr, kvp, cos2, sin2, g, dil, f"b{i}_attn{g}")
            outs.append(o)
            lses.append(l)
        x_next, z = _merge_out(outs, lses, proj, w["w_out"], x, f"b{i}_out")
        b_saved.append((x, hn, proj, z, outs, lses, w))
        x = x_next
        after = x
    loss, dx, dg_f = _final_norm_loss(x, target, vecs["norm_f"], "final")

    vec = {"norm_a": [None] * n_a, "scale_a": [None] * n_a, "norm_b": [None] * n_b, "norm_f": [dg_f]}
    dks, dvs = [], []
    for i in reversed(range(n_b)):
        x_in, hn, proj, z, outs, lses, w = b_saved[i]
        dw_out = grad_out(z, dx, f"b{i}_dwout")
        dos, dlts, dproj = _merge_bwd(dx, w["w_out"], outs, lses, proj, f"b{i}_dmerge")
        for g, dil in enumerate(DILATIONS):
            dproj, dk, dv = _attn_bwd(proj, kr, kvp, cos2, sin2, dos[g], lses[g], dlts[g], dproj, g, dil,
                                      f"b{i}_dattn{g}")
            dks.append(dk)
            dvs.append(dv)
        cb = w["w_in"].shape[2]
        tok = put_grads(f"b{i}", {"w_out": dw_out, "w_in": grad_in(hn, dproj, cb, f"b{i}_dwin")})
        dx, vec["norm_b"][i] = _matmul_nt_dnorm(dproj, w["w_in"], (None, D, cb), lambda t, j: (j, 0, 0), N_DEV, cb, x_in,
                                                vecs["norm_b"][i:i + 1] + tok[0:1, 0:1], dx, f"b{i}_dhn")

    dkv = _kv_bwd(dks, dvs, cos2, sin2, "kv_dsum")
    tok = put_grads("kv", {"w_k": grad_out(hn_kv, dkv, "kv_dwk", 0), "w_v": grad_out(hn_kv, dkv, "kv_dwv", 1)})
    tk = min(E, 1024)
    dx, dg_kv = _matmul_nt_dnorm(dkv, w_kv, (D, tk), lambda t, j: (0, j), 2 * E // tk, tk, x_kv,
                                 vecs["norm_kv"] + tok[0:1, 0:1], dx, "kv_dhn")
    vec["norm_kv"] = [dg_kv]
    dx = _from_class_order(dx)

    for i in reversed(range(n_a)):
        x_in, hn, proj, z, w = a_saved[i]
        dw_out = grad_out(z, dx, f"a{i}_dwout")
        dproj, dwg, dsc = _a_mid_bwd(dx, w["w_out"], proj, w["w_grp"], vecs["scale_a"][i:i + 1], f"a{i}_dmid")
        n_grp, gc, _ = dwg.shape
        dwg = dwg.reshape(n_grp, N_DEV, gc // N_DEV, gc).transpose(1, 0, 2, 3).astype(BF16)
        vec["scale_a"][i] = dsc
        cb = w["w_in"].shape[2]
        tok = put_grads(f"a{i}", {"w_out": dw_out, "w_grp": dwg, "w_in": grad_in(hn, dproj, cb, f"a{i}_dwin")})
        dx, vec["norm_a"][i] = _matmul_nt_dnorm(dproj, w["w_in"], (None, D, cb), lambda t, j: (j, 0, 0), N_DEV, cb, x_in,
                                                vecs["norm_a"][i:i + 1] + tok[0:1, 0:1], dx, f"a{i}_dhn")

    return loss, dx, {k: jnp.concatenate(v, axis=0) for k, v in vec.items()}


VECTORS = ("norm_a", "scale_a", "norm_kv", "norm_b", "norm_f")
SHARDED_VECTORS = ("norm_a", "scale_a")
GROUPS = {
    "a0": (("w_in", "w_in_a", 0), ("w_grp", "w_grp_a", 0), ("w_out", "w_out_a", 0)),
    "a1": (("w_in", "w_in_a", 1), ("w_grp", "w_grp_a", 1), ("w_out", "w_out_a", 1)),
    "kv": (("w_k", "w_k", None), ("w_v", "w_v", None)),
    "b0": (("w_in", "w_in_b", 0), ("w_out", "w_out_b", 0)),
    "b1": (("w_in", "w_in_b", 1), ("w_out", "w_out_b", 1)),
}
PREFETCHED = ("a1", "kv", "b0", "b1")


def kernel(x, norm_a, w_in_a, w_grp_a, scale_a, w_out_a, norm_kv, w_k, w_v, norm_b, w_in_b, w_out_b, norm_f, loss_target, m_norm_a, m_w_in_a, m_w_grp_a, m_scale_a, m_w_out_a, m_norm_kv, m_w_k, m_w_v, m_norm_b, m_w_in_b, m_w_out_b, m_norm_f, v_norm_a, v_w_in_a, v_w_grp_a, v_scale_a, v_w_out_a, v_norm_kv, v_w_k, v_w_v, v_norm_b, v_w_in_b, v_w_out_b, v_norm_f):
    w = dict(norm_a=norm_a, w_in_a=w_in_a, w_grp_a=w_grp_a, scale_a=scale_a, w_out_a=w_out_a, norm_kv=norm_kv,
             w_k=w_k, w_v=w_v, norm_b=norm_b, w_in_b=w_in_b, w_out_b=w_out_b, norm_f=norm_f)
    m = dict(norm_a=m_norm_a, w_in_a=m_w_in_a, w_grp_a=m_w_grp_a, scale_a=m_scale_a, w_out_a=m_w_out_a,
             norm_kv=m_norm_kv, w_k=m_w_k, w_v=m_w_v, norm_b=m_norm_b, w_in_b=m_w_in_b, w_out_b=m_w_out_b,
             norm_f=m_norm_f)
    v = dict(norm_a=v_norm_a, w_in_a=v_w_in_a, w_grp_a=v_w_grp_a, scale_a=v_scale_a, w_out_a=v_w_out_a,
             norm_kv=v_norm_kv, w_k=v_w_k, w_v=v_w_v, norm_b=v_norm_b, w_in_b=v_w_in_b, w_out_b=v_w_out_b,
             norm_f=v_norm_f)
    D = x.shape[2]
    me = _block_index(*_position())

    def shard(group):
        return [w[p].astype(BF16) if layer is None else w[p][layer].astype(BF16) for _, p, layer in GROUPS[group]]

    def as_weights(group, gathered):
        out = dict(zip([n for n, _, _ in GROUPS[group]], gathered))
        if "w_grp" in out:
            g = out["w_grp"]
            out["w_grp"] = g.transpose(1, 0, 2, 3).reshape(g.shape[1], g.shape[3], g.shape[3])
        if "w_k" in out:
            out = {"w_kv": jnp.concatenate([out["w_k"].reshape(D, D), out["w_v"].reshape(D, D)], axis=1)}
        return out

    first = _all_gather(shard("a0") + [w[k] for k in SHARDED_VECTORS], "gather_first")
    n_first = len(GROUPS["a0"])
    vecs = {k: g.transpose(1, 0, 2).reshape(w[k].shape[0], D) for k, g in zip(SHARDED_VECTORS, first[n_first:])}
    vecs.update(norm_kv=norm_kv[None, :], norm_b=norm_b, norm_f=norm_f[None, :])
    srcs, lands = [], []
    for group in PREFETCHED:
        for s in shard(group):
            srcs.append(s)
            lands.append(lax.dynamic_update_index_in_dim(lax.empty((N_DEV,) + s.shape, s.dtype), s[None], me, 0))
    inflight, at = {}, 0
    token = None
    for group in PREFETCHED:
        n = len(GROUPS[group])
        inflight[group] = _push_start(srcs[at:at + n], lands[at:at + n], SIBLING_AND_SAME_CORES, False, True,
                                      first[0] if token is None else token, f"gather_{group}_start")
        token = inflight[group][4]
        at += n
    vecs["norm_a"] = vecs["norm_a"] + token[0:1, 0:1]

    def get_weights(group, after):
        if group == "a0":
            return as_weights(group, first[0:n_first])
        half = _push_wait(inflight[group], SIBLING_AND_SAME_CORES, False, True, after, f"gather_{group}_wait")[1]
        return as_weights(group, _pass_on(half, f"gather_{group}_pass"))

    sent = {}

    def put_grads(group, grads):
        blocks = [grads[n] for n, _, _ in GROUPS[group]]
        lands = [lax.empty((N_DEV - 1,) + b.shape[1:], b.dtype) for b in blocks]
        sent[group] = _push_start(blocks, lands, ALL_PEERS, True, False, jnp.zeros((8, 128), F32),
                                  f"exchange_{group}_start")
        return sent[group][4]

    loss, dx, vec = _local_step(x[0], loss_target[0], vecs, w_in_a.shape[0], w_in_b.shape[0], get_weights, put_grads)
    rows = _all_reduce_rows(jnp.concatenate([vec[k] for k in VECTORS], axis=0), "reduce_vectors")

    out = {}
    after = dx
    for group in sent:
        blocks, arrived = _push_wait(sent[group], ALL_PEERS, True, False, after, f"exchange_{group}_wait")
        for (_, p, layer), blk, got in zip(GROUPS[group], blocks, arrived):
            cols = w[p].shape[-1]
            own = lax.dynamic_index_in_dim(blk, me, 0, keepdims=False).reshape(-1, cols)
            n_layers = 1 if layer is None else w[p].shape[0]
            stacked = lambda t: t.reshape(n_layers, -1, cols)
            res = _adamw_blocks(own, got.reshape(N_DEV - 1, -1, cols), stacked(w[p]), stacked(m[p]), stacked(v[p]),
                                0 if layer is None else layer, out.get(p), f"adamw_{group}_{p}")
            out[p] = res
            after = res[1]
    out = {p: [r.reshape(w[p].shape) for r in res] for p, res in out.items()}
    start = 0
    for k in VECTORS:
        n_rows = vec[k].shape[0]
        g = rows[start:start + n_rows]
        start += n_rows
        if k in SHARDED_VECTORS:
            g = lax.dynamic_slice_in_dim(g, me * (D // N_DEV), D // N_DEV, axis=1)
        res = _adamw_rows(g, w[k].reshape(g.shape), m[k].reshape(g.shape), v[k].reshape(g.shape), f"adamw_{k}")
        out[k] = [r.reshape(w[k].shape) for r in [g] + list(res)]

    names = ("norm_a", "w_in_a", "w_grp_a", "scale_a", "w_out_a", "norm_kv", "w_k", "w_v", "norm_b", "w_in_b",
             "w_out_b", "norm_f")
    total = lax.psum(loss[0, 0], ("x", "y", "c"))
    return (total, dx[None], *[out[k][0] for k in names], *[out[k][1] for k in names],
            *[out[k][2] for k in names], *[out[k][3] for k in names])
```

```python
import math

import jax
import jax.numpy as jnp
from jax import lax
from jax.experimental import pallas as pl
from jax.experimental.pallas import tpu as pltpu

F32 = jnp.float32
BF16 = jnp.bfloat16

N_DEV = 8
MESH = pl.DeviceIdType.MESH
RMS_EPS = 1e-6
HEAD_DIM = 128
HALF_HEAD = HEAD_DIM // 2
BAND = 128
DILATIONS = (1, 4, 16)
POOL_WINDOWS = (2, 4, 8, 16)
POOL_HALO = 16
ROPE_THETA = 10000.0
NEG_INF = -1e30
ATTN_SCALE = 1.0 / math.sqrt(HEAD_DIM)
LOG2_E = math.log2(math.e)
ADAM_LR, ADAM_B1, ADAM_B2, ADAM_EPS, ADAM_WD, ADAM_STEP = 0.001, 0.9, 0.999, 1e-08, 0.01, 10
VMEM_LIMIT_BYTES = 56 * 1024 * 1024
ANY = pl.BlockSpec(memory_space=pl.ANY)
NT = (((1,), (1,)), ((), ()))
TN = (((0,), (0,)), ((), ()))


def _params(*semantics):
    return pltpu.CompilerParams(dimension_semantics=semantics, vmem_limit_bytes=VMEM_LIMIT_BYTES)


def _sigmoid(t):
    return 1.0 / (1.0 + jnp.exp(-t))


def _rope(t, cos2, sin2):
    return t * cos2 + pltpu.roll(t, HALF_HEAD, 1) * sin2


def _rope_bwd(dt, cos2, sin2):
    return dt * cos2 + pltpu.roll(dt * sin2, HALF_HEAD, 1)


def _norm_matmul(x, gain, w, w_block, w_index, n_col_blocks, tn, name):
    S, D = x.shape
    tm = min(1024, S)

    def body(x_ref, g_ref, w_ref, o_ref, hn_ref, hs_ref):
        @pl.when(pl.program_id(1) == 0)
        def _():
            xf = x_ref[...]
            inv = lax.rsqrt(jnp.mean(xf * xf, axis=-1, keepdims=True) + RMS_EPS)
            hb = ((xf * inv) * g_ref[...]).astype(BF16)
            hs_ref[...] = hb
            hn_ref[...] = hb
        o_ref[...] = jnp.dot(hs_ref[...], w_ref[...], preferred_element_type=F32)

    return pl.pallas_call(
        body, name=name, grid=(S // tm, n_col_blocks),
        in_specs=[pl.BlockSpec((tm, D), lambda i, j: (i, 0)),
                  pl.BlockSpec((1, D), lambda i, j: (0, 0)),
                  pl.BlockSpec(w_block, w_index)],
        out_specs=[pl.BlockSpec((tm, tn), lambda i, j: (i, j)),
                   pl.BlockSpec((tm, D), lambda i, j: (i, 0))],
        out_shape=[jax.ShapeDtypeStruct((S, n_col_blocks * tn), F32), jax.ShapeDtypeStruct((S, D), BF16)],
        scratch_shapes=[pltpu.VMEM((tm, D), BF16)],
        compiler_params=_params("parallel", "arbitrary"))(x, gain, w)


def _matmul_res(a, w, res, name):
    S, K = a.shape
    nd, rb, N = w.shape
    tm = min(512, S)

    def body(a_ref, w_ref, r_ref, o_ref):
        acc = jnp.dot(a_ref[:, 0:rb], w_ref[0], preferred_element_type=F32)
        for k in range(1, nd):
            acc = acc + jnp.dot(a_ref[:, k * rb:(k + 1) * rb], w_ref[k], preferred_element_type=F32)
        o_ref[...] = r_ref[...] + acc

    return pl.pallas_call(
        body, name=name, grid=(S // tm,),
        in_specs=[pl.BlockSpec((tm, K), lambda i: (i, 0)),
                  pl.BlockSpec((nd, rb, N), lambda i: (0, 0, 0)),
                  pl.BlockSpec((tm, N), lambda i: (i, 0))],
        out_specs=pl.BlockSpec((tm, N), lambda i: (i, 0)),
        out_shape=jax.ShapeDtypeStruct((S, N), F32),
        compiler_params=_params("parallel"))(a, w, res)


def _matmul_nt_dnorm(dp, w, w_block, w_index, n_red, tc, x, gain, dres, name):
    S, D = x.shape
    tm = min(512, S)

    def body(d_ref, w_ref, x_ref, g_ref, r_ref, dx_ref, dg_ref, acc_ref):
        i, j = pl.program_id(0), pl.program_id(1)

        @pl.when((i == 0) & (j == 0))
        def _():
            dg_ref[...] = jnp.zeros_like(dg_ref)

        @pl.when(j == 0)
        def _():
            acc_ref[...] = jnp.zeros_like(acc_ref)
        acc_ref[...] += lax.dot_general(d_ref[...], w_ref[...], NT, preferred_element_type=F32)

        @pl.when(j == n_red - 1)
        def _():
            xf = x_ref[...]
            inv = lax.rsqrt(jnp.mean(xf * xf, axis=-1, keepdims=True) + RMS_EPS)
            xhat = xf * inv
            dh = acc_ref[...]
            dg_ref[...] += jnp.sum(dh * xhat, axis=0, keepdims=True)
            dxh = dh * g_ref[...]
            dx_ref[...] = r_ref[...] + inv * (dxh - xhat * jnp.mean(dxh * xhat, axis=-1, keepdims=True))

    tile = pl.BlockSpec((tm, D), lambda i, j: (i, 0))
    vec = pl.BlockSpec((1, D), lambda i, j: (0, 0))
    return pl.pallas_call(
        body, name=name, grid=(S // tm, n_red),
        in_specs=[pl.BlockSpec((tm, tc), lambda i, j: (i, j)), pl.BlockSpec(w_block, w_index), tile, vec, tile],
        out_specs=[tile, vec],
        out_shape=[jax.ShapeDtypeStruct((S, D), F32), jax.ShapeDtypeStruct((1, D), F32)],
        scratch_shapes=[pltpu.VMEM((tm, D), F32)],
        compiler_params=_params("arbitrary", "arbitrary"))(dp, w, x, gain, dres)


def _matmul_tn(a, b, a_block, a_index, b_block, b_index, out_shape, out_block, out_index, acc_shape, n_outer, name):
    S = a.shape[0]
    ts = a_block[0]
    n_tok = S // ts

    def body(a_ref, b_ref, o_ref, acc_ref):
        s = pl.program_id(1)

        @pl.when(s == 0)
        def _():
            acc_ref[...] = jnp.zeros_like(acc_ref)
        acc_ref[...] += lax.dot_general(a_ref[...].astype(BF16), b_ref[...].astype(BF16), TN,
                                        preferred_element_type=F32)

        @pl.when(s == n_tok - 1)
        def _():
            o_ref[...] = acc_ref[...].astype(o_ref.dtype)

    return pl.pallas_call(
        body, name=name, grid=(n_outer, n_tok),
        in_specs=[pl.BlockSpec(a_block, a_index), pl.BlockSpec(b_block, b_index)],
        out_specs=pl.BlockSpec(out_block, out_index),
        out_shape=jax.ShapeDtypeStruct(out_shape, BF16),
        scratch_shapes=[pltpu.VMEM(acc_shape, F32)],
        compiler_params=_params("parallel", "arbitrary"))(a, b)


def _pool(scr_ref, u, row0, tm, E):
    gc = E // len(POOL_WINDOWS)
    t1 = row0 + lax.broadcasted_iota(jnp.int32, (tm, 1), 0) + 1
    out = []
    for g, win in enumerate(POOL_WINDOWS):
        cs = slice(g * gc, (g + 1) * gc)
        acc = u[:, cs]
        for k in range(1, win):
            acc = acc + scr_ref[pl.ds(POOL_HALO - k, tm), cs]
        count = jnp.minimum(t1, win).astype(F32)
        out.append(acc / count - u[:, cs])
    return out


def _a_mid_fwd(proj, wg, scale, name):
    S, E2 = proj.shape
    E = E2 // 2
    gc = E // len(POOL_WINDOWS)
    tm = min(256, S)
    hb = tm // POOL_HALO

    def body(u_ref, uh_ref, gt_ref, wg_ref, sc_ref, z_ref, scr_ref):
        i = pl.program_id(0)
        scr_ref[0:POOL_HALO, :] = jnp.where(i > 0, uh_ref[...], 0.0)
        u = u_ref[...]
        scr_ref[POOL_HALO:POOL_HALO + tm, :] = u
        pooled = _pool(scr_ref, u, i * tm, tm, E)
        for g in range(len(POOL_WINDOWS)):
            cs = slice(g * gc, (g + 1) * gc)
            y = jnp.dot(pooled[g].astype(BF16), wg_ref[g], preferred_element_type=F32) * sc_ref[:, cs]
            gate = gt_ref[:, cs]
            z_ref[:, cs] = (y * (gate * _sigmoid(gate))).astype(BF16)

    return pl.pallas_call(
        body, name=name, grid=(S // tm,),
        in_specs=[pl.BlockSpec((tm, E), lambda i: (i, 0)),
                  pl.BlockSpec((POOL_HALO, E), lambda i: (jnp.maximum(i * hb - 1, 0), 0)),
                  pl.BlockSpec((tm, E), lambda i: (i, 1)),
                  pl.BlockSpec((len(POOL_WINDOWS), gc, gc), lambda i: (0, 0, 0)),
                  pl.BlockSpec((1, E), lambda i: (0, 0))],
        out_specs=pl.BlockSpec((tm, E), lambda i: (i, 0)),
        out_shape=jax.ShapeDtypeStruct((S, E), BF16),
        scratch_shapes=[pltpu.VMEM((POOL_HALO + tm, E), F32)],
        compiler_params=_params("parallel"))(proj, proj, proj, wg, scale)


def _a_mid_bwd(dx, w_out, proj, wg, scale, name):
    S, E2 = proj.shape
    E = E2 // 2
    D = dx.shape[1]
    n_grp = len(POOL_WINDOWS)
    gc = E // n_grp
    tm = min(256, S)
    hb = tm // POOL_HALO
    n_tiles = S // tm
    last_halo = S // POOL_HALO - 1

    def body(dx_ref, dxh_ref, wo_ref, u_ref, uh_ref, gt_ref, gth_ref, wg_ref, sc_ref, dp_ref, dwg_ref, dsc_ref,
             scr_ref, q_ref):
        i = pl.program_id(0)
        dxb = jnp.concatenate([dx_ref[...].astype(BF16), dxh_ref[...].astype(BF16)], axis=0)

        @pl.when(i == 0)
        def _():
            dwg_ref[...] = jnp.zeros_like(dwg_ref)
            dsc_ref[...] = jnp.zeros_like(dsc_ref)

        scr_ref[0:POOL_HALO, :] = jnp.where(i > 0, uh_ref[...], 0.0)
        u = u_ref[...]
        scr_ref[POOL_HALO:POOL_HALO + tm, :] = u
        pooled = _pool(scr_ref, u, i * tm, tm, E)
        t1 = i * tm + lax.broadcasted_iota(jnp.int32, (tm, 1), 0) + 1
        t1h = (i + 1) * tm + lax.broadcasted_iota(jnp.int32, (POOL_HALO, 1), 0) + 1
        not_last = i < n_tiles - 1
        for g, win in enumerate(POOL_WINDOWS):
            cs = slice(g * gc, (g + 1) * gc)
            w = wg_ref[g]
            sc = sc_ref[:, cs]
            pb = pooled[g].astype(BF16)
            ypre = jnp.dot(pb, w, preferred_element_type=F32)
            gate = gt_ref[:, cs]
            sg = _sigmoid(gate)
            silu = gate * sg
            dz_all = lax.dot_general(dxb, wo_ref[cs, :], NT, preferred_element_type=F32)
            dzg = dz_all[0:tm]
            dy = dzg * silu
            dp_ref[:, E + g * gc:E + (g + 1) * gc] = (dzg * (ypre * sc) * (sg * (1.0 + gate * (1.0 - sg)))).astype(BF16)
            dsc_ref[:, cs] += jnp.sum(dy * ypre, axis=0, keepdims=True)
            dyp = (dy * sc).astype(BF16)
            dwg_ref[g] += lax.dot_general(pb, dyp, TN, preferred_element_type=F32)
            dpool = lax.dot_general(dyp, w, NT, preferred_element_type=F32)
            gate_h = gth_ref[:, cs]
            dyp_h = (dz_all[tm:tm + POOL_HALO] * (gate_h * _sigmoid(gate_h)) * sc).astype(BF16)
            dpool_h = lax.dot_general(dyp_h, w, NT, preferred_element_type=F32)
            q_ref[0:tm, cs] = dpool / jnp.minimum(t1, win).astype(F32)
            q_ref[tm:tm + POOL_HALO, cs] = jnp.where(not_last, dpool_h / jnp.minimum(t1h, win).astype(F32), 0.0)
            acc = q_ref[0:tm, cs] - dpool
            for k in range(1, win):
                acc = acc + q_ref[pl.ds(k, tm), cs]
            dp_ref[:, cs] = acc.astype(BF16)

    return pl.pallas_call(
        body, name=name, grid=(n_tiles,),
        in_specs=[pl.BlockSpec((tm, D), lambda i: (i, 0)),
                  pl.BlockSpec((POOL_HALO, D), lambda i: (jnp.minimum((i + 1) * hb, last_halo), 0)),
                  pl.BlockSpec((E, D), lambda i: (0, 0)),
                  pl.BlockSpec((tm, E), lambda i: (i, 0)),
                  pl.BlockSpec((POOL_HALO, E), lambda i: (jnp.maximum(i * hb - 1, 0), 0)),
                  pl.BlockSpec((tm, E), lambda i: (i, 1)),
                  pl.BlockSpec((POOL_HALO, E), lambda i: (jnp.minimum((i + 1) * hb, last_halo), 1)),
                  pl.BlockSpec((n_grp, gc, gc), lambda i: (0, 0, 0)),
                  pl.BlockSpec((1, E), lambda i: (0, 0))],
        out_specs=[pl.BlockSpec((tm, E2), lambda i: (i, 0)),
                   pl.BlockSpec((n_grp, gc, gc), lambda i: (0, 0, 0)),
                   pl.BlockSpec((1, E), lambda i: (0, 0))],
        out_shape=[jax.ShapeDtypeStruct((S, E2), BF16),
                   jax.ShapeDtypeStruct((n_grp, gc, gc), F32),
                   jax.ShapeDtypeStruct((1, E), F32)],
        scratch_shapes=[pltpu.VMEM((POOL_HALO + tm, E), F32), pltpu.VMEM((tm + POOL_HALO, E), F32)],
        compiler_params=_params("arbitrary"))(dx, dx, w_out.reshape(E, D), proj, proj, proj, proj, wg, scale)


def _rope_k(kvp, cos2, sin2, name):
    S, E2 = kvp.shape
    E = E2 // 2
    tm = min(256, S)

    def body(k_ref, c_ref, s_ref, ko_ref):
        cosv, sinv = c_ref[...], s_ref[...]
        for h in range(E // HEAD_DIM):
            hs = slice(h * HEAD_DIM, (h + 1) * HEAD_DIM)
            ko_ref[:, hs] = _rope(k_ref[:, hs], cosv, sinv)

    return pl.pallas_call(
        body, name=name, grid=(S // tm,),
        in_specs=[pl.BlockSpec((tm, E), lambda i: (i, 0)),
                  pl.BlockSpec((tm, HEAD_DIM), lambda i: (i, 0)), pl.BlockSpec((tm, HEAD_DIM), lambda i: (i, 0))],
        out_specs=pl.BlockSpec((tm, E), lambda i: (i, 0)),
        out_shape=jax.ShapeDtypeStruct((S, E), F32),
        compiler_params=_params("parallel"))(kvp, cos2, sin2)


CLASSES = 16


def _class_order(a, back, name):
    S, W = a.shape
    M = S // CLASSES

    def body(a_ref, o_ref):
        for r in range(CLASSES):
            if back:
                o_ref[pl.ds(r, M, stride=CLASSES), :] = a_ref[pl.ds(r * M, M), :]
            else:
                o_ref[pl.ds(r * M, M), :] = a_ref[pl.ds(r, M, stride=CLASSES), :]

    col = pl.BlockSpec((S, HEAD_DIM), lambda i: (0, i))
    return pl.pallas_call(
        body, name=name, grid=(W // HEAD_DIM,), in_specs=[col], out_specs=col,
        out_shape=jax.ShapeDtypeStruct((S, W), a.dtype),
        compiler_params=_params("parallel"))(a)


def _runs(r, b, dil, M, back=0):
    nj = CLASSES // dil
    c = BAND // nj
    return [((r + dil * j) * M + (b - back) * c, (1 + back) * c) for j in range(nj)]


def _load(ref, runs):
    parts = [ref[pl.ds(start, n), :] for start, n in runs]
    return parts[0] if len(parts) == 1 else jnp.concatenate(parts, axis=0)


def _store(ref, runs, val, add=False):
    at = 0
    for start, n in runs:
        if add:
            ref[pl.ds(start, n), :] += val[at:at + n]
        else:
            ref[pl.ds(start, n), :] = val[at:at + n]
        at += n


def _keys(ref, r, b, dil, M):
    if b > 0:
        return _load(ref, _runs(r, b, dil, M, back=1)).astype(BF16)
    parts = []
    for start, n in _runs(r, 0, dil, M):
        parts += [jnp.zeros((n, HEAD_DIM), BF16), ref[pl.ds(start, n), :].astype(BF16)]
    return jnp.concatenate(parts, axis=0)


def _band_mask(dil, first):
    nj = CLASSES // dil
    c = BAND // nj
    row = lax.broadcasted_iota(jnp.int32, (BAND, 2 * BAND), 0)
    col = lax.broadcasted_iota(jnp.int32, (BAND, 2 * BAND), 1)
    q_place = (row % c) * nj + row // c
    k_place = (col % (2 * c) - c) * nj + col // (2 * c)
    mask = (q_place >= k_place) & (q_place <= k_place + BAND)
    return mask & (col % (2 * c) >= c) if first else mask


def _lane_column(tile, lane, h):
    return jnp.sum(jnp.where(lane == h, tile, 0.0), axis=-1, keepdims=True)


def _attn_fwd(proj, kr, kvp, cos2, sin2, group, dil, name):
    S, PW = proj.shape
    E = kr.shape[1]
    H = E // HEAD_DIM
    M = S // CLASSES
    nb = S // (BAND * dil)

    def body(q_ref, k_ref, v_ref, c_ref, s_ref, o_ref, l_ref):
        h = pl.program_id(0)
        lane = lax.broadcasted_iota(jnp.int32, (BAND, HEAD_DIM), 1)
        edge, inner = _band_mask(dil, True), _band_mask(dil, False)

        @pl.when(h == 0)
        def _():
            l_ref[...] = jnp.zeros_like(l_ref)

        def scores(r, b):
            runs = _runs(r, b, dil, M)
            qr = _rope(_load(q_ref, runs), _load(c_ref, runs), _load(s_ref, runs)).astype(BF16)
            return lax.dot_general(qr, _keys(k_ref, r, b, dil, M), NT, preferred_element_type=F32)

        units = [(r, b) for r in range(dil) for b in range(nb)]
        ahead = scores(*units[0])
        for i, (r, b) in enumerate(units):
            s = ahead
            if i + 1 < len(units):
                ahead = scores(*units[i + 1])
            runs = _runs(r, b, dil, M)
            s = jnp.where(edge if b == 0 else inner, s, NEG_INF)
            m = jnp.max(s, axis=-1, keepdims=True)
            p = jnp.exp2((s - m) * (ATTN_SCALE * LOG2_E))
            l = jnp.sum(p, axis=-1, keepdims=True)
            _store(o_ref, runs, jnp.dot(p.astype(BF16), _keys(v_ref, r, b, dil, M), preferred_element_type=F32) / l)
            _store(l_ref, runs, jnp.where(lane == h, m * ATTN_SCALE + jnp.log(l), _load(l_ref, runs)))

    col = (S, HEAD_DIM)
    whole = pl.BlockSpec(col, lambda h: (0, 0))
    return pl.pallas_call(
        body, name=name, grid=(H,),
        in_specs=[pl.BlockSpec(col, lambda h: (0, group * H + h)), pl.BlockSpec(col, lambda h: (0, h)),
                  pl.BlockSpec(col, lambda h: (0, H + h)), whole, whole],
        out_specs=[pl.BlockSpec(col, lambda h: (0, h)), whole],
        out_shape=[jax.ShapeDtypeStruct((S, E), F32), jax.ShapeDtypeStruct((S, HEAD_DIM), F32)],
        compiler_params=_params("arbitrary"))(proj, kr, kvp, cos2, sin2)


def _attn_bwd(proj, kr, kvp, cos2, sin2, do, lse, dlt, dproj, group, dil, name):
    S, PW = proj.shape
    E = kr.shape[1]
    H = E // HEAD_DIM
    M = S // CLASSES
    nb = S // (BAND * dil)

    def body(q_ref, k_ref, v_ref, c_ref, s_ref, do_ref, l_ref, dl_ref, dproj_ref,
             dq_ref, dk_ref, dv_ref, dq_scr, dk_scr, dv_scr):
        h = pl.program_id(0)
        lane = lax.broadcasted_iota(jnp.int32, (BAND, HEAD_DIM), 1)
        edge, inner = _band_mask(dil, True), _band_mask(dil, False)
        dk_scr[...] = jnp.zeros_like(dk_scr)
        dv_scr[...] = jnp.zeros_like(dv_scr)

        def scores(r, b):
            runs = _runs(r, b, dil, M)
            qr = _rope(_load(q_ref, runs), _load(c_ref, runs), _load(s_ref, runs)).astype(BF16)
            return qr, lax.dot_general(qr, _keys(k_ref, r, b, dil, M), NT, preferred_element_type=F32)

        units = [(r, b) for r in range(dil) for b in range(nb)]
        ahead = scores(*units[0])
        for i, (r, b) in enumerate(units):
            qr, s = ahead
            if i + 1 < len(units):
                ahead = scores(*units[i + 1])
            runs = _runs(r, b, dil, M)
            dob = _load(do_ref, runs).astype(BF16)
            dpr = lax.dot_general(dob, _keys(v_ref, r, b, dil, M), NT, preferred_element_type=F32)
            s = jnp.where(edge if b == 0 else inner, s, NEG_INF)
            p = jnp.exp2(s * (ATTN_SCALE * LOG2_E) - _lane_column(_load(l_ref, runs), lane, h) * LOG2_E)
            ds = (p * (dpr - _lane_column(_load(dl_ref, runs), lane, h)) * ATTN_SCALE).astype(BF16)
            dq = jnp.dot(ds, _keys(k_ref, r, b, dil, M), preferred_element_type=F32)
            _store(dq_scr, runs, _rope_bwd(dq, _load(c_ref, runs), _load(s_ref, runs)))
            dkc = lax.dot_general(ds, qr, TN, preferred_element_type=F32)
            dvc = lax.dot_general(p.astype(BF16), dob, TN, preferred_element_type=F32)
            if b > 0:
                both = _runs(r, b, dil, M, back=1)
                _store(dk_scr, both, dkc, add=True)
                _store(dv_scr, both, dvc, add=True)
            else:
                n = runs[0][1]
                own = jnp.concatenate([dkc[(2 * j + 1) * n:(2 * j + 2) * n] for j in range(len(runs))], axis=0)
                _store(dk_scr, runs, own, add=True)
                own = jnp.concatenate([dvc[(2 * j + 1) * n:(2 * j + 2) * n] for j in range(len(runs))], axis=0)
                _store(dv_scr, runs, own, add=True)
        dq_ref[...] = dq_scr[...].astype(BF16)
        dk_ref[...] = dk_scr[...].astype(BF16)
        dv_ref[...] = dv_scr[...].astype(BF16)

    col = (S, HEAD_DIM)
    whole = pl.BlockSpec(col, lambda h: (0, 0))
    head = pl.BlockSpec(col, lambda h: (0, h))
    return pl.pallas_call(
        body, name=name, grid=(H,),
        in_specs=[pl.BlockSpec(col, lambda h: (0, group * H + h)), head, pl.BlockSpec(col, lambda h: (0, H + h)),
                  whole, whole, head, whole, whole, ANY],
        out_specs=[pl.BlockSpec(col, lambda h: (0, group * H + h)), head, head],
        out_shape=[jax.ShapeDtypeStruct(dproj.shape, BF16), jax.ShapeDtypeStruct((S, E), BF16),
                   jax.ShapeDtypeStruct((S, E), BF16)],
        scratch_shapes=[pltpu.VMEM(col, F32)] * 3,
        input_output_aliases={8: 0},
        compiler_params=_params("parallel"))(proj, kr, kvp, cos2, sin2, do, lse, dlt, dproj)


def _group_weights(l_refs, h):
    ls = [r[:, h:h + 1] for r in l_refs]
    mx = jnp.maximum(jnp.maximum(ls[0], ls[1]), ls[2])
    es = [jnp.exp(l - mx) for l in ls]
    inv = 1.0 / (es[0] + es[1] + es[2])
    return [e * inv for e in es]


def _merge_out(outs, lses, proj, w_out, res, name):
    S, E = outs[0].shape
    D = res.shape[1]
    tm = min(256, S)
    gate_col = proj.shape[1] // E - 1

    def body(o0, o1, o2, l0, l1, l2, gt_ref, w_ref, r_ref, x_ref, z_ref):
        for h in range(E // HEAD_DIM):
            hs = slice(h * HEAD_DIM, (h + 1) * HEAD_DIM)
            a = _group_weights((l0, l1, l2), h)
            merged = a[0] * o0[:, hs] + a[1] * o1[:, hs] + a[2] * o2[:, hs]
            gate = gt_ref[:, hs]
            z_ref[:, hs] = (merged * (gate * _sigmoid(gate))).astype(BF16)
        x_ref[...] = r_ref[...] + jnp.dot(z_ref[...], w_ref[...], preferred_element_type=F32)

    wide = pl.BlockSpec((tm, E), lambda i: (i, 0))
    thin = pl.BlockSpec((tm, HEAD_DIM), lambda i: (i, 0))
    return pl.pallas_call(
        body, name=name, grid=(S // tm,),
        in_specs=[wide, wide, wide, thin, thin, thin, pl.BlockSpec((tm, E), lambda i: (i, gate_col)),
                  pl.BlockSpec((E, D), lambda i: (0, 0)), pl.BlockSpec((tm, D), lambda i: (i, 0))],
        out_specs=[pl.BlockSpec((tm, D), lambda i: (i, 0)), wide],
        out_shape=[jax.ShapeDtypeStruct((S, D), F32), jax.ShapeDtypeStruct((S, E), BF16)],
        compiler_params=_params("parallel"))(*outs, *lses, proj, w_out.reshape(E, D), res)


def _merge_bwd(dx, w_out, outs, lses, proj, name):
    S, E = outs[0].shape
    D = dx.shape[1]
    tm = min(256, S)
    gate_col = proj.shape[1] // E - 1

    def body(dx_ref, w_ref, o0, o1, o2, l0, l1, l2, gt_ref, d0, d1, d2, t0, t1, t2, dg_ref):
        o_refs, d_refs, t_refs = (o0, o1, o2), (d0, d1, d2), (t0, t1, t2)
        lane = lax.broadcasted_iota(jnp.int32, (tm, HEAD_DIM), 1)
        tiles = [jnp.zeros((tm, HEAD_DIM), F32) for _ in range(3)]
        dxb = dx_ref[...].astype(BF16)
        for h in range(E // HEAD_DIM):
            hs = slice(h * HEAD_DIM, (h + 1) * HEAD_DIM)
            a = _group_weights((l0, l1, l2), h)
            merged = a[0] * o0[:, hs] + a[1] * o1[:, hs] + a[2] * o2[:, hs]
            gate = gt_ref[:, hs]
            sg = _sigmoid(gate)
            dzh = lax.dot_general(dxb, w_ref[hs, :], NT, preferred_element_type=F32)
            dmerged = dzh * (gate * sg)
            dg_ref[:, hs] = (dzh * merged * (sg * (1.0 + gate * (1.0 - sg)))).astype(BF16)
            tot = jnp.sum(dmerged * merged, axis=-1, keepdims=True)
            for g in range(3):
                d_refs[g][:, hs] = a[g] * dmerged
                tiles[g] = jnp.where(lane == h, a[g] * tot, tiles[g])
        for g in range(3):
            t_refs[g][...] = tiles[g]

    wide = pl.BlockSpec((tm, E), lambda i: (i, 0))
    thin = pl.BlockSpec((tm, HEAD_DIM), lambda i: (i, 0))
    res = pl.pallas_call(
        body, name=name, grid=(S // tm,),
        in_specs=[pl.BlockSpec((tm, D), lambda i: (i, 0)), pl.BlockSpec((E, D), lambda i: (0, 0)),
                  wide, wide, wide, thin, thin, thin, pl.BlockSpec((tm, E), lambda i: (i, gate_col))],
        out_specs=[wide, wide, wide, thin, thin, thin, pl.BlockSpec((tm, E), lambda i: (i, gate_col))],
        out_shape=[jax.ShapeDtypeStruct((S, E), F32)] * 3 + [jax.ShapeDtypeStruct((S, HEAD_DIM), F32)] * 3
        + [jax.ShapeDtypeStruct(proj.shape, BF16)],
        compiler_params=_params("parallel"))(dx, w_out.reshape(E, D), *outs, *lses, proj)
    return res[0:3], res[3:6], res[6]


def _kv_bwd(dks, dvs, cos2, sin2, name):
    S, E = dks[0].shape
    n = len(dks)
    tm = min(256, S)

    def body(*refs):
        dk_refs, dv_refs = refs[0:n], refs[n:2 * n]
        c_ref, s_ref, o_ref = refs[2 * n:]
        cosv, sinv = c_ref[...], s_ref[...]
        for h in range(E // HEAD_DIM):
            hs = slice(h * HEAD_DIM, (h + 1) * HEAD_DIM)
            dk = dk_refs[0][:, hs].astype(F32)
            dv = dv_refs[0][:, hs].astype(F32)
            for j in range(1, n):
                dk = dk + dk_refs[j][:, hs].astype(F32)
                dv = dv + dv_refs[j][:, hs].astype(F32)
            o_ref[:, hs] = _rope_bwd(dk, cosv, sinv).astype(BF16)
            o_ref[:, E + h * HEAD_DIM:E + (h + 1) * HEAD_DIM] = dv.astype(BF16)

    wide = pl.BlockSpec((tm, E), lambda i: (i, 0))
    thin = pl.BlockSpec((tm, HEAD_DIM), lambda i: (i, 0))
    return pl.pallas_call(
        body, name=name, grid=(S // tm,),
        in_specs=[wide] * (2 * n) + [thin, thin],
        out_specs=pl.BlockSpec((tm, 2 * E), lambda i: (i, 0)),
        out_shape=jax.ShapeDtypeStruct((S, 2 * E), BF16),
        compiler_params=_params("parallel"))(*dks, *dvs, cos2, sin2)


def _final_norm_loss(x, target, gain, name):
    S, D = x.shape
    tm = min(256, S)

    def body(x_ref, t_ref, g_ref, loss_ref, dx_ref, dg_ref):
        @pl.when(pl.program_id(0) == 0)
        def _():
            loss_ref[...] = jnp.zeros_like(loss_ref)
            dg_ref[...] = jnp.zeros_like(dg_ref)
        xf = x_ref[...]
        inv = lax.rsqrt(jnp.mean(xf * xf, axis=-1, keepdims=True) + RMS_EPS)
        xhat = xf * inv
        g = g_ref[...]
        err = xhat * g - t_ref[...]
        loss_ref[...] += 0.5 * jnp.sum(jnp.mean(err * err, axis=-1, keepdims=True), axis=0, keepdims=True)
        dy = err / D
        dg_ref[...] += jnp.sum(dy * xhat, axis=0, keepdims=True)
        dxh = dy * g
        dx_ref[...] = inv * (dxh - xhat * jnp.mean(dxh * xhat, axis=-1, keepdims=True))

    tile = pl.BlockSpec((tm, D), lambda i: (i, 0))
    vec = pl.BlockSpec((1, D), lambda i: (0, 0))
    return pl.pallas_call(
        body, name=name, grid=(S // tm,),
        in_specs=[tile, tile, vec],
        out_specs=[pl.BlockSpec((1, 1), lambda i: (0, 0)), tile, vec],
        out_shape=[jax.ShapeDtypeStruct((1, 1), F32), jax.ShapeDtypeStruct((S, D), F32),
                   jax.ShapeDtypeStruct((1, D), F32)],
        compiler_params=_params("arbitrary"))(x, target, gain)


def _adamw_math(g, w, m, v):
    m = ADAM_B1 * m + (1.0 - ADAM_B1) * g
    v = ADAM_B2 * v + (1.0 - ADAM_B2) * (g * g)
    m_hat = m / (1.0 - ADAM_B1 ** ADAM_STEP)
    v_hat = v / (1.0 - ADAM_B2 ** ADAM_STEP)
    delta = -ADAM_LR * (m_hat / (jnp.sqrt(v_hat) + ADAM_EPS) + ADAM_WD * w)
    return delta, m, v


def _adamw_rows(g, w, m, v, name):
    def body(g_ref, w_ref, m_ref, v_ref, d_ref, mo_ref, vo_ref):
        d_ref[...], mo_ref[...], vo_ref[...] = _adamw_math(g_ref[...], w_ref[...], m_ref[...], v_ref[...])

    whole = pl.BlockSpec(memory_space=pltpu.VMEM)
    return pl.pallas_call(
        body, name=name, in_specs=[whole] * 4, out_specs=[whole] * 3,
        out_shape=[jax.ShapeDtypeStruct(g.shape, F32)] * 3)(g, w, m, v)


def _adamw_blocks(own, others, w, m, v, layer, earlier, name):
    L, R, C = w.shape
    n = others.shape[0]
    tr = R
    while tr * C > 128 * 1024 and tr % 16 == 0:
        tr //= 2

    def body(o_ref, p_ref, w_ref, m_ref, v_ref, *rest):
        g_ref, d_ref, mo_ref, vo_ref = rest[-4:]
        g = o_ref[...].astype(F32)
        for j in range(n):
            g = g + p_ref[j].astype(F32)
        g_ref[...] = g
        d_ref[...], mo_ref[...], vo_ref[...] = _adamw_math(g, w_ref[...], m_ref[...], v_ref[...])

    tile = pl.BlockSpec((None, tr, C), lambda i: (layer, i, 0))
    kept = [] if earlier is None else list(earlier)
    return pl.pallas_call(
        body, name=name, grid=(R // tr,),
        in_specs=[pl.BlockSpec((tr, C), lambda i: (i, 0)), pl.BlockSpec((n, tr, C), lambda i: (0, i, 0)),
                  tile, tile, tile] + [ANY] * len(kept),
        out_specs=[tile] * 4,
        out_shape=[jax.ShapeDtypeStruct((L, R, C), F32)] * 4,
        input_output_aliases={5 + j: j for j in range(len(kept))},
        compiler_params=_params("parallel"))(own, others, w, m, v, *kept)


def _position():
    return lax.axis_index("x"), lax.axis_index("y"), lax.axis_index("c")


def _block_index(px, py, pc):
    return 4 * px + 2 * py + pc


def _all_gather(shards, name):
    n = len(shards)

    def body(*refs):
        ins, outs = refs[0:n], refs[n:2 * n]
        send_sems, recv_sems, local_sems = refs[2 * n:]
        x, y, c = _position()
        me, sibling = (x, y, c), (x, y, 1 - c)
        chips = [(1 - x, y), (x, 1 - y), (1 - x, 1 - y)]

        def copy(a, k, block, to, src=None):
            rows = outs[a].at[_block_index(*block)]
            return pltpu.make_async_remote_copy(
                src_ref=rows if src is None else src, dst_ref=rows,
                send_sem=send_sems.at[a, k], recv_sem=recv_sems.at[a, k], device_id=to, device_id_type=MESH)

        mine, first, passed = [], [], []
        for a in range(n):
            cp = pltpu.make_async_copy(ins[a], outs[a].at[_block_index(*me)], local_sems.at[a])
            cp.start()
            mine.append(cp)
            first.append(copy(a, 0, me, sibling, src=ins[a]))
            first += [copy(a, 1 + j, me, (*chip, c), src=ins[a]) for j, chip in enumerate(chips)]
        for cp in first:
            cp.start()
        for j, chip in enumerate(chips):
            for a in range(n):
                copy(a, 1 + j, (*chip, c), me).wait_recv()
                fwd = copy(a, 4 + j, (*chip, c), sibling)
                fwd.start()
                passed.append(fwd)
        for a in range(n):
            copy(a, 0, sibling, me).wait_recv()
            for j, chip in enumerate(chips):
                copy(a, 4 + j, (*chip, 1 - c), me).wait_recv()
        for cp in first + passed:
            cp.wait_send()
        for cp in mine:
            cp.wait()

    return pl.pallas_call(
        body, name=name,
        in_specs=[ANY] * n, out_specs=[ANY] * n,
        out_shape=[jax.ShapeDtypeStruct((N_DEV,) + s.shape, s.dtype) for s in shards],
        scratch_shapes=[pltpu.SemaphoreType.DMA((n, 7)), pltpu.SemaphoreType.DMA((n, 7)),
                        pltpu.SemaphoreType.DMA((n,))],
    )(*shards)


def _peers(x, y, c):
    return [((1 - x) if k & 4 else x, (1 - y) if k & 2 else y, (1 - c) if k & 1 else c) for k in range(1, N_DEV)]


HBM = pl.BlockSpec(memory_space=pltpu.HBM)
SEM = pl.BlockSpec(memory_space=pltpu.SEMAPHORE)
EFFECT = pltpu.SideEffectType.DATAFLOW_SIDE_EFFECTING


ALL_PEERS = (1, 2, 3, 4, 5, 6, 7)
SIBLING_AND_SAME_CORES = (1, 2, 4, 6)


def _push_copy(src_refs, land_refs, send_sems, recv_sems, a, i, relations, per_peer, by_sender, arriving):
    peer = _peers(*_position())[relations[i] - 1]
    me_idx, p_idx = _block_index(*_position()), _block_index(*peer)
    src = src_refs[a].at[p_idx] if per_peer else src_refs[a]
    if by_sender:
        slot = p_idx if arriving else me_idx
    else:
        slot = relations[i] - 1
    sem = a * len(relations) + i
    return pltpu.make_async_remote_copy(
        src_ref=src, dst_ref=land_refs[a].at[slot], send_sem=send_sems.at[sem], recv_sem=recv_sems.at[sem],
        device_id=peer, device_id_type=MESH)


def _push_start(srcs, lands, relations, per_peer, by_sender, after, name):
    n = len(srcs)

    def body(*refs):
        src_refs, land_refs = refs[0:n], refs[n:2 * n]
        send_sems, recv_sems = refs[2 * n + 1], refs[2 * n + 2]
        token = refs[-1]
        for a in range(n):
            for i in range(len(relations)):
                _push_copy(src_refs, land_refs, send_sems, recv_sems, a, i, relations, per_peer, by_sender, False).start()
        token[...] = jnp.zeros_like(token)

    args = [pltpu.with_memory_space_constraint(t, pltpu.HBM) for t in list(srcs) + list(lands)]
    res = pl.pallas_call(
        body, name=name,
        in_specs=[HBM] * (2 * n) + [ANY],
        out_specs=[SEM, SEM] + [HBM] * (2 * n) + [pl.BlockSpec(memory_space=pltpu.VMEM)],
        out_shape=[pltpu.SemaphoreType.DMA((n * len(relations),)), pltpu.SemaphoreType.DMA((n * len(relations),))]
        + [pltpu.HBM(t.shape, t.dtype) for t in args] + [jax.ShapeDtypeStruct((8, 128), F32)],
        input_output_aliases={i: 2 + i for i in range(2 * n)},
        compiler_params=pltpu.CompilerParams(has_side_effects=EFFECT))(*args, after)
    return res[0], res[1], res[2:2 + n], res[2 + n:2 + 2 * n], res[-1]


def _push_wait(started, relations, per_peer, by_sender, after, name):
    send_sems, recv_sems, srcs, lands, _ = started
    n = len(srcs)

    def body(*refs):
        src_refs, land_refs = refs[0:n], refs[n:2 * n]
        send_s, recv_s = refs[2 * n], refs[2 * n + 1]
        for a in range(n):
            for i in range(len(relations)):
                _push_copy(src_refs, land_refs, send_s, recv_s, a, i, relations, per_peer, by_sender, False).wait_send()
                _push_copy(src_refs, land_refs, send_s, recv_s, a, i, relations, per_peer, by_sender, True).wait_recv()

    res = pl.pallas_call(
        body, name=name,
        in_specs=[HBM] * (2 * n) + [SEM, SEM, ANY],
        out_specs=[HBM] * (2 * n),
        out_shape=[pltpu.HBM(t.shape, t.dtype) for t in list(srcs) + list(lands)],
        input_output_aliases={i: i for i in range(2 * n)},
        compiler_params=pltpu.CompilerParams(has_side_effects=EFFECT))(*srcs, *lands, send_sems, recv_sems, after)
    return res[0:n], res[n:2 * n]


def _pass_on(lands, name):
    n = len(lands)

    def body(*refs):
        outs = refs[n:2 * n]
        send_sems, recv_sems = refs[2 * n:]
        x, y, c = _position()
        chips = [(1 - x, y), (x, 1 - y), (1 - x, 1 - y)]
        copies = []
        for a in range(n):
            for j, chip in enumerate(chips):
                def copy(core):
                    rows = outs[a].at[_block_index(*chip, core)]
                    return pltpu.make_async_remote_copy(
                        src_ref=rows, dst_ref=rows, send_sem=send_sems.at[a, j], recv_sem=recv_sems.at[a, j],
                        device_id=(x, y, 1 - c), device_id_type=MESH)
                copy(c).start()
                copies.append((copy(c), copy(1 - c)))
        for sending, arriving in copies:
            sending.wait_send()
            arriving.wait_recv()

    return pl.pallas_call(
        body, name=name,
        in_specs=[ANY] * n, out_specs=[ANY] * n,
        out_shape=[jax.ShapeDtypeStruct(t.shape, t.dtype) for t in lands],
        input_output_aliases={a: a for a in range(n)},
        scratch_shapes=[pltpu.SemaphoreType.DMA((n, 3)), pltpu.SemaphoreType.DMA((n, 3))],
    )(*lands)


def _all_reduce_rows(v, name):
    R, D = v.shape

    def body(v_ref, o_ref, buf_ref, send_sems, recv_sems):
        x, y, c = _position()
        me_idx = _block_index(x, y, c)
        buf_ref[me_idx] = v_ref[...]
        copies = []
        for k in range(1, N_DEV):
            px = (1 - x) if k & 4 else x
            py = (1 - y) if k & 2 else y
            pc = (1 - c) if k & 1 else c
            rc = pltpu.make_async_remote_copy(
                src_ref=v_ref, dst_ref=buf_ref.at[me_idx],
                send_sem=send_sems.at[k - 1], recv_sem=recv_sems.at[k - 1],
                device_id=(px, py, pc), device_id_type=MESH)
            rc.start()
            copies.append((rc, pltpu.make_async_remote_copy(
                src_ref=v_ref, dst_ref=buf_ref.at[_block_index(px, py, pc)],
                send_sem=send_sems.at[k - 1], recv_sem=recv_sems.at[k - 1],
                device_id=(px, py, pc), device_id_type=MESH)))
        for rc, arrival in copies:
            rc.wait_send()
            arrival.wait_recv()
        acc = buf_ref[0]
        for j in range(1, N_DEV):
            acc = acc + buf_ref[j]
        o_ref[...] = acc

    return pl.pallas_call(
        body, name=name,
        in_specs=[pl.BlockSpec(memory_space=pltpu.VMEM)],
        out_specs=pl.BlockSpec(memory_space=pltpu.VMEM),
        out_shape=jax.ShapeDtypeStruct((R, D), F32),
        scratch_shapes=[pltpu.VMEM((N_DEV, R, D), F32),
                        pltpu.SemaphoreType.DMA((7,)), pltpu.SemaphoreType.DMA((7,))],
    )(v)


def _rope_tables(S):
    inv_freq = 1.0 / (ROPE_THETA ** (jnp.arange(0, HEAD_DIM, 2, dtype=F32) / HEAD_DIM))
    ang = jnp.arange(S, dtype=F32)[:, None] * inv_freq[None, :]
    cos, sin = jnp.cos(ang), jnp.sin(ang)
    return jnp.concatenate([cos, cos], axis=1), jnp.concatenate([-sin, sin], axis=1)


def _local_step(xs, target, vecs, n_a, n_b, get_weights, put_grads):
    S, D = xs.shape
    E = D
    cos2, sin2 = _rope_tables(S)
    ts = min(1024, S)

    def col_blocks(w):
        cb = w.shape[2]
        tn = min(cb, 1024)
        per = cb // tn
        return (None, D, tn), (lambda i, j: (j // per, 0, j % per)), N_DEV * per, tn

    def grad_in(hn, dproj, cb, name):
        return _matmul_tn(hn, dproj, (ts, D), lambda j, s: (s, 0), (ts, cb), lambda j, s: (s, j),
                          (N_DEV, D, cb), (None, D, cb), lambda j, s: (j, 0, 0), (D, cb), N_DEV, name)

    def grad_out(z, dx, name, col=0):
        rows = z.shape[1]
        ta = min(1024, rows)
        out = _matmul_tn(z, dx, (ts, ta), lambda a, s: (s, a), (ts, E), lambda a, s: (s, col),
                         (rows, E), (ta, E), lambda a, s: (a, 0), (ta, E), rows // ta, name)
        return out.reshape(N_DEV, rows // N_DEV, E)

    x = xs
    a_saved, b_saved = [], []
    for i in range(n_a):
        w = get_weights(f"a{i}", x)
        blk, idx, nblocks, tn = col_blocks(w["w_in"])
        proj, hn = _norm_matmul(x, vecs["norm_a"][i:i + 1], w["w_in"], blk, idx, nblocks, tn, f"a{i}_in")
        z = _a_mid_fwd(proj, w["w_grp"], vecs["scale_a"][i:i + 1], f"a{i}_mid")
        x_next = _matmul_res(z, w["w_out"], x, f"a{i}_out")
        a_saved.append((x, hn, proj, z, w))
        x = x_next
    w_kv = get_weights("kv", x)["w_kv"]
    x = x_kv = _class_order(x, False, "kv_x_order")
    target = _class_order(target, False, "target_order")
    cos2, sin2 = _class_order(cos2, False, "cos_order"), _class_order(sin2, False, "sin_order")
    tn = min(E, 1024)
    kvp, hn_kv = _norm_matmul(x, vecs["norm_kv"], w_kv, (D, tn), lambda i, j: (0, j), 2 * E // tn, tn, "kv_in")
    kr = _rope_k(kvp, cos2, sin2, "kv_rope")
    after = kr
    for i in range(n_b):
        w = get_weights(f"b{i}", after)
        blk, idx, nblocks, tn = col_blocks(w["w_in"])
        proj, hn = _norm_matmul(x, vecs["norm_b"][i:i + 1], w["w_in"], blk, idx, nblocks, tn, f"b{i}_in")
        outs, lses = [], []
        for g, dil in enumerate(DILATIONS):
            o, l = _attn_fwd(proj, kr, kvp, cos2, sin2, g, dil, f"b{i}_attn{g}")
            outs.append(o)
            lses.append(l)
        x_next, z = _merge_out(outs, lses, proj, w["w_out"], x, f"b{i}_out")
        b_saved.append((x, hn, proj, z, outs, lses, w))
        x = x_next
        after = x
    loss, dx, dg_f = _final_norm_loss(x, target, vecs["norm_f"], "final")

    vec = {"norm_a": [None] * n_a, "scale_a": [None] * n_a, "norm_b": [None] * n_b, "norm_f": [dg_f]}
    dks, dvs = [], []
    for i in reversed(range(n_b)):
        x_in, hn, proj, z, outs, lses, w = b_saved[i]
        dw_out = grad_out(z, dx, f"b{i}_dwout")
        dos, dlts, dproj = _merge_bwd(dx, w["w_out"], outs, lses, proj, f"b{i}_dmerge")
        for g, dil in enumerate(DILATIONS):
            dproj, dk, dv = _attn_bwd(proj, kr, kvp, cos2, sin2, dos[g], lses[g], dlts[g], dproj, g, dil,
                                      f"b{i}_dattn{g}")
            dks.append(dk)
            dvs.append(dv)
        cb = w["w_in"].shape[2]
        tok = put_grads(f"b{i}", {"w_out": dw_out, "w_in": grad_in(hn, dproj, cb, f"b{i}_dwin")})
        dx, vec["norm_b"][i] = _matmul_nt_dnorm(dproj, w["w_in"], (None, D, cb), lambda t, j: (j, 0, 0), N_DEV, cb, x_in,
                                                vecs["norm_b"][i:i + 1] + tok[0:1, 0:1], dx, f"b{i}_dhn")

    dkv = _kv_bwd(dks, dvs, cos2, sin2, "kv_dsum")
    tok = put_grads("kv", {"w_k": grad_out(hn_kv, dkv, "kv_dwk", 0), "w_v": grad_out(hn_kv, dkv, "kv_dwv", 1)})
    tk = min(E, 1024)
    dx, dg_kv = _matmul_nt_dnorm(dkv, w_kv, (D, tk), lambda t, j: (0, j), 2 * E // tk, tk, x_kv,
                                 vecs["norm_kv"] + tok[0:1, 0:1], dx, "kv_dhn")
    vec["norm_kv"] = [dg_kv]
    dx = _class_order(dx, True, "kv_dx_order")

    for i in reversed(range(n_a)):
        x_in, hn, proj, z, w = a_saved[i]
        dw_out = grad_out(z, dx, f"a{i}_dwout")
        dproj, dwg, dsc = _a_mid_bwd(dx, w["w_out"], proj, w["w_grp"], vecs["scale_a"][i:i + 1], f"a{i}_dmid")
        n_grp, gc, _ = dwg.shape
        dwg = dwg.reshape(n_grp, N_DEV, gc // N_DEV, gc).transpose(1, 0, 2, 3).astype(BF16)
        vec["scale_a"][i] = dsc
        cb = w["w_in"].shape[2]
        tok = put_grads(f"a{i}", {"w_out": dw_out, "w_grp": dwg, "w_in": grad_in(hn, dproj, cb, f"a{i}_dwin")})
        dx, vec["norm_a"][i] = _matmul_nt_dnorm(dproj, w["w_in"], (None, D, cb), lambda t, j: (j, 0, 0), N_DEV, cb, x_in,
                                                vecs["norm_a"][i:i + 1] + tok[0:1, 0:1], dx, f"a{i}_dhn")

    return loss, dx, {k: jnp.concatenate(v, axis=0) for k, v in vec.items()}


VECTORS = ("norm_a", "scale_a", "norm_kv", "norm_b", "norm_f")
SHARDED_VECTORS = ("norm_a", "scale_a")
GROUPS = {
    "a0": (("w_in", "w_in_a", 0), ("w_grp", "w_grp_a", 0), ("w_out", "w_out_a", 0)),
    "a1": (("w_in", "w_in_a", 1), ("w_grp", "w_grp_a", 1), ("w_out", "w_out_a", 1)),
    "kv": (("w_k", "w_k", None), ("w_v", "w_v", None)),
    "b0": (("w_in", "w_in_b", 0), ("w_out", "w_out_b", 0)),
    "b1": (("w_in", "w_in_b", 1), ("w_out", "w_out_b", 1)),
}
PREFETCHED = ("a1", "kv", "b0", "b1")


def kernel(x, norm_a, w_in_a, w_grp_a, scale_a, w_out_a, norm_kv, w_k, w_v, norm_b, w_in_b, w_out_b, norm_f, loss_target, m_norm_a, m_w_in_a, m_w_grp_a, m_scale_a, m_w_out_a, m_norm_kv, m_w_k, m_w_v, m_norm_b, m_w_in_b, m_w_out_b, m_norm_f, v_norm_a, v_w_in_a, v_w_grp_a, v_scale_a, v_w_out_a, v_norm_kv, v_w_k, v_w_v, v_norm_b, v_w_in_b, v_w_out_b, v_norm_f):
    w = dict(norm_a=norm_a, w_in_a=w_in_a, w_grp_a=w_grp_a, scale_a=scale_a, w_out_a=w_out_a, norm_kv=norm_kv,
             w_k=w_k, w_v=w_v, norm_b=norm_b, w_in_b=w_in_b, w_out_b=w_out_b, norm_f=norm_f)
    m = dict(norm_a=m_norm_a, w_in_a=m_w_in_a, w_grp_a=m_w_grp_a, scale_a=m_scale_a, w_out_a=m_w_out_a,
             norm_kv=m_norm_kv, w_k=m_w_k, w_v=m_w_v, norm_b=m_norm_b, w_in_b=m_w_in_b, w_out_b=m_w_out_b,
             norm_f=m_norm_f)
    v = dict(norm_a=v_norm_a, w_in_a=v_w_in_a, w_grp_a=v_w_grp_a, scale_a=v_scale_a, w_out_a=v_w_out_a,
             norm_kv=v_norm_kv, w_k=v_w_k, w_v=v_w_v, norm_b=v_norm_b, w_in_b=v_w_in_b, w_out_b=v_w_out_b,
             norm_f=v_norm_f)
    D = x.shape[2]
    me = _block_index(*_position())

    def shard(group):
        return [w[p].astype(BF16) if layer is None else w[p][layer].astype(BF16) for _, p, layer in GROUPS[group]]

    def as_weights(group, gathered):
        out = dict(zip([n for n, _, _ in GROUPS[group]], gathered))
        if "w_grp" in out:
            g = out["w_grp"]
            out["w_grp"] = g.transpose(1, 0, 2, 3).reshape(g.shape[1], g.shape[3], g.shape[3])
        if "w_k" in out:
            out = {"w_kv": jnp.concatenate([out["w_k"].reshape(D, D), out["w_v"].reshape(D, D)], axis=1)}
        return out

    first = _all_gather(shard("a0") + [w[k] for k in SHARDED_VECTORS], "gather_first")
    n_first = len(GROUPS["a0"])
    vecs = {k: g.transpose(1, 0, 2).reshape(w[k].shape[0], D) for k, g in zip(SHARDED_VECTORS, first[n_first:])}
    vecs.update(norm_kv=norm_kv[None, :], norm_b=norm_b, norm_f=norm_f[None, :])
    srcs, lands = [], []
    for group in PREFETCHED:
        for s in shard(group):
            srcs.append(s)
            lands.append(lax.dynamic_update_index_in_dim(lax.empty((N_DEV,) + s.shape, s.dtype), s[None], me, 0))
    inflight, at = {}, 0
    token = None
    for group in PREFETCHED:
        n = len(GROUPS[group])
        inflight[group] = _push_start(srcs[at:at + n], lands[at:at + n], SIBLING_AND_SAME_CORES, False, True,
                                      first[0] if token is None else token, f"gather_{group}_start")
        token = inflight[group][4]
        at += n
    vecs["norm_a"] = vecs["norm_a"] + token[0:1, 0:1]

    def get_weights(group, after):
        if group == "a0":
            return as_weights(group, first[0:n_first])
        half = _push_wait(inflight[group], SIBLING_AND_SAME_CORES, False, True, after, f"gather_{group}_wait")[1]
        return as_weights(group, _pass_on(half, f"gather_{group}_pass"))

    sent = {}

    def put_grads(group, grads):
        blocks = [grads[n] for n, _, _ in GROUPS[group]]
        lands = [lax.empty((N_DEV - 1,) + b.shape[1:], b.dtype) for b in blocks]
        sent[group] = _push_start(blocks, lands, ALL_PEERS, True, False, jnp.zeros((8, 128), F32),
                                  f"exchange_{group}_start")
        return sent[group][4]

    loss, dx, vec = _local_step(x[0], loss_target[0], vecs, w_in_a.shape[0], w_in_b.shape[0], get_weights, put_grads)
    rows = _all_reduce_rows(jnp.concatenate([vec[k] for k in VECTORS], axis=0), "reduce_vectors")

    out = {}
    after = dx
    for group in sent:
        blocks, arrived = _push_wait(sent[group], ALL_PEERS, True, False, after, f"exchange_{group}_wait")
        for (_, p, layer), blk, got in zip(GROUPS[group], blocks, arrived):
            cols = w[p].shape[-1]
            own = lax.dynamic_index_in_dim(blk, me, 0, keepdims=False).reshape(-1, cols)
            n_layers = 1 if layer is None else w[p].shape[0]
            stacked = lambda t: t.reshape(n_layers, -1, cols)
            res = _adamw_blocks(own, got.reshape(N_DEV - 1, -1, cols), stacked(w[p]), stacked(m[p]), stacked(v[p]),
                                0 if layer is None else layer, out.get(p), f"adamw_{group}_{p}")
            out[p] = res
            after = res[1]
    out = {p: [r.reshape(w[p].shape) for r in res] for p, res in out.items()}
    start = 0
    for k in VECTORS:
        n_rows = vec[k].shape[0]
        g = rows[start:start + n_rows]
        start += n_rows
        if k in SHARDED_VECTORS:
            g = lax.dynamic_slice_in_dim(g, me * (D // N_DEV), D // N_DEV, axis=1)
        res = _adamw_rows(g, w[k].reshape(g.shape), m[k].reshape(g.shape), v[k].reshape(g.shape), f"adamw_{k}")
        out[k] = [r.reshape(w[k].shape) for r in [g] + list(res)]

    names = ("norm_a", "w_in_a", "w_grp_a", "scale_a", "w_out_a", "norm_kv", "w_k", "w_v", "norm_b", "w_in_b",
             "w_out_b", "norm_f")
    total = lax.psum(loss[0, 0], ("x", "y", "c"))
    return (total, dx[None], *[out[k][0] for k in names], *[out[k][1] for k in names],
            *[out[k][2] for k in names], *[out[k][3] for k in names])
```

```python
import math

import jax
import jax.numpy as jnp
from jax import lax
from jax.experimental import pallas as pl
from jax.experimental.pallas import tpu as pltpu

F32 = jnp.float32
BF16 = jnp.bfloat16

N_DEV = 8
MESH = pl.DeviceIdType.MESH
RMS_EPS = 1e-6
HEAD_DIM = 128
HALF_HEAD = HEAD_DIM // 2
BAND = 128
DILATIONS = (1, 4, 16)
POOL_WINDOWS = (2, 4, 8, 16)
POOL_HALO = 16
ROPE_THETA = 10000.0
NEG_INF = -1e30
ATTN_SCALE = 1.0 / math.sqrt(HEAD_DIM)
LOG2_E = math.log2(math.e)
ADAM_LR, ADAM_B1, ADAM_B2, ADAM_EPS, ADAM_WD, ADAM_STEP = 0.001, 0.9, 0.999, 1e-08, 0.01, 10
VMEM_LIMIT_BYTES = 56 * 1024 * 1024
ANY = pl.BlockSpec(memory_space=pl.ANY)
NT = (((1,), (1,)), ((), ()))
TN = (((0,), (0,)), ((), ()))


def _params(*semantics):
    return pltpu.CompilerParams(dimension_semantics=semantics, vmem_limit_bytes=VMEM_LIMIT_BYTES)


def _sigmoid(t):
    return 1.0 / (1.0 + jnp.exp(-t))


def _rope(t, cos2, sin2):
    return t * cos2 + pltpu.roll(t, HALF_HEAD, 1) * sin2


def _rope_bwd(dt, cos2, sin2):
    return dt * cos2 + pltpu.roll(dt * sin2, HALF_HEAD, 1)


def _norm_matmul(x, gain, w, w_block, w_index, n_col_blocks, tn, name):
    S, D = x.shape
    tm = min(1024, S)

    def body(x_ref, g_ref, w_ref, o_ref, hn_ref, hs_ref):
        @pl.when(pl.program_id(1) == 0)
        def _():
            xf = x_ref[...]
            inv = lax.rsqrt(jnp.mean(xf * xf, axis=-1, keepdims=True) + RMS_EPS)
            hb = ((xf * inv) * g_ref[...]).astype(BF16)
            hs_ref[...] = hb
            hn_ref[...] = hb
        o_ref[...] = jnp.dot(hs_ref[...], w_ref[...], preferred_element_type=F32)

    return pl.pallas_call(
        body, name=name, grid=(S // tm, n_col_blocks),
        in_specs=[pl.BlockSpec((tm, D), lambda i, j: (i, 0)),
                  pl.BlockSpec((1, D), lambda i, j: (0, 0)),
                  pl.BlockSpec(w_block, w_index)],
        out_specs=[pl.BlockSpec((tm, tn), lambda i, j: (i, j)),
                   pl.BlockSpec((tm, D), lambda i, j: (i, 0))],
        out_shape=[jax.ShapeDtypeStruct((S, n_col_blocks * tn), F32), jax.ShapeDtypeStruct((S, D), BF16)],
        scratch_shapes=[pltpu.VMEM((tm, D), BF16)],
        compiler_params=_params("parallel", "arbitrary"))(x, gain, w)


def _matmul_res(a, w, res, name):
    S, K = a.shape
    nd, rb, N = w.shape
    tm = min(512, S)

    def body(a_ref, w_ref, r_ref, o_ref):
        acc = jnp.dot(a_ref[:, 0:rb], w_ref[0], preferred_element_type=F32)
        for k in range(1, nd):
            acc = acc + jnp.dot(a_ref[:, k * rb:(k + 1) * rb], w_ref[k], preferred_element_type=F32)
        o_ref[...] = r_ref[...] + acc

    return pl.pallas_call(
        body, name=name, grid=(S // tm,),
        in_specs=[pl.BlockSpec((tm, K), lambda i: (i, 0)),
                  pl.BlockSpec((nd, rb, N), lambda i: (0, 0, 0)),
                  pl.BlockSpec((tm, N), lambda i: (i, 0))],
        out_specs=pl.BlockSpec((tm, N), lambda i: (i, 0)),
        out_shape=jax.ShapeDtypeStruct((S, N), F32),
        compiler_params=_params("parallel"))(a, w, res)


def _matmul_nt_dnorm(dp, w, w_block, w_index, n_red, tc, x, gain, dres, name):
    S, D = x.shape
    tm = min(1024, S)
    parts = 4
    tp = tm // parts

    def body(d_ref, w_ref, x_ref, g_ref, r_ref, dx_ref, dg_ref, acc_ref):
        i, j = pl.program_id(0), pl.program_id(1)

        @pl.when((i == 0) & (j == 0))
        def _():
            dg_ref[...] = jnp.zeros_like(dg_ref)

        @pl.when(j == 0)
        def _():
            acc_ref[...] = jnp.zeros_like(acc_ref)

        @pl.when(j < n_red)
        def _():
            acc_ref[...] += lax.dot_general(d_ref[...], w_ref[...], NT, preferred_element_type=F32)

        @pl.when(j >= n_red)
        def _():
            xf = x_ref[...]
            inv = lax.rsqrt(jnp.mean(xf * xf, axis=-1, keepdims=True) + RMS_EPS)
            xhat = xf * inv
            dh = acc_ref[pl.ds(pl.multiple_of((j - n_red) * tp, tp), tp), :]
            dg_ref[...] += jnp.sum(dh * xhat, axis=0, keepdims=True)
            dxh = dh * g_ref[...]
            dx_ref[...] = r_ref[...] + inv * (dxh - xhat * jnp.mean(dxh * xhat, axis=-1, keepdims=True))

    last = n_red - 1
    part = pl.BlockSpec((tp, D), lambda i, j: (i * parts + jnp.clip(j - n_red, 0, parts - 1), 0))
    vec = pl.BlockSpec((1, D), lambda i, j: (0, 0))
    return pl.pallas_call(
        body, name=name, grid=(S // tm, n_red + parts),
        in_specs=[pl.BlockSpec((tm, tc), lambda i, j: (i, jnp.minimum(j, last))),
                  pl.BlockSpec(w_block, lambda i, j: w_index(i, jnp.minimum(j, last))), part, vec, part],
        out_specs=[part, vec],
        out_shape=[jax.ShapeDtypeStruct((S, D), F32), jax.ShapeDtypeStruct((1, D), F32)],
        scratch_shapes=[pltpu.VMEM((tm, D), F32)],
        compiler_params=_params("arbitrary", "arbitrary"))(dp, w, x, gain, dres)


def _matmul_tn(a, b, a_block, a_index, b_block, b_index, out_shape, out_block, out_index, acc_shape, n_outer, name):
    S = a.shape[0]
    ts = a_block[0]
    n_tok = S // ts

    def body(a_ref, b_ref, o_ref, acc_ref):
        s = pl.program_id(1)

        @pl.when(s == 0)
        def _():
            acc_ref[...] = jnp.zeros_like(acc_ref)
        acc_ref[...] += lax.dot_general(a_ref[...].astype(BF16), b_ref[...].astype(BF16), TN,
                                        preferred_element_type=F32)

        @pl.when(s == n_tok - 1)
        def _():
            o_ref[...] = acc_ref[...].astype(o_ref.dtype)

    return pl.pallas_call(
        body, name=name, grid=(n_outer, n_tok),
        in_specs=[pl.BlockSpec(a_block, a_index), pl.BlockSpec(b_block, b_index)],
        out_specs=pl.BlockSpec(out_block, out_index),
        out_shape=jax.ShapeDtypeStruct(out_shape, BF16),
        scratch_shapes=[pltpu.VMEM(acc_shape, F32)],
        compiler_params=_params("parallel", "arbitrary"))(a, b)


def _pool(scr_ref, u, row0, tm, E):
    gc = E // len(POOL_WINDOWS)
    t1 = row0 + lax.broadcasted_iota(jnp.int32, (tm, 1), 0) + 1
    out = []
    for g, win in enumerate(POOL_WINDOWS):
        cs = slice(g * gc, (g + 1) * gc)
        acc = u[:, cs]
        for k in range(1, win):
            acc = acc + scr_ref[pl.ds(POOL_HALO - k, tm), cs]
        count = jnp.minimum(t1, win).astype(F32)
        out.append(acc / count - u[:, cs])
    return out


def _a_mid_fwd(proj, wg, scale, name):
    S, E2 = proj.shape
    E = E2 // 2
    gc = E // len(POOL_WINDOWS)
    tm = min(256, S)
    hb = tm // POOL_HALO

    def body(u_ref, uh_ref, gt_ref, wg_ref, sc_ref, z_ref, scr_ref):
        i = pl.program_id(0)
        scr_ref[0:POOL_HALO, :] = jnp.where(i > 0, uh_ref[...], 0.0)
        u = u_ref[...]
        scr_ref[POOL_HALO:POOL_HALO + tm, :] = u
        pooled = _pool(scr_ref, u, i * tm, tm, E)
        for g in range(len(POOL_WINDOWS)):
            cs = slice(g * gc, (g + 1) * gc)
            y = jnp.dot(pooled[g].astype(BF16), wg_ref[g], preferred_element_type=F32) * sc_ref[:, cs]
            gate = gt_ref[:, cs]
            z_ref[:, cs] = (y * (gate * _sigmoid(gate))).astype(BF16)

    return pl.pallas_call(
        body, name=name, grid=(S // tm,),
        in_specs=[pl.BlockSpec((tm, E), lambda i: (i, 0)),
                  pl.BlockSpec((POOL_HALO, E), lambda i: (jnp.maximum(i * hb - 1, 0), 0)),
                  pl.BlockSpec((tm, E), lambda i: (i, 1)),
                  pl.BlockSpec((len(POOL_WINDOWS), gc, gc), lambda i: (0, 0, 0)),
                  pl.BlockSpec((1, E), lambda i: (0, 0))],
        out_specs=pl.BlockSpec((tm, E), lambda i: (i, 0)),
        out_shape=jax.ShapeDtypeStruct((S, E), BF16),
        scratch_shapes=[pltpu.VMEM((POOL_HALO + tm, E), F32)],
        compiler_params=_params("parallel"))(proj, proj, proj, wg, scale)


def _a_mid_bwd(dx, w_out, proj, wg, scale, name):
    S, E2 = proj.shape
    E = E2 // 2
    D = dx.shape[1]
    n_grp = len(POOL_WINDOWS)
    gc = E // n_grp
    tm = min(256, S)
    hb = tm // POOL_HALO
    n_tiles = S // tm
    last_halo = S // POOL_HALO - 1

    def body(dx_ref, dxh_ref, wo_ref, u_ref, uh_ref, gt_ref, gth_ref, wg_ref, sc_ref, dp_ref, dwg_ref, dsc_ref,
             scr_ref, q_ref):
        i = pl.program_id(0)
        dxb = jnp.concatenate([dx_ref[...].astype(BF16), dxh_ref[...].astype(BF16)], axis=0)

        @pl.when(i == 0)
        def _():
            dwg_ref[...] = jnp.zeros_like(dwg_ref)
            dsc_ref[...] = jnp.zeros_like(dsc_ref)

        scr_ref[0:POOL_HALO, :] = jnp.where(i > 0, uh_ref[...], 0.0)
        u = u_ref[...]
        scr_ref[POOL_HALO:POOL_HALO + tm, :] = u
        pooled = _pool(scr_ref, u, i * tm, tm, E)
        t1 = i * tm + lax.broadcasted_iota(jnp.int32, (tm, 1), 0) + 1
        t1h = (i + 1) * tm + lax.broadcasted_iota(jnp.int32, (POOL_HALO, 1), 0) + 1
        not_last = i < n_tiles - 1
        for g, win in enumerate(POOL_WINDOWS):
            cs = slice(g * gc, (g + 1) * gc)
            w = wg_ref[g]
            sc = sc_ref[:, cs]
            pb = pooled[g].astype(BF16)
            ypre = jnp.dot(pb, w, preferred_element_type=F32)
            gate = gt_ref[:, cs]
            sg = _sigmoid(gate)
            silu = gate * sg
            dz_all = lax.dot_general(dxb, wo_ref[cs, :], NT, preferred_element_type=F32)
            dzg = dz_all[0:tm]
            dy = dzg * silu
            dp_ref[:, E + g * gc:E + (g + 1) * gc] = (dzg * (ypre * sc) * (sg * (1.0 + gate * (1.0 - sg)))).astype(BF16)
            dsc_ref[:, cs] += jnp.sum(dy * ypre, axis=0, keepdims=True)
            dyp = (dy * sc).astype(BF16)
            dwg_ref[g] += lax.dot_general(pb, dyp, TN, preferred_element_type=F32)
            dpool = lax.dot_general(dyp, w, NT, preferred_element_type=F32)
            gate_h = gth_ref[:, cs]
            dyp_h = (dz_all[tm:tm + POOL_HALO] * (gate_h * _sigmoid(gate_h)) * sc).astype(BF16)
            dpool_h = lax.dot_general(dyp_h, w, NT, preferred_element_type=F32)
            q_ref[0:tm, cs] = dpool / jnp.minimum(t1, win).astype(F32)
            q_ref[tm:tm + POOL_HALO, cs] = jnp.where(not_last, dpool_h / jnp.minimum(t1h, win).astype(F32), 0.0)
            acc = q_ref[0:tm, cs] - dpool
            for k in range(1, win):
                acc = acc + q_ref[pl.ds(k, tm), cs]
            dp_ref[:, cs] = acc.astype(BF16)

    return pl.pallas_call(
        body, name=name, grid=(n_tiles,),
        in_specs=[pl.BlockSpec((tm, D), lambda i: (i, 0)),
                  pl.BlockSpec((POOL_HALO, D), lambda i: (jnp.minimum((i + 1) * hb, last_halo), 0)),
                  pl.BlockSpec((E, D), lambda i: (0, 0)),
                  pl.BlockSpec((tm, E), lambda i: (i, 0)),
                  pl.BlockSpec((POOL_HALO, E), lambda i: (jnp.maximum(i * hb - 1, 0), 0)),
                  pl.BlockSpec((tm, E), lambda i: (i, 1)),
                  pl.BlockSpec((POOL_HALO, E), lambda i: (jnp.minimum((i + 1) * hb, last_halo), 1)),
                  pl.BlockSpec((n_grp, gc, gc), lambda i: (0, 0, 0)),
                  pl.BlockSpec((1, E), lambda i: (0, 0))],
        out_specs=[pl.BlockSpec((tm, E2), lambda i: (i, 0)),
                   pl.BlockSpec((n_grp, gc, gc), lambda i: (0, 0, 0)),
                   pl.BlockSpec((1, E), lambda i: (0, 0))],
        out_shape=[jax.ShapeDtypeStruct((S, E2), BF16),
                   jax.ShapeDtypeStruct((n_grp, gc, gc), F32),
                   jax.ShapeDtypeStruct((1, E), F32)],
        scratch_shapes=[pltpu.VMEM((POOL_HALO + tm, E), F32), pltpu.VMEM((tm + POOL_HALO, E), F32)],
        compiler_params=_params("arbitrary"))(dx, dx, w_out.reshape(E, D), proj, proj, proj, proj, wg, scale)


def _rope_k(kvp, cos2, sin2, name):
    S, E2 = kvp.shape
    E = E2 // 2
    tm = min(256, S)

    def body(k_ref, c_ref, s_ref, ko_ref):
        cosv, sinv = c_ref[...], s_ref[...]
        for h in range(E // HEAD_DIM):
            hs = slice(h * HEAD_DIM, (h + 1) * HEAD_DIM)
            ko_ref[:, hs] = _rope(k_ref[:, hs], cosv, sinv)

    return pl.pallas_call(
        body, name=name, grid=(S // tm,),
        in_specs=[pl.BlockSpec((tm, E), lambda i: (i, 0)),
                  pl.BlockSpec((tm, HEAD_DIM), lambda i: (i, 0)), pl.BlockSpec((tm, HEAD_DIM), lambda i: (i, 0))],
        out_specs=pl.BlockSpec((tm, E), lambda i: (i, 0)),
        out_shape=jax.ShapeDtypeStruct((S, E), F32),
        compiler_params=_params("parallel"))(kvp, cos2, sin2)


CLASSES = 16


def _class_order(a, back, name):
    S, W = a.shape
    M = S // CLASSES

    def body(a_ref, o_ref):
        for r in range(CLASSES):
            if back:
                o_ref[pl.ds(r, M, stride=CLASSES), :] = a_ref[pl.ds(r * M, M), :]
            else:
                o_ref[pl.ds(r * M, M), :] = a_ref[pl.ds(r, M, stride=CLASSES), :]

    col = pl.BlockSpec((S, HEAD_DIM), lambda i: (0, i))
    return pl.pallas_call(
        body, name=name, grid=(W // HEAD_DIM,), in_specs=[col], out_specs=col,
        out_shape=jax.ShapeDtypeStruct((S, W), a.dtype),
        compiler_params=_params("parallel"))(a)


def _runs(r, b, dil, M, back=0):
    nj = CLASSES // dil
    c = BAND // nj
    return [((r + dil * j) * M + (b - back) * c, (1 + back) * c) for j in range(nj)]


def _load(ref, runs):
    parts = [ref[pl.ds(start, n), :] for start, n in runs]
    return parts[0] if len(parts) == 1 else jnp.concatenate(parts, axis=0)


def _store(ref, runs, val, add=False):
    at = 0
    for start, n in runs:
        if add:
            ref[pl.ds(start, n), :] += val[at:at + n]
        else:
            ref[pl.ds(start, n), :] = val[at:at + n]
        at += n


def _keys(ref, r, b, dil, M):
    if b > 0:
        return _load(ref, _runs(r, b, dil, M, back=1)).astype(BF16)
    parts = []
    for start, n in _runs(r, 0, dil, M):
        parts += [jnp.zeros((n, HEAD_DIM), BF16), ref[pl.ds(start, n), :].astype(BF16)]
    return jnp.concatenate(parts, axis=0)


def _band_mask(dil, first):
    nj = CLASSES // dil
    c = BAND // nj
    row = lax.broadcasted_iota(jnp.int32, (BAND, 2 * BAND), 0)
    col = lax.broadcasted_iota(jnp.int32, (BAND, 2 * BAND), 1)
    q_place = (row % c) * nj + row // c
    k_place = (col % (2 * c) - c) * nj + col // (2 * c)
    mask = (q_place >= k_place) & (q_place <= k_place + BAND)
    return mask & (col % (2 * c) >= c) if first else mask


def _lane_column(tile, lane, h):
    return jnp.sum(jnp.where(lane == h, tile, 0.0), axis=-1, keepdims=True)


def _attn_fwd(proj, kr, kvp, cos2, sin2, group, dil, name):
    S, PW = proj.shape
    E = kr.shape[1]
    H = E // HEAD_DIM
    M = S // CLASSES
    nb = S // (BAND * dil)

    def body(q_ref, k_ref, v_ref, c_ref, s_ref, o_ref, l_ref):
        h = pl.program_id(0)
        lane = lax.broadcasted_iota(jnp.int32, (BAND, HEAD_DIM), 1)
        edge, inner = _band_mask(dil, True), _band_mask(dil, False)

        @pl.when(h == 0)
        def _():
            l_ref[...] = jnp.zeros_like(l_ref)

        def scores(r, b):
            runs = _runs(r, b, dil, M)
            qr = _rope(_load(q_ref, runs), _load(c_ref, runs), _load(s_ref, runs)).astype(BF16)
            return lax.dot_general(qr, _keys(k_ref, r, b, dil, M), NT, preferred_element_type=F32)

        units = [(r, b) for r in range(dil) for b in range(nb)]
        ahead = scores(*units[0])
        for i, (r, b) in enumerate(units):
            s = ahead
            if i + 1 < len(units):
                ahead = scores(*units[i + 1])
            runs = _runs(r, b, dil, M)
            s = jnp.where(edge if b == 0 else inner, s, NEG_INF)
            m = jnp.max(s, axis=-1, keepdims=True)
            p = jnp.exp2((s - m) * (ATTN_SCALE * LOG2_E))
            l = jnp.sum(p, axis=-1, keepdims=True)
            _store(o_ref, runs, jnp.dot(p.astype(BF16), _keys(v_ref, r, b, dil, M), preferred_element_type=F32) / l)
            _store(l_ref, runs, jnp.where(lane == h, m * ATTN_SCALE + jnp.log(l), _load(l_ref, runs)))

    col = (S, HEAD_DIM)
    whole = pl.BlockSpec(col, lambda h: (0, 0))
    return pl.pallas_call(
        body, name=name, grid=(H,),
        in_specs=[pl.BlockSpec(col, lambda h: (0, group * H + h)), pl.BlockSpec(col, lambda h: (0, h)),
                  pl.BlockSpec(col, lambda h: (0, H + h)), whole, whole],
        out_specs=[pl.BlockSpec(col, lambda h: (0, h)), whole],
        out_shape=[jax.ShapeDtypeStruct((S, E), F32), jax.ShapeDtypeStruct((S, HEAD_DIM), F32)],
        compiler_params=_params("arbitrary"))(proj, kr, kvp, cos2, sin2)


def _attn_bwd(proj, kr, kvp, cos2, sin2, do, lse, dlt, dproj, group, dil, name):
    S, PW = proj.shape
    E = kr.shape[1]
    H = E // HEAD_DIM
    M = S // CLASSES
    nb = S // (BAND * dil)

    def body(q_ref, k_ref, v_ref, c_ref, s_ref, do_ref, l_ref, dl_ref, dproj_ref,
             dq_ref, dk_ref, dv_ref, dq_scr, dk_scr, dv_scr):
        h = pl.program_id(0)
        lane = lax.broadcasted_iota(jnp.int32, (BAND, HEAD_DIM), 1)
        edge, inner = _band_mask(dil, True), _band_mask(dil, False)
        dk_scr[...] = jnp.zeros_like(dk_scr)
        dv_scr[...] = jnp.zeros_like(dv_scr)

        def scores(r, b):
            runs = _runs(r, b, dil, M)
            qr = _rope(_load(q_ref, runs), _load(c_ref, runs), _load(s_ref, runs)).astype(BF16)
            return qr, lax.dot_general(qr, _keys(k_ref, r, b, dil, M), NT, preferred_element_type=F32)

        units = [(r, b) for r in range(dil) for b in range(nb)]
        ahead = scores(*units[0])
        for i, (r, b) in enumerate(units):
            qr, s = ahead
            if i + 1 < len(units):
                ahead = scores(*units[i + 1])
            runs = _runs(r, b, dil, M)
            dob = _load(do_ref, runs).astype(BF16)
            dpr = lax.dot_general(dob, _keys(v_ref, r, b, dil, M), NT, preferred_element_type=F32)
            s = jnp.where(edge if b == 0 else inner, s, NEG_INF)
            p = jnp.exp2(s * (ATTN_SCALE * LOG2_E) - _lane_column(_load(l_ref, runs), lane, h) * LOG2_E)
            ds = (p * (dpr - _lane_column(_load(dl_ref, runs), lane, h)) * ATTN_SCALE).astype(BF16)
            dq = jnp.dot(ds, _keys(k_ref, r, b, dil, M), preferred_element_type=F32)
            _store(dq_scr, runs, _rope_bwd(dq, _load(c_ref, runs), _load(s_ref, runs)))
            dkc = lax.dot_general(ds, qr, TN, preferred_element_type=F32)
            dvc = lax.dot_general(p.astype(BF16), dob, TN, preferred_element_type=F32)
            if b > 0:
                both = _runs(r, b, dil, M, back=1)
                _store(dk_scr, both, dkc, add=True)
                _store(dv_scr, both, dvc, add=True)
            else:
                n = runs[0][1]
                own = jnp.concatenate([dkc[(2 * j + 1) * n:(2 * j + 2) * n] for j in range(len(runs))], axis=0)
                _store(dk_scr, runs, own, add=True)
                own = jnp.concatenate([dvc[(2 * j + 1) * n:(2 * j + 2) * n] for j in range(len(runs))], axis=0)
                _store(dv_scr, runs, own, add=True)
        dq_ref[...] = dq_scr[...].astype(BF16)
        dk_ref[...] = dk_scr[...].astype(BF16)
        dv_ref[...] = dv_scr[...].astype(BF16)

    col = (S, HEAD_DIM)
    whole = pl.BlockSpec(col, lambda h: (0, 0))
    head = pl.BlockSpec(col, lambda h: (0, h))
    return pl.pallas_call(
        body, name=name, grid=(H,),
        in_specs=[pl.BlockSpec(col, lambda h: (0, group * H + h)), head, pl.BlockSpec(col, lambda h: (0, H + h)),
                  whole, whole, head, whole, whole, ANY],
        out_specs=[pl.BlockSpec(col, lambda h: (0, group * H + h)), head, head],
        out_shape=[jax.ShapeDtypeStruct(dproj.shape, BF16), jax.ShapeDtypeStruct((S, E), BF16),
                   jax.ShapeDtypeStruct((S, E), BF16)],
        scratch_shapes=[pltpu.VMEM(col, F32)] * 3,
        input_output_aliases={8: 0},
        compiler_params=_params("parallel"))(proj, kr, kvp, cos2, sin2, do, lse, dlt, dproj)


def _group_weights(l_refs, h):
    ls = [r[:, h:h + 1] for r in l_refs]
    mx = jnp.maximum(jnp.maximum(ls[0], ls[1]), ls[2])
    es = [jnp.exp(l - mx) for l in ls]
    inv = 1.0 / (es[0] + es[1] + es[2])
    return [e * inv for e in es]


def _merge_out(outs, lses, proj, w_out, res, name):
    S, E = outs[0].shape
    D = res.shape[1]
    tm = min(256, S)
    gate_col = proj.shape[1] // E - 1

    def body(o0, o1, o2, l0, l1, l2, gt_ref, w_ref, r_ref, x_ref, z_ref):
        for h in range(E // HEAD_DIM):
            hs = slice(h * HEAD_DIM, (h + 1) * HEAD_DIM)
            a = _group_weights((l0, l1, l2), h)
            merged = a[0] * o0[:, hs] + a[1] * o1[:, hs] + a[2] * o2[:, hs]
            gate = gt_ref[:, hs]
            z_ref[:, hs] = (merged * (gate * _sigmoid(gate))).astype(BF16)
        x_ref[...] = r_ref[...] + jnp.dot(z_ref[...], w_ref[...], preferred_element_type=F32)

    wide = pl.BlockSpec((tm, E), lambda i: (i, 0))
    thin = pl.BlockSpec((tm, HEAD_DIM), lambda i: (i, 0))
    return pl.pallas_call(
        body, name=name, grid=(S // tm,),
        in_specs=[wide, wide, wide, thin, thin, thin, pl.BlockSpec((tm, E), lambda i: (i, gate_col)),
                  pl.BlockSpec((E, D), lambda i: (0, 0)), pl.BlockSpec((tm, D), lambda i: (i, 0))],
        out_specs=[pl.BlockSpec((tm, D), lambda i: (i, 0)), wide],
        out_shape=[jax.ShapeDtypeStruct((S, D), F32), jax.ShapeDtypeStruct((S, E), BF16)],
        compiler_params=_params("parallel"))(*outs, *lses, proj, w_out.reshape(E, D), res)


def _merge_bwd(dx, w_out, outs, lses, proj, name):
    S, E = outs[0].shape
    D = dx.shape[1]
    tm = min(256, S)
    gate_col = proj.shape[1] // E - 1

    def body(dx_ref, w_ref, o0, o1, o2, l0, l1, l2, gt_ref, d0, d1, d2, t0, t1, t2, dg_ref):
        o_refs, d_refs, t_refs = (o0, o1, o2), (d0, d1, d2), (t0, t1, t2)
        lane = lax.broadcasted_iota(jnp.int32, (tm, HEAD_DIM), 1)
        tiles = [jnp.zeros((tm, HEAD_DIM), F32) for _ in range(3)]
        dxb = dx_ref[...].astype(BF16)
        for h in range(E // HEAD_DIM):
            hs = slice(h * HEAD_DIM, (h + 1) * HEAD_DIM)
            a = _group_weights((l0, l1, l2), h)
            merged = a[0] * o0[:, hs] + a[1] * o1[:, hs] + a[2] * o2[:, hs]
            gate = gt_ref[:, hs]
            sg = _sigmoid(gate)
            dzh = lax.dot_general(dxb, w_ref[hs, :], NT, preferred_element_type=F32)
            dmerged = dzh * (gate * sg)
            dg_ref[:, hs] = (dzh * merged * (sg * (1.0 + gate * (1.0 - sg)))).astype(BF16)
            tot = jnp.sum(dmerged * merged, axis=-1, keepdims=True)
            for g in range(3):
                d_refs[g][:, hs] = a[g] * dmerged
                tiles[g] = jnp.where(lane == h, a[g] * tot, tiles[g])
        for g in range(3):
            t_refs[g][...] = tiles[g]

    wide = pl.BlockSpec((tm, E), lambda i: (i, 0))
    thin = pl.BlockSpec((tm, HEAD_DIM), lambda i: (i, 0))
    res = pl.pallas_call(
        body, name=name, grid=(S // tm,),
        in_specs=[pl.BlockSpec((tm, D), lambda i: (i, 0)), pl.BlockSpec((E, D), lambda i: (0, 0)),
                  wide, wide, wide, thin, thin, thin, pl.BlockSpec((tm, E), lambda i: (i, gate_col))],
        out_specs=[wide, wide, wide, thin, thin, thin, pl.BlockSpec((tm, E), lambda i: (i, gate_col))],
        out_shape=[jax.ShapeDtypeStruct((S, E), F32)] * 3 + [jax.ShapeDtypeStruct((S, HEAD_DIM), F32)] * 3
        + [jax.ShapeDtypeStruct(proj.shape, BF16)],
        compiler_params=_params("parallel"))(dx, w_out.reshape(E, D), *outs, *lses, proj)
    return res[0:3], res[3:6], res[6]


def _kv_bwd(dks, dvs, cos2, sin2, name):
    S, E = dks[0].shape
    n = len(dks)
    tm = min(256, S)

    def body(*refs):
        dk_refs, dv_refs = refs[0:n], refs[n:2 * n]
        c_ref, s_ref, o_ref = refs[2 * n:]
        cosv, sinv = c_ref[...], s_ref[...]
        for h in range(E // HEAD_DIM):
            hs = slice(h * HEAD_DIM, (h + 1) * HEAD_DIM)
            dk = dk_refs[0][:, hs].astype(F32)
            dv = dv_refs[0][:, hs].astype(F32)
            for j in range(1, n):
                dk = dk + dk_refs[j][:, hs].astype(F32)
                dv = dv + dv_refs[j][:, hs].astype(F32)
            o_ref[:, hs] = _rope_bwd(dk, cosv, sinv).astype(BF16)
            o_ref[:, E + h * HEAD_DIM:E + (h + 1) * HEAD_DIM] = dv.astype(BF16)

    wide = pl.BlockSpec((tm, E), lambda i: (i, 0))
    thin = pl.BlockSpec((tm, HEAD_DIM), lambda i: (i, 0))
    return pl.pallas_call(
        body, name=name, grid=(S // tm,),
        in_specs=[wide] * (2 * n) + [thin, thin],
        out_specs=pl.BlockSpec((tm, 2 * E), lambda i: (i, 0)),
        out_shape=jax.ShapeDtypeStruct((S, 2 * E), BF16),
        compiler_params=_params("parallel"))(*dks, *dvs, cos2, sin2)


def _final_norm_loss(x, target, gain, name):
    S, D = x.shape
    tm = min(256, S)

    def body(x_ref, t_ref, g_ref, loss_ref, dx_ref, dg_ref):
        @pl.when(pl.program_id(0) == 0)
        def _():
            loss_ref[...] = jnp.zeros_like(loss_ref)
            dg_ref[...] = jnp.zeros_like(dg_ref)
        xf = x_ref[...]
        inv = lax.rsqrt(jnp.mean(xf * xf, axis=-1, keepdims=True) + RMS_EPS)
        xhat = xf * inv
        g = g_ref[...]
        err = xhat * g - t_ref[...]
        loss_ref[...] += 0.5 * jnp.sum(jnp.mean(err * err, axis=-1, keepdims=True), axis=0, keepdims=True)
        dy = err / D
        dg_ref[...] += jnp.sum(dy * xhat, axis=0, keepdims=True)
        dxh = dy * g
        dx_ref[...] = inv * (dxh - xhat * jnp.mean(dxh * xhat, axis=-1, keepdims=True))

    tile = pl.BlockSpec((tm, D), lambda i: (i, 0))
    vec = pl.BlockSpec((1, D), lambda i: (0, 0))
    return pl.pallas_call(
        body, name=name, grid=(S // tm,),
        in_specs=[tile, tile, vec],
        out_specs=[pl.BlockSpec((1, 1), lambda i: (0, 0)), tile, vec],
        out_shape=[jax.ShapeDtypeStruct((1, 1), F32), jax.ShapeDtypeStruct((S, D), F32),
                   jax.ShapeDtypeStruct((1, D), F32)],
        compiler_params=_params("arbitrary"))(x, target, gain)


def _adamw_math(g, w, m, v):
    m = ADAM_B1 * m + (1.0 - ADAM_B1) * g
    v = ADAM_B2 * v + (1.0 - ADAM_B2) * (g * g)
    m_hat = m / (1.0 - ADAM_B1 ** ADAM_STEP)
    v_hat = v / (1.0 - ADAM_B2 ** ADAM_STEP)
    delta = -ADAM_LR * (m_hat / (jnp.sqrt(v_hat) + ADAM_EPS) + ADAM_WD * w)
    return delta, m, v


def _adamw_rows(g, w, m, v, name):
    def body(g_ref, w_ref, m_ref, v_ref, d_ref, mo_ref, vo_ref):
        d_ref[...], mo_ref[...], vo_ref[...] = _adamw_math(g_ref[...], w_ref[...], m_ref[...], v_ref[...])

    whole = pl.BlockSpec(memory_space=pltpu.VMEM)
    return pl.pallas_call(
        body, name=name, in_specs=[whole] * 4, out_specs=[whole] * 3,
        out_shape=[jax.ShapeDtypeStruct(g.shape, F32)] * 3)(g, w, m, v)


def _adamw_blocks(own, others, w, m, v, layer, earlier, name):
    L, R, C = w.shape
    n = others.shape[0]
    tr = R
    while tr * C > 128 * 1024 and tr % 16 == 0:
        tr //= 2

    def body(o_ref, p_ref, w_ref, m_ref, v_ref, *rest):
        g_ref, d_ref, mo_ref, vo_ref = rest[-4:]
        g = o_ref[...].astype(F32)
        for j in range(n):
            g = g + p_ref[j].astype(F32)
        g_ref[...] = g
        d_ref[...], mo_ref[...], vo_ref[...] = _adamw_math(g, w_ref[...], m_ref[...], v_ref[...])

    tile = pl.BlockSpec((None, tr, C), lambda i: (layer, i, 0))
    kept = [] if earlier is None else list(earlier)
    return pl.pallas_call(
        body, name=name, grid=(R // tr,),
        in_specs=[pl.BlockSpec((tr, C), lambda i: (i, 0)), pl.BlockSpec((n, tr, C), lambda i: (0, i, 0)),
                  tile, tile, tile] + [ANY] * len(kept),
        out_specs=[tile] * 4,
        out_shape=[jax.ShapeDtypeStruct((L, R, C), F32)] * 4,
        input_output_aliases={5 + j: j for j in range(len(kept))},
        compiler_params=_params("parallel"))(own, others, w, m, v, *kept)


def _position():
    return lax.axis_index("x"), lax.axis_index("y"), lax.axis_index("c")


def _block_index(px, py, pc):
    return 4 * px + 2 * py + pc


def _all_gather(shards, name):
    n = len(shards)

    def body(*refs):
        ins, outs = refs[0:n], refs[n:2 * n]
        send_sems, recv_sems, local_sems = refs[2 * n:]
        x, y, c = _position()
        me, sibling = (x, y, c), (x, y, 1 - c)
        chips = [(1 - x, y), (x, 1 - y), (1 - x, 1 - y)]

        def copy(a, k, block, to, src=None):
            rows = outs[a].at[_block_index(*block)]
            return pltpu.make_async_remote_copy(
                src_ref=rows if src is None else src, dst_ref=rows,
                send_sem=send_sems.at[a, k], recv_sem=recv_sems.at[a, k], device_id=to, device_id_type=MESH)

        mine, first, passed = [], [], []
        for a in range(n):
            cp = pltpu.make_async_copy(ins[a], outs[a].at[_block_index(*me)], local_sems.at[a])
            cp.start()
            mine.append(cp)
            first.append(copy(a, 0, me, sibling, src=ins[a]))
            first += [copy(a, 1 + j, me, (*chip, c), src=ins[a]) for j, chip in enumerate(chips)]
        for cp in first:
            cp.start()
        for j, chip in enumerate(chips):
            for a in range(n):
                copy(a, 1 + j, (*chip, c), me).wait_recv()
                fwd = copy(a, 4 + j, (*chip, c), sibling)
                fwd.start()
                passed.append(fwd)
        for a in range(n):
            copy(a, 0, sibling, me).wait_recv()
            for j, chip in enumerate(chips):
                copy(a, 4 + j, (*chip, 1 - c), me).wait_recv()
        for cp in first + passed:
            cp.wait_send()
        for cp in mine:
            cp.wait()

    return pl.pallas_call(
        body, name=name,
        in_specs=[ANY] * n, out_specs=[ANY] * n,
        out_shape=[jax.ShapeDtypeStruct((N_DEV,) + s.shape, s.dtype) for s in shards],
        scratch_shapes=[pltpu.SemaphoreType.DMA((n, 7)), pltpu.SemaphoreType.DMA((n, 7)),
                        pltpu.SemaphoreType.DMA((n,))],
    )(*shards)


def _peers(x, y, c):
    return [((1 - x) if k & 4 else x, (1 - y) if k & 2 else y, (1 - c) if k & 1 else c) for k in range(1, N_DEV)]


HBM = pl.BlockSpec(memory_space=pltpu.HBM)
SEM = pl.BlockSpec(memory_space=pltpu.SEMAPHORE)
EFFECT = pltpu.SideEffectType.DATAFLOW_SIDE_EFFECTING


ALL_PEERS = (1, 2, 3, 4, 5, 6, 7)
SIBLING_AND_SAME_CORES = (1, 2, 4, 6)


def _push_copy(src_refs, land_refs, send_sems, recv_sems, a, i, relations, per_peer, by_sender, arriving):
    peer = _peers(*_position())[relations[i] - 1]
    me_idx, p_idx = _block_index(*_position()), _block_index(*peer)
    src = src_refs[a].at[p_idx] if per_peer else src_refs[a]
    if by_sender:
        slot = p_idx if arriving else me_idx
    else:
        slot = relations[i] - 1
    sem = a * len(relations) + i
    return pltpu.make_async_remote_copy(
        src_ref=src, dst_ref=land_refs[a].at[slot], send_sem=send_sems.at[sem], recv_sem=recv_sems.at[sem],
        device_id=peer, device_id_type=MESH)


def _push_start(srcs, lands, relations, per_peer, by_sender, after, name):
    n = len(srcs)

    def body(*refs):
        src_refs, land_refs = refs[0:n], refs[n:2 * n]
        send_sems, recv_sems = refs[2 * n + 1], refs[2 * n + 2]
        token = refs[-1]
        for a in range(n):
            for i in range(len(relations)):
                _push_copy(src_refs, land_refs, send_sems, recv_sems, a, i, relations, per_peer, by_sender, False).start()
        token[...] = jnp.zeros_like(token)

    args = [pltpu.with_memory_space_constraint(t, pltpu.HBM) for t in list(srcs) + list(lands)]
    res = pl.pallas_call(
        body, name=name,
        in_specs=[HBM] * (2 * n) + [ANY],
        out_specs=[SEM, SEM] + [HBM] * (2 * n) + [pl.BlockSpec(memory_space=pltpu.VMEM)],
        out_shape=[pltpu.SemaphoreType.DMA((n * len(relations),)), pltpu.SemaphoreType.DMA((n * len(relations),))]
        + [pltpu.HBM(t.shape, t.dtype) for t in args] + [jax.ShapeDtypeStruct((8, 128), F32)],
        input_output_aliases={i: 2 + i for i in range(2 * n)},
        compiler_params=pltpu.CompilerParams(has_side_effects=EFFECT))(*args, after)
    return res[0], res[1], res[2:2 + n], res[2 + n:2 + 2 * n], res[-1]


def _push_wait(started, relations, per_peer, by_sender, after, name):
    send_sems, recv_sems, srcs, lands, _ = started
    n = len(srcs)

    def body(*refs):
        src_refs, land_refs = refs[0:n], refs[n:2 * n]
        send_s, recv_s = refs[2 * n], refs[2 * n + 1]
        for a in range(n):
            for i in range(len(relations)):
                _push_copy(src_refs, land_refs, send_s, recv_s, a, i, relations, per_peer, by_sender, False).wait_send()
                _push_copy(src_refs, land_refs, send_s, recv_s, a, i, relations, per_peer, by_sender, True).wait_recv()

    res = pl.pallas_call(
        body, name=name,
        in_specs=[HBM] * (2 * n) + [SEM, SEM, ANY],
        out_specs=[HBM] * (2 * n),
        out_shape=[pltpu.HBM(t.shape, t.dtype) for t in list(srcs) + list(lands)],
        input_output_aliases={i: i for i in range(2 * n)},
        compiler_params=pltpu.CompilerParams(has_side_effects=EFFECT))(*srcs, *lands, send_sems, recv_sems, after)
    return res[0:n], res[n:2 * n]


def _pass_on(lands, name):
    n = len(lands)

    def body(*refs):
        outs = refs[n:2 * n]
        send_sems, recv_sems = refs[2 * n:]
        x, y, c = _position()
        chips = [(1 - x, y), (x, 1 - y), (1 - x, 1 - y)]
        copies = []
        for a in range(n):
            for j, chip in enumerate(chips):
                def copy(core):
                    rows = outs[a].at[_block_index(*chip, core)]
                    return pltpu.make_async_remote_copy(
                        src_ref=rows, dst_ref=rows, send_sem=send_sems.at[a, j], recv_sem=recv_sems.at[a, j],
                        device_id=(x, y, 1 - c), device_id_type=MESH)
                copy(c).start()
                copies.append((copy(c), copy(1 - c)))
        for sending, arriving in copies:
            sending.wait_send()
            arriving.wait_recv()

    return pl.pallas_call(
        body, name=name,
        in_specs=[ANY] * n, out_specs=[ANY] * n,
        out_shape=[jax.ShapeDtypeStruct(t.shape, t.dtype) for t in lands],
        input_output_aliases={a: a for a in range(n)},
        scratch_shapes=[pltpu.SemaphoreType.DMA((n, 3)), pltpu.SemaphoreType.DMA((n, 3))],
    )(*lands)


def _all_reduce_rows(v, name):
    R, D = v.shape

    def body(v_ref, o_ref, buf_ref, send_sems, recv_sems):
        x, y, c = _position()
        me_idx = _block_index(x, y, c)
        buf_ref[me_idx] = v_ref[...]
        copies = []
        for k in range(1, N_DEV):
            px = (1 - x) if k & 4 else x
            py = (1 - y) if k & 2 else y
            pc = (1 - c) if k & 1 else c
            rc = pltpu.make_async_remote_copy(
                src_ref=v_ref, dst_ref=buf_ref.at[me_idx],
                send_sem=send_sems.at[k - 1], recv_sem=recv_sems.at[k - 1],
                device_id=(px, py, pc), device_id_type=MESH)
            rc.start()
            copies.append((rc, pltpu.make_async_remote_copy(
                src_ref=v_ref, dst_ref=buf_ref.at[_block_index(px, py, pc)],
                send_sem=send_sems.at[k - 1], recv_sem=recv_sems.at[k - 1],
                device_id=(px, py, pc), device_id_type=MESH)))
        for rc, arrival in copies:
            rc.wait_send()
            arrival.wait_recv()
        acc = buf_ref[0]
        for j in range(1, N_DEV):
            acc = acc + buf_ref[j]
        o_ref[...] = acc

    return pl.pallas_call(
        body, name=name,
        in_specs=[pl.BlockSpec(memory_space=pltpu.VMEM)],
        out_specs=pl.BlockSpec(memory_space=pltpu.VMEM),
        out_shape=jax.ShapeDtypeStruct((R, D), F32),
        scratch_shapes=[pltpu.VMEM((N_DEV, R, D), F32),
                        pltpu.SemaphoreType.DMA((7,)), pltpu.SemaphoreType.DMA((7,))],
    )(v)


def _rope_tables(S):
    inv_freq = 1.0 / (ROPE_THETA ** (jnp.arange(0, HEAD_DIM, 2, dtype=F32) / HEAD_DIM))
    ang = jnp.arange(S, dtype=F32)[:, None] * inv_freq[None, :]
    cos, sin = jnp.cos(ang), jnp.sin(ang)
    return jnp.concatenate([cos, cos], axis=1), jnp.concatenate([-sin, sin], axis=1)


def _local_step(xs, target, vecs, n_a, n_b, get_weights, put_grads):
    S, D = xs.shape
    E = D
    cos2, sin2 = _rope_tables(S)
    ts = min(1024, S)

    def col_blocks(w):
        cb = w.shape[2]
        tn = min(cb, 1024)
        per = cb // tn
        return (None, D, tn), (lambda i, j: (j // per, 0, j % per)), N_DEV * per, tn

    def grad_in(hn, dproj, cb, name):
        return _matmul_tn(hn, dproj, (ts, D), lambda j, s: (s, 0), (ts, cb), lambda j, s: (s, j),
                          (N_DEV, D, cb), (None, D, cb), lambda j, s: (j, 0, 0), (D, cb), N_DEV, name)

    def grad_out(z, dx, name, col=0):
        rows = z.shape[1]
        ta = min(1024, rows)
        out = _matmul_tn(z, dx, (ts, ta), lambda a, s: (s, a), (ts, E), lambda a, s: (s, col),
                         (rows, E), (ta, E), lambda a, s: (a, 0), (ta, E), rows // ta, name)
        return out.reshape(N_DEV, rows // N_DEV, E)

    x = xs
    a_saved, b_saved = [], []
    for i in range(n_a):
        w = get_weights(f"a{i}", x)
        blk, idx, nblocks, tn = col_blocks(w["w_in"])
        proj, hn = _norm_matmul(x, vecs["norm_a"][i:i + 1], w["w_in"], blk, idx, nblocks, tn, f"a{i}_in")
        w = {**w, **get_weights(f"a{i}_rest", proj)}
        z = _a_mid_fwd(proj, w["w_grp"], vecs["scale_a"][i:i + 1], f"a{i}_mid")
        x_next = _matmul_res(z, w["w_out"], x, f"a{i}_out")
        a_saved.append((x, hn, proj, z, w))
        x = x_next
    w_kv = get_weights("kv", x)["w_kv"]
    x = x_kv = _class_order(x, False, "kv_x_order")
    target = _class_order(target, False, "target_order")
    cos2, sin2 = _class_order(cos2, False, "cos_order"), _class_order(sin2, False, "sin_order")
    tn = min(E, 1024)
    kvp, hn_kv = _norm_matmul(x, vecs["norm_kv"], w_kv, (D, tn), lambda i, j: (0, j), 2 * E // tn, tn, "kv_in")
    kr = _rope_k(kvp, cos2, sin2, "kv_rope")
    after = kr
    for i in range(n_b):
        w = get_weights(f"b{i}", after)
        blk, idx, nblocks, tn = col_blocks(w["w_in"])
        proj, hn = _norm_matmul(x, vecs["norm_b"][i:i + 1], w["w_in"], blk, idx, nblocks, tn, f"b{i}_in")
        outs, lses = [], []
        for g, dil in enumerate(DILATIONS):
            o, l = _attn_fwd(proj, kr, kvp, cos2, sin2, g, dil, f"b{i}_attn{g}")
            outs.append(o)
            lses.append(l)
        x_next, z = _merge_out(outs, lses, proj, w["w_out"], x, f"b{i}_out")
        b_saved.append((x, hn, proj, z, outs, lses, w))
        x = x_next
        after = x
    loss, dx, dg_f = _final_norm_loss(x, target, vecs["norm_f"], "final")

    vec = {"norm_a": [None] * n_a, "scale_a": [None] * n_a, "norm_b": [None] * n_b, "norm_f": [dg_f]}
    dks, dvs = [], []
    for i in reversed(range(n_b)):
        x_in, hn, proj, z, outs, lses, w = b_saved[i]
        dw_out = grad_out(z, dx, f"b{i}_dwout")
        dos, dlts, dproj = _merge_bwd(dx, w["w_out"], outs, lses, proj, f"b{i}_dmerge")
        for g, dil in enumerate(DILATIONS):
            dproj, dk, dv = _attn_bwd(proj, kr, kvp, cos2, sin2, dos[g], lses[g], dlts[g], dproj, g, dil,
                                      f"b{i}_dattn{g}")
            dks.append(dk)
            dvs.append(dv)
        cb = w["w_in"].shape[2]
        tok = put_grads(f"b{i}", {"w_out": dw_out, "w_in": grad_in(hn, dproj, cb, f"b{i}_dwin")})
        dx, vec["norm_b"][i] = _matmul_nt_dnorm(dproj, w["w_in"], (None, D, cb), lambda t, j: (j, 0, 0), N_DEV, cb, x_in,
                                                vecs["norm_b"][i:i + 1] + tok[0:1, 0:1], dx, f"b{i}_dhn")

    dkv = _kv_bwd(dks, dvs, cos2, sin2, "kv_dsum")
    tok = put_grads("kv", {"w_k": grad_out(hn_kv, dkv, "kv_dwk", 0), "w_v": grad_out(hn_kv, dkv, "kv_dwv", 1)})
    tk = min(E, 1024)
    dx, dg_kv = _matmul_nt_dnorm(dkv, w_kv, (D, tk), lambda t, j: (0, j), 2 * E // tk, tk, x_kv,
                                 vecs["norm_kv"] + tok[0:1, 0:1], dx, "kv_dhn")
    vec["norm_kv"] = [dg_kv]
    dx = _class_order(dx, True, "kv_dx_order")

    for i in reversed(range(n_a)):
        x_in, hn, proj, z, w = a_saved[i]
        dw_out = grad_out(z, dx, f"a{i}_dwout")
        dproj, dwg, dsc = _a_mid_bwd(dx, w["w_out"], proj, w["w_grp"], vecs["scale_a"][i:i + 1], f"a{i}_dmid")
        n_grp, gc, _ = dwg.shape
        dwg = dwg.reshape(n_grp, N_DEV, gc // N_DEV, gc).transpose(1, 0, 2, 3).astype(BF16)
        vec["scale_a"][i] = dsc
        cb = w["w_in"].shape[2]
        tok = put_grads(f"a{i}", {"w_out": dw_out, "w_grp": dwg, "w_in": grad_in(hn, dproj, cb, f"a{i}_dwin")})
        dx, vec["norm_a"][i] = _matmul_nt_dnorm(dproj, w["w_in"], (None, D, cb), lambda t, j: (j, 0, 0), N_DEV, cb, x_in,
                                                vecs["norm_a"][i:i + 1] + tok[0:1, 0:1], dx, f"a{i}_dhn")

    return loss, dx, {k: jnp.concatenate(v, axis=0) for k, v in vec.items()}


VECTORS = ("norm_a", "scale_a", "norm_kv", "norm_b", "norm_f")
SHARDED_VECTORS = ("norm_a", "scale_a")
GROUPS = {
    "a0": (("w_in", "w_in_a", 0), ("w_grp", "w_grp_a", 0), ("w_out", "w_out_a", 0)),
    "a1": (("w_in", "w_in_a", 1), ("w_grp", "w_grp_a", 1), ("w_out", "w_out_a", 1)),
    "kv": (("w_k", "w_k", None), ("w_v", "w_v", None)),
    "b0": (("w_in", "w_in_b", 0), ("w_out", "w_out_b", 0)),
    "b1": (("w_in", "w_in_b", 1), ("w_out", "w_out_b", 1)),
}
FIRST = (("w_in", "w_in_a", 0),)
PREFETCHED = {"a0_rest": GROUPS["a0"][1:], "a1": GROUPS["a1"], "kv": GROUPS["kv"], "b0": GROUPS["b0"], "b1": GROUPS["b1"]}


def kernel(x, norm_a, w_in_a, w_grp_a, scale_a, w_out_a, norm_kv, w_k, w_v, norm_b, w_in_b, w_out_b, norm_f, loss_target, m_norm_a, m_w_in_a, m_w_grp_a, m_scale_a, m_w_out_a, m_norm_kv, m_w_k, m_w_v, m_norm_b, m_w_in_b, m_w_out_b, m_norm_f, v_norm_a, v_w_in_a, v_w_grp_a, v_scale_a, v_w_out_a, v_norm_kv, v_w_k, v_w_v, v_norm_b, v_w_in_b, v_w_out_b, v_norm_f):
    w = dict(norm_a=norm_a, w_in_a=w_in_a, w_grp_a=w_grp_a, scale_a=scale_a, w_out_a=w_out_a, norm_kv=norm_kv,
             w_k=w_k, w_v=w_v, norm_b=norm_b, w_in_b=w_in_b, w_out_b=w_out_b, norm_f=norm_f)
    m = dict(norm_a=m_norm_a, w_in_a=m_w_in_a, w_grp_a=m_w_grp_a, scale_a=m_scale_a, w_out_a=m_w_out_a,
             norm_kv=m_norm_kv, w_k=m_w_k, w_v=m_w_v, norm_b=m_norm_b, w_in_b=m_w_in_b, w_out_b=m_w_out_b,
             norm_f=m_norm_f)
    v = dict(norm_a=v_norm_a, w_in_a=v_w_in_a, w_grp_a=v_w_grp_a, scale_a=v_scale_a, w_out_a=v_w_out_a,
             norm_kv=v_norm_kv, w_k=v_w_k, w_v=v_w_v, norm_b=v_norm_b, w_in_b=v_w_in_b, w_out_b=v_w_out_b,
             norm_f=v_norm_f)
    D = x.shape[2]
    me = _block_index(*_position())

    def shard(members):
        return [w[p].astype(BF16) if layer is None else w[p][layer].astype(BF16) for _, p, layer in members]

    def as_weights(members, gathered):
        out = dict(zip([n for n, _, _ in members], gathered))
        if "w_grp" in out:
            g = out["w_grp"]
            out["w_grp"] = g.transpose(1, 0, 2, 3).reshape(g.shape[1], g.shape[3], g.shape[3])
        if "w_k" in out:
            out = {"w_kv": jnp.concatenate([out["w_k"].reshape(D, D), out["w_v"].reshape(D, D)], axis=1)}
        return out

    first = _all_gather(shard(FIRST) + [w[k] for k in SHARDED_VECTORS], "gather_first")
    n_first = len(FIRST)
    vecs = {k: g.transpose(1, 0, 2).reshape(w[k].shape[0], D) for k, g in zip(SHARDED_VECTORS, first[n_first:])}
    vecs.update(norm_kv=norm_kv[None, :], norm_b=norm_b, norm_f=norm_f[None, :])
    srcs, lands = [], []
    for group in PREFETCHED:
        for s in shard(PREFETCHED[group]):
            srcs.append(s)
            lands.append(lax.dynamic_update_index_in_dim(lax.empty((N_DEV,) + s.shape, s.dtype), s[None], me, 0))
    inflight, at = {}, 0
    token = None
    for group in PREFETCHED:
        n = len(PREFETCHED[group])
        inflight[group] = _push_start(srcs[at:at + n], lands[at:at + n], SIBLING_AND_SAME_CORES, False, True,
                                      first[0] if token is None else token, f"gather_{group}_start")
        token = inflight[group][4]
        at += n
    vecs["norm_a"] = vecs["norm_a"] + token[0:1, 0:1]

    def get_weights(group, after):
        if group == "a0":
            return as_weights(FIRST, first[0:n_first])
        if group not in PREFETCHED:
            return {}
        half = _push_wait(inflight[group], SIBLING_AND_SAME_CORES, False, True, after, f"gather_{group}_wait")[1]
        return as_weights(PREFETCHED[group], _pass_on(half, f"gather_{group}_pass"))

    sent = {}

    def put_grads(group, grads):
        blocks = [grads[n] for n, _, _ in GROUPS[group]]
        lands = [lax.empty((N_DEV - 1,) + b.shape[1:], b.dtype) for b in blocks]
        sent[group] = _push_start(blocks, lands, ALL_PEERS, True, False, jnp.zeros((8, 128), F32),
                                  f"exchange_{group}_start")
        return sent[group][4]

    loss, dx, vec = _local_step(x[0], loss_target[0], vecs, w_in_a.shape[0], w_in_b.shape[0], get_weights, put_grads)
    rows = _all_reduce_rows(jnp.concatenate([vec[k] for k in VECTORS], axis=0), "reduce_vectors")

    out = {}
    after = dx
    for group in sent:
        blocks, arrived = _push_wait(sent[group], ALL_PEERS, True, False, after, f"exchange_{group}_wait")
        for (_, p, layer), blk, got in zip(GROUPS[group], blocks, arrived):
            cols = w[p].shape[-1]
            own = lax.dynamic_index_in_dim(blk, me, 0, keepdims=False).reshape(-1, cols)
            n_layers = 1 if layer is None else w[p].shape[0]
            stacked = lambda t: t.reshape(n_layers, -1, cols)
            res = _adamw_blocks(own, got.reshape(N_DEV - 1, -1, cols), stacked(w[p]), stacked(m[p]), stacked(v[p]),
                                0 if layer is None else layer, out.get(p), f"adamw_{group}_{p}")
            out[p] = res
            after = res[1]
    out = {p: [r.reshape(w[p].shape) for r in res] for p, res in out.items()}
    start = 0
    for k in VECTORS:
        n_rows = vec[k].shape[0]
        g = rows[start:start + n_rows]
        start += n_rows
        if k in SHARDED_VECTORS:
            g = lax.dynamic_slice_in_dim(g, me * (D // N_DEV), D // N_DEV, axis=1)
        res = _adamw_rows(g, w[k].reshape(g.shape), m[k].reshape(g.shape), v[k].reshape(g.shape), f"adamw_{k}")
        out[k] = [r.reshape(w[k].shape) for r in [g] + list(res)]

    names = ("norm_a", "w_in_a", "w_grp_a", "scale_a", "w_out_a", "norm_kv", "w_k", "w_v", "norm_b", "w_in_b",
             "w_out_b", "norm_f")
    total = lax.psum(loss[0, 0], ("x", "y", "c"))
    return (total, dx[None], *[out[k][0] for k in names], *[out[k][1] for k in names],
            *[out[k][2] for k in names], *[out[k][3] for k in names])
```

```python
import math

import jax
import jax.numpy as jnp
from jax import lax
from jax.experimental import pallas as pl
from jax.experimental.pallas import tpu as pltpu

F32 = jnp.float32
BF16 = jnp.bfloat16

N_DEV = 8
MESH = pl.DeviceIdType.MESH
RMS_EPS = 1e-6
HEAD_DIM = 128
HALF_HEAD = HEAD_DIM // 2
BAND = 128
DILATIONS = (1, 4, 16)
POOL_WINDOWS = (2, 4, 8, 16)
POOL_HALO = 16
ROPE_THETA = 10000.0
NEG_INF = -1e30
ATTN_SCALE = 1.0 / math.sqrt(HEAD_DIM)
LOG2_E = math.log2(math.e)
ADAM_LR, ADAM_B1, ADAM_B2, ADAM_EPS, ADAM_WD, ADAM_STEP = 0.001, 0.9, 0.999, 1e-08, 0.01, 10
VMEM_LIMIT_BYTES = 56 * 1024 * 1024
ANY = pl.BlockSpec(memory_space=pl.ANY)
NT = (((1,), (1,)), ((), ()))
TN = (((0,), (0,)), ((), ()))


def _params(*semantics):
    return pltpu.CompilerParams(dimension_semantics=semantics, vmem_limit_bytes=VMEM_LIMIT_BYTES)


def _sigmoid(t):
    return 1.0 / (1.0 + jnp.exp(-t))


def _rope(t, cos2, sin2):
    return t * cos2 + pltpu.roll(t, HALF_HEAD, 1) * sin2


def _rope_bwd(dt, cos2, sin2):
    return dt * cos2 + pltpu.roll(dt * sin2, HALF_HEAD, 1)


def _norm_matmul(x, gain, w, w_block, w_index, n_col_blocks, tn, name):
    S, D = x.shape
    tm = min(1024, S)

    def body(x_ref, g_ref, w_ref, o_ref, hn_ref, hs_ref):
        @pl.when(pl.program_id(1) == 0)
        def _():
            xf = x_ref[...]
            inv = lax.rsqrt(jnp.mean(xf * xf, axis=-1, keepdims=True) + RMS_EPS)
            hb = ((xf * inv) * g_ref[...]).astype(BF16)
            hs_ref[...] = hb
            hn_ref[...] = hb
        o_ref[...] = jnp.dot(hs_ref[...], w_ref[...], preferred_element_type=F32)

    return pl.pallas_call(
        body, name=name, grid=(S // tm, n_col_blocks),
        in_specs=[pl.BlockSpec((tm, D), lambda i, j: (i, 0)),
                  pl.BlockSpec((1, D), lambda i, j: (0, 0)),
                  pl.BlockSpec(w_block, w_index)],
        out_specs=[pl.BlockSpec((tm, tn), lambda i, j: (i, j)),
                   pl.BlockSpec((tm, D), lambda i, j: (i, 0))],
        out_shape=[jax.ShapeDtypeStruct((S, n_col_blocks * tn), F32), jax.ShapeDtypeStruct((S, D), BF16)],
        scratch_shapes=[pltpu.VMEM((tm, D), BF16)],
        compiler_params=_params("parallel", "arbitrary"))(x, gain, w)


def _matmul_nt_dnorm(dp, w, w_block, w_index, n_red, tc, x, gain, dres, name):
    S, D = x.shape
    tm = min(1024, S)
    parts = 4
    tp = tm // parts

    def body(d_ref, w_ref, x_ref, g_ref, r_ref, dx_ref, dg_ref, acc_ref):
        i, j = pl.program_id(0), pl.program_id(1)

        @pl.when((i == 0) & (j == 0))
        def _():
            dg_ref[...] = jnp.zeros_like(dg_ref)

        @pl.when(j == 0)
        def _():
            acc_ref[...] = jnp.zeros_like(acc_ref)

        @pl.when(j < n_red)
        def _():
            acc_ref[...] += lax.dot_general(d_ref[...], w_ref[...], NT, preferred_element_type=F32)

        @pl.when(j >= n_red)
        def _():
            xf = x_ref[...]
            inv = lax.rsqrt(jnp.mean(xf * xf, axis=-1, keepdims=True) + RMS_EPS)
            xhat = xf * inv
            dh = acc_ref[pl.ds(pl.multiple_of((j - n_red) * tp, tp), tp), :]
            dg_ref[...] += jnp.sum(dh * xhat, axis=0, keepdims=True)
            dxh = dh * g_ref[...]
            dx_ref[...] = r_ref[...] + inv * (dxh - xhat * jnp.mean(dxh * xhat, axis=-1, keepdims=True))

    last = n_red - 1
    part = pl.BlockSpec((tp, D), lambda i, j: (i * parts + jnp.clip(j - n_red, 0, parts - 1), 0))
    vec = pl.BlockSpec((1, D), lambda i, j: (0, 0))
    return pl.pallas_call(
        body, name=name, grid=(S // tm, n_red + parts),
        in_specs=[pl.BlockSpec((tm, tc), lambda i, j: (i, jnp.minimum(j, last))),
                  pl.BlockSpec(w_block, lambda i, j: w_index(i, jnp.minimum(j, last))), part, vec, part],
        out_specs=[part, vec],
        out_shape=[jax.ShapeDtypeStruct((S, D), F32), jax.ShapeDtypeStruct((1, D), F32)],
        scratch_shapes=[pltpu.VMEM((tm, D), F32)],
        compiler_params=_params("arbitrary", "arbitrary"))(dp, w, x, gain, dres)


def _matmul_tn(a, b, a_block, a_index, b_block, b_index, out_shape, out_block, out_index, acc_shape, n_outer, name):
    S = a.shape[0]
    ts = a_block[0]
    n_tok = S // ts

    def body(a_ref, b_ref, o_ref, acc_ref):
        s = pl.program_id(1)

        @pl.when(s == 0)
        def _():
            acc_ref[...] = jnp.zeros_like(acc_ref)
        acc_ref[...] += lax.dot_general(a_ref[...].astype(BF16), b_ref[...].astype(BF16), TN,
                                        preferred_element_type=F32)

        @pl.when(s == n_tok - 1)
        def _():
            o_ref[...] = acc_ref[...].astype(o_ref.dtype)

    return pl.pallas_call(
        body, name=name, grid=(n_outer, n_tok),
        in_specs=[pl.BlockSpec(a_block, a_index), pl.BlockSpec(b_block, b_index)],
        out_specs=pl.BlockSpec(out_block, out_index),
        out_shape=jax.ShapeDtypeStruct(out_shape, BF16),
        scratch_shapes=[pltpu.VMEM(acc_shape, F32)],
        compiler_params=_params("parallel", "arbitrary"))(a, b)


def _pool(scr_ref, u, row0, tm, E):
    gc = E // len(POOL_WINDOWS)
    t1 = row0 + lax.broadcasted_iota(jnp.int32, (tm, 1), 0) + 1
    out = []
    for g, win in enumerate(POOL_WINDOWS):
        cs = slice(g * gc, (g + 1) * gc)
        acc = u[:, cs]
        for k in range(1, win):
            acc = acc + scr_ref[pl.ds(POOL_HALO - k, tm), cs]
        count = jnp.minimum(t1, win).astype(F32)
        out.append(acc / count - u[:, cs])
    return out


def _a_mid_out(proj, wg, scale, w_out, res, name):
    S, E2 = proj.shape
    E = E2 // 2
    D = res.shape[1]
    gc = E // len(POOL_WINDOWS)
    tm = min(256, S)
    hb = tm // POOL_HALO

    def body(u_ref, uh_ref, gt_ref, wg_ref, sc_ref, wo_ref, r_ref, x_ref, z_ref, scr_ref):
        i = pl.program_id(0)
        scr_ref[0:POOL_HALO, :] = jnp.where(i > 0, uh_ref[...], 0.0)
        u = u_ref[...]
        scr_ref[POOL_HALO:POOL_HALO + tm, :] = u
        pooled = _pool(scr_ref, u, i * tm, tm, E)
        for g in range(len(POOL_WINDOWS)):
            cs = slice(g * gc, (g + 1) * gc)
            y = jnp.dot(pooled[g].astype(BF16), wg_ref[g], preferred_element_type=F32) * sc_ref[:, cs]
            gate = gt_ref[:, cs]
            z_ref[:, cs] = (y * (gate * _sigmoid(gate))).astype(BF16)
        x_ref[...] = r_ref[...] + jnp.dot(z_ref[...], wo_ref[...], preferred_element_type=F32)

    return pl.pallas_call(
        body, name=name, grid=(S // tm,),
        in_specs=[pl.BlockSpec((tm, E), lambda i: (i, 0)),
                  pl.BlockSpec((POOL_HALO, E), lambda i: (jnp.maximum(i * hb - 1, 0), 0)),
                  pl.BlockSpec((tm, E), lambda i: (i, 1)),
                  pl.BlockSpec((len(POOL_WINDOWS), gc, gc), lambda i: (0, 0, 0)),
                  pl.BlockSpec((1, E), lambda i: (0, 0)),
                  pl.BlockSpec((E, D), lambda i: (0, 0)),
                  pl.BlockSpec((tm, D), lambda i: (i, 0))],
        out_specs=[pl.BlockSpec((tm, D), lambda i: (i, 0)), pl.BlockSpec((tm, E), lambda i: (i, 0))],
        out_shape=[jax.ShapeDtypeStruct((S, D), F32), jax.ShapeDtypeStruct((S, E), BF16)],
        scratch_shapes=[pltpu.VMEM((POOL_HALO + tm, E), F32)],
        compiler_params=_params("parallel"))(proj, proj, proj, wg, scale, w_out.reshape(E, D), res)


def _a_mid_bwd(dx, w_out, proj, wg, scale, name):
    S, E2 = proj.shape
    E = E2 // 2
    D = dx.shape[1]
    n_grp = len(POOL_WINDOWS)
    gc = E // n_grp
    tm = min(256, S)
    hb = tm // POOL_HALO
    n_tiles = S // tm
    last_halo = S // POOL_HALO - 1

    def body(dx_ref, dxh_ref, wo_ref, u_ref, uh_ref, gt_ref, gth_ref, wg_ref, sc_ref, dp_ref, dwg_ref, dsc_ref,
             scr_ref, q_ref):
        i = pl.program_id(0)
        dxb = jnp.concatenate([dx_ref[...].astype(BF16), dxh_ref[...].astype(BF16)], axis=0)

        @pl.when(i == 0)
        def _():
            dwg_ref[...] = jnp.zeros_like(dwg_ref)
            dsc_ref[...] = jnp.zeros_like(dsc_ref)

        scr_ref[0:POOL_HALO, :] = jnp.where(i > 0, uh_ref[...], 0.0)
        u = u_ref[...]
        scr_ref[POOL_HALO:POOL_HALO + tm, :] = u
        pooled = _pool(scr_ref, u, i * tm, tm, E)
        t1 = i * tm + lax.broadcasted_iota(jnp.int32, (tm, 1), 0) + 1
        t1h = (i + 1) * tm + lax.broadcasted_iota(jnp.int32, (POOL_HALO, 1), 0) + 1
        not_last = i < n_tiles - 1
        for g, win in enumerate(POOL_WINDOWS):
            cs = slice(g * gc, (g + 1) * gc)
            w = wg_ref[g]
            sc = sc_ref[:, cs]
            pb = pooled[g].astype(BF16)
            ypre = jnp.dot(pb, w, preferred_element_type=F32)
            gate = gt_ref[:, cs]
            sg = _sigmoid(gate)
            silu = gate * sg
            dz_all = lax.dot_general(dxb, wo_ref[cs, :], NT, preferred_element_type=F32)
            dzg = dz_all[0:tm]
            dy = dzg * silu
            dp_ref[:, E + g * gc:E + (g + 1) * gc] = (dzg * (ypre * sc) * (sg * (1.0 + gate * (1.0 - sg)))).astype(BF16)
            dsc_ref[:, cs] += jnp.sum(dy * ypre, axis=0, keepdims=True)
            dyp = (dy * sc).astype(BF16)
            dwg_ref[g] += lax.dot_general(pb, dyp, TN, preferred_element_type=F32)
            dpool = lax.dot_general(dyp, w, NT, preferred_element_type=F32)
            gate_h = gth_ref[:, cs]
            dyp_h = (dz_all[tm:tm + POOL_HALO] * (gate_h * _sigmoid(gate_h)) * sc).astype(BF16)
            dpool_h = lax.dot_general(dyp_h, w, NT, preferred_element_type=F32)
            q_ref[0:tm, cs] = dpool / jnp.minimum(t1, win).astype(F32)
            q_ref[tm:tm + POOL_HALO, cs] = jnp.where(not_last, dpool_h / jnp.minimum(t1h, win).astype(F32), 0.0)
            acc = q_ref[0:tm, cs] - dpool
            for k in range(1, win):
                acc = acc + q_ref[pl.ds(k, tm), cs]
            dp_ref[:, cs] = acc.astype(BF16)

    return pl.pallas_call(
        body, name=name, grid=(n_tiles,),
        in_specs=[pl.BlockSpec((tm, D), lambda i: (i, 0)),
                  pl.BlockSpec((POOL_HALO, D), lambda i: (jnp.minimum((i + 1) * hb, last_halo), 0)),
                  pl.BlockSpec((E, D), lambda i: (0, 0)),
                  pl.BlockSpec((tm, E), lambda i: (i, 0)),
                  pl.BlockSpec((POOL_HALO, E), lambda i: (jnp.maximum(i * hb - 1, 0), 0)),
                  pl.BlockSpec((tm, E), lambda i: (i, 1)),
                  pl.BlockSpec((POOL_HALO, E), lambda i: (jnp.minimum((i + 1) * hb, last_halo), 1)),
                  pl.BlockSpec((n_grp, gc, gc), lambda i: (0, 0, 0)),
                  pl.BlockSpec((1, E), lambda i: (0, 0))],
        out_specs=[pl.BlockSpec((tm, E2), lambda i: (i, 0)),
                   pl.BlockSpec((n_grp, gc, gc), lambda i: (0, 0, 0)),
                   pl.BlockSpec((1, E), lambda i: (0, 0))],
        out_shape=[jax.ShapeDtypeStruct((S, E2), BF16),
                   jax.ShapeDtypeStruct((n_grp, gc, gc), F32),
                   jax.ShapeDtypeStruct((1, E), F32)],
        scratch_shapes=[pltpu.VMEM((POOL_HALO + tm, E), F32), pltpu.VMEM((tm + POOL_HALO, E), F32)],
        compiler_params=_params("arbitrary"))(dx, dx, w_out.reshape(E, D), proj, proj, proj, proj, wg, scale)


def _rope_k(kvp, cos2, sin2, name):
    S, E2 = kvp.shape
    E = E2 // 2
    tm = min(256, S)

    def body(k_ref, c_ref, s_ref, ko_ref):
        cosv, sinv = c_ref[...], s_ref[...]
        for h in range(E // HEAD_DIM):
            hs = slice(h * HEAD_DIM, (h + 1) * HEAD_DIM)
            ko_ref[:, hs] = _rope(k_ref[:, hs], cosv, sinv)

    return pl.pallas_call(
        body, name=name, grid=(S // tm,),
        in_specs=[pl.BlockSpec((tm, E), lambda i: (i, 0)),
                  pl.BlockSpec((tm, HEAD_DIM), lambda i: (i, 0)), pl.BlockSpec((tm, HEAD_DIM), lambda i: (i, 0))],
        out_specs=pl.BlockSpec((tm, E), lambda i: (i, 0)),
        out_shape=jax.ShapeDtypeStruct((S, E), F32),
        compiler_params=_params("parallel"))(kvp, cos2, sin2)


CLASSES = 16


def _class_order(a, back, name):
    S, W = a.shape
    M = S // CLASSES

    def body(a_ref, o_ref):
        for r in range(CLASSES):
            if back:
                o_ref[pl.ds(r, M, stride=CLASSES), :] = a_ref[pl.ds(r * M, M), :]
            else:
                o_ref[pl.ds(r * M, M), :] = a_ref[pl.ds(r, M, stride=CLASSES), :]

    col = pl.BlockSpec((S, HEAD_DIM), lambda i: (0, i))
    return pl.pallas_call(
        body, name=name, grid=(W // HEAD_DIM,), in_specs=[col], out_specs=col,
        out_shape=jax.ShapeDtypeStruct((S, W), a.dtype),
        compiler_params=_params("parallel"))(a)


def _runs(r, b, dil, M, back=0):
    nj = CLASSES // dil
    c = BAND // nj
    return [((r + dil * j) * M + (b - back) * c, (1 + back) * c) for j in range(nj)]


def _load(ref, runs):
    parts = [ref[pl.ds(start, n), :] for start, n in runs]
    return parts[0] if len(parts) == 1 else jnp.concatenate(parts, axis=0)


def _store(ref, runs, val, add=False):
    at = 0
    for start, n in runs:
        if add:
            ref[pl.ds(start, n), :] += val[at:at + n]
        else:
            ref[pl.ds(start, n), :] = val[at:at + n]
        at += n


def _keys(ref, r, b, dil, M):
    if b > 0:
        return _load(ref, _runs(r, b, dil, M, back=1)).astype(BF16)
    parts = []
    for start, n in _runs(r, 0, dil, M):
        parts += [jnp.zeros((n, HEAD_DIM), BF16), ref[pl.ds(start, n), :].astype(BF16)]
    return jnp.concatenate(parts, axis=0)


def _band_mask(dil, first):
    nj = CLASSES // dil
    c = BAND // nj
    row = lax.broadcasted_iota(jnp.int32, (BAND, 2 * BAND), 0)
    col = lax.broadcasted_iota(jnp.int32, (BAND, 2 * BAND), 1)
    q_place = (row % c) * nj + row // c
    k_place = (col % (2 * c) - c) * nj + col // (2 * c)
    mask = (q_place >= k_place) & (q_place <= k_place + BAND)
    return mask & (col % (2 * c) >= c) if first else mask


def _lane_column(tile, lane, h):
    return jnp.sum(jnp.where(lane == h, tile, 0.0), axis=-1, keepdims=True)


def _attn_fwd(proj, kr, kvp, cos2, sin2, group, dil, name):
    S, PW = proj.shape
    E = kr.shape[1]
    H = E // HEAD_DIM
    M = S // CLASSES
    nb = S // (BAND * dil)

    def body(q_ref, k_ref, v_ref, c_ref, s_ref, o_ref, l_ref):
        h = pl.program_id(0)
        lane = lax.broadcasted_iota(jnp.int32, (BAND, HEAD_DIM), 1)
        edge, inner = _band_mask(dil, True), _band_mask(dil, False)

        @pl.when(h == 0)
        def _():
            l_ref[...] = jnp.zeros_like(l_ref)

        def scores(r, b):
            runs = _runs(r, b, dil, M)
            qr = _rope(_load(q_ref, runs), _load(c_ref, runs), _load(s_ref, runs)).astype(BF16)
            return lax.dot_general(qr, _keys(k_ref, r, b, dil, M), NT, preferred_element_type=F32)

        units = [(r, b) for r in range(dil) for b in range(nb)]
        ahead = scores(*units[0])
        for i, (r, b) in enumerate(units):
            s = ahead
            if i + 1 < len(units):
                ahead = scores(*units[i + 1])
            runs = _runs(r, b, dil, M)
            s = jnp.where(edge if b == 0 else inner, s, NEG_INF)
            m = jnp.max(s, axis=-1, keepdims=True)
            p = jnp.exp2((s - m) * (ATTN_SCALE * LOG2_E))
            l = jnp.sum(p, axis=-1, keepdims=True)
            _store(o_ref, runs, jnp.dot(p.astype(BF16), _keys(v_ref, r, b, dil, M), preferred_element_type=F32) / l)
            _store(l_ref, runs, jnp.where(lane == h, m * ATTN_SCALE + jnp.log(l), _load(l_ref, runs)))

    col = (S, HEAD_DIM)
    whole = pl.BlockSpec(col, lambda h: (0, 0))
    return pl.pallas_call(
        body, name=name, grid=(H,),
        in_specs=[pl.BlockSpec(col, lambda h: (0, group * H + h)), pl.BlockSpec(col, lambda h: (0, h)),
                  pl.BlockSpec(col, lambda h: (0, H + h)), whole, whole],
        out_specs=[pl.BlockSpec(col, lambda h: (0, h)), whole],
        out_shape=[jax.ShapeDtypeStruct((S, E), F32), jax.ShapeDtypeStruct((S, HEAD_DIM), F32)],
        compiler_params=_params("arbitrary"))(proj, kr, kvp, cos2, sin2)


def _attn_bwd(proj, kr, kvp, cos2, sin2, do, lse, dlt, dproj, group, dil, name):
    S, PW = proj.shape
    E = kr.shape[1]
    H = E // HEAD_DIM
    M = S // CLASSES
    nb = S // (BAND * dil)

    def body(q_ref, k_ref, v_ref, c_ref, s_ref, do_ref, l_ref, dl_ref, dproj_ref,
             dq_ref, dk_ref, dv_ref, dq_scr, dk_scr, dv_scr):
        h = pl.program_id(0)
        lane = lax.broadcasted_iota(jnp.int32, (BAND, HEAD_DIM), 1)
        edge, inner = _band_mask(dil, True), _band_mask(dil, False)
        dk_scr[...] = jnp.zeros_like(dk_scr)
        dv_scr[...] = jnp.zeros_like(dv_scr)

        def scores(r, b):
            runs = _runs(r, b, dil, M)
            qr = _rope(_load(q_ref, runs), _load(c_ref, runs), _load(s_ref, runs)).astype(BF16)
            return qr, lax.dot_general(qr, _keys(k_ref, r, b, dil, M), NT, preferred_element_type=F32)

        units = [(r, b) for r in range(dil) for b in range(nb)]
        ahead = scores(*units[0])
        for i, (r, b) in enumerate(units):
            qr, s = ahead
            if i + 1 < len(units):
                ahead = scores(*units[i + 1])
            runs = _runs(r, b, dil, M)
            dob = _load(do_ref, runs).astype(BF16)
            dpr = lax.dot_general(dob, _keys(v_ref, r, b, dil, M), NT, preferred_element_type=F32)
            s = jnp.where(edge if b == 0 else inner, s, NEG_INF)
            p = jnp.exp2(s * (ATTN_SCALE * LOG2_E) - _lane_column(_load(l_ref, runs), lane, h) * LOG2_E)
            ds = (p * (dpr - _lane_column(_load(dl_ref, runs), lane, h)) * ATTN_SCALE).astype(BF16)
            dq = jnp.dot(ds, _keys(k_ref, r, b, dil, M), preferred_element_type=F32)
            _store(dq_scr, runs, _rope_bwd(dq, _load(c_ref, runs), _load(s_ref, runs)))
            dkc = lax.dot_general(ds, qr, TN, preferred_element_type=F32)
            dvc = lax.dot_general(p.astype(BF16), dob, TN, preferred_element_type=F32)
            if b > 0:
                both = _runs(r, b, dil, M, back=1)
                _store(dk_scr, both, dkc, add=True)
                _store(dv_scr, both, dvc, add=True)
            else:
                n = runs[0][1]
                own = jnp.concatenate([dkc[(2 * j + 1) * n:(2 * j + 2) * n] for j in range(len(runs))], axis=0)
                _store(dk_scr, runs, own, add=True)
                own = jnp.concatenate([dvc[(2 * j + 1) * n:(2 * j + 2) * n] for j in range(len(runs))], axis=0)
                _store(dv_scr, runs, own, add=True)
        dq_ref[...] = dq_scr[...].astype(BF16)
        dk_ref[...] = dk_scr[...].astype(BF16)
        dv_ref[...] = dv_scr[...].astype(BF16)

    col = (S, HEAD_DIM)
    whole = pl.BlockSpec(col, lambda h: (0, 0))
    head = pl.BlockSpec(col, lambda h: (0, h))
    return pl.pallas_call(
        body, name=name, grid=(H,),
        in_specs=[pl.BlockSpec(col, lambda h: (0, group * H + h)), head, pl.BlockSpec(col, lambda h: (0, H + h)),
                  whole, whole, head, whole, whole, ANY],
        out_specs=[pl.BlockSpec(col, lambda h: (0, group * H + h)), head, head],
        out_shape=[jax.ShapeDtypeStruct(dproj.shape, BF16), jax.ShapeDtypeStruct((S, E), BF16),
                   jax.ShapeDtypeStruct((S, E), BF16)],
        scratch_shapes=[pltpu.VMEM(col, F32)] * 3,
        input_output_aliases={8: 0},
        compiler_params=_params("parallel"))(proj, kr, kvp, cos2, sin2, do, lse, dlt, dproj)


def _group_weights(l_refs, h):
    ls = [r[:, h:h + 1] for r in l_refs]
    mx = jnp.maximum(jnp.maximum(ls[0], ls[1]), ls[2])
    es = [jnp.exp(l - mx) for l in ls]
    inv = 1.0 / (es[0] + es[1] + es[2])
    return [e * inv for e in es]


def _merge_out(outs, lses, proj, w_out, res, name):
    S, E = outs[0].shape
    D = res.shape[1]
    tm = min(256, S)
    gate_col = proj.shape[1] // E - 1

    def body(o0, o1, o2, l0, l1, l2, gt_ref, w_ref, r_ref, x_ref, z_ref):
        for h in range(E // HEAD_DIM):
            hs = slice(h * HEAD_DIM, (h + 1) * HEAD_DIM)
            a = _group_weights((l0, l1, l2), h)
            merged = a[0] * o0[:, hs] + a[1] * o1[:, hs] + a[2] * o2[:, hs]
            gate = gt_ref[:, hs]
            z_ref[:, hs] = (merged * (gate * _sigmoid(gate))).astype(BF16)
        x_ref[...] = r_ref[...] + jnp.dot(z_ref[...], w_ref[...], preferred_element_type=F32)

    wide = pl.BlockSpec((tm, E), lambda i: (i, 0))
    thin = pl.BlockSpec((tm, HEAD_DIM), lambda i: (i, 0))
    return pl.pallas_call(
        body, name=name, grid=(S // tm,),
        in_specs=[wide, wide, wide, thin, thin, thin, pl.BlockSpec((tm, E), lambda i: (i, gate_col)),
                  pl.BlockSpec((E, D), lambda i: (0, 0)), pl.BlockSpec((tm, D), lambda i: (i, 0))],
        out_specs=[pl.BlockSpec((tm, D), lambda i: (i, 0)), wide],
        out_shape=[jax.ShapeDtypeStruct((S, D), F32), jax.ShapeDtypeStruct((S, E), BF16)],
        compiler_params=_params("parallel"))(*outs, *lses, proj, w_out.reshape(E, D), res)


def _merge_bwd(dx, w_out, outs, lses, proj, name):
    S, E = outs[0].shape
    D = dx.shape[1]
    tm = min(256, S)
    gate_col = proj.shape[1] // E - 1

    def body(dx_ref, w_ref, o0, o1, o2, l0, l1, l2, gt_ref, d0, d1, d2, t0, t1, t2, dg_ref):
        o_refs, d_refs, t_refs = (o0, o1, o2), (d0, d1, d2), (t0, t1, t2)
        lane = lax.broadcasted_iota(jnp.int32, (tm, HEAD_DIM), 1)
        tiles = [jnp.zeros((tm, HEAD_DIM), F32) for _ in range(3)]
        dxb = dx_ref[...].astype(BF16)
        for h in range(E // HEAD_DIM):
            hs = slice(h * HEAD_DIM, (h + 1) * HEAD_DIM)
            a = _group_weights((l0, l1, l2), h)
            merged = a[0] * o0[:, hs] + a[1] * o1[:, hs] + a[2] * o2[:, hs]
            gate = gt_ref[:, hs]
            sg = _sigmoid(gate)
            dzh = lax.dot_general(dxb, w_ref[hs, :], NT, preferred_element_type=F32)
            dmerged = dzh * (gate * sg)
            dg_ref[:, hs] = (dzh * merged * (sg * (1.0 + gate * (1.0 - sg)))).astype(BF16)
            tot = jnp.sum(dmerged * merged, axis=-1, keepdims=True)
            for g in range(3):
                d_refs[g][:, hs] = a[g] * dmerged
                tiles[g] = jnp.where(lane == h, a[g] * tot, tiles[g])
        for g in range(3):
            t_refs[g][...] = tiles[g]

    wide = pl.BlockSpec((tm, E), lambda i: (i, 0))
    thin = pl.BlockSpec((tm, HEAD_DIM), lambda i: (i, 0))
    res = pl.pallas_call(
        body, name=name, grid=(S // tm,),
        in_specs=[pl.BlockSpec((tm, D), lambda i: (i, 0)), pl.BlockSpec((E, D), lambda i: (0, 0)),
                  wide, wide, wide, thin, thin, thin, pl.BlockSpec((tm, E), lambda i: (i, gate_col))],
        out_specs=[wide, wide, wide, thin, thin, thin, pl.BlockSpec((tm, E), lambda i: (i, gate_col))],
        out_shape=[jax.ShapeDtypeStruct((S, E), F32)] * 3 + [jax.ShapeDtypeStruct((S, HEAD_DIM), F32)] * 3
        + [jax.ShapeDtypeStruct(proj.shape, BF16)],
        compiler_params=_params("parallel"))(dx, w_out.reshape(E, D), *outs, *lses, proj)
    return res[0:3], res[3:6], res[6]


def _kv_bwd(dks, dvs, cos2, sin2, name):
    S, E = dks[0].shape
    n = len(dks)
    tm = min(256, S)

    def body(*refs):
        dk_refs, dv_refs = refs[0:n], refs[n:2 * n]
        c_ref, s_ref, o_ref = refs[2 * n:]
        cosv, sinv = c_ref[...], s_ref[...]
        for h in range(E // HEAD_DIM):
            hs = slice(h * HEAD_DIM, (h + 1) * HEAD_DIM)
            dk = dk_refs[0][:, hs].astype(F32)
            dv = dv_refs[0][:, hs].astype(F32)
            for j in range(1, n):
                dk = dk + dk_refs[j][:, hs].astype(F32)
                dv = dv + dv_refs[j][:, hs].astype(F32)
            o_ref[:, hs] = _rope_bwd(dk, cosv, sinv).astype(BF16)
            o_ref[:, E + h * HEAD_DIM:E + (h + 1) * HEAD_DIM] = dv.astype(BF16)

    wide = pl.BlockSpec((tm, E), lambda i: (i, 0))
    thin = pl.BlockSpec((tm, HEAD_DIM), lambda i: (i, 0))
    return pl.pallas_call(
        body, name=name, grid=(S // tm,),
        in_specs=[wide] * (2 * n) + [thin, thin],
        out_specs=pl.BlockSpec((tm, 2 * E), lambda i: (i, 0)),
        out_shape=jax.ShapeDtypeStruct((S, 2 * E), BF16),
        compiler_params=_params("parallel"))(*dks, *dvs, cos2, sin2)


def _final_norm_loss(x, target, gain, name):
    S, D = x.shape
    tm = min(256, S)

    def body(x_ref, t_ref, g_ref, loss_ref, dx_ref, dg_ref):
        @pl.when(pl.program_id(0) == 0)
        def _():
            loss_ref[...] = jnp.zeros_like(loss_ref)
            dg_ref[...] = jnp.zeros_like(dg_ref)
        xf = x_ref[...]
        inv = lax.rsqrt(jnp.mean(xf * xf, axis=-1, keepdims=True) + RMS_EPS)
        xhat = xf * inv
        g = g_ref[...]
        err = xhat * g - t_ref[...]
        loss_ref[...] += 0.5 * jnp.sum(jnp.mean(err * err, axis=-1, keepdims=True), axis=0, keepdims=True)
        dy = err / D
        dg_ref[...] += jnp.sum(dy * xhat, axis=0, keepdims=True)
        dxh = dy * g
        dx_ref[...] = inv * (dxh - xhat * jnp.mean(dxh * xhat, axis=-1, keepdims=True))

    tile = pl.BlockSpec((tm, D), lambda i: (i, 0))
    vec = pl.BlockSpec((1, D), lambda i: (0, 0))
    return pl.pallas_call(
        body, name=name, grid=(S // tm,),
        in_specs=[tile, tile, vec],
        out_specs=[pl.BlockSpec((1, 1), lambda i: (0, 0)), tile, vec],
        out_shape=[jax.ShapeDtypeStruct((1, 1), F32), jax.ShapeDtypeStruct((S, D), F32),
                   jax.ShapeDtypeStruct((1, D), F32)],
        compiler_params=_params("arbitrary"))(x, target, gain)


def _adamw_math(g, w, m, v):
    m = ADAM_B1 * m + (1.0 - ADAM_B1) * g
    v = ADAM_B2 * v + (1.0 - ADAM_B2) * (g * g)
    m_hat = m / (1.0 - ADAM_B1 ** ADAM_STEP)
    v_hat = v / (1.0 - ADAM_B2 ** ADAM_STEP)
    delta = -ADAM_LR * (m_hat / (jnp.sqrt(v_hat) + ADAM_EPS) + ADAM_WD * w)
    return delta, m, v


def _adamw_rows(g, w, m, v, name):
    def body(g_ref, w_ref, m_ref, v_ref, d_ref, mo_ref, vo_ref):
        d_ref[...], mo_ref[...], vo_ref[...] = _adamw_math(g_ref[...], w_ref[...], m_ref[...], v_ref[...])

    whole = pl.BlockSpec(memory_space=pltpu.VMEM)
    return pl.pallas_call(
        body, name=name, in_specs=[whole] * 4, out_specs=[whole] * 3,
        out_shape=[jax.ShapeDtypeStruct(g.shape, F32)] * 3)(g, w, m, v)


def _adamw_blocks(own, others, w, m, v, layer, earlier, name):
    L, R, C = w.shape
    n = others.shape[0]
    tr = R
    while tr * C > 128 * 1024 and tr % 16 == 0:
        tr //= 2

    def body(o_ref, p_ref, w_ref, m_ref, v_ref, *rest):
        g_ref, d_ref, mo_ref, vo_ref = rest[-4:]
        g = o_ref[...].astype(F32)
        for j in range(n):
            g = g + p_ref[j].astype(F32)
        g_ref[...] = g
        d_ref[...], mo_ref[...], vo_ref[...] = _adamw_math(g, w_ref[...], m_ref[...], v_ref[...])

    tile = pl.BlockSpec((None, tr, C), lambda i: (layer, i, 0))
    kept = [] if earlier is None else list(earlier)
    return pl.pallas_call(
        body, name=name, grid=(R // tr,),
        in_specs=[pl.BlockSpec((tr, C), lambda i: (i, 0)), pl.BlockSpec((n, tr, C), lambda i: (0, i, 0)),
                  tile, tile, tile] + [ANY] * len(kept),
        out_specs=[tile] * 4,
        out_shape=[jax.ShapeDtypeStruct((L, R, C), F32)] * 4,
        input_output_aliases={5 + j: j for j in range(len(kept))},
        compiler_params=_params("parallel"))(own, others, w, m, v, *kept)


def _position():
    return lax.axis_index("x"), lax.axis_index("y"), lax.axis_index("c")


def _block_index(px, py, pc):
    return 4 * px + 2 * py + pc


def _all_gather(shards, name):
    n = len(shards)

    def body(*refs):
        ins, outs = refs[0:n], refs[n:2 * n]
        send_sems, recv_sems, local_sems = refs[2 * n:]
        x, y, c = _position()
        me, sibling = (x, y, c), (x, y, 1 - c)
        chips = [(1 - x, y), (x, 1 - y), (1 - x, 1 - y)]

        def copy(a, k, block, to, src=None):
            rows = outs[a].at[_block_index(*block)]
            return pltpu.make_async_remote_copy(
                src_ref=rows if src is None else src, dst_ref=rows,
                send_sem=send_sems.at[a, k], recv_sem=recv_sems.at[a, k], device_id=to, device_id_type=MESH)

        mine, first, passed = [], [], []
        for a in range(n):
            cp = pltpu.make_async_copy(ins[a], outs[a].at[_block_index(*me)], local_sems.at[a])
            cp.start()
            mine.append(cp)
            first.append(copy(a, 0, me, sibling, src=ins[a]))
            first += [copy(a, 1 + j, me, (*chip, c), src=ins[a]) for j, chip in enumerate(chips)]
        for cp in first:
            cp.start()
        for j, chip in enumerate(chips):
            for a in range(n):
                copy(a, 1 + j, (*chip, c), me).wait_recv()
                fwd = copy(a, 4 + j, (*chip, c), sibling)
                fwd.start()
                passed.append(fwd)
        for a in range(n):
            copy(a, 0, sibling, me).wait_recv()
            for j, chip in enumerate(chips):
                copy(a, 4 + j, (*chip, 1 - c), me).wait_recv()
        for cp in first + passed:
            cp.wait_send()
        for cp in mine:
            cp.wait()

    return pl.pallas_call(
        body, name=name,
        in_specs=[ANY] * n, out_specs=[ANY] * n,
        out_shape=[jax.ShapeDtypeStruct((N_DEV,) + s.shape, s.dtype) for s in shards],
        scratch_shapes=[pltpu.SemaphoreType.DMA((n, 7)), pltpu.SemaphoreType.DMA((n, 7)),
                        pltpu.SemaphoreType.DMA((n,))],
    )(*shards)


def _peers(x, y, c):
    return [((1 - x) if k & 4 else x, (1 - y) if k & 2 else y, (1 - c) if k & 1 else c) for k in range(1, N_DEV)]


HBM = pl.BlockSpec(memory_space=pltpu.HBM)
SEM = pl.BlockSpec(memory_space=pltpu.SEMAPHORE)
EFFECT = pltpu.SideEffectType.DATAFLOW_SIDE_EFFECTING


ALL_PEERS = (1, 2, 3, 4, 5, 6, 7)
SIBLING_AND_SAME_CORES = (1, 2, 4, 6)


def _push_copy(src_refs, land_refs, send_sems, recv_sems, a, i, relations, per_peer, by_sender, arriving):
    peer = _peers(*_position())[relations[i] - 1]
    me_idx, p_idx = _block_index(*_position()), _block_index(*peer)
    src = src_refs[a].at[p_idx] if per_peer else src_refs[a]
    if by_sender:
        slot = p_idx if arriving else me_idx
    else:
        slot = relations[i] - 1
    sem = a * len(relations) + i
    return pltpu.make_async_remote_copy(
        src_ref=src, dst_ref=land_refs[a].at[slot], send_sem=send_sems.at[sem], recv_sem=recv_sems.at[sem],
        device_id=peer, device_id_type=MESH)


def _push_start(srcs, lands, relations, per_peer, by_sender, after, name):
    n = len(srcs)

    def body(*refs):
        src_refs, land_refs = refs[0:n], refs[n:2 * n]
        send_sems, recv_sems = refs[2 * n + 1], refs[2 * n + 2]
        token = refs[-1]
        for a in range(n):
            for i in range(len(relations)):
                _push_copy(src_refs, land_refs, send_sems, recv_sems, a, i, relations, per_peer, by_sender, False).start()
        token[...] = jnp.zeros_like(token)

    args = [pltpu.with_memory_space_constraint(t, pltpu.HBM) for t in list(srcs) + list(lands)]
    res = pl.pallas_call(
        body, name=name,
        in_specs=[HBM] * (2 * n) + [ANY],
        out_specs=[SEM, SEM] + [HBM] * (2 * n) + [pl.BlockSpec(memory_space=pltpu.VMEM)],
        out_shape=[pltpu.SemaphoreType.DMA((n * len(relations),)), pltpu.SemaphoreType.DMA((n * len(relations),))]
        + [pltpu.HBM(t.shape, t.dtype) for t in args] + [jax.ShapeDtypeStruct((8, 128), F32)],
        input_output_aliases={i: 2 + i for i in range(2 * n)},
        compiler_params=pltpu.CompilerParams(has_side_effects=EFFECT))(*args, after)
    return res[0], res[1], res[2:2 + n], res[2 + n:2 + 2 * n], res[-1]


def _push_wait(started, relations, per_peer, by_sender, after, name):
    send_sems, recv_sems, srcs, lands, _ = started
    n = len(srcs)

    def body(*refs):
        src_refs, land_refs = refs[0:n], refs[n:2 * n]
        send_s, recv_s = refs[2 * n], refs[2 * n + 1]
        for a in range(n):
            for i in range(len(relations)):
                _push_copy(src_refs, land_refs, send_s, recv_s, a, i, relations, per_peer, by_sender, False).wait_send()
                _push_copy(src_refs, land_refs, send_s, recv_s, a, i, relations, per_peer, by_sender, True).wait_recv()

    res = pl.pallas_call(
        body, name=name,
        in_specs=[HBM] * (2 * n) + [SEM, SEM, ANY],
        out_specs=[HBM] * (2 * n),
        out_shape=[pltpu.HBM(t.shape, t.dtype) for t in list(srcs) + list(lands)],
        input_output_aliases={i: i for i in range(2 * n)},
        compiler_params=pltpu.CompilerParams(has_side_effects=EFFECT))(*srcs, *lands, send_sems, recv_sems, after)
    return res[0:n], res[n:2 * n]


def _pass_on(lands, name):
    n = len(lands)

    def body(*refs):
        outs = refs[n:2 * n]
        send_sems, recv_sems = refs[2 * n:]
        x, y, c = _position()
        chips = [(1 - x, y), (x, 1 - y), (1 - x, 1 - y)]
        copies = []
        for a in range(n):
            for j, chip in enumerate(chips):
                def copy(core):
                    rows = outs[a].at[_block_index(*chip, core)]
                    return pltpu.make_async_remote_copy(
                        src_ref=rows, dst_ref=rows, send_sem=send_sems.at[a, j], recv_sem=recv_sems.at[a, j],
                        device_id=(x, y, 1 - c), device_id_type=MESH)
                copy(c).start()
                copies.append((copy(c), copy(1 - c)))
        for sending, arriving in copies:
            sending.wait_send()
            arriving.wait_recv()

    return pl.pallas_call(
        body, name=name,
        in_specs=[ANY] * n, out_specs=[ANY] * n,
        out_shape=[jax.ShapeDtypeStruct(t.shape, t.dtype) for t in lands],
        input_output_aliases={a: a for a in range(n)},
        scratch_shapes=[pltpu.SemaphoreType.DMA((n, 3)), pltpu.SemaphoreType.DMA((n, 3))],
    )(*lands)


def _all_reduce_rows(v, name):
    R, D = v.shape

    def body(v_ref, o_ref, buf_ref, send_sems, recv_sems):
        x, y, c = _position()
        me_idx = _block_index(x, y, c)
        buf_ref[me_idx] = v_ref[...]
        copies = []
        for k in range(1, N_DEV):
            px = (1 - x) if k & 4 else x
            py = (1 - y) if k & 2 else y
            pc = (1 - c) if k & 1 else c
            rc = pltpu.make_async_remote_copy(
                src_ref=v_ref, dst_ref=buf_ref.at[me_idx],
                send_sem=send_sems.at[k - 1], recv_sem=recv_sems.at[k - 1],
                device_id=(px, py, pc), device_id_type=MESH)
            rc.start()
            copies.append((rc, pltpu.make_async_remote_copy(
                src_ref=v_ref, dst_ref=buf_ref.at[_block_index(px, py, pc)],
                send_sem=send_sems.at[k - 1], recv_sem=recv_sems.at[k - 1],
                device_id=(px, py, pc), device_id_type=MESH)))
        for rc, arrival in copies:
            rc.wait_send()
            arrival.wait_recv()
        acc = buf_ref[0]
        for j in range(1, N_DEV):
            acc = acc + buf_ref[j]
        o_ref[...] = acc

    return pl.pallas_call(
        body, name=name,
        in_specs=[pl.BlockSpec(memory_space=pltpu.VMEM)],
        out_specs=pl.BlockSpec(memory_space=pltpu.VMEM),
        out_shape=jax.ShapeDtypeStruct((R, D), F32),
        scratch_shapes=[pltpu.VMEM((N_DEV, R, D), F32),
                        pltpu.SemaphoreType.DMA((7,)), pltpu.SemaphoreType.DMA((7,))],
    )(v)


def _rope_tables(S):
    inv_freq = 1.0 / (ROPE_THETA ** (jnp.arange(0, HEAD_DIM, 2, dtype=F32) / HEAD_DIM))
    ang = jnp.arange(S, dtype=F32)[:, None] * inv_freq[None, :]
    cos, sin = jnp.cos(ang), jnp.sin(ang)
    return jnp.concatenate([cos, cos], axis=1), jnp.concatenate([-sin, sin], axis=1)


def _local_step(xs, target, vecs, n_a, n_b, get_weights, put_grads):
    S, D = xs.shape
    E = D
    cos2, sin2 = _rope_tables(S)
    ts = min(1024, S)

    def col_blocks(w):
        cb = w.shape[2]
        tn = min(cb, 1024)
        per = cb // tn
        return (None, D, tn), (lambda i, j: (j // per, 0, j % per)), N_DEV * per, tn

    def grad_in(hn, dproj, cb, name):
        return _matmul_tn(hn, dproj, (ts, D), lambda j, s: (s, 0), (ts, cb), lambda j, s: (s, j),
                          (N_DEV, D, cb), (None, D, cb), lambda j, s: (j, 0, 0), (D, cb), N_DEV, name)

    def grad_out(z, dx, name, col=0):
        rows = z.shape[1]
        ta = min(1024, rows)
        out = _matmul_tn(z, dx, (ts, ta), lambda a, s: (s, a), (ts, E), lambda a, s: (s, col),
                         (rows, E), (ta, E), lambda a, s: (a, 0), (ta, E), rows // ta, name)
        return out.reshape(N_DEV, rows // N_DEV, E)

    x = xs
    a_saved, b_saved = [], []
    for i in range(n_a):
        w = get_weights(f"a{i}", x)
        blk, idx, nblocks, tn = col_blocks(w["w_in"])
        proj, hn = _norm_matmul(x, vecs["norm_a"][i:i + 1], w["w_in"], blk, idx, nblocks, tn, f"a{i}_in")
        w = {**w, **get_weights(f"a{i}_rest", proj)}
        x_next, z = _a_mid_out(proj, w["w_grp"], vecs["scale_a"][i:i + 1], w["w_out"], x, f"a{i}_out")
        a_saved.append((x, hn, proj, z, w))
        x = x_next
    w_kv = get_weights("kv", x)["w_kv"]
    x = x_kv = _class_order(x, False, "kv_x_order")
    target = _class_order(target, False, "target_order")
    cos2, sin2 = _class_order(cos2, False, "cos_order"), _class_order(sin2, False, "sin_order")
    tn = min(E, 1024)
    kvp, hn_kv = _norm_matmul(x, vecs["norm_kv"], w_kv, (D, tn), lambda i, j: (0, j), 2 * E // tn, tn, "kv_in")
    kr = _rope_k(kvp, cos2, sin2, "kv_rope")
    after = kr
    for i in range(n_b):
        w = get_weights(f"b{i}", after)
        blk, idx, nblocks, tn = col_blocks(w["w_in"])
        proj, hn = _norm_matmul(x, vecs["norm_b"][i:i + 1], w["w_in"], blk, idx, nblocks, tn, f"b{i}_in")
        outs, lses = [], []
        for g, dil in enumerate(DILATIONS):
            o, l = _attn_fwd(proj, kr, kvp, cos2, sin2, g, dil, f"b{i}_attn{g}")
            outs.append(o)
            lses.append(l)
        x_next, z = _merge_out(outs, lses, proj, w["w_out"], x, f"b{i}_out")
        b_saved.append((x, hn, proj, z, outs, lses, w))
        x = x_next
        after = x
    loss, dx, dg_f = _final_norm_loss(x, target, vecs["norm_f"], "final")

    vec = {"norm_a": [None] * n_a, "scale_a": [None] * n_a, "norm_b": [None] * n_b, "norm_f": [dg_f]}
    dks, dvs = [], []
    for i in reversed(range(n_b)):
        x_in, hn, proj, z, outs, lses, w = b_saved[i]
        dw_out = grad_out(z, dx, f"b{i}_dwout")
        dos, dlts, dproj = _merge_bwd(dx, w["w_out"], outs, lses, proj, f"b{i}_dmerge")
        for g, dil in enumerate(DILATIONS):
            dproj, dk, dv = _attn_bwd(proj, kr, kvp, cos2, sin2, dos[g], lses[g], dlts[g], dproj, g, dil,
                                      f"b{i}_dattn{g}")
            dks.append(dk)
            dvs.append(dv)
        cb = w["w_in"].shape[2]
        tok = put_grads(f"b{i}", {"w_out": dw_out, "w_in": grad_in(hn, dproj, cb, f"b{i}_dwin")})
        dx, vec["norm_b"][i] = _matmul_nt_dnorm(dproj, w["w_in"], (None, D, cb), lambda t, j: (j, 0, 0), N_DEV, cb, x_in,
                                                vecs["norm_b"][i:i + 1] + tok[0:1, 0:1], dx, f"b{i}_dhn")

    dkv = _kv_bwd(dks, dvs, cos2, sin2, "kv_dsum")
    tok = put_grads("kv", {"w_k": grad_out(hn_kv, dkv, "kv_dwk", 0), "w_v": grad_out(hn_kv, dkv, "kv_dwv", 1)})
    tk = min(E, 1024)
    dx, dg_kv = _matmul_nt_dnorm(dkv, w_kv, (D, tk), lambda t, j: (0, j), 2 * E // tk, tk, x_kv,
                                 vecs["norm_kv"] + tok[0:1, 0:1], dx, "kv_dhn")
    vec["norm_kv"] = [dg_kv]
    dx = _class_order(dx, True, "kv_dx_order")

    for i in reversed(range(n_a)):
        x_in, hn, proj, z, w = a_saved[i]
        dw_out = grad_out(z, dx, f"a{i}_dwout")
        dproj, dwg, dsc = _a_mid_bwd(dx, w["w_out"], proj, w["w_grp"], vecs["scale_a"][i:i + 1], f"a{i}_dmid")
        n_grp, gc, _ = dwg.shape
        dwg = dwg.reshape(n_grp, N_DEV, gc // N_DEV, gc).transpose(1, 0, 2, 3).astype(BF16)
        vec["scale_a"][i] = dsc
        cb = w["w_in"].shape[2]
        tok = put_grads(f"a{i}", {"w_out": dw_out, "w_grp": dwg, "w_in": grad_in(hn, dproj, cb, f"a{i}_dwin")})
        dx, vec["norm_a"][i] = _matmul_nt_dnorm(dproj, w["w_in"], (None, D, cb), lambda t, j: (j, 0, 0), N_DEV, cb, x_in,
                                                vecs["norm_a"][i:i + 1] + tok[0:1, 0:1], dx, f"a{i}_dhn")

    return loss, dx, {k: jnp.concatenate(v, axis=0) for k, v in vec.items()}


VECTORS = ("norm_a", "scale_a", "norm_kv", "norm_b", "norm_f")
SHARDED_VECTORS = ("norm_a", "scale_a")
GROUPS = {
    "a0": (("w_in", "w_in_a", 0), ("w_grp", "w_grp_a", 0), ("w_out", "w_out_a", 0)),
    "a1": (("w_in", "w_in_a", 1), ("w_grp", "w_grp_a", 1), ("w_out", "w_out_a", 1)),
    "kv": (("w_k", "w_k", None), ("w_v", "w_v", None)),
    "b0": (("w_in", "w_in_b", 0), ("w_out", "w_out_b", 0)),
    "b1": (("w_in", "w_in_b", 1), ("w_out", "w_out_b", 1)),
}
FIRST = (("w_in", "w_in_a", 0),)
PREFETCHED = {"a0_rest": GROUPS["a0"][1:], "a1": GROUPS["a1"], "kv": GROUPS["kv"], "b0": GROUPS["b0"], "b1": GROUPS["b1"]}


def kernel(x, norm_a, w_in_a, w_grp_a, scale_a, w_out_a, norm_kv, w_k, w_v, norm_b, w_in_b, w_out_b, norm_f, loss_target, m_norm_a, m_w_in_a, m_w_grp_a, m_scale_a, m_w_out_a, m_norm_kv, m_w_k, m_w_v, m_norm_b, m_w_in_b, m_w_out_b, m_norm_f, v_norm_a, v_w_in_a, v_w_grp_a, v_scale_a, v_w_out_a, v_norm_kv, v_w_k, v_w_v, v_norm_b, v_w_in_b, v_w_out_b, v_norm_f):
    w = dict(norm_a=norm_a, w_in_a=w_in_a, w_grp_a=w_grp_a, scale_a=scale_a, w_out_a=w_out_a, norm_kv=norm_kv,
             w_k=w_k, w_v=w_v, norm_b=norm_b, w_in_b=w_in_b, w_out_b=w_out_b, norm_f=norm_f)
    m = dict(norm_a=m_norm_a, w_in_a=m_w_in_a, w_grp_a=m_w_grp_a, scale_a=m_scale_a, w_out_a=m_w_out_a,
             norm_kv=m_norm_kv, w_k=m_w_k, w_v=m_w_v, norm_b=m_norm_b, w_in_b=m_w_in_b, w_out_b=m_w_out_b,
             norm_f=m_norm_f)
    v = dict(norm_a=v_norm_a, w_in_a=v_w_in_a, w_grp_a=v_w_grp_a, scale_a=v_scale_a, w_out_a=v_w_out_a,
             norm_kv=v_norm_kv, w_k=v_w_k, w_v=v_w_v, norm_b=v_norm_b, w_in_b=v_w_in_b, w_out_b=v_w_out_b,
             norm_f=v_norm_f)
    D = x.shape[2]
    me = _block_index(*_position())

    def shard(members):
        return [w[p].astype(BF16) if layer is None else w[p][layer].astype(BF16) for _, p, layer in members]

    def as_weights(members, gathered):
        out = dict(zip([n for n, _, _ in members], gathered))
        if "w_grp" in out:
            g = out["w_grp"]
            out["w_grp"] = g.transpose(1, 0, 2, 3).reshape(g.shape[1], g.shape[3], g.shape[3])
        if "w_k" in out:
            out = {"w_kv": jnp.concatenate([out["w_k"].reshape(D, D), out["w_v"].reshape(D, D)], axis=1)}
        return out

    first = _all_gather(shard(FIRST) + [w[k] for k in SHARDED_VECTORS], "gather_first")
    n_first = len(FIRST)
    vecs = {k: g.transpose(1, 0, 2).reshape(w[k].shape[0], D) for k, g in zip(SHARDED_VECTORS, first[n_first:])}
    vecs.update(norm_kv=norm_kv[None, :], norm_b=norm_b, norm_f=norm_f[None, :])
    srcs, lands = [], []
    for group in PREFETCHED:
        for s in shard(PREFETCHED[group]):
            srcs.append(s)
            lands.append(lax.dynamic_update_index_in_dim(lax.empty((N_DEV,) + s.shape, s.dtype), s[None], me, 0))
    inflight, at = {}, 0
    token = None
    for group in PREFETCHED:
        n = len(PREFETCHED[group])
        inflight[group] = _push_start(srcs[at:at + n], lands[at:at + n], SIBLING_AND_SAME_CORES, False, True,
                                      first[0] if token is None else token, f"gather_{group}_start")
        token = inflight[group][4]
        at += n
    vecs["norm_a"] = vecs["norm_a"] + token[0:1, 0:1]

    def get_weights(group, after):
        if group == "a0":
            return as_weights(FIRST, first[0:n_first])
        if group not in PREFETCHED:
            return {}
        half = _push_wait(inflight[group], SIBLING_AND_SAME_CORES, False, True, after, f"gather_{group}_wait")[1]
        return as_weights(PREFETCHED[group], _pass_on(half, f"gather_{group}_pass"))

    sent = {}

    def put_grads(group, grads):
        blocks = [grads[n] for n, _, _ in GROUPS[group]]
        lands = [lax.empty((N_DEV - 1,) + b.shape[1:], b.dtype) for b in blocks]
        sent[group] = _push_start(blocks, lands, ALL_PEERS, True, False, jnp.zeros((8, 128), F32),
                                  f"exchange_{group}_start")
        return sent[group][4]

    loss, dx, vec = _local_step(x[0], loss_target[0], vecs, w_in_a.shape[0], w_in_b.shape[0], get_weights, put_grads)
    rows = _all_reduce_rows(jnp.concatenate([vec[k] for k in VECTORS], axis=0), "reduce_vectors")

    out = {}
    after = dx
    for group in sent:
        blocks, arrived = _push_wait(sent[group], ALL_PEERS, True, False, after, f"exchange_{group}_wait")
        for (_, p, layer), blk, got in zip(GROUPS[group], blocks, arrived):
            cols = w[p].shape[-1]
            own = lax.dynamic_index_in_dim(blk, me, 0, keepdims=False).reshape(-1, cols)
            n_layers = 1 if layer is None else w[p].shape[0]
            stacked = lambda t: t.reshape(n_layers, -1, cols)
            res = _adamw_blocks(own, got.reshape(N_DEV - 1, -1, cols), stacked(w[p]), stacked(m[p]), stacked(v[p]),
                                0 if layer is None else layer, out.get(p), f"adamw_{group}_{p}")
            out[p] = res
            after = res[1]
    out = {p: [r.reshape(w[p].shape) for r in res] for p, res in out.items()}
    start = 0
    for k in VECTORS:
        n_rows = vec[k].shape[0]
        g = rows[start:start + n_rows]
        start += n_rows
        if k in SHARDED_VECTORS:
            g = lax.dynamic_slice_in_dim(g, me * (D // N_DEV), D // N_DEV, axis=1)
        res = _adamw_rows(g, w[k].reshape(g.shape), m[k].reshape(g.shape), v[k].reshape(g.shape), f"adamw_{k}")
        out[k] = [r.reshape(w[k].shape) for r in [g] + list(res)]

    names = ("norm_a", "w_in_a", "w_grp_a", "scale_a", "w_out_a", "norm_kv", "w_k", "w_v", "norm_b", "w_in_b",
             "w_out_b", "norm_f")
    total = lax.psum(loss[0, 0], ("x", "y", "c"))
    return (total, dx[None], *[out[k][0] for k in names], *[out[k][1] for k in names],
            *[out[k][2] for k in names], *[out[k][3] for k in names])
```

```python
import math

import jax
import jax.numpy as jnp
from jax import lax
from jax.experimental import pallas as pl
from jax.experimental.pallas import tpu as pltpu

F32 = jnp.float32
BF16 = jnp.bfloat16

N_DEV = 8
MESH = pl.DeviceIdType.MESH
RMS_EPS = 1e-6
HEAD_DIM = 128
HALF_HEAD = HEAD_DIM // 2
BAND = 128
DILATIONS = (1, 4, 16)
POOL_WINDOWS = (2, 4, 8, 16)
POOL_HALO = 16
ROPE_THETA = 10000.0
NEG_INF = -1e30
ATTN_SCALE = 1.0 / math.sqrt(HEAD_DIM)
LOG2_E = math.log2(math.e)
ADAM_LR, ADAM_B1, ADAM_B2, ADAM_EPS, ADAM_WD, ADAM_STEP = 0.001, 0.9, 0.999, 1e-08, 0.01, 10
VMEM_LIMIT_BYTES = 56 * 1024 * 1024
ANY = pl.BlockSpec(memory_space=pl.ANY)
NT = (((1,), (1,)), ((), ()))
TN = (((0,), (0,)), ((), ()))


def _params(*semantics):
    return pltpu.CompilerParams(dimension_semantics=semantics, vmem_limit_bytes=VMEM_LIMIT_BYTES)


def _sigmoid(t):
    return 1.0 / (1.0 + jnp.exp(-t))


def _rope(t, cos2, sin2):
    return t * cos2 + pltpu.roll(t, HALF_HEAD, 1) * sin2


def _rope_bwd(dt, cos2, sin2):
    return dt * cos2 + pltpu.roll(dt * sin2, HALF_HEAD, 1)


def _norm_matmul(x, gain, w, w_block, w_index, n_col_blocks, tn, name):
    S, D = x.shape
    tm = min(1024, S)

    def body(x_ref, g_ref, w_ref, o_ref, hn_ref, hs_ref):
        @pl.when(pl.program_id(1) == 0)
        def _():
            xf = x_ref[...]
            inv = lax.rsqrt(jnp.mean(xf * xf, axis=-1, keepdims=True) + RMS_EPS)
            hb = ((xf * inv) * g_ref[...]).astype(BF16)
            hs_ref[...] = hb
            hn_ref[...] = hb
        o_ref[...] = jnp.dot(hs_ref[...], w_ref[...], preferred_element_type=F32)

    return pl.pallas_call(
        body, name=name, grid=(S // tm, n_col_blocks),
        in_specs=[pl.BlockSpec((tm, D), lambda i, j: (i, 0)),
                  pl.BlockSpec((1, D), lambda i, j: (0, 0)),
                  pl.BlockSpec(w_block, w_index)],
        out_specs=[pl.BlockSpec((tm, tn), lambda i, j: (i, j)),
                   pl.BlockSpec((tm, D), lambda i, j: (i, 0))],
        out_shape=[jax.ShapeDtypeStruct((S, n_col_blocks * tn), F32), jax.ShapeDtypeStruct((S, D), BF16)],
        scratch_shapes=[pltpu.VMEM((tm, D), BF16)],
        compiler_params=_params("parallel", "arbitrary"))(x, gain, w)


def _matmul_nt_dnorm(dp, w, w_block, w_index, n_red, tc, x, gain, dres, name):
    S, D = x.shape
    tm = min(1024, S)
    parts = 4
    tp = tm // parts

    def body(d_ref, w_ref, x_ref, g_ref, r_ref, dx_ref, dg_ref, acc_ref):
        i, j = pl.program_id(0), pl.program_id(1)

        @pl.when((i == 0) & (j == 0))
        def _():
            dg_ref[...] = jnp.zeros_like(dg_ref)

        @pl.when(j == 0)
        def _():
            acc_ref[...] = jnp.zeros_like(acc_ref)

        @pl.when(j < n_red)
        def _():
            acc_ref[...] += lax.dot_general(d_ref[...], w_ref[...], NT, preferred_element_type=F32)

        @pl.when(j >= n_red)
        def _():
            xf = x_ref[...]
            inv = lax.rsqrt(jnp.mean(xf * xf, axis=-1, keepdims=True) + RMS_EPS)
            xhat = xf * inv
            dh = acc_ref[pl.ds(pl.multiple_of((j - n_red) * tp, tp), tp), :]
            dg_ref[...] += jnp.sum(dh * xhat, axis=0, keepdims=True)
            dxh = dh * g_ref[...]
            dx_ref[...] = r_ref[...] + inv * (dxh - xhat * jnp.mean(dxh * xhat, axis=-1, keepdims=True))

    last = n_red - 1
    part = pl.BlockSpec((tp, D), lambda i, j: (i * parts + jnp.clip(j - n_red, 0, parts - 1), 0))
    vec = pl.BlockSpec((1, D), lambda i, j: (0, 0))
    return pl.pallas_call(
        body, name=name, grid=(S // tm, n_red + parts),
        in_specs=[pl.BlockSpec((tm, tc), lambda i, j: (i, jnp.minimum(j, last))),
                  pl.BlockSpec(w_block, lambda i, j: w_index(i, jnp.minimum(j, last))), part, vec, part],
        out_specs=[part, vec],
        out_shape=[jax.ShapeDtypeStruct((S, D), F32), jax.ShapeDtypeStruct((1, D), F32)],
        scratch_shapes=[pltpu.VMEM((tm, D), F32)],
        compiler_params=_params("arbitrary", "arbitrary"))(dp, w, x, gain, dres)


def _matmul_tn(a, b, a_block, a_index, b_block, b_index, out_shape, out_block, out_index, acc_shape, n_outer, name):
    S = a.shape[0]
    ts = a_block[0]
    n_tok = S // ts

    def body(a_ref, b_ref, o_ref, acc_ref):
        s = pl.program_id(1)

        @pl.when(s == 0)
        def _():
            acc_ref[...] = jnp.zeros_like(acc_ref)
        acc_ref[...] += lax.dot_general(a_ref[...].astype(BF16), b_ref[...].astype(BF16), TN,
                                        preferred_element_type=F32)

        @pl.when(s == n_tok - 1)
        def _():
            o_ref[...] = acc_ref[...].astype(o_ref.dtype)

    return pl.pallas_call(
        body, name=name, grid=(n_outer, n_tok),
        in_specs=[pl.BlockSpec(a_block, a_index), pl.BlockSpec(b_block, b_index)],
        out_specs=pl.BlockSpec(out_block, out_index),
        out_shape=jax.ShapeDtypeStruct(out_shape, BF16),
        scratch_shapes=[pltpu.VMEM(acc_shape, F32)],
        compiler_params=_params("parallel", "arbitrary"))(a, b)


def _pool(scr_ref, u, row0, tm, E):
    gc = E // len(POOL_WINDOWS)
    t1 = row0 + lax.broadcasted_iota(jnp.int32, (tm, 1), 0) + 1
    out = []
    for g, win in enumerate(POOL_WINDOWS):
        cs = slice(g * gc, (g + 1) * gc)
        acc = u[:, cs]
        for k in range(1, win):
            acc = acc + scr_ref[pl.ds(POOL_HALO - k, tm), cs]
        count = jnp.minimum(t1, win).astype(F32)
        out.append(acc / count - u[:, cs])
    return out


def _a_mid_out(proj, wg, scale, w_out, res, name):
    S, E2 = proj.shape
    E = E2 // 2
    D = res.shape[1]
    gc = E // len(POOL_WINDOWS)
    tm = min(256, S)
    hb = tm // POOL_HALO

    def body(u_ref, uh_ref, gt_ref, wg_ref, sc_ref, wo_ref, r_ref, x_ref, z_ref, scr_ref):
        i = pl.program_id(0)
        scr_ref[0:POOL_HALO, :] = jnp.where(i > 0, uh_ref[...], 0.0)
        u = u_ref[...]
        scr_ref[POOL_HALO:POOL_HALO + tm, :] = u
        pooled = _pool(scr_ref, u, i * tm, tm, E)
        for g in range(len(POOL_WINDOWS)):
            cs = slice(g * gc, (g + 1) * gc)
            y = jnp.dot(pooled[g].astype(BF16), wg_ref[g], preferred_element_type=F32) * sc_ref[:, cs]
            gate = gt_ref[:, cs]
            z_ref[:, cs] = (y * (gate * _sigmoid(gate))).astype(BF16)
        x_ref[...] = r_ref[...] + jnp.dot(z_ref[...], wo_ref[...], preferred_element_type=F32)

    return pl.pallas_call(
        body, name=name, grid=(S // tm,),
        in_specs=[pl.BlockSpec((tm, E), lambda i: (i, 0)),
                  pl.BlockSpec((POOL_HALO, E), lambda i: (jnp.maximum(i * hb - 1, 0), 0)),
                  pl.BlockSpec((tm, E), lambda i: (i, 1)),
                  pl.BlockSpec((len(POOL_WINDOWS), gc, gc), lambda i: (0, 0, 0)),
                  pl.BlockSpec((1, E), lambda i: (0, 0)),
                  pl.BlockSpec((E, D), lambda i: (0, 0)),
                  pl.BlockSpec((tm, D), lambda i: (i, 0))],
        out_specs=[pl.BlockSpec((tm, D), lambda i: (i, 0)), pl.BlockSpec((tm, E), lambda i: (i, 0))],
        out_shape=[jax.ShapeDtypeStruct((S, D), F32), jax.ShapeDtypeStruct((S, E), BF16)],
        scratch_shapes=[pltpu.VMEM((POOL_HALO + tm, E), F32)],
        compiler_params=_params("parallel"))(proj, proj, proj, wg, scale, w_out.reshape(E, D), res)


def _a_mid_bwd(dx, w_out, proj, wg, scale, name):
    S, E2 = proj.shape
    E = E2 // 2
    D = dx.shape[1]
    n_grp = len(POOL_WINDOWS)
    gc = E // n_grp
    tm = min(256, S)
    hb = tm // POOL_HALO
    n_tiles = S // tm
    last_halo = S // POOL_HALO - 1

    def body(dx_ref, dxh_ref, wo_ref, u_ref, uh_ref, gt_ref, gth_ref, wg_ref, sc_ref, dp_ref, dwg_ref, dsc_ref,
             scr_ref, q_ref):
        i = pl.program_id(0)
        dxb = jnp.concatenate([dx_ref[...].astype(BF16), dxh_ref[...].astype(BF16)], axis=0)

        @pl.when(i == 0)
        def _():
            dwg_ref[...] = jnp.zeros_like(dwg_ref)
            dsc_ref[...] = jnp.zeros_like(dsc_ref)

        scr_ref[0:POOL_HALO, :] = jnp.where(i > 0, uh_ref[...], 0.0)
        u = u_ref[...]
        scr_ref[POOL_HALO:POOL_HALO + tm, :] = u
        pooled = _pool(scr_ref, u, i * tm, tm, E)
        t1 = i * tm + lax.broadcasted_iota(jnp.int32, (tm, 1), 0) + 1
        t1h = (i + 1) * tm + lax.broadcasted_iota(jnp.int32, (POOL_HALO, 1), 0) + 1
        not_last = i < n_tiles - 1
        for g, win in enumerate(POOL_WINDOWS):
            cs = slice(g * gc, (g + 1) * gc)
            w = wg_ref[g]
            sc = sc_ref[:, cs]
            pb = pooled[g].astype(BF16)
            ypre = jnp.dot(pb, w, preferred_element_type=F32)
            gate = gt_ref[:, cs]
            sg = _sigmoid(gate)
            silu = gate * sg
            dz_all = lax.dot_general(dxb, wo_ref[cs, :], NT, preferred_element_type=F32)
            dzg = dz_all[0:tm]
            dy = dzg * silu
            dp_ref[:, E + g * gc:E + (g + 1) * gc] = (dzg * (ypre * sc) * (sg * (1.0 + gate * (1.0 - sg)))).astype(BF16)
            dsc_ref[:, cs] += jnp.sum(dy * ypre, axis=0, keepdims=True)
            dyp = (dy * sc).astype(BF16)
            dwg_ref[g] += lax.dot_general(pb, dyp, TN, preferred_element_type=F32)
            dpool = lax.dot_general(dyp, w, NT, preferred_element_type=F32)
            gate_h = gth_ref[:, cs]
            dyp_h = (dz_all[tm:tm + POOL_HALO] * (gate_h * _sigmoid(gate_h)) * sc).astype(BF16)
            dpool_h = lax.dot_general(dyp_h, w, NT, preferred_element_type=F32)
            q_ref[0:tm, cs] = dpool / jnp.minimum(t1, win).astype(F32)
            q_ref[tm:tm + POOL_HALO, cs] = jnp.where(not_last, dpool_h / jnp.minimum(t1h, win).astype(F32), 0.0)
            acc = q_ref[0:tm, cs] - dpool
            for k in range(1, win):
                acc = acc + q_ref[pl.ds(k, tm), cs]
            dp_ref[:, cs] = acc.astype(BF16)

    return pl.pallas_call(
        body, name=name, grid=(n_tiles,),
        in_specs=[pl.BlockSpec((tm, D), lambda i: (i, 0)),
                  pl.BlockSpec((POOL_HALO, D), lambda i: (jnp.minimum((i + 1) * hb, last_halo), 0)),
                  pl.BlockSpec((E, D), lambda i: (0, 0)),
                  pl.BlockSpec((tm, E), lambda i: (i, 0)),
                  pl.BlockSpec((POOL_HALO, E), lambda i: (jnp.maximum(i * hb - 1, 0), 0)),
                  pl.BlockSpec((tm, E), lambda i: (i, 1)),
                  pl.BlockSpec((POOL_HALO, E), lambda i: (jnp.minimum((i + 1) * hb, last_halo), 1)),
                  pl.BlockSpec((n_grp, gc, gc), lambda i: (0, 0, 0)),
                  pl.BlockSpec((1, E), lambda i: (0, 0))],
        out_specs=[pl.BlockSpec((tm, E2), lambda i: (i, 0)),
                   pl.BlockSpec((n_grp, gc, gc), lambda i: (0, 0, 0)),
                   pl.BlockSpec((1, E), lambda i: (0, 0))],
        out_shape=[jax.ShapeDtypeStruct((S, E2), BF16),
                   jax.ShapeDtypeStruct((n_grp, gc, gc), F32),
                   jax.ShapeDtypeStruct((1, E), F32)],
        scratch_shapes=[pltpu.VMEM((POOL_HALO + tm, E), F32), pltpu.VMEM((tm + POOL_HALO, E), F32)],
        compiler_params=_params("arbitrary"))(dx, dx, w_out.reshape(E, D), proj, proj, proj, proj, wg, scale)


def _rope_k(kvp, cos2, sin2, name):
    S, E2 = kvp.shape
    E = E2 // 2
    tm = min(256, S)

    def body(k_ref, c_ref, s_ref, ko_ref):
        cosv, sinv = c_ref[...], s_ref[...]
        for h in range(E // HEAD_DIM):
            hs = slice(h * HEAD_DIM, (h + 1) * HEAD_DIM)
            ko_ref[:, hs] = _rope(k_ref[:, hs], cosv, sinv)

    return pl.pallas_call(
        body, name=name, grid=(S // tm,),
        in_specs=[pl.BlockSpec((tm, E), lambda i: (i, 0)),
                  pl.BlockSpec((tm, HEAD_DIM), lambda i: (i, 0)), pl.BlockSpec((tm, HEAD_DIM), lambda i: (i, 0))],
        out_specs=pl.BlockSpec((tm, E), lambda i: (i, 0)),
        out_shape=jax.ShapeDtypeStruct((S, E), F32),
        compiler_params=_params("parallel"))(kvp, cos2, sin2)


CLASSES = 16


def _class_order(a, back, name):
    S, W = a.shape
    M = S // CLASSES

    def body(a_ref, o_ref):
        for r in range(CLASSES):
            if back:
                o_ref[pl.ds(r, M, stride=CLASSES), :] = a_ref[pl.ds(r * M, M), :]
            else:
                o_ref[pl.ds(r * M, M), :] = a_ref[pl.ds(r, M, stride=CLASSES), :]

    col = pl.BlockSpec((S, HEAD_DIM), lambda i: (0, i))
    return pl.pallas_call(
        body, name=name, grid=(W // HEAD_DIM,), in_specs=[col], out_specs=col,
        out_shape=jax.ShapeDtypeStruct((S, W), a.dtype),
        compiler_params=_params("parallel"))(a)


def _runs(r, b, dil, M, back=0):
    nj = CLASSES // dil
    c = BAND // nj
    return [((r + dil * j) * M + (b - back) * c, (1 + back) * c) for j in range(nj)]


def _load(ref, runs):
    parts = [ref[pl.ds(start, n), :] for start, n in runs]
    return parts[0] if len(parts) == 1 else jnp.concatenate(parts, axis=0)


def _store(ref, runs, val, add=False):
    at = 0
    for start, n in runs:
        if add:
            ref[pl.ds(start, n), :] += val[at:at + n]
        else:
            ref[pl.ds(start, n), :] = val[at:at + n]
        at += n


def _keys(ref, r, b, dil, M):
    if b > 0:
        return _load(ref, _runs(r, b, dil, M, back=1)).astype(BF16)
    parts = []
    for start, n in _runs(r, 0, dil, M):
        parts += [jnp.zeros((n, HEAD_DIM), BF16), ref[pl.ds(start, n), :].astype(BF16)]
    return jnp.concatenate(parts, axis=0)


def _band_mask(dil, first):
    nj = CLASSES // dil
    c = BAND // nj
    row = lax.broadcasted_iota(jnp.int32, (BAND, 2 * BAND), 0)
    col = lax.broadcasted_iota(jnp.int32, (BAND, 2 * BAND), 1)
    q_place = (row % c) * nj + row // c
    k_place = (col % (2 * c) - c) * nj + col // (2 * c)
    mask = (q_place >= k_place) & (q_place <= k_place + BAND)
    return mask & (col % (2 * c) >= c) if first else mask


def _lane_column(tile, lane, h):
    return jnp.sum(jnp.where(lane == h, tile, 0.0), axis=-1, keepdims=True)


def _attn_fwd(proj, kr, kvp, cos2, sin2, group, dil, name):
    S, PW = proj.shape
    E = kr.shape[1]
    H = E // HEAD_DIM
    M = S // CLASSES
    nb = S // (BAND * dil)

    def body(q_ref, k_ref, v_ref, c_ref, s_ref, o_ref, l_ref, o_scr):
        h = pl.program_id(0)
        lane = lax.broadcasted_iota(jnp.int32, (BAND, HEAD_DIM), 1)
        edge, inner = _band_mask(dil, True), _band_mask(dil, False)

        @pl.when(h == 0)
        def _():
            l_ref[...] = jnp.zeros_like(l_ref)

        def scores(r, b):
            runs = _runs(r, b, dil, M)
            qr = _rope(_load(q_ref, runs), _load(c_ref, runs), _load(s_ref, runs)).astype(BF16)
            return lax.dot_general(qr, _keys(k_ref, r, b, dil, M), NT, preferred_element_type=F32)

        units = [(r, b) for r in range(dil) for b in range(nb)]
        ahead = scores(*units[0])
        for i, (r, b) in enumerate(units):
            s = ahead
            if i + 1 < len(units):
                ahead = scores(*units[i + 1])
            runs = _runs(r, b, dil, M)
            s = jnp.where(edge if b == 0 else inner, s, NEG_INF)
            m = jnp.max(s, axis=-1, keepdims=True)
            p = jnp.exp2((s - m) * (ATTN_SCALE * LOG2_E))
            l = jnp.sum(p, axis=-1, keepdims=True)
            _store(o_scr, runs, jnp.dot(p.astype(BF16), _keys(v_ref, r, b, dil, M), preferred_element_type=F32) / l)
            _store(l_ref, runs, jnp.where(lane == h, m * ATTN_SCALE + jnp.log(l), _load(l_ref, runs)))
        o_ref[...] = o_scr[...].astype(BF16)

    col = (S, HEAD_DIM)
    whole = pl.BlockSpec(col, lambda h: (0, 0))
    return pl.pallas_call(
        body, name=name, grid=(H,),
        in_specs=[pl.BlockSpec(col, lambda h: (0, group * H + h)), pl.BlockSpec(col, lambda h: (0, h)),
                  pl.BlockSpec(col, lambda h: (0, H + h)), whole, whole],
        out_specs=[pl.BlockSpec(col, lambda h: (0, h)), whole],
        out_shape=[jax.ShapeDtypeStruct((S, E), BF16), jax.ShapeDtypeStruct((S, HEAD_DIM), F32)],
        scratch_shapes=[pltpu.VMEM(col, F32)],
        compiler_params=_params("arbitrary"))(proj, kr, kvp, cos2, sin2)


def _attn_bwd(proj, kr, kvp, cos2, sin2, do, lse, dlt, dproj, group, dil, name):
    S, PW = proj.shape
    E = kr.shape[1]
    H = E // HEAD_DIM
    M = S // CLASSES
    nb = S // (BAND * dil)

    def body(q_ref, k_ref, v_ref, c_ref, s_ref, do_ref, l_ref, dl_ref, dproj_ref,
             dq_ref, dk_ref, dv_ref, dq_scr, dk_scr, dv_scr):
        h = pl.program_id(0)
        lane = lax.broadcasted_iota(jnp.int32, (BAND, HEAD_DIM), 1)
        edge, inner = _band_mask(dil, True), _band_mask(dil, False)
        dk_scr[...] = jnp.zeros_like(dk_scr)
        dv_scr[...] = jnp.zeros_like(dv_scr)

        def scores(r, b):
            runs = _runs(r, b, dil, M)
            qr = _rope(_load(q_ref, runs), _load(c_ref, runs), _load(s_ref, runs)).astype(BF16)
            return qr, lax.dot_general(qr, _keys(k_ref, r, b, dil, M), NT, preferred_element_type=F32)

        units = [(r, b) for r in range(dil) for b in range(nb)]
        ahead = scores(*units[0])
        for i, (r, b) in enumerate(units):
            qr, s = ahead
            if i + 1 < len(units):
                ahead = scores(*units[i + 1])
            runs = _runs(r, b, dil, M)
            dob = _load(do_ref, runs).astype(BF16)
            dpr = lax.dot_general(dob, _keys(v_ref, r, b, dil, M), NT, preferred_element_type=F32)
            s = jnp.where(edge if b == 0 else inner, s, NEG_INF)
            p = jnp.exp2(s * (ATTN_SCALE * LOG2_E) - _lane_column(_load(l_ref, runs), lane, h) * LOG2_E)
            ds = (p * (dpr - _lane_column(_load(dl_ref, runs), lane, h)) * ATTN_SCALE).astype(BF16)
            dq = jnp.dot(ds, _keys(k_ref, r, b, dil, M), preferred_element_type=F32)
            _store(dq_scr, runs, _rope_bwd(dq, _load(c_ref, runs), _load(s_ref, runs)))
            dkc = lax.dot_general(ds, qr, TN, preferred_element_type=F32)
            dvc = lax.dot_general(p.astype(BF16), dob, TN, preferred_element_type=F32)
            if b > 0:
                both = _runs(r, b, dil, M, back=1)
                _store(dk_scr, both, dkc, add=True)
                _store(dv_scr, both, dvc, add=True)
            else:
                n = runs[0][1]
                own = jnp.concatenate([dkc[(2 * j + 1) * n:(2 * j + 2) * n] for j in range(len(runs))], axis=0)
                _store(dk_scr, runs, own, add=True)
                own = jnp.concatenate([dvc[(2 * j + 1) * n:(2 * j + 2) * n] for j in range(len(runs))], axis=0)
                _store(dv_scr, runs, own, add=True)
        dq_ref[...] = dq_scr[...].astype(BF16)
        dk_ref[...] = dk_scr[...].astype(BF16)
        dv_ref[...] = dv_scr[...].astype(BF16)

    col = (S, HEAD_DIM)
    whole = pl.BlockSpec(col, lambda h: (0, 0))
    head = pl.BlockSpec(col, lambda h: (0, h))
    return pl.pallas_call(
        body, name=name, grid=(H,),
        in_specs=[pl.BlockSpec(col, lambda h: (0, group * H + h)), head, pl.BlockSpec(col, lambda h: (0, H + h)),
                  whole, whole, head, whole, whole, ANY],
        out_specs=[pl.BlockSpec(col, lambda h: (0, group * H + h)), head, head],
        out_shape=[jax.ShapeDtypeStruct(dproj.shape, BF16), jax.ShapeDtypeStruct((S, E), BF16),
                   jax.ShapeDtypeStruct((S, E), BF16)],
        scratch_shapes=[pltpu.VMEM(col, F32)] * 3,
        input_output_aliases={8: 0},
        compiler_params=_params("parallel"))(proj, kr, kvp, cos2, sin2, do, lse, dlt, dproj)


def _group_weights(l_refs, h):
    ls = [r[:, h:h + 1] for r in l_refs]
    mx = jnp.maximum(jnp.maximum(ls[0], ls[1]), ls[2])
    es = [jnp.exp(l - mx) for l in ls]
    inv = 1.0 / (es[0] + es[1] + es[2])
    return [e * inv for e in es]


def _merge_out(outs, lses, proj, w_out, res, name):
    S, E = outs[0].shape
    D = res.shape[1]
    tm = min(256, S)
    gate_col = proj.shape[1] // E - 1

    def body(o0, o1, o2, l0, l1, l2, gt_ref, w_ref, r_ref, x_ref, z_ref):
        for h in range(E // HEAD_DIM):
            hs = slice(h * HEAD_DIM, (h + 1) * HEAD_DIM)
            a = _group_weights((l0, l1, l2), h)
            merged = a[0] * o0[:, hs] + a[1] * o1[:, hs] + a[2] * o2[:, hs]
            gate = gt_ref[:, hs]
            z_ref[:, hs] = (merged * (gate * _sigmoid(gate))).astype(BF16)
        x_ref[...] = r_ref[...] + jnp.dot(z_ref[...], w_ref[...], preferred_element_type=F32)

    wide = pl.BlockSpec((tm, E), lambda i: (i, 0))
    thin = pl.BlockSpec((tm, HEAD_DIM), lambda i: (i, 0))
    return pl.pallas_call(
        body, name=name, grid=(S // tm,),
        in_specs=[wide, wide, wide, thin, thin, thin, pl.BlockSpec((tm, E), lambda i: (i, gate_col)),
                  pl.BlockSpec((E, D), lambda i: (0, 0)), pl.BlockSpec((tm, D), lambda i: (i, 0))],
        out_specs=[pl.BlockSpec((tm, D), lambda i: (i, 0)), wide],
        out_shape=[jax.ShapeDtypeStruct((S, D), F32), jax.ShapeDtypeStruct((S, E), BF16)],
        compiler_params=_params("parallel"))(*outs, *lses, proj, w_out.reshape(E, D), res)


def _merge_bwd(dx, w_out, outs, lses, proj, name):
    S, E = outs[0].shape
    D = dx.shape[1]
    tm = min(256, S)
    gate_col = proj.shape[1] // E - 1

    def body(dx_ref, w_ref, o0, o1, o2, l0, l1, l2, gt_ref, d0, d1, d2, t0, t1, t2, dg_ref):
        o_refs, d_refs, t_refs = (o0, o1, o2), (d0, d1, d2), (t0, t1, t2)
        lane = lax.broadcasted_iota(jnp.int32, (tm, HEAD_DIM), 1)
        tiles = [jnp.zeros((tm, HEAD_DIM), F32) for _ in range(3)]
        dxb = dx_ref[...].astype(BF16)
        for h in range(E // HEAD_DIM):
            hs = slice(h * HEAD_DIM, (h + 1) * HEAD_DIM)
            a = _group_weights((l0, l1, l2), h)
            merged = a[0] * o0[:, hs] + a[1] * o1[:, hs] + a[2] * o2[:, hs]
            gate = gt_ref[:, hs]
            sg = _sigmoid(gate)
            dzh = lax.dot_general(dxb, w_ref[hs, :], NT, preferred_element_type=F32)
            dmerged = dzh * (gate * sg)
            dg_ref[:, hs] = (dzh * merged * (sg * (1.0 + gate * (1.0 - sg)))).astype(BF16)
            tot = jnp.sum(dmerged * merged, axis=-1, keepdims=True)
            for g in range(3):
                d_refs[g][:, hs] = a[g] * dmerged
                tiles[g] = jnp.where(lane == h, a[g] * tot, tiles[g])
        for g in range(3):
            t_refs[g][...] = tiles[g]

    wide = pl.BlockSpec((tm, E), lambda i: (i, 0))
    thin = pl.BlockSpec((tm, HEAD_DIM), lambda i: (i, 0))
    res = pl.pallas_call(
        body, name=name, grid=(S // tm,),
        in_specs=[pl.BlockSpec((tm, D), lambda i: (i, 0)), pl.BlockSpec((E, D), lambda i: (0, 0)),
                  wide, wide, wide, thin, thin, thin, pl.BlockSpec((tm, E), lambda i: (i, gate_col))],
        out_specs=[wide, wide, wide, thin, thin, thin, pl.BlockSpec((tm, E), lambda i: (i, gate_col))],
        out_shape=[jax.ShapeDtypeStruct((S, E), F32)] * 3 + [jax.ShapeDtypeStruct((S, HEAD_DIM), F32)] * 3
        + [jax.ShapeDtypeStruct(proj.shape, BF16)],
        compiler_params=_params("parallel"))(dx, w_out.reshape(E, D), *outs, *lses, proj)
    return res[0:3], res[3:6], res[6]


def _kv_bwd(dks, dvs, cos2, sin2, name):
    S, E = dks[0].shape
    n = len(dks)
    tm = min(256, S)

    def body(*refs):
        dk_refs, dv_refs = refs[0:n], refs[n:2 * n]
        c_ref, s_ref, o_ref = refs[2 * n:]
        cosv, sinv = c_ref[...], s_ref[...]
        for h in range(E // HEAD_DIM):
            hs = slice(h * HEAD_DIM, (h + 1) * HEAD_DIM)
            dk = dk_refs[0][:, hs].astype(F32)
            dv = dv_refs[0][:, hs].astype(F32)
            for j in range(1, n):
                dk = dk + dk_refs[j][:, hs].astype(F32)
                dv = dv + dv_refs[j][:, hs].astype(F32)
            o_ref[:, hs] = _rope_bwd(dk, cosv, sinv).astype(BF16)
            o_ref[:, E + h * HEAD_DIM:E + (h + 1) * HEAD_DIM] = dv.astype(BF16)

    wide = pl.BlockSpec((tm, E), lambda i: (i, 0))
    thin = pl.BlockSpec((tm, HEAD_DIM), lambda i: (i, 0))
    return pl.pallas_call(
        body, name=name, grid=(S // tm,),
        in_specs=[wide] * (2 * n) + [thin, thin],
        out_specs=pl.BlockSpec((tm, 2 * E), lambda i: (i, 0)),
        out_shape=jax.ShapeDtypeStruct((S, 2 * E), BF16),
        compiler_params=_params("parallel"))(*dks, *dvs, cos2, sin2)


def _final_norm_loss(x, target, gain, name):
    S, D = x.shape
    tm = min(256, S)

    def body(x_ref, t_ref, g_ref, loss_ref, dx_ref, dg_ref):
        @pl.when(pl.program_id(0) == 0)
        def _():
            loss_ref[...] = jnp.zeros_like(loss_ref)
            dg_ref[...] = jnp.zeros_like(dg_ref)
        xf = x_ref[...]
        inv = lax.rsqrt(jnp.mean(xf * xf, axis=-1, keepdims=True) + RMS_EPS)
        xhat = xf * inv
        g = g_ref[...]
        err = xhat * g - t_ref[...]
        loss_ref[...] += 0.5 * jnp.sum(jnp.mean(err * err, axis=-1, keepdims=True), axis=0, keepdims=True)
        dy = err / D
        dg_ref[...] += jnp.sum(dy * xhat, axis=0, keepdims=True)
        dxh = dy * g
        dx_ref[...] = inv * (dxh - xhat * jnp.mean(dxh * xhat, axis=-1, keepdims=True))

    tile = pl.BlockSpec((tm, D), lambda i: (i, 0))
    vec = pl.BlockSpec((1, D), lambda i: (0, 0))
    return pl.pallas_call(
        body, name=name, grid=(S // tm,),
        in_specs=[tile, tile, vec],
        out_specs=[pl.BlockSpec((1, 1), lambda i: (0, 0)), tile, vec],
        out_shape=[jax.ShapeDtypeStruct((1, 1), F32), jax.ShapeDtypeStruct((S, D), F32),
                   jax.ShapeDtypeStruct((1, D), F32)],
        compiler_params=_params("arbitrary"))(x, target, gain)


def _adamw_math(g, w, m, v):
    m = ADAM_B1 * m + (1.0 - ADAM_B1) * g
    v = ADAM_B2 * v + (1.0 - ADAM_B2) * (g * g)
    m_hat = m / (1.0 - ADAM_B1 ** ADAM_STEP)
    v_hat = v / (1.0 - ADAM_B2 ** ADAM_STEP)
    delta = -ADAM_LR * (m_hat / (jnp.sqrt(v_hat) + ADAM_EPS) + ADAM_WD * w)
    return delta, m, v


def _adamw_rows(g, w, m, v, name):
    def body(g_ref, w_ref, m_ref, v_ref, d_ref, mo_ref, vo_ref):
        d_ref[...], mo_ref[...], vo_ref[...] = _adamw_math(g_ref[...], w_ref[...], m_ref[...], v_ref[...])

    whole = pl.BlockSpec(memory_space=pltpu.VMEM)
    return pl.pallas_call(
        body, name=name, in_specs=[whole] * 4, out_specs=[whole] * 3,
        out_shape=[jax.ShapeDtypeStruct(g.shape, F32)] * 3)(g, w, m, v)


def _adamw_blocks(own, others, w, m, v, layer, earlier, name):
    L, R, C = w.shape
    n = others.shape[0]
    tr = R
    while tr * C > 128 * 1024 and tr % 16 == 0:
        tr //= 2

    def body(o_ref, p_ref, w_ref, m_ref, v_ref, *rest):
        g_ref, d_ref, mo_ref, vo_ref = rest[-4:]
        g = o_ref[...].astype(F32)
        for j in range(n):
            g = g + p_ref[j].astype(F32)
        g_ref[...] = g
        d_ref[...], mo_ref[...], vo_ref[...] = _adamw_math(g, w_ref[...], m_ref[...], v_ref[...])

    tile = pl.BlockSpec((None, tr, C), lambda i: (layer, i, 0))
    kept = [] if earlier is None else list(earlier)
    return pl.pallas_call(
        body, name=name, grid=(R // tr,),
        in_specs=[pl.BlockSpec((tr, C), lambda i: (i, 0)), pl.BlockSpec((n, tr, C), lambda i: (0, i, 0)),
                  tile, tile, tile] + [ANY] * len(kept),
        out_specs=[tile] * 4,
        out_shape=[jax.ShapeDtypeStruct((L, R, C), F32)] * 4,
        input_output_aliases={5 + j: j for j in range(len(kept))},
        compiler_params=_params("parallel"))(own, others, w, m, v, *kept)


def _position():
    return lax.axis_index("x"), lax.axis_index("y"), lax.axis_index("c")


def _block_index(px, py, pc):
    return 4 * px + 2 * py + pc


def _all_gather(shards, name):
    n = len(shards)

    def body(*refs):
        ins, outs = refs[0:n], refs[n:2 * n]
        send_sems, recv_sems, local_sems = refs[2 * n:]
        x, y, c = _position()
        me, sibling = (x, y, c), (x, y, 1 - c)
        chips = [(1 - x, y), (x, 1 - y), (1 - x, 1 - y)]

        def copy(a, k, block, to, src=None):
            rows = outs[a].at[_block_index(*block)]
            return pltpu.make_async_remote_copy(
                src_ref=rows if src is None else src, dst_ref=rows,
                send_sem=send_sems.at[a, k], recv_sem=recv_sems.at[a, k], device_id=to, device_id_type=MESH)

        mine, first, passed = [], [], []
        for a in range(n):
            cp = pltpu.make_async_copy(ins[a], outs[a].at[_block_index(*me)], local_sems.at[a])
            cp.start()
            mine.append(cp)
            first.append(copy(a, 0, me, sibling, src=ins[a]))
            first += [copy(a, 1 + j, me, (*chip, c), src=ins[a]) for j, chip in enumerate(chips)]
        for cp in first:
            cp.start()
        for j, chip in enumerate(chips):
            for a in range(n):
                copy(a, 1 + j, (*chip, c), me).wait_recv()
                fwd = copy(a, 4 + j, (*chip, c), sibling)
                fwd.start()
                passed.append(fwd)
        for a in range(n):
            copy(a, 0, sibling, me).wait_recv()
            for j, chip in enumerate(chips):
                copy(a, 4 + j, (*chip, 1 - c), me).wait_recv()
        for cp in first + passed:
            cp.wait_send()
        for cp in mine:
            cp.wait()

    return pl.pallas_call(
        body, name=name,
        in_specs=[ANY] * n, out_specs=[ANY] * n,
        out_shape=[jax.ShapeDtypeStruct((N_DEV,) + s.shape, s.dtype) for s in shards],
        scratch_shapes=[pltpu.SemaphoreType.DMA((n, 7)), pltpu.SemaphoreType.DMA((n, 7)),
                        pltpu.SemaphoreType.DMA((n,))],
    )(*shards)


def _peers(x, y, c):
    return [((1 - x) if k & 4 else x, (1 - y) if k & 2 else y, (1 - c) if k & 1 else c) for k in range(1, N_DEV)]


HBM = pl.BlockSpec(memory_space=pltpu.HBM)
SEM = pl.BlockSpec(memory_space=pltpu.SEMAPHORE)
EFFECT = pltpu.SideEffectType.DATAFLOW_SIDE_EFFECTING


ALL_PEERS = (1, 2, 3, 4, 5, 6, 7)
SIBLING_AND_SAME_CORES = (1, 2, 4, 6)


def _push_copy(src_refs, land_refs, send_sems, recv_sems, a, i, relations, per_peer, by_sender, arriving):
    peer = _peers(*_position())[relations[i] - 1]
    me_idx, p_idx = _block_index(*_position()), _block_index(*peer)
    src = src_refs[a].at[p_idx] if per_peer else src_refs[a]
    if by_sender:
        slot = p_idx if arriving else me_idx
    else:
        slot = relations[i] - 1
    sem = a * len(relations) + i
    return pltpu.make_async_remote_copy(
        src_ref=src, dst_ref=land_refs[a].at[slot], send_sem=send_sems.at[sem], recv_sem=recv_sems.at[sem],
        device_id=peer, device_id_type=MESH)


def _push_start(srcs, lands, relations, per_peer, by_sender, after, name):
    n = len(srcs)

    def body(*refs):
        src_refs, land_refs = refs[0:n], refs[n:2 * n]
        send_sems, recv_sems = refs[2 * n + 1], refs[2 * n + 2]
        token = refs[-1]
        for a in range(n):
            for i in range(len(relations)):
                _push_copy(src_refs, land_refs, send_sems, recv_sems, a, i, relations, per_peer, by_sender, False).start()
        token[...] = jnp.zeros_like(token)

    args = [pltpu.with_memory_space_constraint(t, pltpu.HBM) for t in list(srcs) + list(lands)]
    res = pl.pallas_call(
        body, name=name,
        in_specs=[HBM] * (2 * n) + [ANY],
        out_specs=[SEM, SEM] + [HBM] * (2 * n) + [pl.BlockSpec(memory_space=pltpu.VMEM)],
        out_shape=[pltpu.SemaphoreType.DMA((n * len(relations),)), pltpu.SemaphoreType.DMA((n * len(relations),))]
        + [pltpu.HBM(t.shape, t.dtype) for t in args] + [jax.ShapeDtypeStruct((8, 128), F32)],
        input_output_aliases={i: 2 + i for i in range(2 * n)},
        compiler_params=pltpu.CompilerParams(has_side_effects=EFFECT))(*args, after)
    return res[0], res[1], res[2:2 + n], res[2 + n:2 + 2 * n], res[-1]


def _push_wait(started, relations, per_peer, by_sender, after, name):
    send_sems, recv_sems, srcs, lands, _ = started
    n = len(srcs)

    def body(*refs):
        src_refs, land_refs = refs[0:n], refs[n:2 * n]
        send_s, recv_s = refs[2 * n], refs[2 * n + 1]
        for a in range(n):
            for i in range(len(relations)):
                _push_copy(src_refs, land_refs, send_s, recv_s, a, i, relations, per_peer, by_sender, False).wait_send()
                _push_copy(src_refs, land_refs, send_s, recv_s, a, i, relations, per_peer, by_sender, True).wait_recv()

    res = pl.pallas_call(
        body, name=name,
        in_specs=[HBM] * (2 * n) + [SEM, SEM, ANY],
        out_specs=[HBM] * (2 * n),
        out_shape=[pltpu.HBM(t.shape, t.dtype) for t in list(srcs) + list(lands)],
        input_output_aliases={i: i for i in range(2 * n)},
        compiler_params=pltpu.CompilerParams(has_side_effects=EFFECT))(*srcs, *lands, send_sems, recv_sems, after)
    return res[0:n], res[n:2 * n]


def _pass_on(lands, name):
    n = len(lands)

    def body(*refs):
        outs = refs[n:2 * n]
        send_sems, recv_sems = refs[2 * n:]
        x, y, c = _position()
        chips = [(1 - x, y), (x, 1 - y), (1 - x, 1 - y)]
        copies = []
        for a in range(n):
            for j, chip in enumerate(chips):
                def copy(core):
                    rows = outs[a].at[_block_index(*chip, core)]
                    return pltpu.make_async_remote_copy(
                        src_ref=rows, dst_ref=rows, send_sem=send_sems.at[a, j], recv_sem=recv_sems.at[a, j],
                        device_id=(x, y, 1 - c), device_id_type=MESH)
                copy(c).start()
                copies.append((copy(c), copy(1 - c)))
        for sending, arriving in copies:
            sending.wait_send()
            arriving.wait_recv()

    return pl.pallas_call(
        body, name=name,
        in_specs=[ANY] * n, out_specs=[ANY] * n,
        out_shape=[jax.ShapeDtypeStruct(t.shape, t.dtype) for t in lands],
        input_output_aliases={a: a for a in range(n)},
        scratch_shapes=[pltpu.SemaphoreType.DMA((n, 3)), pltpu.SemaphoreType.DMA((n, 3))],
    )(*lands)


def _all_reduce_rows(v, name):
    R, D = v.shape

    def body(v_ref, o_ref, buf_ref, send_sems, recv_sems):
        x, y, c = _position()
        me_idx = _block_index(x, y, c)
        buf_ref[me_idx] = v_ref[...]
        copies = []
        for k in range(1, N_DEV):
            px = (1 - x) if k & 4 else x
            py = (1 - y) if k & 2 else y
            pc = (1 - c) if k & 1 else c
            rc = pltpu.make_async_remote_copy(
                src_ref=v_ref, dst_ref=buf_ref.at[me_idx],
                send_sem=send_sems.at[k - 1], recv_sem=recv_sems.at[k - 1],
                device_id=(px, py, pc), device_id_type=MESH)
            rc.start()
            copies.append((rc, pltpu.make_async_remote_copy(
                src_ref=v_ref, dst_ref=buf_ref.at[_block_index(px, py, pc)],
                send_sem=send_sems.at[k - 1], recv_sem=recv_sems.at[k - 1],
                device_id=(px, py, pc), device_id_type=MESH)))
        for rc, arrival in copies:
            rc.wait_send()
            arrival.wait_recv()
        acc = buf_ref[0]
        for j in range(1, N_DEV):
            acc = acc + buf_ref[j]
        o_ref[...] = acc

    return pl.pallas_call(
        body, name=name,
        in_specs=[pl.BlockSpec(memory_space=pltpu.VMEM)],
        out_specs=pl.BlockSpec(memory_space=pltpu.VMEM),
        out_shape=jax.ShapeDtypeStruct((R, D), F32),
        scratch_shapes=[pltpu.VMEM((N_DEV, R, D), F32),
                        pltpu.SemaphoreType.DMA((7,)), pltpu.SemaphoreType.DMA((7,))],
    )(v)


def _rope_tables(S):
    inv_freq = 1.0 / (ROPE_THETA ** (jnp.arange(0, HEAD_DIM, 2, dtype=F32) / HEAD_DIM))
    ang = jnp.arange(S, dtype=F32)[:, None] * inv_freq[None, :]
    cos, sin = jnp.cos(ang), jnp.sin(ang)
    return jnp.concatenate([cos, cos], axis=1), jnp.concatenate([-sin, sin], axis=1)


def _local_step(xs, target, vecs, n_a, n_b, get_weights, put_grads):
    S, D = xs.shape
    E = D
    cos2, sin2 = _rope_tables(S)
    ts = min(1024, S)

    def col_blocks(w):
        cb = w.shape[2]
        tn = min(cb, 1024)
        per = cb // tn
        return (None, D, tn), (lambda i, j: (j // per, 0, j % per)), N_DEV * per, tn

    def grad_in(hn, dproj, cb, name):
        return _matmul_tn(hn, dproj, (ts, D), lambda j, s: (s, 0), (ts, cb), lambda j, s: (s, j),
                          (N_DEV, D, cb), (None, D, cb), lambda j, s: (j, 0, 0), (D, cb), N_DEV, name)

    def grad_out(z, dx, name, col=0):
        rows = z.shape[1]
        ta = min(1024, rows)
        out = _matmul_tn(z, dx, (ts, ta), lambda a, s: (s, a), (ts, E), lambda a, s: (s, col),
                         (rows, E), (ta, E), lambda a, s: (a, 0), (ta, E), rows // ta, name)
        return out.reshape(N_DEV, rows // N_DEV, E)

    x = xs
    a_saved, b_saved = [], []
    for i in range(n_a):
        w = get_weights(f"a{i}", x)
        blk, idx, nblocks, tn = col_blocks(w["w_in"])
        proj, hn = _norm_matmul(x, vecs["norm_a"][i:i + 1], w["w_in"], blk, idx, nblocks, tn, f"a{i}_in")
        w = {**w, **get_weights(f"a{i}_rest", proj)}
        x_next, z = _a_mid_out(proj, w["w_grp"], vecs["scale_a"][i:i + 1], w["w_out"], x, f"a{i}_out")
        a_saved.append((x, hn, proj, z, w))
        x = x_next
    w_kv = get_weights("kv", x)["w_kv"]
    x = x_kv = _class_order(x, False, "kv_x_order")
    target = _class_order(target, False, "target_order")
    cos2, sin2 = _class_order(cos2, False, "cos_order"), _class_order(sin2, False, "sin_order")
    tn = min(E, 1024)
    kvp, hn_kv = _norm_matmul(x, vecs["norm_kv"], w_kv, (D, tn), lambda i, j: (0, j), 2 * E // tn, tn, "kv_in")
    kr = _rope_k(kvp, cos2, sin2, "kv_rope")
    after = kr
    for i in range(n_b):
        w = get_weights(f"b{i}", after)
        blk, idx, nblocks, tn = col_blocks(w["w_in"])
        proj, hn = _norm_matmul(x, vecs["norm_b"][i:i + 1], w["w_in"], blk, idx, nblocks, tn, f"b{i}_in")
        outs, lses = [], []
        for g, dil in enumerate(DILATIONS):
            o, l = _attn_fwd(proj, kr, kvp, cos2, sin2, g, dil, f"b{i}_attn{g}")
            outs.append(o)
            lses.append(l)
        x_next, z = _merge_out(outs, lses, proj, w["w_out"], x, f"b{i}_out")
        b_saved.append((x, hn, proj, z, outs, lses, w))
        x = x_next
        after = x
    loss, dx, dg_f = _final_norm_loss(x, target, vecs["norm_f"], "final")

    vec = {"norm_a": [None] * n_a, "scale_a": [None] * n_a, "norm_b": [None] * n_b, "norm_f": [dg_f]}
    dks, dvs = [], []
    for i in reversed(range(n_b)):
        x_in, hn, proj, z, outs, lses, w = b_saved[i]
        dw_out = grad_out(z, dx, f"b{i}_dwout")
        dos, dlts, dproj = _merge_bwd(dx, w["w_out"], outs, lses, proj, f"b{i}_dmerge")
        for g, dil in enumerate(DILATIONS):
            dproj, dk, dv = _attn_bwd(proj, kr, kvp, cos2, sin2, dos[g], lses[g], dlts[g], dproj, g, dil,
                                      f"b{i}_dattn{g}")
            dks.append(dk)
            dvs.append(dv)
        cb = w["w_in"].shape[2]
        tok = put_grads(f"b{i}", {"w_out": dw_out, "w_in": grad_in(hn, dproj, cb, f"b{i}_dwin")})
        dx, vec["norm_b"][i] = _matmul_nt_dnorm(dproj, w["w_in"], (None, D, cb), lambda t, j: (j, 0, 0), N_DEV, cb, x_in,
                                                vecs["norm_b"][i:i + 1] + tok[0:1, 0:1], dx, f"b{i}_dhn")

    dkv = _kv_bwd(dks, dvs, cos2, sin2, "kv_dsum")
    tok = put_grads("kv", {"w_k": grad_out(hn_kv, dkv, "kv_dwk", 0), "w_v": grad_out(hn_kv, dkv, "kv_dwv", 1)})
    tk = min(E, 1024)
    dx, dg_kv = _matmul_nt_dnorm(dkv, w_kv, (D, tk), lambda t, j: (0, j), 2 * E // tk, tk, x_kv,
                                 vecs["norm_kv"] + tok[0:1, 0:1], dx, "kv_dhn")
    vec["norm_kv"] = [dg_kv]
    dx = _class_order(dx, True, "kv_dx_order")

    for i in reversed(range(n_a)):
        x_in, hn, proj, z, w = a_saved[i]
        dw_out = grad_out(z, dx, f"a{i}_dwout")
        dproj, dwg, dsc = _a_mid_bwd(dx, w["w_out"], proj, w["w_grp"], vecs["scale_a"][i:i + 1], f"a{i}_dmid")
        n_grp, gc, _ = dwg.shape
        dwg = dwg.reshape(n_grp, N_DEV, gc // N_DEV, gc).transpose(1, 0, 2, 3).astype(BF16)
        vec["scale_a"][i] = dsc
        cb = w["w_in"].shape[2]
        tok = put_grads(f"a{i}", {"w_out": dw_out, "w_grp": dwg, "w_in": grad_in(hn, dproj, cb, f"a{i}_dwin")})
        dx, vec["norm_a"][i] = _matmul_nt_dnorm(dproj, w["w_in"], (None, D, cb), lambda t, j: (j, 0, 0), N_DEV, cb, x_in,
                                                vecs["norm_a"][i:i + 1] + tok[0:1, 0:1], dx, f"a{i}_dhn")

    return loss, dx, {k: jnp.concatenate(v, axis=0) for k, v in vec.items()}


VECTORS = ("norm_a", "scale_a", "norm_kv", "norm_b", "norm_f")
SHARDED_VECTORS = ("norm_a", "scale_a")
GROUPS = {
    "a0": (("w_in", "w_in_a", 0), ("w_grp", "w_grp_a", 0), ("w_out", "w_out_a", 0)),
    "a1": (("w_in", "w_in_a", 1), ("w_grp", "w_grp_a", 1), ("w_out", "w_out_a", 1)),
    "kv": (("w_k", "w_k", None), ("w_v", "w_v", None)),
    "b0": (("w_in", "w_in_b", 0), ("w_out", "w_out_b", 0)),
    "b1": (("w_in", "w_in_b", 1), ("w_out", "w_out_b", 1)),
}
FIRST = (("w_in", "w_in_a", 0),)
PREFETCHED = {"a0_rest": GROUPS["a0"][1:], "a1": GROUPS["a1"], "kv": GROUPS["kv"], "b0": GROUPS["b0"], "b1": GROUPS["b1"]}


def kernel(x, norm_a, w_in_a, w_grp_a, scale_a, w_out_a, norm_kv, w_k, w_v, norm_b, w_in_b, w_out_b, norm_f, loss_target, m_norm_a, m_w_in_a, m_w_grp_a, m_scale_a, m_w_out_a, m_norm_kv, m_w_k, m_w_v, m_norm_b, m_w_in_b, m_w_out_b, m_norm_f, v_norm_a, v_w_in_a, v_w_grp_a, v_scale_a, v_w_out_a, v_norm_kv, v_w_k, v_w_v, v_norm_b, v_w_in_b, v_w_out_b, v_norm_f):
    w = dict(norm_a=norm_a, w_in_a=w_in_a, w_grp_a=w_grp_a, scale_a=scale_a, w_out_a=w_out_a, norm_kv=norm_kv,
             w_k=w_k, w_v=w_v, norm_b=norm_b, w_in_b=w_in_b, w_out_b=w_out_b, norm_f=norm_f)
    m = dict(norm_a=m_norm_a, w_in_a=m_w_in_a, w_grp_a=m_w_grp_a, scale_a=m_scale_a, w_out_a=m_w_out_a,
             norm_kv=m_norm_kv, w_k=m_w_k, w_v=m_w_v, norm_b=m_norm_b, w_in_b=m_w_in_b, w_out_b=m_w_out_b,
             norm_f=m_norm_f)
    v = dict(norm_a=v_norm_a, w_in_a=v_w_in_a, w_grp_a=v_w_grp_a, scale_a=v_scale_a, w_out_a=v_w_out_a,
             norm_kv=v_norm_kv, w_k=v_w_k, w_v=v_w_v, norm_b=v_norm_b, w_in_b=v_w_in_b, w_out_b=v_w_out_b,
             norm_f=v_norm_f)
    D = x.shape[2]
    me = _block_index(*_position())

    def shard(members):
        return [w[p].astype(BF16) if layer is None else w[p][layer].astype(BF16) for _, p, layer in members]

    def as_weights(members, gathered):
        out = dict(zip([n for n, _, _ in members], gathered))
        if "w_grp" in out:
            g = out["w_grp"]
            out["w_grp"] = g.transpose(1, 0, 2, 3).reshape(g.shape[1], g.shape[3], g.shape[3])
        if "w_k" in out:
            out = {"w_kv": jnp.concatenate([out["w_k"].reshape(D, D), out["w_v"].reshape(D, D)], axis=1)}
        return out

    first = _all_gather(shard(FIRST) + [w[k] for k in SHARDED_VECTORS], "gather_first")
    n_first = len(FIRST)
    vecs = {k: g.transpose(1, 0, 2).reshape(w[k].shape[0], D) for k, g in zip(SHARDED_VECTORS, first[n_first:])}
    vecs.update(norm_kv=norm_kv[None, :], norm_b=norm_b, norm_f=norm_f[None, :])
    srcs, lands = [], []
    for group in PREFETCHED:
        for s in shard(PREFETCHED[group]):
            srcs.append(s)
            lands.append(lax.dynamic_update_index_in_dim(lax.empty((N_DEV,) + s.shape, s.dtype), s[None], me, 0))
    inflight, at = {}, 0
    token = None
    for group in PREFETCHED:
        n = len(PREFETCHED[group])
        inflight[group] = _push_start(srcs[at:at + n], lands[at:at + n], SIBLING_AND_SAME_CORES, False, True,
                                      first[0] if token is None else token, f"gather_{group}_start")
        token = inflight[group][4]
        at += n
    vecs["norm_a"] = vecs["norm_a"] + token[0:1, 0:1]

    def get_weights(group, after):
        if group == "a0":
            return as_weights(FIRST, first[0:n_first])
        if group not in PREFETCHED:
            return {}
        half = _push_wait(inflight[group], SIBLING_AND_SAME_CORES, False, True, after, f"gather_{group}_wait")[1]
        return as_weights(PREFETCHED[group], _pass_on(half, f"gather_{group}_pass"))

    sent = {}

    def put_grads(group, grads):
        blocks = [grads[n] for n, _, _ in GROUPS[group]]
        lands = [lax.empty((N_DEV - 1,) + b.shape[1:], b.dtype) for b in blocks]
        sent[group] = _push_start(blocks, lands, ALL_PEERS, True, False, jnp.zeros((8, 128), F32),
                                  f"exchange_{group}_start")
        return sent[group][4]

    loss, dx, vec = _local_step(x[0], loss_target[0], vecs, w_in_a.shape[0], w_in_b.shape[0], get_weights, put_grads)
    rows = _all_reduce_rows(jnp.concatenate([vec[k] for k in VECTORS], axis=0), "reduce_vectors")

    out = {}
    after = dx
    for group in sent:
        blocks, arrived = _push_wait(sent[group], ALL_PEERS, True, False, after, f"exchange_{group}_wait")
        for (_, p, layer), blk, got in zip(GROUPS[group], blocks, arrived):
            cols = w[p].shape[-1]
            own = lax.dynamic_index_in_dim(blk, me, 0, keepdims=False).reshape(-1, cols)
            n_layers = 1 if layer is None else w[p].shape[0]
            stacked = lambda t: t.reshape(n_layers, -1, cols)
            res = _adamw_blocks(own, got.reshape(N_DEV - 1, -1, cols), stacked(w[p]), stacked(m[p]), stacked(v[p]),
                                0 if layer is None else layer, out.get(p), f"adamw_{group}_{p}")
            out[p] = res
            after = res[1]
    out = {p: [r.reshape(w[p].shape) for r in res] for p, res in out.items()}
    start = 0
    for k in VECTORS:
        n_rows = vec[k].shape[0]
        g = rows[start:start + n_rows]
        start += n_rows
        if k in SHARDED_VECTORS:
            g = lax.dynamic_slice_in_dim(g, me * (D // N_DEV), D // N_DEV, axis=1)
        res = _adamw_rows(g, w[k].reshape(g.shape), m[k].reshape(g.shape), v[k].reshape(g.shape), f"adamw_{k}")
        out[k] = [r.reshape(w[k].shape) for r in [g] + list(res)]

    names = ("norm_a", "w_in_a", "w_grp_a", "scale_a", "w_out_a", "norm_kv", "w_k", "w_v", "norm_b", "w_in_b",
             "w_out_b", "norm_f")
    total = lax.psum(loss[0, 0], ("x", "y", "c"))
    return (total, dx[None], *[out[k][0] for k in names], *[out[k][1] for k in names],
            *[out[k][2] for k in names], *[out[k][3] for k in names])
```

```python
import math

import jax
import jax.numpy as jnp
from jax import lax
from jax.experimental import pallas as pl
from jax.experimental.pallas import tpu as pltpu

F32 = jnp.float32
BF16 = jnp.bfloat16

N_DEV = 8
MESH = pl.DeviceIdType.MESH
RMS_EPS = 1e-6
HEAD_DIM = 128
HALF_HEAD = HEAD_DIM // 2
BAND = 128
DILATIONS = (1, 4, 16)
POOL_WINDOWS = (2, 4, 8, 16)
POOL_HALO = 16
ROPE_THETA = 10000.0
NEG_INF = -1e30
ATTN_SCALE = 1.0 / math.sqrt(HEAD_DIM)
LOG2_E = math.log2(math.e)
ADAM_LR, ADAM_B1, ADAM_B2, ADAM_EPS, ADAM_WD, ADAM_STEP = 0.001, 0.9, 0.999, 1e-08, 0.01, 10
VMEM_LIMIT_BYTES = 56 * 1024 * 1024
ANY = pl.BlockSpec(memory_space=pl.ANY)
NT = (((1,), (1,)), ((), ()))
TN = (((0,), (0,)), ((), ()))


def _params(*semantics):
    return pltpu.CompilerParams(dimension_semantics=semantics, vmem_limit_bytes=VMEM_LIMIT_BYTES)


def _sigmoid(t):
    return 1.0 / (1.0 + jnp.exp(-t))


def _rope(t, cos2, sin2):
    return t * cos2 + pltpu.roll(t, HALF_HEAD, 1) * sin2


def _rope_bwd(dt, cos2, sin2):
    return dt * cos2 + pltpu.roll(dt * sin2, HALF_HEAD, 1)


def _norm_matmul(x, gain, w, w_block, w_index, n_col_blocks, tn, name):
    S, D = x.shape
    tm = min(1024, S)

    def body(x_ref, g_ref, w_ref, o_ref, hn_ref, hs_ref):
        @pl.when(pl.program_id(1) == 0)
        def _():
            xf = x_ref[...]
            inv = lax.rsqrt(jnp.mean(xf * xf, axis=-1, keepdims=True) + RMS_EPS)
            hb = ((xf * inv) * g_ref[...]).astype(BF16)
            hs_ref[...] = hb
            hn_ref[...] = hb
        o_ref[...] = jnp.dot(hs_ref[...], w_ref[...], preferred_element_type=F32)

    return pl.pallas_call(
        body, name=name, grid=(S // tm, n_col_blocks),
        in_specs=[pl.BlockSpec((tm, D), lambda i, j: (i, 0)),
                  pl.BlockSpec((1, D), lambda i, j: (0, 0)),
                  pl.BlockSpec(w_block, w_index)],
        out_specs=[pl.BlockSpec((tm, tn), lambda i, j: (i, j)),
                   pl.BlockSpec((tm, D), lambda i, j: (i, 0))],
        out_shape=[jax.ShapeDtypeStruct((S, n_col_blocks * tn), F32), jax.ShapeDtypeStruct((S, D), BF16)],
        scratch_shapes=[pltpu.VMEM((tm, D), BF16)],
        compiler_params=_params("parallel", "arbitrary"))(x, gain, w)


def _matmul_nt_dnorm(dp, w, w_block, w_index, n_red, tc, x, gain, dres, name):
    S, D = x.shape
    tm = min(1024, S)
    parts = 4
    tp = tm // parts

    def body(d_ref, w_ref, x_ref, g_ref, r_ref, dx_ref, dg_ref, acc_ref):
        i, j = pl.program_id(0), pl.program_id(1)

        @pl.when((i == 0) & (j == 0))
        def _():
            dg_ref[...] = jnp.zeros_like(dg_ref)

        @pl.when(j == 0)
        def _():
            acc_ref[...] = jnp.zeros_like(acc_ref)

        @pl.when(j < n_red)
        def _():
            acc_ref[...] += lax.dot_general(d_ref[...], w_ref[...], NT, preferred_element_type=F32)

        @pl.when(j >= n_red)
        def _():
            xf = x_ref[...]
            inv = lax.rsqrt(jnp.mean(xf * xf, axis=-1, keepdims=True) + RMS_EPS)
            xhat = xf * inv
            dh = acc_ref[pl.ds(pl.multiple_of((j - n_red) * tp, tp), tp), :]
            dg_ref[...] += jnp.sum(dh * xhat, axis=0, keepdims=True)
            dxh = dh * g_ref[...]
            dx_ref[...] = r_ref[...] + inv * (dxh - xhat * jnp.mean(dxh * xhat, axis=-1, keepdims=True))

    last = n_red - 1
    part = pl.BlockSpec((tp, D), lambda i, j: (i * parts + jnp.clip(j - n_red, 0, parts - 1), 0))
    vec = pl.BlockSpec((1, D), lambda i, j: (0, 0))
    return pl.pallas_call(
        body, name=name, grid=(S // tm, n_red + parts),
        in_specs=[pl.BlockSpec((tm, tc), lambda i, j: (i, jnp.minimum(j, last))),
                  pl.BlockSpec(w_block, lambda i, j: w_index(i, jnp.minimum(j, last))), part, vec, part],
        out_specs=[part, vec],
        out_shape=[jax.ShapeDtypeStruct((S, D), F32), jax.ShapeDtypeStruct((1, D), F32)],
        scratch_shapes=[pltpu.VMEM((tm, D), F32)],
        compiler_params=_params("arbitrary", "arbitrary"))(dp, w, x, gain, dres)


def _matmul_tn(a, b, a_block, a_index, b_block, b_index, out_shape, out_block, out_index, acc_shape, n_outer, name):
    S = a.shape[0]
    ts = a_block[0]
    n_tok = S // ts

    def body(a_ref, b_ref, o_ref, acc_ref):
        s = pl.program_id(1)

        @pl.when(s == 0)
        def _():
            acc_ref[...] = jnp.zeros_like(acc_ref)
        acc_ref[...] += lax.dot_general(a_ref[...].astype(BF16), b_ref[...].astype(BF16), TN,
                                        preferred_element_type=F32)

        @pl.when(s == n_tok - 1)
        def _():
            o_ref[...] = acc_ref[...].astype(o_ref.dtype)

    return pl.pallas_call(
        body, name=name, grid=(n_outer, n_tok),
        in_specs=[pl.BlockSpec(a_block, a_index), pl.BlockSpec(b_block, b_index)],
        out_specs=pl.BlockSpec(out_block, out_index),
        out_shape=jax.ShapeDtypeStruct(out_shape, BF16),
        scratch_shapes=[pltpu.VMEM(acc_shape, F32)],
        compiler_params=_params("parallel", "arbitrary"))(a, b)


def _pool(scr_ref, u, row0, tm, E):
    gc = E // len(POOL_WINDOWS)
    t1 = row0 + lax.broadcasted_iota(jnp.int32, (tm, 1), 0) + 1
    out = []
    for g, win in enumerate(POOL_WINDOWS):
        cs = slice(g * gc, (g + 1) * gc)
        acc = u[:, cs]
        for k in range(1, win):
            acc = acc + scr_ref[pl.ds(POOL_HALO - k, tm), cs]
        count = jnp.minimum(t1, win).astype(F32)
        out.append(acc / count - u[:, cs])
    return out


def _a_mid_out(proj, wg, scale, w_out, res, name):
    S, E2 = proj.shape
    E = E2 // 2
    D = res.shape[1]
    gc = E // len(POOL_WINDOWS)
    tm = min(256, S)
    hb = tm // POOL_HALO

    def body(u_ref, uh_ref, gt_ref, wg_ref, sc_ref, wo_ref, r_ref, x_ref, z_ref, scr_ref):
        i = pl.program_id(0)
        scr_ref[0:POOL_HALO, :] = jnp.where(i > 0, uh_ref[...], 0.0)
        u = u_ref[...]
        scr_ref[POOL_HALO:POOL_HALO + tm, :] = u
        pooled = _pool(scr_ref, u, i * tm, tm, E)
        for g in range(len(POOL_WINDOWS)):
            cs = slice(g * gc, (g + 1) * gc)
            y = jnp.dot(pooled[g].astype(BF16), wg_ref[g], preferred_element_type=F32) * sc_ref[:, cs]
            gate = gt_ref[:, cs]
            z_ref[:, cs] = (y * (gate * _sigmoid(gate))).astype(BF16)
        x_ref[...] = r_ref[...] + jnp.dot(z_ref[...], wo_ref[...], preferred_element_type=F32)

    return pl.pallas_call(
        body, name=name, grid=(S // tm,),
        in_specs=[pl.BlockSpec((tm, E), lambda i: (i, 0)),
                  pl.BlockSpec((POOL_HALO, E), lambda i: (jnp.maximum(i * hb - 1, 0), 0)),
                  pl.BlockSpec((tm, E), lambda i: (i, 1)),
                  pl.BlockSpec((len(POOL_WINDOWS), gc, gc), lambda i: (0, 0, 0)),
                  pl.BlockSpec((1, E), lambda i: (0, 0)),
                  pl.BlockSpec((E, D), lambda i: (0, 0)),
                  pl.BlockSpec((tm, D), lambda i: (i, 0))],
        out_specs=[pl.BlockSpec((tm, D), lambda i: (i, 0)), pl.BlockSpec((tm, E), lambda i: (i, 0))],
        out_shape=[jax.ShapeDtypeStruct((S, D), F32), jax.ShapeDtypeStruct((S, E), BF16)],
        scratch_shapes=[pltpu.VMEM((POOL_HALO + tm, E), F32)],
        compiler_params=_params("parallel"))(proj, proj, proj, wg, scale, w_out.reshape(E, D), res)


def _a_mid_bwd(dx, w_out, proj, wg, scale, name):
    S, E2 = proj.shape
    E = E2 // 2
    D = dx.shape[1]
    n_grp = len(POOL_WINDOWS)
    gc = E // n_grp
    tm = min(256, S)
    hb = tm // POOL_HALO
    n_tiles = S // tm
    last_halo = S // POOL_HALO - 1

    def body(dx_ref, dxh_ref, wo_ref, u_ref, uh_ref, gt_ref, gth_ref, wg_ref, sc_ref, dp_ref, dwg_ref, dsc_ref,
             scr_ref, q_ref, dxb_ref):
        i = pl.program_id(0)
        dxb_ref[0:tm, :] = dx_ref[...].astype(BF16)
        dxb_ref[tm:tm + POOL_HALO, :] = dxh_ref[...].astype(BF16)

        @pl.when(i == 0)
        def _():
            dwg_ref[...] = jnp.zeros_like(dwg_ref)
            dsc_ref[...] = jnp.zeros_like(dsc_ref)

        scr_ref[0:POOL_HALO, :] = jnp.where(i > 0, uh_ref[...], 0.0)
        u = u_ref[...]
        scr_ref[POOL_HALO:POOL_HALO + tm, :] = u
        pooled = _pool(scr_ref, u, i * tm, tm, E)
        t1 = i * tm + lax.broadcasted_iota(jnp.int32, (tm, 1), 0) + 1
        t1h = (i + 1) * tm + lax.broadcasted_iota(jnp.int32, (POOL_HALO, 1), 0) + 1
        not_last = i < n_tiles - 1
        for g, win in enumerate(POOL_WINDOWS):
            cs = slice(g * gc, (g + 1) * gc)
            w = wg_ref[g]
            sc = sc_ref[:, cs]
            pb = pooled[g].astype(BF16)
            ypre = jnp.dot(pb, w, preferred_element_type=F32)
            gate = gt_ref[:, cs]
            sg = _sigmoid(gate)
            silu = gate * sg
            dz_all = lax.dot_general(dxb_ref[...], wo_ref[cs, :], NT, preferred_element_type=F32)
            dzg = dz_all[0:tm]
            dy = dzg * silu
            dp_ref[:, E + g * gc:E + (g + 1) * gc] = (dzg * (ypre * sc) * (sg * (1.0 + gate * (1.0 - sg)))).astype(BF16)
            dsc_ref[:, cs] += jnp.sum(dy * ypre, axis=0, keepdims=True)
            dyp = (dy * sc).astype(BF16)
            dwg_ref[g] += lax.dot_general(pb, dyp, TN, preferred_element_type=F32)
            dpool = lax.dot_general(dyp, w, NT, preferred_element_type=F32)
            gate_h = gth_ref[:, cs]
            dyp_h = (dz_all[tm:tm + POOL_HALO] * (gate_h * _sigmoid(gate_h)) * sc).astype(BF16)
            dpool_h = lax.dot_general(dyp_h, w, NT, preferred_element_type=F32)
            q_ref[0:tm, cs] = dpool / jnp.minimum(t1, win).astype(F32)
            q_ref[tm:tm + POOL_HALO, cs] = jnp.where(not_last, dpool_h / jnp.minimum(t1h, win).astype(F32), 0.0)
            acc = q_ref[0:tm, cs] - dpool
            for k in range(1, win):
                acc = acc + q_ref[pl.ds(k, tm), cs]
            dp_ref[:, cs] = acc.astype(BF16)

    return pl.pallas_call(
        body, name=name, grid=(n_tiles,),
        in_specs=[pl.BlockSpec((tm, D), lambda i: (i, 0)),
                  pl.BlockSpec((POOL_HALO, D), lambda i: (jnp.minimum((i + 1) * hb, last_halo), 0)),
                  pl.BlockSpec((E, D), lambda i: (0, 0)),
                  pl.BlockSpec((tm, E), lambda i: (i, 0)),
                  pl.BlockSpec((POOL_HALO, E), lambda i: (jnp.maximum(i * hb - 1, 0), 0)),
                  pl.BlockSpec((tm, E), lambda i: (i, 1)),
                  pl.BlockSpec((POOL_HALO, E), lambda i: (jnp.minimum((i + 1) * hb, last_halo), 1)),
                  pl.BlockSpec((n_grp, gc, gc), lambda i: (0, 0, 0)),
                  pl.BlockSpec((1, E), lambda i: (0, 0))],
        out_specs=[pl.BlockSpec((tm, E2), lambda i: (i, 0)),
                   pl.BlockSpec((n_grp, gc, gc), lambda i: (0, 0, 0)),
                   pl.BlockSpec((1, E), lambda i: (0, 0))],
        out_shape=[jax.ShapeDtypeStruct((S, E2), BF16),
                   jax.ShapeDtypeStruct((n_grp, gc, gc), F32),
                   jax.ShapeDtypeStruct((1, E), F32)],
        scratch_shapes=[pltpu.VMEM((POOL_HALO + tm, E), F32), pltpu.VMEM((tm + POOL_HALO, E), F32),
                        pltpu.VMEM((tm + POOL_HALO, D), BF16)],
        compiler_params=_params("arbitrary"))(dx, dx, w_out.reshape(E, D), proj, proj, proj, proj, wg, scale)


def _rope_k(kvp, cos2, sin2, name):
    S, E2 = kvp.shape
    E = E2 // 2
    tm = min(256, S)

    def body(k_ref, c_ref, s_ref, ko_ref):
        cosv, sinv = c_ref[...], s_ref[...]
        for h in range(E // HEAD_DIM):
            hs = slice(h * HEAD_DIM, (h + 1) * HEAD_DIM)
            ko_ref[:, hs] = _rope(k_ref[:, hs], cosv, sinv)

    return pl.pallas_call(
        body, name=name, grid=(S // tm,),
        in_specs=[pl.BlockSpec((tm, E), lambda i: (i, 0)),
                  pl.BlockSpec((tm, HEAD_DIM), lambda i: (i, 0)), pl.BlockSpec((tm, HEAD_DIM), lambda i: (i, 0))],
        out_specs=pl.BlockSpec((tm, E), lambda i: (i, 0)),
        out_shape=jax.ShapeDtypeStruct((S, E), F32),
        compiler_params=_params("parallel"))(kvp, cos2, sin2)


CLASSES = 16


def _class_order(a, back, name):
    S, W = a.shape
    M = S // CLASSES

    def body(a_ref, o_ref):
        for r in range(CLASSES):
            if back:
                o_ref[pl.ds(r, M, stride=CLASSES), :] = a_ref[pl.ds(r * M, M), :]
            else:
                o_ref[pl.ds(r * M, M), :] = a_ref[pl.ds(r, M, stride=CLASSES), :]

    col = pl.BlockSpec((S, HEAD_DIM), lambda i: (0, i))
    return pl.pallas_call(
        body, name=name, grid=(W // HEAD_DIM,), in_specs=[col], out_specs=col,
        out_shape=jax.ShapeDtypeStruct((S, W), a.dtype),
        compiler_params=_params("parallel"))(a)


def _runs(r, b, dil, M, back=0):
    nj = CLASSES // dil
    c = BAND // nj
    return [((r + dil * j) * M + (b - back) * c, (1 + back) * c) for j in range(nj)]


def _load(ref, runs):
    parts = [ref[pl.ds(start, n), :] for start, n in runs]
    return parts[0] if len(parts) == 1 else jnp.concatenate(parts, axis=0)


def _store(ref, runs, val, add=False):
    at = 0
    for start, n in runs:
        if add:
            ref[pl.ds(start, n), :] += val[at:at + n]
        else:
            ref[pl.ds(start, n), :] = val[at:at + n]
        at += n


def _keys(ref, r, b, dil, M):
    if b > 0:
        return _load(ref, _runs(r, b, dil, M, back=1)).astype(BF16)
    parts = []
    for start, n in _runs(r, 0, dil, M):
        parts += [jnp.zeros((n, HEAD_DIM), BF16), ref[pl.ds(start, n), :].astype(BF16)]
    return jnp.concatenate(parts, axis=0)


def _band_mask(dil, first):
    nj = CLASSES // dil
    c = BAND // nj
    row = lax.broadcasted_iota(jnp.int32, (BAND, 2 * BAND), 0)
    col = lax.broadcasted_iota(jnp.int32, (BAND, 2 * BAND), 1)
    q_place = (row % c) * nj + row // c
    k_place = (col % (2 * c) - c) * nj + col // (2 * c)
    mask = (q_place >= k_place) & (q_place <= k_place + BAND)
    return mask & (col % (2 * c) >= c) if first else mask


def _lane_column(tile, lane, h):
    return jnp.sum(jnp.where(lane == h, tile, 0.0), axis=-1, keepdims=True)


def _attn_fwd(proj, kr, kvp, cos2, sin2, group, dil, name):
    S, PW = proj.shape
    E = kr.shape[1]
    H = E // HEAD_DIM
    M = S // CLASSES
    nb = S // (BAND * dil)

    def body(q_ref, k_ref, v_ref, c_ref, s_ref, o_ref, l_ref, o_scr):
        h = pl.program_id(0)
        lane = lax.broadcasted_iota(jnp.int32, (BAND, HEAD_DIM), 1)
        edge, inner = _band_mask(dil, True), _band_mask(dil, False)

        @pl.when(h == 0)
        def _():
            l_ref[...] = jnp.zeros_like(l_ref)

        def scores(r, b):
            runs = _runs(r, b, dil, M)
            qr = _rope(_load(q_ref, runs), _load(c_ref, runs), _load(s_ref, runs)).astype(BF16)
            return lax.dot_general(qr, _keys(k_ref, r, b, dil, M), NT, preferred_element_type=F32)

        units = [(r, b) for r in range(dil) for b in range(nb)]
        ahead = scores(*units[0])
        for i, (r, b) in enumerate(units):
            s = ahead
            if i + 1 < len(units):
                ahead = scores(*units[i + 1])
            runs = _runs(r, b, dil, M)
            s = jnp.where(edge if b == 0 else inner, s, NEG_INF)
            m = jnp.max(s, axis=-1, keepdims=True)
            p = jnp.exp2((s - m) * (ATTN_SCALE * LOG2_E))
            l = jnp.sum(p, axis=-1, keepdims=True)
            _store(o_scr, runs, jnp.dot(p.astype(BF16), _keys(v_ref, r, b, dil, M), preferred_element_type=F32) / l)
            _store(l_ref, runs, jnp.where(lane == h, m * ATTN_SCALE + jnp.log(l), _load(l_ref, runs)))
        o_ref[...] = o_scr[...].astype(BF16)

    col = (S, HEAD_DIM)
    whole = pl.BlockSpec(col, lambda h: (0, 0))
    return pl.pallas_call(
        body, name=name, grid=(H,),
        in_specs=[pl.BlockSpec(col, lambda h: (0, group * H + h)), pl.BlockSpec(col, lambda h: (0, h)),
                  pl.BlockSpec(col, lambda h: (0, H + h)), whole, whole],
        out_specs=[pl.BlockSpec(col, lambda h: (0, h)), whole],
        out_shape=[jax.ShapeDtypeStruct((S, E), BF16), jax.ShapeDtypeStruct((S, HEAD_DIM), F32)],
        scratch_shapes=[pltpu.VMEM(col, F32)],
        compiler_params=_params("arbitrary"))(proj, kr, kvp, cos2, sin2)


def _attn_bwd(proj, kr, kvp, cos2, sin2, do, lse, dlt, dproj, group, dil, name):
    S, PW = proj.shape
    E = kr.shape[1]
    H = E // HEAD_DIM
    M = S // CLASSES
    nb = S // (BAND * dil)

    def body(q_ref, k_ref, v_ref, c_ref, s_ref, do_ref, l_ref, dl_ref, dproj_ref,
             dq_ref, dk_ref, dv_ref, dq_scr, dk_scr, dv_scr):
        h = pl.program_id(0)
        lane = lax.broadcasted_iota(jnp.int32, (BAND, HEAD_DIM), 1)
        edge, inner = _band_mask(dil, True), _band_mask(dil, False)
        dk_scr[...] = jnp.zeros_like(dk_scr)
        dv_scr[...] = jnp.zeros_like(dv_scr)

        def scores(r, b):
            runs = _runs(r, b, dil, M)
            qr = _rope(_load(q_ref, runs), _load(c_ref, runs), _load(s_ref, runs)).astype(BF16)
            return qr, lax.dot_general(qr, _keys(k_ref, r, b, dil, M), NT, preferred_element_type=F32)

        units = [(r, b) for r in range(dil) for b in range(nb)]
        ahead = scores(*units[0])
        for i, (r, b) in enumerate(units):
            qr, s = ahead
            if i + 1 < len(units):
                ahead = scores(*units[i + 1])
            runs = _runs(r, b, dil, M)
            dob = _load(do_ref, runs).astype(BF16)
            dpr = lax.dot_general(dob, _keys(v_ref, r, b, dil, M), NT, preferred_element_type=F32)
            s = jnp.where(edge if b == 0 else inner, s, NEG_INF)
            p = jnp.exp2(s * (ATTN_SCALE * LOG2_E) - _lane_column(_load(l_ref, runs), lane, h) * LOG2_E)
            ds = (p * (dpr - _lane_column(_load(dl_ref, runs), lane, h)) * ATTN_SCALE).astype(BF16)
            dq = jnp.dot(ds, _keys(k_ref, r, b, dil, M), preferred_element_type=F32)
            _store(dq_scr, runs, _rope_bwd(dq, _load(c_ref, runs), _load(s_ref, runs)))
            dkc = lax.dot_general(ds, qr, TN, preferred_element_type=F32)
            dvc = lax.dot_general(p.astype(BF16), dob, TN, preferred_element_type=F32)
            if b > 0:
                both = _runs(r, b, dil, M, back=1)
                _store(dk_scr, both, dkc, add=True)
                _store(dv_scr, both, dvc, add=True)
            else:
                n = runs[0][1]
                own = jnp.concatenate([dkc[(2 * j + 1) * n:(2 * j + 2) * n] for j in range(len(runs))], axis=0)
                _store(dk_scr, runs, own, add=True)
                own = jnp.concatenate([dvc[(2 * j + 1) * n:(2 * j + 2) * n] for j in range(len(runs))], axis=0)
                _store(dv_scr, runs, own, add=True)
        dq_ref[...] = dq_scr[...].astype(BF16)
        dk_ref[...] = dk_scr[...].astype(BF16)
        dv_ref[...] = dv_scr[...].astype(BF16)

    col = (S, HEAD_DIM)
    whole = pl.BlockSpec(col, lambda h: (0, 0))
    head = pl.BlockSpec(col, lambda h: (0, h))
    return pl.pallas_call(
        body, name=name, grid=(H,),
        in_specs=[pl.BlockSpec(col, lambda h: (0, group * H + h)), head, pl.BlockSpec(col, lambda h: (0, H + h)),
                  whole, whole, head, whole, whole, ANY],
        out_specs=[pl.BlockSpec(col, lambda h: (0, group * H + h)), head, head],
        out_shape=[jax.ShapeDtypeStruct(dproj.shape, BF16), jax.ShapeDtypeStruct((S, E), BF16),
                   jax.ShapeDtypeStruct((S, E), BF16)],
        scratch_shapes=[pltpu.VMEM(col, F32)] * 3,
        input_output_aliases={8: 0},
        compiler_params=_params("parallel"))(proj, kr, kvp, cos2, sin2, do, lse, dlt, dproj)


def _group_weights(l_refs, h):
    ls = [r[:, h:h + 1] for r in l_refs]
    mx = jnp.maximum(jnp.maximum(ls[0], ls[1]), ls[2])
    es = [jnp.exp(l - mx) for l in ls]
    inv = 1.0 / (es[0] + es[1] + es[2])
    return [e * inv for e in es]


def _merge_out(outs, lses, proj, w_out, res, name):
    S, E = outs[0].shape
    D = res.shape[1]
    tm = min(256, S)
    gate_col = proj.shape[1] // E - 1

    def body(o0, o1, o2, l0, l1, l2, gt_ref, w_ref, r_ref, x_ref, z_ref):
        for h in range(E // HEAD_DIM):
            hs = slice(h * HEAD_DIM, (h + 1) * HEAD_DIM)
            a = _group_weights((l0, l1, l2), h)
            merged = a[0] * o0[:, hs] + a[1] * o1[:, hs] + a[2] * o2[:, hs]
            gate = gt_ref[:, hs]
            z_ref[:, hs] = (merged * (gate * _sigmoid(gate))).astype(BF16)
        x_ref[...] = r_ref[...] + jnp.dot(z_ref[...], w_ref[...], preferred_element_type=F32)

    wide = pl.BlockSpec((tm, E), lambda i: (i, 0))
    thin = pl.BlockSpec((tm, HEAD_DIM), lambda i: (i, 0))
    return pl.pallas_call(
        body, name=name, grid=(S // tm,),
        in_specs=[wide, wide, wide, thin, thin, thin, pl.BlockSpec((tm, E), lambda i: (i, gate_col)),
                  pl.BlockSpec((E, D), lambda i: (0, 0)), pl.BlockSpec((tm, D), lambda i: (i, 0))],
        out_specs=[pl.BlockSpec((tm, D), lambda i: (i, 0)), wide],
        out_shape=[jax.ShapeDtypeStruct((S, D), F32), jax.ShapeDtypeStruct((S, E), BF16)],
        compiler_params=_params("parallel"))(*outs, *lses, proj, w_out.reshape(E, D), res)


def _merge_bwd(dx, w_out, outs, lses, proj, name):
    S, E = outs[0].shape
    D = dx.shape[1]
    tm = min(256, S)
    gate_col = proj.shape[1] // E - 1

    def body(dx_ref, w_ref, o0, o1, o2, l0, l1, l2, gt_ref, d0, d1, d2, t0, t1, t2, dg_ref, dxb_ref):
        o_refs, d_refs, t_refs = (o0, o1, o2), (d0, d1, d2), (t0, t1, t2)
        lane = lax.broadcasted_iota(jnp.int32, (tm, HEAD_DIM), 1)
        tiles = [jnp.zeros((tm, HEAD_DIM), F32) for _ in range(3)]
        dxb_ref[...] = dx_ref[...].astype(BF16)
        for h in range(E // HEAD_DIM):
            hs = slice(h * HEAD_DIM, (h + 1) * HEAD_DIM)
            a = _group_weights((l0, l1, l2), h)
            merged = a[0] * o0[:, hs] + a[1] * o1[:, hs] + a[2] * o2[:, hs]
            gate = gt_ref[:, hs]
            sg = _sigmoid(gate)
            dzh = lax.dot_general(dxb_ref[...], w_ref[hs, :], NT, preferred_element_type=F32)
            dmerged = dzh * (gate * sg)
            dg_ref[:, hs] = (dzh * merged * (sg * (1.0 + gate * (1.0 - sg)))).astype(BF16)
            tot = jnp.sum(dmerged * merged, axis=-1, keepdims=True)
            for g in range(3):
                d_refs[g][:, hs] = a[g] * dmerged
                tiles[g] = jnp.where(lane == h, a[g] * tot, tiles[g])
        for g in range(3):
            t_refs[g][...] = tiles[g]

    wide = pl.BlockSpec((tm, E), lambda i: (i, 0))
    thin = pl.BlockSpec((tm, HEAD_DIM), lambda i: (i, 0))
    res = pl.pallas_call(
        body, name=name, grid=(S // tm,),
        in_specs=[pl.BlockSpec((tm, D), lambda i: (i, 0)), pl.BlockSpec((E, D), lambda i: (0, 0)),
                  wide, wide, wide, thin, thin, thin, pl.BlockSpec((tm, E), lambda i: (i, gate_col))],
        out_specs=[wide, wide, wide, thin, thin, thin, pl.BlockSpec((tm, E), lambda i: (i, gate_col))],
        out_shape=[jax.ShapeDtypeStruct((S, E), F32)] * 3 + [jax.ShapeDtypeStruct((S, HEAD_DIM), F32)] * 3
        + [jax.ShapeDtypeStruct(proj.shape, BF16)],
        scratch_shapes=[pltpu.VMEM((tm, D), BF16)],
        compiler_params=_params("parallel"))(dx, w_out.reshape(E, D), *outs, *lses, proj)
    return res[0:3], res[3:6], res[6]


def _kv_bwd(dks, dvs, cos2, sin2, name):
    S, E = dks[0].shape
    n = len(dks)
    tm = min(256, S)

    def body(*refs):
        dk_refs, dv_refs = refs[0:n], refs[n:2 * n]
        c_ref, s_ref, o_ref = refs[2 * n:]
        cosv, sinv = c_ref[...], s_ref[...]
        for h in range(E // HEAD_DIM):
            hs = slice(h * HEAD_DIM, (h + 1) * HEAD_DIM)
            dk = dk_refs[0][:, hs].astype(F32)
            dv = dv_refs[0][:, hs].astype(F32)
            for j in range(1, n):
                dk = dk + dk_refs[j][:, hs].astype(F32)
                dv = dv + dv_refs[j][:, hs].astype(F32)
            o_ref[:, hs] = _rope_bwd(dk, cosv, sinv).astype(BF16)
            o_ref[:, E + h * HEAD_DIM:E + (h + 1) * HEAD_DIM] = dv.astype(BF16)

    wide = pl.BlockSpec((tm, E), lambda i: (i, 0))
    thin = pl.BlockSpec((tm, HEAD_DIM), lambda i: (i, 0))
    return pl.pallas_call(
        body, name=name, grid=(S // tm,),
        in_specs=[wide] * (2 * n) + [thin, thin],
        out_specs=pl.BlockSpec((tm, 2 * E), lambda i: (i, 0)),
        out_shape=jax.ShapeDtypeStruct((S, 2 * E), BF16),
        compiler_params=_params("parallel"))(*dks, *dvs, cos2, sin2)


def _final_norm_loss(x, target, gain, name):
    S, D = x.shape
    tm = min(256, S)

    def body(x_ref, t_ref, g_ref, loss_ref, dx_ref, dg_ref):
        @pl.when(pl.program_id(0) == 0)
        def _():
            loss_ref[...] = jnp.zeros_like(loss_ref)
            dg_ref[...] = jnp.zeros_like(dg_ref)
        xf = x_ref[...]
        inv = lax.rsqrt(jnp.mean(xf * xf, axis=-1, keepdims=True) + RMS_EPS)
        xhat = xf * inv
        g = g_ref[...]
        err = xhat * g - t_ref[...]
        loss_ref[...] += 0.5 * jnp.sum(jnp.mean(err * err, axis=-1, keepdims=True), axis=0, keepdims=True)
        dy = err / D
        dg_ref[...] += jnp.sum(dy * xhat, axis=0, keepdims=True)
        dxh = dy * g
        dx_ref[...] = inv * (dxh - xhat * jnp.mean(dxh * xhat, axis=-1, keepdims=True))

    tile = pl.BlockSpec((tm, D), lambda i: (i, 0))
    vec = pl.BlockSpec((1, D), lambda i: (0, 0))
    return pl.pallas_call(
        body, name=name, grid=(S // tm,),
        in_specs=[tile, tile, vec],
        out_specs=[pl.BlockSpec((1, 1), lambda i: (0, 0)), tile, vec],
        out_shape=[jax.ShapeDtypeStruct((1, 1), F32), jax.ShapeDtypeStruct((S, D), F32),
                   jax.ShapeDtypeStruct((1, D), F32)],
        compiler_params=_params("arbitrary"))(x, target, gain)


def _adamw_math(g, w, m, v):
    m = ADAM_B1 * m + (1.0 - ADAM_B1) * g
    v = ADAM_B2 * v + (1.0 - ADAM_B2) * (g * g)
    m_hat = m / (1.0 - ADAM_B1 ** ADAM_STEP)
    v_hat = v / (1.0 - ADAM_B2 ** ADAM_STEP)
    delta = -ADAM_LR * (m_hat / (jnp.sqrt(v_hat) + ADAM_EPS) + ADAM_WD * w)
    return delta, m, v


def _adamw_rows(g, w, m, v, name):
    def body(g_ref, w_ref, m_ref, v_ref, d_ref, mo_ref, vo_ref):
        d_ref[...], mo_ref[...], vo_ref[...] = _adamw_math(g_ref[...], w_ref[...], m_ref[...], v_ref[...])

    whole = pl.BlockSpec(memory_space=pltpu.VMEM)
    return pl.pallas_call(
        body, name=name, in_specs=[whole] * 4, out_specs=[whole] * 3,
        out_shape=[jax.ShapeDtypeStruct(g.shape, F32)] * 3)(g, w, m, v)


def _adamw_blocks(own, others, w, m, v, layer, earlier, name):
    L, R, C = w.shape
    n = others.shape[0]
    tr = R
    while tr * C > 128 * 1024 and tr % 16 == 0:
        tr //= 2

    def body(o_ref, p_ref, w_ref, m_ref, v_ref, *rest):
        g_ref, d_ref, mo_ref, vo_ref = rest[-4:]
        g = o_ref[...].astype(F32)
        for j in range(n):
            g = g + p_ref[j].astype(F32)
        g_ref[...] = g
        d_ref[...], mo_ref[...], vo_ref[...] = _adamw_math(g, w_ref[...], m_ref[...], v_ref[...])

    tile = pl.BlockSpec((None, tr, C), lambda i: (layer, i, 0))
    kept = [] if earlier is None else list(earlier)
    return pl.pallas_call(
        body, name=name, grid=(R // tr,),
        in_specs=[pl.BlockSpec((tr, C), lambda i: (i, 0)), pl.BlockSpec((n, tr, C), lambda i: (0, i, 0)),
                  tile, tile, tile] + [ANY] * len(kept),
        out_specs=[tile] * 4,
        out_shape=[jax.ShapeDtypeStruct((L, R, C), F32)] * 4,
        input_output_aliases={5 + j: j for j in range(len(kept))},
        compiler_params=_params("parallel"))(own, others, w, m, v, *kept)


def _position():
    return lax.axis_index("x"), lax.axis_index("y"), lax.axis_index("c")


def _block_index(px, py, pc):
    return 4 * px + 2 * py + pc


def _all_gather(shards, name):
    n = len(shards)

    def body(*refs):
        ins, outs = refs[0:n], refs[n:2 * n]
        send_sems, recv_sems, local_sems = refs[2 * n:]
        x, y, c = _position()
        me, sibling = (x, y, c), (x, y, 1 - c)
        chips = [(1 - x, y), (x, 1 - y), (1 - x, 1 - y)]

        def copy(a, k, block, to, src=None):
            rows = outs[a].at[_block_index(*block)]
            return pltpu.make_async_remote_copy(
                src_ref=rows if src is None else src, dst_ref=rows,
                send_sem=send_sems.at[a, k], recv_sem=recv_sems.at[a, k], device_id=to, device_id_type=MESH)

        mine, first, passed = [], [], []
        for a in range(n):
            cp = pltpu.make_async_copy(ins[a], outs[a].at[_block_index(*me)], local_sems.at[a])
            cp.start()
            mine.append(cp)
            first.append(copy(a, 0, me, sibling, src=ins[a]))
            first += [copy(a, 1 + j, me, (*chip, c), src=ins[a]) for j, chip in enumerate(chips)]
        for cp in first:
            cp.start()
        for j, chip in enumerate(chips):
            for a in range(n):
                copy(a, 1 + j, (*chip, c), me).wait_recv()
                fwd = copy(a, 4 + j, (*chip, c), sibling)
                fwd.start()
                passed.append(fwd)
        for a in range(n):
            copy(a, 0, sibling, me).wait_recv()
            for j, chip in enumerate(chips):
                copy(a, 4 + j, (*chip, 1 - c), me).wait_recv()
        for cp in first + passed:
            cp.wait_send()
        for cp in mine:
            cp.wait()

    return pl.pallas_call(
        body, name=name,
        in_specs=[ANY] * n, out_specs=[ANY] * n,
        out_shape=[jax.ShapeDtypeStruct((N_DEV,) + s.shape, s.dtype) for s in shards],
        scratch_shapes=[pltpu.SemaphoreType.DMA((n, 7)), pltpu.SemaphoreType.DMA((n, 7)),
                        pltpu.SemaphoreType.DMA((n,))],
    )(*shards)


def _peers(x, y, c):
    return [((1 - x) if k & 4 else x, (1 - y) if k & 2 else y, (1 - c) if k & 1 else c) for k in range(1, N_DEV)]


HBM = pl.BlockSpec(memory_space=pltpu.HBM)
SEM = pl.BlockSpec(memory_space=pltpu.SEMAPHORE)
EFFECT = pltpu.SideEffectType.DATAFLOW_SIDE_EFFECTING


ALL_PEERS = (1, 2, 3, 4, 5, 6, 7)
SIBLING_AND_SAME_CORES = (1, 2, 4, 6)


def _push_copy(src_refs, land_refs, send_sems, recv_sems, a, i, relations, per_peer, by_sender, arriving):
    peer = _peers(*_position())[relations[i] - 1]
    me_idx, p_idx = _block_index(*_position()), _block_index(*peer)
    src = src_refs[a].at[p_idx] if per_peer else src_refs[a]
    if by_sender:
        slot = p_idx if arriving else me_idx
    else:
        slot = relations[i] - 1
    sem = a * len(relations) + i
    return pltpu.make_async_remote_copy(
        src_ref=src, dst_ref=land_refs[a].at[slot], send_sem=send_sems.at[sem], recv_sem=recv_sems.at[sem],
        device_id=peer, device_id_type=MESH)


def _push_start(srcs, lands, relations, per_peer, by_sender, after, name):
    n = len(srcs)

    def body(*refs):
        src_refs, land_refs = refs[0:n], refs[n:2 * n]
        send_sems, recv_sems = refs[2 * n + 1], refs[2 * n + 2]
        token = refs[-1]
        for a in range(n):
            for i in range(len(relations)):
                _push_copy(src_refs, land_refs, send_sems, recv_sems, a, i, relations, per_peer, by_sender, False).start()
        token[...] = jnp.zeros_like(token)

    args = [pltpu.with_memory_space_constraint(t, pltpu.HBM) for t in list(srcs) + list(lands)]
    res = pl.pallas_call(
        body, name=name,
        in_specs=[HBM] * (2 * n) + [ANY],
        out_specs=[SEM, SEM] + [HBM] * (2 * n) + [pl.BlockSpec(memory_space=pltpu.VMEM)],
        out_shape=[pltpu.SemaphoreType.DMA((n * len(relations),)), pltpu.SemaphoreType.DMA((n * len(relations),))]
        + [pltpu.HBM(t.shape, t.dtype) for t in args] + [jax.ShapeDtypeStruct((8, 128), F32)],
        input_output_aliases={i: 2 + i for i in range(2 * n)},
        compiler_params=pltpu.CompilerParams(has_side_effects=EFFECT))(*args, after)
    return res[0], res[1], res[2:2 + n], res[2 + n:2 + 2 * n], res[-1]


def _push_wait(started, relations, per_peer, by_sender, after, name):
    send_sems, recv_sems, srcs, lands, _ = started
    n = len(srcs)

    def body(*refs):
        src_refs, land_refs = refs[0:n], refs[n:2 * n]
        send_s, recv_s = refs[2 * n], refs[2 * n + 1]
        for a in range(n):
            for i in range(len(relations)):
                _push_copy(src_refs, land_refs, send_s, recv_s, a, i, relations, per_peer, by_sender, False).wait_send()
                _push_copy(src_refs, land_refs, send_s, recv_s, a, i, relations, per_peer, by_sender, True).wait_recv()

    res = pl.pallas_call(
        body, name=name,
        in_specs=[HBM] * (2 * n) + [SEM, SEM, ANY],
        out_specs=[HBM] * (2 * n),
        out_shape=[pltpu.HBM(t.shape, t.dtype) for t in list(srcs) + list(lands)],
        input_output_aliases={i: i for i in range(2 * n)},
        compiler_params=pltpu.CompilerParams(has_side_effects=EFFECT))(*srcs, *lands, send_sems, recv_sems, after)
    return res[0:n], res[n:2 * n]


def _pass_on(lands, name):
    n = len(lands)

    def body(*refs):
        outs = refs[n:2 * n]
        send_sems, recv_sems = refs[2 * n:]
        x, y, c = _position()
        chips = [(1 - x, y), (x, 1 - y), (1 - x, 1 - y)]
        copies = []
        for a in range(n):
            for j, chip in enumerate(chips):
                def copy(core):
                    rows = outs[a].at[_block_index(*chip, core)]
                    return pltpu.make_async_remote_copy(
                        src_ref=rows, dst_ref=rows, send_sem=send_sems.at[a, j], recv_sem=recv_sems.at[a, j],
                        device_id=(x, y, 1 - c), device_id_type=MESH)
                copy(c).start()
                copies.append((copy(c), copy(1 - c)))
        for sending, arriving in copies:
            sending.wait_send()
            arriving.wait_recv()

    return pl.pallas_call(
        body, name=name,
        in_specs=[ANY] * n, out_specs=[ANY] * n,
        out_shape=[jax.ShapeDtypeStruct(t.shape, t.dtype) for t in lands],
        input_output_aliases={a: a for a in range(n)},
        scratch_shapes=[pltpu.SemaphoreType.DMA((n, 3)), pltpu.SemaphoreType.DMA((n, 3))],
    )(*lands)


def _all_reduce_rows(v, name):
    R, D = v.shape

    def body(v_ref, o_ref, buf_ref, send_sems, recv_sems):
        x, y, c = _position()
        me_idx = _block_index(x, y, c)
        buf_ref[me_idx] = v_ref[...]
        copies = []
        for k in range(1, N_DEV):
            px = (1 - x) if k & 4 else x
            py = (1 - y) if k & 2 else y
            pc = (1 - c) if k & 1 else c
            rc = pltpu.make_async_remote_copy(
                src_ref=v_ref, dst_ref=buf_ref.at[me_idx],
                send_sem=send_sems.at[k - 1], recv_sem=recv_sems.at[k - 1],
                device_id=(px, py, pc), device_id_type=MESH)
            rc.start()
            copies.append((rc, pltpu.make_async_remote_copy(
                src_ref=v_ref, dst_ref=buf_ref.at[_block_index(px, py, pc)],
                send_sem=send_sems.at[k - 1], recv_sem=recv_sems.at[k - 1],
                device_id=(px, py, pc), device_id_type=MESH)))
        for rc, arrival in copies:
            rc.wait_send()
            arrival.wait_recv()
        acc = buf_ref[0]
        for j in range(1, N_DEV):
            acc = acc + buf_ref[j]
        o_ref[...] = acc

    return pl.pallas_call(
        body, name=name,
        in_specs=[pl.BlockSpec(memory_space=pltpu.VMEM)],
        out_specs=pl.BlockSpec(memory_space=pltpu.VMEM),
        out_shape=jax.ShapeDtypeStruct((R, D), F32),
        scratch_shapes=[pltpu.VMEM((N_DEV, R, D), F32),
                        pltpu.SemaphoreType.DMA((7,)), pltpu.SemaphoreType.DMA((7,))],
    )(v)


def _rope_tables(S):
    inv_freq = 1.0 / (ROPE_THETA ** (jnp.arange(0, HEAD_DIM, 2, dtype=F32) / HEAD_DIM))
    ang = jnp.arange(S, dtype=F32)[:, None] * inv_freq[None, :]
    cos, sin = jnp.cos(ang), jnp.sin(ang)
    return jnp.concatenate([cos, cos], axis=1), jnp.concatenate([-sin, sin], axis=1)


def _local_step(xs, target, vecs, n_a, n_b, get_weights, put_grads):
    S, D = xs.shape
    E = D
    cos2, sin2 = _rope_tables(S)
    ts = min(1024, S)

    def col_blocks(w):
        cb = w.shape[2]
        tn = min(cb, 1024)
        per = cb // tn
        return (None, D, tn), (lambda i, j: (j // per, 0, j % per)), N_DEV * per, tn

    def grad_in(hn, dproj, cb, name):
        return _matmul_tn(hn, dproj, (ts, D), lambda j, s: (s, 0), (ts, cb), lambda j, s: (s, j),
                          (N_DEV, D, cb), (None, D, cb), lambda j, s: (j, 0, 0), (D, cb), N_DEV, name)

    def grad_out(z, dx, name, col=0):
        rows = z.shape[1]
        ta = min(1024, rows)
        out = _matmul_tn(z, dx, (ts, ta), lambda a, s: (s, a), (ts, E), lambda a, s: (s, col),
                         (rows, E), (ta, E), lambda a, s: (a, 0), (ta, E), rows // ta, name)
        return out.reshape(N_DEV, rows // N_DEV, E)

    x = xs
    a_saved, b_saved = [], []
    for i in range(n_a):
        w = get_weights(f"a{i}", x)
        blk, idx, nblocks, tn = col_blocks(w["w_in"])
        proj, hn = _norm_matmul(x, vecs["norm_a"][i:i + 1], w["w_in"], blk, idx, nblocks, tn, f"a{i}_in")
        w = {**w, **get_weights(f"a{i}_rest", proj)}
        x_next, z = _a_mid_out(proj, w["w_grp"], vecs["scale_a"][i:i + 1], w["w_out"], x, f"a{i}_out")
        a_saved.append((x, hn, proj, z, w))
        x = x_next
    w_kv = get_weights("kv", x)["w_kv"]
    x = x_kv = _class_order(x, False, "kv_x_order")
    target = _class_order(target, False, "target_order")
    cos2, sin2 = _class_order(cos2, False, "cos_order"), _class_order(sin2, False, "sin_order")
    tn = min(E, 1024)
    kvp, hn_kv = _norm_matmul(x, vecs["norm_kv"], w_kv, (D, tn), lambda i, j: (0, j), 2 * E // tn, tn, "kv_in")
    kr = _rope_k(kvp, cos2, sin2, "kv_rope")
    after = kr
    for i in range(n_b):
        w = get_weights(f"b{i}", after)
        blk, idx, nblocks, tn = col_blocks(w["w_in"])
        proj, hn = _norm_matmul(x, vecs["norm_b"][i:i + 1], w["w_in"], blk, idx, nblocks, tn, f"b{i}_in")
        outs, lses = [], []
        for g, dil in enumerate(DILATIONS):
            o, l = _attn_fwd(proj, kr, kvp, cos2, sin2, g, dil, f"b{i}_attn{g}")
            outs.append(o)
            lses.append(l)
        x_next, z = _merge_out(outs, lses, proj, w["w_out"], x, f"b{i}_out")
        b_saved.append((x, hn, proj, z, outs, lses, w))
        x = x_next
        after = x
    loss, dx, dg_f = _final_norm_loss(x, target, vecs["norm_f"], "final")

    vec = {"norm_a": [None] * n_a, "scale_a": [None] * n_a, "norm_b": [None] * n_b, "norm_f": [dg_f]}
    dks, dvs = [], []
    for i in reversed(range(n_b)):
        x_in, hn, proj, z, outs, lses, w = b_saved[i]
        dw_out = grad_out(z, dx, f"b{i}_dwout")
        dos, dlts, dproj = _merge_bwd(dx, w["w_out"], outs, lses, proj, f"b{i}_dmerge")
        for g, dil in enumerate(DILATIONS):
            dproj, dk, dv = _attn_bwd(proj, kr, kvp, cos2, sin2, dos[g], lses[g], dlts[g], dproj, g, dil,
                                      f"b{i}_dattn{g}")
            dks.append(dk)
            dvs.append(dv)
        cb = w["w_in"].shape[2]
        tok = put_grads(f"b{i}", {"w_out": dw_out, "w_in": grad_in(hn, dproj, cb, f"b{i}_dwin")})
        dx, vec["norm_b"][i] = _matmul_nt_dnorm(dproj, w["w_in"], (None, D, cb), lambda t, j: (j, 0, 0), N_DEV, cb, x_in,
                                                vecs["norm_b"][i:i + 1] + tok[0:1, 0:1], dx, f"b{i}_dhn")

    dkv = _kv_bwd(dks, dvs, cos2, sin2, "kv_dsum")
    tok = put_grads("kv", {"w_k": grad_out(hn_kv, dkv, "kv_dwk", 0), "w_v": grad_out(hn_kv, dkv, "kv_dwv", 1)})
    tk = min(E, 1024)
    dx, dg_kv = _matmul_nt_dnorm(dkv, w_kv, (D, tk), lambda t, j: (0, j), 2 * E // tk, tk, x_kv,
                                 vecs["norm_kv"] + tok[0:1, 0:1], dx, "kv_dhn")
    vec["norm_kv"] = [dg_kv]
    dx = _class_order(dx, True, "kv_dx_order")

    for i in reversed(range(n_a)):
        x_in, hn, proj, z, w = a_saved[i]
        dw_out = grad_out(z, dx, f"a{i}_dwout")
        dproj, dwg, dsc = _a_mid_bwd(dx, w["w_out"], proj, w["w_grp"], vecs["scale_a"][i:i + 1], f"a{i}_dmid")
        n_grp, gc, _ = dwg.shape
        dwg = dwg.reshape(n_grp, N_DEV, gc // N_DEV, gc).transpose(1, 0, 2, 3).astype(BF16)
        vec["scale_a"][i] = dsc
        cb = w["w_in"].shape[2]
        tok = put_grads(f"a{i}", {"w_out": dw_out, "w_grp": dwg, "w_in": grad_in(hn, dproj, cb, f"a{i}_dwin")})
        dx, vec["norm_a"][i] = _matmul_nt_dnorm(dproj, w["w_in"], (None, D, cb), lambda t, j: (j, 0, 0), N_DEV, cb, x_in,
                                                vecs["norm_a"][i:i + 1] + tok[0:1, 0:1], dx, f"a{i}_dhn")

    return loss, dx, {k: jnp.concatenate(v, axis=0) for k, v in vec.items()}


VECTORS = ("norm_a", "scale_a", "norm_kv", "norm_b", "norm_f")
SHARDED_VECTORS = ("norm_a", "scale_a")
GROUPS = {
    "a0": (("w_in", "w_in_a", 0), ("w_grp", "w_grp_a", 0), ("w_out", "w_out_a", 0)),
    "a1": (("w_in", "w_in_a", 1), ("w_grp", "w_grp_a", 1), ("w_out", "w_out_a", 1)),
    "kv": (("w_k", "w_k", None), ("w_v", "w_v", None)),
    "b0": (("w_in", "w_in_b", 0), ("w_out", "w_out_b", 0)),
    "b1": (("w_in", "w_in_b", 1), ("w_out", "w_out_b", 1)),
}
FIRST = (("w_in", "w_in_a", 0),)
PREFETCHED = {"a0_rest": GROUPS["a0"][1:], "a1": GROUPS["a1"], "kv": GROUPS["kv"], "b0": GROUPS["b0"], "b1": GROUPS["b1"]}


def kernel(x, norm_a, w_in_a, w_grp_a, scale_a, w_out_a, norm_kv, w_k, w_v, norm_b, w_in_b, w_out_b, norm_f, loss_target, m_norm_a, m_w_in_a, m_w_grp_a, m_scale_a, m_w_out_a, m_norm_kv, m_w_k, m_w_v, m_norm_b, m_w_in_b, m_w_out_b, m_norm_f, v_norm_a, v_w_in_a, v_w_grp_a, v_scale_a, v_w_out_a, v_norm_kv, v_w_k, v_w_v, v_norm_b, v_w_in_b, v_w_out_b, v_norm_f):
    w = dict(norm_a=norm_a, w_in_a=w_in_a, w_grp_a=w_grp_a, scale_a=scale_a, w_out_a=w_out_a, norm_kv=norm_kv,
             w_k=w_k, w_v=w_v, norm_b=norm_b, w_in_b=w_in_b, w_out_b=w_out_b, norm_f=norm_f)
    m = dict(norm_a=m_norm_a, w_in_a=m_w_in_a, w_grp_a=m_w_grp_a, scale_a=m_scale_a, w_out_a=m_w_out_a,
             norm_kv=m_norm_kv, w_k=m_w_k, w_v=m_w_v, norm_b=m_norm_b, w_in_b=m_w_in_b, w_out_b=m_w_out_b,
             norm_f=m_norm_f)
    v = dict(norm_a=v_norm_a, w_in_a=v_w_in_a, w_grp_a=v_w_grp_a, scale_a=v_scale_a, w_out_a=v_w_out_a,
             norm_kv=v_norm_kv, w_k=v_w_k, w_v=v_w_v, norm_b=v_norm_b, w_in_b=v_w_in_b, w_out_b=v_w_out_b,
             norm_f=v_norm_f)
    D = x.shape[2]
    me = _block_index(*_position())

    def shard(members):
        return [w[p].astype(BF16) if layer is None else w[p][layer].astype(BF16) for _, p, layer in members]

    def as_weights(members, gathered):
        out = dict(zip([n for n, _, _ in members], gathered))
        if "w_grp" in out:
            g = out["w_grp"]
            out["w_grp"] = g.transpose(1, 0, 2, 3).reshape(g.shape[1], g.shape[3], g.shape[3])
        if "w_k" in out:
            out = {"w_kv": jnp.concatenate([out["w_k"].reshape(D, D), out["w_v"].reshape(D, D)], axis=1)}
        return out

    first = _all_gather(shard(FIRST) + [w[k] for k in SHARDED_VECTORS], "gather_first")
    n_first = len(FIRST)
    vecs = {k: g.transpose(1, 0, 2).reshape(w[k].shape[0], D) for k, g in zip(SHARDED_VECTORS, first[n_first:])}
    vecs.update(norm_kv=norm_kv[None, :], norm_b=norm_b, norm_f=norm_f[None, :])
    srcs, lands = [], []
    for group in PREFETCHED:
        for s in shard(PREFETCHED[group]):
            srcs.append(s)
            lands.append(lax.dynamic_update_index_in_dim(lax.empty((N_DEV,) + s.shape, s.dtype), s[None], me, 0))
    inflight, at = {}, 0
    token = None
    for group in PREFETCHED:
        n = len(PREFETCHED[group])
        inflight[group] = _push_start(srcs[at:at + n], lands[at:at + n], SIBLING_AND_SAME_CORES, False, True,
                                      first[0] if token is None else token, f"gather_{group}_start")
        token = inflight[group][4]
        at += n
    vecs["norm_a"] = vecs["norm_a"] + token[0:1, 0:1]

    def get_weights(group, after):
        if group == "a0":
            return as_weights(FIRST, first[0:n_first])
        if group not in PREFETCHED:
            return {}
        half = _push_wait(inflight[group], SIBLING_AND_SAME_CORES, False, True, after, f"gather_{group}_wait")[1]
        return as_weights(PREFETCHED[group], _pass_on(half, f"gather_{group}_pass"))

    sent = {}

    def put_grads(group, grads):
        blocks = [grads[n] for n, _, _ in GROUPS[group]]
        lands = [lax.empty((N_DEV - 1,) + b.shape[1:], b.dtype) for b in blocks]
        sent[group] = _push_start(blocks, lands, ALL_PEERS, True, False, jnp.zeros((8, 128), F32),
                                  f"exchange_{group}_start")
        return sent[group][4]

    loss, dx, vec = _local_step(x[0], loss_target[0], vecs, w_in_a.shape[0], w_in_b.shape[0], get_weights, put_grads)
    rows = _all_reduce_rows(jnp.concatenate([vec[k] for k in VECTORS], axis=0), "reduce_vectors")

    out = {}
    after = dx
    for group in sent:
        blocks, arrived = _push_wait(sent[group], ALL_PEERS, True, False, after, f"exchange_{group}_wait")
        for (_, p, layer), blk, got in zip(GROUPS[group], blocks, arrived):
            cols = w[p].shape[-1]
            own = lax.dynamic_index_in_dim(blk, me, 0, keepdims=False).reshape(-1, cols)
            n_layers = 1 if layer is None else w[p].shape[0]
            stacked = lambda t: t.reshape(n_layers, -1, cols)
            res = _adamw_blocks(own, got.reshape(N_DEV - 1, -1, cols), stacked(w[p]), stacked(m[p]), stacked(v[p]),
                                0 if layer is None else layer, out.get(p), f"adamw_{group}_{p}")
            out[p] = res
            after = res[1]
    out = {p: [r.reshape(w[p].shape) for r in res] for p, res in out.items()}
    start = 0
    for k in VECTORS:
        n_rows = vec[k].shape[0]
        g = rows[start:start + n_rows]
        start += n_rows
        if k in SHARDED_VECTORS:
            g = lax.dynamic_slice_in_dim(g, me * (D // N_DEV), D // N_DEV, axis=1)
        res = _adamw_rows(g, w[k].reshape(g.shape), m[k].reshape(g.shape), v[k].reshape(g.shape), f"adamw_{k}")
        out[k] = [r.reshape(w[k].shape) for r in [g] + list(res)]

    names = ("norm_a", "w_in_a", "w_grp_a", "scale_a", "w_out_a", "norm_kv", "w_k", "w_v", "norm_b", "w_in_b",
             "w_out_b", "norm_f")
    total = lax.psum(loss[0, 0], ("x", "y", "c"))
    return (total, dx[None], *[out[k][0] for k in names], *[out[k][1] for k in names],
            *[out[k][2] for k in names], *[out[k][3] for k in names])
```
